```python
import jax, jax.numpy as jnp
from jax import lax
import numpy as np

D_MODEL = 1024
BATCH = 16
SEQ = 2048
DEPTH = 1
DEC_BATCH = 32
DEC_SEQ = 4
PAST_LEN = 16384
PAGE_SIZE = 128

R_HEAD_DIM = 64
R_WIDTH = D_MODEL // 2
R_HEADS = R_WIDTH // R_HEAD_DIM
DECAY_LORA = 64
AAA_LORA = 64
R_SHIFT_WIDTH = 3 * R_WIDTH + DECAY_LORA + AAA_LORA
N_HEAD_DIM = 64
N_WIDTH = D_MODEL - R_WIDTH
N_HEADS = N_WIDTH // N_HEAD_DIM
N_KV_HEADS = 2
N_GROUP = N_HEADS // N_KV_HEADS
N_BRANCH = 3
KV_SLOTS = 2 * N_BRANCH
CMP_BLOCK = 32
CMP_STRIDE = 16
SEL_BLOCK = 64
SEL_TOPK = 16
WINDOW = 512
SEL_QUERY_BLOCK = 32
WIN_QUERY_BLOCK = 128
MIX_WIDTH = R_WIDTH + N_WIDTH
PROJ_SPLITS = (R_SHIFT_WIDTH, R_WIDTH, N_WIDTH, N_WIDTH, KV_SLOTS * N_KV_HEADS * N_HEAD_DIM, N_BRANCH * N_HEADS)
PROJ_WIDTH = sum(PROJ_SPLITS)
RMS_EPS = 1e-6
GN_EPS = 64e-5
MASK_NEG = -1e30
SEL_BIAS = 1e4
ATTN_SCALE = N_HEAD_DIM ** -0.5

kernel_name = 'hymba_rwkv7_nsa_decode_step'


def rms_norm(x, g):
    xf = x.astype(jnp.float32)
    y = xf * lax.rsqrt(jnp.mean(xf * xf, axis=-1, keepdims=True) + RMS_EPS)
    return (y * g.astype(jnp.float32)).astype(x.dtype)


def masked_softmax(s, mask):
    s = jnp.where(mask, s.astype(jnp.float32), MASK_NEG)
    p = jax.nn.softmax(s, axis=-1)
    return jnp.where(mask, p, 0.0)


def split_proj(p):
    offs = np.cumsum((0,) + PROJ_SPLITS)
    return tuple(p[..., int(offs[i]):int(offs[i + 1])] for i in range(len(PROJ_SPLITS)))


def to_heads(q):
    B, T, _ = q.shape
    return q.reshape(B, T, N_KV_HEADS, N_GROUP, N_HEAD_DIM).transpose(0, 2, 3, 1, 4)


def rwkv_mix(z, z_prev, s0, gate, mu_shift, w0, w_decay_up, a0, w_aaa_up, k_k, k_a, r_k, gn_w, gn_b):
    B, T, _ = z.shape
    f32 = jnp.float32
    zp = jnp.concatenate([z_prev[:, None].astype(z.dtype), z[:, :-1]], axis=1)
    zs = z + (zp - z) * mu_shift
    r, k, v, wd, ad = jnp.split(zs, [R_WIDTH, 2 * R_WIDTH, 3 * R_WIDTH, 3 * R_WIDTH + DECAY_LORA], axis=-1)
    w_log = -jax.nn.softplus(-(w0 + jnp.tanh(wd) @ w_decay_up).astype(f32)) - 0.5
    decay = jnp.exp(-jnp.exp(w_log))
    a = jax.nn.sigmoid((a0 + ad @ w_aaa_up).astype(f32))
    hs = (B, T, R_HEADS, R_HEAD_DIM)
    r, k, v, decay, a = (t.astype(f32).reshape(hs) for t in (r, k, v, decay, a))
    hshape = (R_HEADS, R_HEAD_DIM)
    kk = k * k_k.astype(f32).reshape(hshape)
    kk = kk * lax.rsqrt(jnp.maximum(jnp.sum(kk * kk, axis=-1, keepdims=True), 1e-24))
    k = k * (1.0 + (a - 1.0) * k_a.astype(f32).reshape(hshape))

    def step(S, inp):
        r_t, w_t, k_t, v_t, kk_t, a_t = inp
        sa = jnp.einsum('bhij,bhj->bhi', S, kk_t)
        S = (S * w_t[:, :, None, :] - sa[..., None] * (kk_t * a_t)[:, :, None, :]
             + v_t[..., None] * k_t[:, :, None, :])
        return S, jnp.einsum('bhij,bhj->bhi', S, r_t)

    s_fin, y = lax.scan(step, s0.astype(f32), tuple(jnp.moveaxis(t, 1, 0) for t in (r, decay, k, v, kk, a)))
    y = jnp.moveaxis(y, 0, 1)
    yc = y - jnp.mean(y, axis=-1, keepdims=True)
    y = yc * lax.rsqrt(jnp.mean(yc * yc, axis=-1, keepdims=True) + GN_EPS)
    y = y * gn_w.astype(f32).reshape(hshape) + gn_b.astype(f32).reshape(hshape)
    y = y + jnp.sum(r * k * r_k.astype(f32).reshape(hshape), axis=-1, keepdims=True) * v
    out = y.reshape(B, T, R_WIDTH).astype(z.dtype) * jax.nn.silu(gate)
    return out, s_fin, z[:, -1]


def compress(kv, w_pos, w_mix):
    B, L = kv.shape[:2]
    n_sub = CMP_BLOCK // CMP_STRIDE
    n_chunk = L // CMP_STRIDE
    n_cmp = n_chunk - n_sub + 1
    c = kv[:, :n_chunk * CMP_STRIDE].reshape(B, n_chunk, CMP_STRIDE, 2, N_KV_HEADS, N_HEAD_DIM)
    wp = w_pos.reshape(2, n_sub, CMP_STRIDE, N_HEAD_DIM)
    pooled = jnp.einsum('bjpekd,epd->bjekd', c[:, 0:n_cmp], wp[:, 0])
    for m in range(1, n_sub):
        pooled = pooled + jnp.einsum('bjpekd,epd->bjekd', c[:, m:m + n_cmp], wp[:, m])
    return jnp.einsum('bjekd,edf->bjekf', pooled, w_mix)


def compressed_attn(q, kc, qpos):
    n_cmp = kc.shape[1]
    kend = jnp.arange(n_cmp) * CMP_STRIDE + (CMP_BLOCK - 1)
    mask = kend[None, :] <= qpos[:, None]
    s = jnp.einsum('bkgtd,bjkd->bkgtj', q, kc[:, :, 0]) * ATTN_SCALE
    p = masked_softmax(s, mask)
    o = jnp.einsum('bkgtj,bjkd->bkgtd', p.astype(q.dtype), kc[:, :, 1])
    return o, jnp.sum(p, axis=2)


def select_blocks(imp, qpos, total_len):
    n_cmp = imp.shape[-1]
    n_sel = -(-total_len // SEL_BLOCK)
    cs = jnp.arange(n_cmp) * CMP_STRIDE
    ss = jnp.arange(n_sel) * SEL_BLOCK
    overlap = ((cs[:, None] < ss[None, :] + SEL_BLOCK) & (cs[:, None] + CMP_BLOCK > ss[None, :])).astype(jnp.float32)
    score = jnp.einsum('bktj,js->bkts', imp, overlap)
    cur = (qpos // SEL_BLOCK)[:, None]
    sid = jnp.arange(n_sel)[None, :]
    valid = sid <= cur
    forced = (sid == 0) | (sid == cur) | (sid == cur - 1)
    score = jnp.where(valid, score + jnp.where(forced, SEL_BIAS, 0.0), -SEL_BIAS)
    vals, idx = lax.top_k(score, min(SEL_TOPK, n_sel))
    return idx, vals > -0.5 * SEL_BIAS


def selected_attn(q, ks, vs, idx, valid, qpos):
    B, KV, T, K = idx.shape
    kpos = idx[..., None] * SEL_BLOCK + jnp.arange(SEL_BLOCK)
    mask = (valid[..., None] & (kpos <= qpos[:, None, None])).reshape(B, KV, 1, T, K * SEL_BLOCK)
    ks = ks.reshape(B, KV, T, K * SEL_BLOCK, N_HEAD_DIM)
    vs = vs.reshape(B, KV, T, K * SEL_BLOCK, N_HEAD_DIM)
    s = jnp.einsum('bkgtd,bktnd->bkgtn', q, ks) * ATTN_SCALE
    p = masked_softmax(s, mask)
    return jnp.einsum('bkgtn,bktnd->bkgtd', p.astype(q.dtype), vs)


def selected_prompt_attn(q, kv_sel, idx, valid, qpos):
    B, T = kv_sel.shape[:2]
    n_sel = T // SEL_BLOCK
    blocks = kv_sel.reshape(B, n_sel, SEL_BLOCK, 2, N_KV_HEADS, N_HEAD_DIM).transpose(0, 4, 1, 2, 3, 5)
    bk, bv = blocks[..., 0, :], blocks[..., 1, :]
    bi = jnp.arange(B)[:, None, None, None]
    gi = jnp.arange(N_KV_HEADS)[None, :, None, None]
    nq = T // SEL_QUERY_BLOCK

    def split(t, ax):
        return jnp.moveaxis(t.reshape(t.shape[:ax] + (nq, SEL_QUERY_BLOCK) + t.shape[ax + 1:]), ax, 0)

    def chunk(args):
        qc, ic, vc, pc = args
        return selected_attn(qc, bk[bi, gi, ic], bv[bi, gi, ic], ic, vc, pc)

    out = lax.map(chunk, (split(q, 3), split(idx, 2), split(valid, 2), qpos.reshape(nq, SEL_QUERY_BLOCK)))
    return jnp.moveaxis(out, 0, 3).reshape(B, N_KV_HEADS, N_GROUP, T, N_HEAD_DIM)


def selected_sample_attn(q, kv_new, cache, page_table, idx, valid, qpos):
    B, T = kv_new.shape[:2]
    bpp = PAGE_SIZE // SEL_BLOCK
    n_past = page_table.shape[1] * bpp
    pool = cache.reshape(cache.shape[0] * bpp, SEL_BLOCK, cache.shape[2], N_KV_HEADS, N_HEAD_DIM)
    n_tail = -(-T // SEL_BLOCK)
    tail = jnp.pad(kv_new, ((0, 0), (0, n_tail * SEL_BLOCK - T), (0, 0), (0, 0), (0, 0)))
    tail = tail.reshape(B, n_tail, SEL_BLOCK, 2, N_KV_HEADS, N_HEAD_DIM).transpose(0, 4, 1, 2, 3, 5)
    bi = jnp.arange(B)[:, None, None, None]
    gi = jnp.arange(N_KV_HEADS)[None, :, None, None]
    ip = jnp.minimum(idx, n_past - 1)
    phys = page_table[bi, ip // bpp] * bpp + ip % bpp
    past_rows = pool[phys, :, 2:4, gi]
    tail_rows = tail[bi, gi, jnp.clip(idx - n_past, 0, n_tail - 1)]
    rows = jnp.where((idx < n_past)[..., None, None, None], past_rows.astype(tail_rows.dtype), tail_rows)
    return selected_attn(q, rows[..., 0, :], rows[..., 1, :], idx, valid, qpos)


def window_attn(q, kw, qpos, kpos):
    mask = (kpos[None, :] <= qpos[:, None]) & (kpos[None, :] > qpos[:, None] - WINDOW) & (kpos[None, :] >= 0)
    s = jnp.einsum('bkgtd,bskd->bkgts', q, kw[:, :, 0]) * ATTN_SCALE
    p = masked_softmax(s, mask)
    return jnp.einsum('bkgts,bskd->bkgtd', p.astype(q.dtype), kw[:, :, 1])


def window_prompt_attn(q, kv_win):
    B, T = kv_win.shape[:2]
    padded = jnp.pad(kv_win, ((0, 0), (WINDOW, 0), (0, 0), (0, 0), (0, 0)))
    nq = T // WIN_QUERY_BLOCK

    def chunk(i):
        start = i * WIN_QUERY_BLOCK
        kw = lax.dynamic_slice_in_dim(padded, start, WINDOW + WIN_QUERY_BLOCK, axis=1)
        qc = lax.dynamic_slice_in_dim(q, start, WIN_QUERY_BLOCK, axis=3)
        qpos = start + jnp.arange(WIN_QUERY_BLOCK)
        kpos = start - WINDOW + jnp.arange(WINDOW + WIN_QUERY_BLOCK)
        return window_attn(qc, kw, qpos, kpos)

    out = lax.map(chunk, jnp.arange(nq))
    return jnp.moveaxis(out, 0, 3).reshape(B, N_KV_HEADS, N_GROUP, T, N_HEAD_DIM)


def mixer_out(y_r, o_c, o_s, o_w, gl, gate_n, w_out):
    B, T, _ = y_r.shape
    g = jax.nn.sigmoid(gl.astype(jnp.float32)).reshape(B, T, N_BRANCH, N_KV_HEADS, N_GROUP)
    g = g.transpose(2, 0, 3, 4, 1)[..., None]
    o = g[0] * o_c + g[1] * o_s + g[2] * o_w
    o = o.transpose(0, 3, 1, 2, 4).reshape(B, T, N_WIDTH).astype(y_r.dtype) * jax.nn.silu(gate_n)
    return jnp.concatenate([y_r, o], axis=-1) @ w_out


def prompt_layer(x, norm_g, w_in, mu_shift, w0, w_decay_up, a0, w_aaa_up, k_k, k_a, r_k, gn_w, gn_b,
                 w_cmp_pos, w_cmp_mix, w_out):
    B, T, _ = x.shape
    zr, gate_r, q, gate_n, kv, gl = split_proj(rms_norm(x, norm_g) @ w_in)
    y_r, wkv, shift = rwkv_mix(zr, jnp.zeros((B, R_SHIFT_WIDTH), zr.dtype),
                               jnp.zeros((B, R_HEADS, R_HEAD_DIM, R_HEAD_DIM), jnp.float32), gate_r,
                               mu_shift, w0, w_decay_up, a0, w_aaa_up, k_k, k_a, r_k, gn_w, gn_b)
    kv = kv.reshape(B, T, KV_SLOTS, N_KV_HEADS, N_HEAD_DIM)
    qh = to_heads(q)
    qpos = jnp.arange(T)
    o_c, imp = compressed_attn(qh, compress(kv[:, :, 0:2], w_cmp_pos, w_cmp_mix), qpos)
    idx, valid = select_blocks(imp, qpos, T)
    o_s = selected_prompt_attn(qh, kv[:, :, 2:4], idx, valid, qpos)
    o_w = window_prompt_attn(qh, kv[:, :, 4:6])
    y = x + mixer_out(y_r, o_c, o_s, o_w, gl, gate_n, w_out)
    return y, kv[:, :, 0:4], kv[:, T - min(WINDOW, T):, 4:6], wkv, shift


def sample_layer(x, cache, win_buf, wkv0, shift0, page_table, norm_g, w_in, mu_shift, w0, w_decay_up, a0,
                 w_aaa_up, k_k, k_a, r_k, gn_w, gn_b, w_cmp_pos, w_cmp_mix, w_out):
    B, T, _ = x.shape
    past = page_table.shape[1] * PAGE_SIZE
    zr, gate_r, q, gate_n, kv, gl = split_proj(rms_norm(x, norm_g) @ w_in)
    y_r, wkv, shift = rwkv_mix(zr, shift0, wkv0, gate_r,
                               mu_shift, w0, w_decay_up, a0, w_aaa_up, k_k, k_a, r_k, gn_w, gn_b)
    kv = kv.reshape(B, T, KV_SLOTS, N_KV_HEADS, N_HEAD_DIM)
    qh = to_heads(q)
    qpos = past + jnp.arange(T)
    past_cmp = cache[page_table, :, 0:2].reshape(B, past, 2, N_KV_HEADS, N_HEAD_DIM)
    full_cmp = jnp.concatenate([past_cmp.astype(kv.dtype), kv[:, :, 0:2]], axis=1)
    o_c, imp = compressed_attn(qh, compress(full_cmp, w_cmp_pos, w_cmp_mix), qpos)
    idx, valid = select_blocks(imp, qpos, past + T)
    o_s = selected_sample_attn(qh, kv[:, :, 2:4], cache, page_table, idx, valid, qpos)
    nb = win_buf.shape[1]
    keys_w = jnp.concatenate([win_buf.astype(kv.dtype), kv[:, :, 4:6]], axis=1)
    o_w = window_attn(qh, keys_w, qpos, past - nb + jnp.arange(nb + T))
    y = x + mixer_out(y_r, o_c, o_s, o_w, gl, gate_n, w_out)
    n_keep = min(WINDOW, nb + T)
    return y, kv[:, :, 0:4], keys_w[:, nb + T - n_keep:], wkv, shift


def setup_inputs(seed: int = 0) -> dict:
    key = jax.random.key(seed)
    k = jax.random.split(key, 24)
    f32 = jnp.float32

    def nrm(i, shape, scale):
        return scale * jax.random.normal(k[i], shape, f32)

    n_pages = PAST_LEN // PAGE_SIZE
    n_phys = (5 * DEC_BATCH * n_pages) // 4
    win_len = min(WINDOW, PAST_LEN)
    hd = N_HEAD_DIM
    page_table = jax.random.permutation(k[6], n_phys)[:DEC_BATCH * n_pages].reshape(DEC_BATCH, n_pages).astype(jnp.int32)
    return {
        'x_prompt': nrm(0, (BATCH, SEQ, D_MODEL), 1.0),
        'x_sample': nrm(1, (DEC_BATCH, DEC_SEQ, D_MODEL), 1.0),
        'cache_kv': nrm(2, (DEPTH, n_phys, PAGE_SIZE, 4, N_KV_HEADS, hd), 1.0),
        'cache_kv_win': nrm(3, (DEPTH, DEC_BATCH, win_len, 2, N_KV_HEADS, hd), 1.0),
        'state_wkv': nrm(4, (DEPTH, DEC_BATCH, R_HEADS, R_HEAD_DIM, R_HEAD_DIM), 0.5),
        'state_shift': nrm(5, (DEPTH, DEC_BATCH, R_SHIFT_WIDTH), 1.0),
        'page_table': page_table,
        'norm_g': 1.0 + nrm(7, (DEPTH, D_MODEL), 0.02),
        'w_in': nrm(8, (DEPTH, D_MODEL, PROJ_WIDTH), D_MODEL ** -0.5),
        'mu_shift': jax.random.uniform(k[9], (DEPTH, R_SHIFT_WIDTH), f32),
        'w0': nrm(10, (DEPTH, R_WIDTH), 0.5),
        'w_decay_up': nrm(11, (DEPTH, DECAY_LORA, R_WIDTH), 0.1),
        'a0': nrm(12, (DEPTH, R_WIDTH), 0.1),
        'w_aaa_up': nrm(13, (DEPTH, AAA_LORA, R_WIDTH), AAA_LORA ** -0.5),
        'k_k': 0.85 + nrm(14, (DEPTH, R_WIDTH), 0.02),
        'k_a': 1.0 + nrm(15, (DEPTH, R_WIDTH), 0.02),
        'r_k': nrm(16, (DEPTH, R_WIDTH), 0.1),
        'gn_w': 1.0 + nrm(17, (DEPTH, R_WIDTH), 0.02),
        'gn_b': nrm(18, (DEPTH, R_WIDTH), 0.02),
        'w_cmp_pos': (1.0 + nrm(19, (DEPTH, 2, CMP_BLOCK, hd), 0.1)) * CMP_BLOCK ** -0.5,
        'w_cmp_mix': nrm(20, (DEPTH, 2, hd, hd), hd ** -0.5),
        'w_out': nrm(21, (DEPTH, MIX_WIDTH, D_MODEL), MIX_WIDTH ** -0.5),
        'final_g': 1.0 + nrm(22, (D_MODEL,), 0.02),
    }


def reference(x_prompt, x_sample, cache_kv, cache_kv_win, state_wkv, state_shift, page_table,
              norm_g, w_in, mu_shift, w0, w_decay_up, a0, w_aaa_up, k_k, k_a, r_k, gn_w, gn_b,
              w_cmp_pos, w_cmp_mix, w_out, final_g):
    y_p, y_s = x_prompt, x_sample
    kvp, kvs, wnp, wns, skp, sks, shp, shs = [], [], [], [], [], [], [], []
    for layer in range(DEPTH):
        lw = (norm_g[layer], w_in[layer], mu_shift[layer], w0[layer], w_decay_up[layer], a0[layer],
              w_aaa_up[layer], k_k[layer], k_a[layer], r_k[layer], gn_w[layer], gn_b[layer],
              w_cmp_pos[layer], w_cmp_mix[layer], w_out[layer])
        y_p, p_kv, p_win, p_wkv, p_shift = prompt_layer(y_p, *lw)
        y_s, s_kv, s_win, s_wkv, s_shift = sample_layer(y_s, cache_kv[layer], cache_kv_win[layer],
                                                        state_wkv[layer], state_shift[layer], page_table, *lw)
        kvp.append(p_kv); kvs.append(s_kv); wnp.append(p_win); wns.append(s_win)
        skp.append(p_wkv); sks.append(s_wkv); shp.append(p_shift); shs.append(s_shift)
    y_prompt = rms_norm(y_p, final_g)
    y_sample = rms_norm(y_s, final_g)
    kv_prompt = jnp.stack(kvp)
    kv_sample = jnp.stack(kvs)
    win_prompt = jnp.stack(wnp)
    win_sample = jnp.stack(wns)
    wkv_prompt = jnp.stack(skp)
    wkv_sample = jnp.stack(sks)
    shift_prompt = jnp.stack(shp)
    shift_sample = jnp.stack(shs)
    return (y_prompt, y_sample, kv_prompt, kv_sample, win_prompt, win_sample, wkv_prompt, wkv_sample, shift_prompt, shift_sample)
```

```python
import jax, jax.numpy as jnp
from jax import lax
import numpy as np
from jax.experimental import pallas as pl
from jax.experimental.pallas import tpu as pltpu

D_MODEL = 1024
BATCH = 16
SEQ = 2048
DEPTH = 1
DEC_BATCH = 32
DEC_SEQ = 4
PAST_LEN = 16384
PAGE_SIZE = 128

R_HEAD_DIM = 64
R_WIDTH = D_MODEL // 2
R_HEADS = R_WIDTH // R_HEAD_DIM
DECAY_LORA = 64
AAA_LORA = 64
R_SHIFT_WIDTH = 3 * R_WIDTH + DECAY_LORA + AAA_LORA
N_HEAD_DIM = 64
N_WIDTH = D_MODEL - R_WIDTH
N_HEADS = N_WIDTH // N_HEAD_DIM
N_KV_HEADS = 2
N_GROUP = N_HEADS // N_KV_HEADS
N_BRANCH = 3
KV_SLOTS = 2 * N_BRANCH
CMP_BLOCK = 32
CMP_STRIDE = 16
SEL_BLOCK = 64
SEL_TOPK = 16
WINDOW = 512
SEL_QUERY_BLOCK = 32
WIN_QUERY_BLOCK = 128
MIX_WIDTH = R_WIDTH + N_WIDTH
PROJ_SPLITS = (R_SHIFT_WIDTH, R_WIDTH, N_WIDTH, N_WIDTH, KV_SLOTS * N_KV_HEADS * N_HEAD_DIM, N_BRANCH * N_HEADS)
PROJ_WIDTH = sum(PROJ_SPLITS)
RMS_EPS = 1e-6
GN_EPS = 64e-5
MASK_NEG = -1e30
SEL_BIAS = 1e4
ATTN_SCALE = N_HEAD_DIM ** -0.5


def rms_norm(x, g):
    xf = x.astype(jnp.float32)
    y = xf * lax.rsqrt(jnp.mean(xf * xf, axis=-1, keepdims=True) + RMS_EPS)
    return (y * g.astype(jnp.float32)).astype(x.dtype)


def masked_softmax(s, mask):
    s = jnp.where(mask, s.astype(jnp.float32), MASK_NEG)
    p = jax.nn.softmax(s, axis=-1)
    return jnp.where(mask, p, 0.0)


def split_proj(p):
    offs = np.cumsum((0,) + PROJ_SPLITS)
    return tuple(p[..., int(offs[i]):int(offs[i + 1])] for i in range(len(PROJ_SPLITS)))


def to_heads(q):
    B, T, _ = q.shape
    return q.reshape(B, T, N_KV_HEADS, N_GROUP, N_HEAD_DIM).transpose(0, 2, 3, 1, 4)


def rwkv_mix(z, z_prev, s0, gate, mu_shift, w0, w_decay_up, a0, w_aaa_up, k_k, k_a, r_k, gn_w, gn_b):
    B, T, _ = z.shape
    f32 = jnp.float32
    zp = jnp.concatenate([z_prev[:, None].astype(z.dtype), z[:, :-1]], axis=1)
    zs = z + (zp - z) * mu_shift
    r, k, v, wd, ad = jnp.split(zs, [R_WIDTH, 2 * R_WIDTH, 3 * R_WIDTH, 3 * R_WIDTH + DECAY_LORA], axis=-1)
    w_log = -jax.nn.softplus(-(w0 + jnp.tanh(wd) @ w_decay_up).astype(f32)) - 0.5
    decay = jnp.exp(-jnp.exp(w_log))
    a = jax.nn.sigmoid((a0 + ad @ w_aaa_up).astype(f32))
    hs = (B, T, R_HEADS, R_HEAD_DIM)
    r, k, v, decay, a = (t.astype(f32).reshape(hs) for t in (r, k, v, decay, a))
    hshape = (R_HEADS, R_HEAD_DIM)
    kk = k * k_k.astype(f32).reshape(hshape)
    kk = kk * lax.rsqrt(jnp.maximum(jnp.sum(kk * kk, axis=-1, keepdims=True), 1e-24))
    k = k * (1.0 + (a - 1.0) * k_a.astype(f32).reshape(hshape))

    def step(S, inp):
        r_t, w_t, k_t, v_t, kk_t, a_t = inp
        sa = jnp.einsum('bhij,bhj->bhi', S, kk_t)
        S = (S * w_t[:, :, None, :] - sa[..., None] * (kk_t * a_t)[:, :, None, :]
             + v_t[..., None] * k_t[:, :, None, :])
        return S, jnp.einsum('bhij,bhj->bhi', S, r_t)

    s_fin, y = lax.scan(step, s0.astype(f32), tuple(jnp.moveaxis(t, 1, 0) for t in (r, decay, k, v, kk, a)))
    y = jnp.moveaxis(y, 0, 1)
    yc = y - jnp.mean(y, axis=-1, keepdims=True)
    y = yc * lax.rsqrt(jnp.mean(yc * yc, axis=-1, keepdims=True) + GN_EPS)
    y = y * gn_w.astype(f32).reshape(hshape) + gn_b.astype(f32).reshape(hshape)
    y = y + jnp.sum(r * k * r_k.astype(f32).reshape(hshape), axis=-1, keepdims=True) * v
    out = y.reshape(B, T, R_WIDTH).astype(z.dtype) * jax.nn.silu(gate)
    return out, s_fin, z[:, -1]


def compress(kv, w_pos, w_mix):
    B, L = kv.shape[:2]
    n_sub = CMP_BLOCK // CMP_STRIDE
    n_chunk = L // CMP_STRIDE
    n_cmp = n_chunk - n_sub + 1
    c = kv[:, :n_chunk * CMP_STRIDE].reshape(B, n_chunk, CMP_STRIDE, 2, N_KV_HEADS, N_HEAD_DIM)
    wp = w_pos.reshape(2, n_sub, CMP_STRIDE, N_HEAD_DIM)
    pooled = jnp.einsum('bjpekd,epd->bjekd', c[:, 0:n_cmp], wp[:, 0])
    for m in range(1, n_sub):
        pooled = pooled + jnp.einsum('bjpekd,epd->bjekd', c[:, m:m + n_cmp], wp[:, m])
    return jnp.einsum('bjekd,edf->bjekf', pooled, w_mix)


def compressed_attn(q, kc, qpos):
    n_cmp = kc.shape[1]
    kend = jnp.arange(n_cmp) * CMP_STRIDE + (CMP_BLOCK - 1)
    mask = kend[None, :] <= qpos[:, None]
    s = jnp.einsum('bkgtd,bjkd->bkgtj', q, kc[:, :, 0]) * ATTN_SCALE
    p = masked_softmax(s, mask)
    o = jnp.einsum('bkgtj,bjkd->bkgtd', p.astype(q.dtype), kc[:, :, 1])
    return o, jnp.sum(p, axis=2)


def select_blocks(imp, qpos, total_len):
    n_cmp = imp.shape[-1]
    n_sel = -(-total_len // SEL_BLOCK)
    cs = jnp.arange(n_cmp) * CMP_STRIDE
    ss = jnp.arange(n_sel) * SEL_BLOCK
    overlap = ((cs[:, None] < ss[None, :] + SEL_BLOCK) & (cs[:, None] + CMP_BLOCK > ss[None, :])).astype(jnp.float32)
    score = jnp.einsum('bktj,js->bkts', imp, overlap)
    cur = (qpos // SEL_BLOCK)[:, None]
    sid = jnp.arange(n_sel)[None, :]
    valid = sid <= cur
    forced = (sid == 0) | (sid == cur) | (sid == cur - 1)
    score = jnp.where(valid, score + jnp.where(forced, SEL_BIAS, 0.0), -SEL_BIAS)
    vals, idx = lax.top_k(score, min(SEL_TOPK, n_sel))
    return idx, vals > -0.5 * SEL_BIAS


def selected_attn(q, ks, vs, idx, valid, qpos):
    B, KV, T, K = idx.shape
    kpos = idx[..., None] * SEL_BLOCK + jnp.arange(SEL_BLOCK)
    mask = (valid[..., None] & (kpos <= qpos[:, None, None])).reshape(B, KV, 1, T, K * SEL_BLOCK)
    ks = ks.reshape(B, KV, T, K * SEL_BLOCK, N_HEAD_DIM)
    vs = vs.reshape(B, KV, T, K * SEL_BLOCK, N_HEAD_DIM)
    s = jnp.einsum('bkgtd,bktnd->bkgtn', q, ks) * ATTN_SCALE
    p = masked_softmax(s, mask)
    return jnp.einsum('bkgtn,bktnd->bkgtd', p.astype(q.dtype), vs)


def selected_prompt_attn(q, kv_sel, idx, valid, qpos):
    B, T = kv_sel.shape[:2]
    n_sel = T // SEL_BLOCK
    blocks = kv_sel.reshape(B, n_sel, SEL_BLOCK, 2, N_KV_HEADS, N_HEAD_DIM).transpose(0, 4, 1, 2, 3, 5)
    bk, bv = blocks[..., 0, :], blocks[..., 1, :]
    bi = jnp.arange(B)[:, None, None, None]
    gi = jnp.arange(N_KV_HEADS)[None, :, None, None]
    nq = T // SEL_QUERY_BLOCK

    def split(t, ax):
        return jnp.moveaxis(t.reshape(t.shape[:ax] + (nq, SEL_QUERY_BLOCK) + t.shape[ax + 1:]), ax, 0)

    def chunk(args):
        qc, ic, vc, pc = args
        return selected_attn(qc, bk[bi, gi, ic], bv[bi, gi, ic], ic, vc, pc)

    out = lax.map(chunk, (split(q, 3), split(idx, 2), split(valid, 2), qpos.reshape(nq, SEL_QUERY_BLOCK)))
    return jnp.moveaxis(out, 0, 3).reshape(B, N_KV_HEADS, N_GROUP, T, N_HEAD_DIM)


def selected_sample_attn(q, kv_new, cache, page_table, idx, valid, qpos):
    B, T = kv_new.shape[:2]
    bpp = PAGE_SIZE // SEL_BLOCK
    n_past = page_table.shape[1] * bpp
    pool = cache.reshape(cache.shape[0] * bpp, SEL_BLOCK, cache.shape[2], N_KV_HEADS, N_HEAD_DIM)
    n_tail = -(-T // SEL_BLOCK)
    tail = jnp.pad(kv_new, ((0, 0), (0, n_tail * SEL_BLOCK - T), (0, 0), (0, 0), (0, 0)))
    tail = tail.reshape(B, n_tail, SEL_BLOCK, 2, N_KV_HEADS, N_HEAD_DIM).transpose(0, 4, 1, 2, 3, 5)
    bi = jnp.arange(B)[:, None, None, None]
    gi = jnp.arange(N_KV_HEADS)[None, :, None, None]
    ip = jnp.minimum(idx, n_past - 1)
    phys = page_table[bi, ip // bpp] * bpp + ip % bpp
    past_rows = pool[phys, :, 2:4, gi]
    tail_rows = tail[bi, gi, jnp.clip(idx - n_past, 0, n_tail - 1)]
    rows = jnp.where((idx < n_past)[..., None, None, None], past_rows.astype(tail_rows.dtype), tail_rows)
    return selected_attn(q, rows[..., 0, :], rows[..., 1, :], idx, valid, qpos)


def window_attn(q, kw, qpos, kpos):
    mask = (kpos[None, :] <= qpos[:, None]) & (kpos[None, :] > qpos[:, None] - WINDOW) & (kpos[None, :] >= 0)
    s = jnp.einsum('bkgtd,bskd->bkgts', q, kw[:, :, 0]) * ATTN_SCALE
    p = masked_softmax(s, mask)
    return jnp.einsum('bkgts,bskd->bkgtd', p.astype(q.dtype), kw[:, :, 1])


def window_prompt_attn(q, kv_win):
    B, T = kv_win.shape[:2]
    padded = jnp.pad(kv_win, ((0, 0), (WINDOW, 0), (0, 0), (0, 0), (0, 0)))
    nq = T // WIN_QUERY_BLOCK

    def chunk(i):
        start = i * WIN_QUERY_BLOCK
        kw = lax.dynamic_slice_in_dim(padded, start, WINDOW + WIN_QUERY_BLOCK, axis=1)
        qc = lax.dynamic_slice_in_dim(q, start, WIN_QUERY_BLOCK, axis=3)
        qpos = start + jnp.arange(WIN_QUERY_BLOCK)
        kpos = start - WINDOW + jnp.arange(WINDOW + WIN_QUERY_BLOCK)
        return window_attn(qc, kw, qpos, kpos)

    out = lax.map(chunk, jnp.arange(nq))
    return jnp.moveaxis(out, 0, 3).reshape(B, N_KV_HEADS, N_GROUP, T, N_HEAD_DIM)


def mixer_out(y_r, o_c, o_s, o_w, gl, gate_n, w_out):
    B, T, _ = y_r.shape
    g = jax.nn.sigmoid(gl.astype(jnp.float32)).reshape(B, T, N_BRANCH, N_KV_HEADS, N_GROUP)
    g = g.transpose(2, 0, 3, 4, 1)[..., None]
    o = g[0] * o_c + g[1] * o_s + g[2] * o_w
    o = o.transpose(0, 3, 1, 2, 4).reshape(B, T, N_WIDTH).astype(y_r.dtype) * jax.nn.silu(gate_n)
    return jnp.concatenate([y_r, o], axis=-1) @ w_out


def prompt_layer(x, norm_g, w_in, mu_shift, w0, w_decay_up, a0, w_aaa_up, k_k, k_a, r_k, gn_w, gn_b,
                 w_cmp_pos, w_cmp_mix, w_out):
    B, T, _ = x.shape
    zr, gate_r, q, gate_n, kv, gl = split_proj(rms_norm(x, norm_g) @ w_in)
    y_r, wkv, shift = rwkv_mix(zr, jnp.zeros((B, R_SHIFT_WIDTH), zr.dtype),
                               jnp.zeros((B, R_HEADS, R_HEAD_DIM, R_HEAD_DIM), jnp.float32), gate_r,
                               mu_shift, w0, w_decay_up, a0, w_aaa_up, k_k, k_a, r_k, gn_w, gn_b)
    kv = kv.reshape(B, T, KV_SLOTS, N_KV_HEADS, N_HEAD_DIM)
    qh = to_heads(q)
    qpos = jnp.arange(T)
    o_c, imp = compressed_attn(qh, compress(kv[:, :, 0:2], w_cmp_pos, w_cmp_mix), qpos)
    idx, valid = select_blocks(imp, qpos, T)
    o_s = selected_prompt_attn(qh, kv[:, :, 2:4], idx, valid, qpos)
    o_w = window_prompt_attn(qh, kv[:, :, 4:6])
    y = x + mixer_out(y_r, o_c, o_s, o_w, gl, gate_n, w_out)
    return y, kv[:, :, 0:4], kv[:, T - min(WINDOW, T):, 4:6], wkv, shift


def sample_layer(x, cache, win_buf, wkv0, shift0, page_table, norm_g, w_in, mu_shift, w0, w_decay_up, a0,
                 w_aaa_up, k_k, k_a, r_k, gn_w, gn_b, w_cmp_pos, w_cmp_mix, w_out):
    B, T, _ = x.shape
    past = page_table.shape[1] * PAGE_SIZE
    zr, gate_r, q, gate_n, kv, gl = split_proj(rms_norm(x, norm_g) @ w_in)
    y_r, wkv, shift = rwkv_mix(zr, shift0, wkv0, gate_r,
                               mu_shift, w0, w_decay_up, a0, w_aaa_up, k_k, k_a, r_k, gn_w, gn_b)
    kv = kv.reshape(B, T, KV_SLOTS, N_KV_HEADS, N_HEAD_DIM)
    qh = to_heads(q)
    qpos = past + jnp.arange(T)
    past_cmp = cache[page_table, :, 0:2].reshape(B, past, 2, N_KV_HEADS, N_HEAD_DIM)
    full_cmp = jnp.concatenate([past_cmp.astype(kv.dtype), kv[:, :, 0:2]], axis=1)
    o_c, imp = compressed_attn(qh, compress(full_cmp, w_cmp_pos, w_cmp_mix), qpos)
    idx, valid = select_blocks(imp, qpos, past + T)
    o_s = selected_sample_attn(qh, kv[:, :, 2:4], cache, page_table, idx, valid, qpos)
    nb = win_buf.shape[1]
    keys_w = jnp.concatenate([win_buf.astype(kv.dtype), kv[:, :, 4:6]], axis=1)
    o_w = window_attn(qh, keys_w, qpos, past - nb + jnp.arange(nb + T))
    y = x + mixer_out(y_r, o_c, o_s, o_w, gl, gate_n, w_out)
    n_keep = min(WINDOW, nb + T)
    return y, kv[:, :, 0:4], keys_w[:, nb + T - n_keep:], wkv, shift


def _final_norm_body(x_ref, g_ref, o_ref):
    x = x_ref[...]
    o_ref[...] = x * lax.rsqrt(jnp.mean(x * x, axis=-1, keepdims=True) + RMS_EPS) * g_ref[...]


def final_norm(x, g):
    B, T, D = x.shape
    x2 = x.reshape(B * T, D)
    tm = min(512, B * T)
    out = pl.pallas_call(
        _final_norm_body,
        grid=(B * T // tm,),
        in_specs=[pl.BlockSpec((tm, D), lambda i: (i, 0)), pl.BlockSpec((1, D), lambda i: (0, 0))],
        out_specs=pl.BlockSpec((tm, D), lambda i: (i, 0)),
        out_shape=jax.ShapeDtypeStruct((B * T, D), x.dtype),
        name="final_norm",
    )(x2, g.reshape(1, D))
    return out.reshape(B, T, D)


def kernel(x_prompt, x_sample, cache_kv, cache_kv_win, state_wkv, state_shift, page_table,
           norm_g, w_in, mu_shift, w0, w_decay_up, a0, w_aaa_up, k_k, k_a, r_k, gn_w, gn_b,
           w_cmp_pos, w_cmp_mix, w_out, final_g):
    y_p, y_s = x_prompt, x_sample
    kvp, kvs, wnp, wns, skp, sks, shp, shs = [], [], [], [], [], [], [], []
    for layer in range(DEPTH):
        lw = (norm_g[layer], w_in[layer], mu_shift[layer], w0[layer], w_decay_up[layer], a0[layer],
              w_aaa_up[layer], k_k[layer], k_a[layer], r_k[layer], gn_w[layer], gn_b[layer],
              w_cmp_pos[layer], w_cmp_mix[layer], w_out[layer])
        y_p, p_kv, p_win, p_wkv, p_shift = prompt_layer(y_p, *lw)
        y_s, s_kv, s_win, s_wkv, s_shift = sample_layer(y_s, cache_kv[layer], cache_kv_win[layer],
                                                        state_wkv[layer], state_shift[layer], page_table, *lw)
        kvp.append(p_kv); kvs.append(s_kv); wnp.append(p_win); wns.append(s_win)
        skp.append(p_wkv); sks.append(s_wkv); shp.append(p_shift); shs.append(s_shift)
    y_prompt = final_norm(y_p, final_g)
    y_sample = final_norm(y_s, final_g)
    return (y_prompt, y_sample, jnp.stack(kvp), jnp.stack(kvs), jnp.stack(wnp), jnp.stack(wns),
            jnp.stack(skp), jnp.stack(sks), jnp.stack(shp), jnp.stack(shs))
```

```python
import jax, jax.numpy as jnp
from jax import lax
import numpy as np
from jax.experimental import pallas as pl
from jax.experimental.pallas import tpu as pltpu

D_MODEL = 1024
BATCH = 16
SEQ = 2048
DEPTH = 1
DEC_BATCH = 32
DEC_SEQ = 4
PAST_LEN = 16384
PAGE_SIZE = 128

R_HEAD_DIM = 64
R_WIDTH = D_MODEL // 2
R_HEADS = R_WIDTH // R_HEAD_DIM
DECAY_LORA = 64
AAA_LORA = 64
R_SHIFT_WIDTH = 3 * R_WIDTH + DECAY_LORA + AAA_LORA
N_HEAD_DIM = 64
N_WIDTH = D_MODEL - R_WIDTH
N_HEADS = N_WIDTH // N_HEAD_DIM
N_KV_HEADS = 2
N_GROUP = N_HEADS // N_KV_HEADS
N_BRANCH = 3
KV_SLOTS = 2 * N_BRANCH
CMP_BLOCK = 32
CMP_STRIDE = 16
SEL_BLOCK = 64
SEL_TOPK = 16
WINDOW = 512
SEL_QUERY_BLOCK = 32
WIN_QUERY_BLOCK = 128
MIX_WIDTH = R_WIDTH + N_WIDTH
PROJ_SPLITS = (R_SHIFT_WIDTH, R_WIDTH, N_WIDTH, N_WIDTH, KV_SLOTS * N_KV_HEADS * N_HEAD_DIM, N_BRANCH * N_HEADS)
PROJ_WIDTH = sum(PROJ_SPLITS)
RMS_EPS = 1e-6
GN_EPS = 64e-5
MASK_NEG = -1e30
SEL_BIAS = 1e4
ATTN_SCALE = N_HEAD_DIM ** -0.5


def rms_norm(x, g):
    xf = x.astype(jnp.float32)
    y = xf * lax.rsqrt(jnp.mean(xf * xf, axis=-1, keepdims=True) + RMS_EPS)
    return (y * g.astype(jnp.float32)).astype(x.dtype)


def masked_softmax(s, mask):
    s = jnp.where(mask, s.astype(jnp.float32), MASK_NEG)
    p = jax.nn.softmax(s, axis=-1)
    return jnp.where(mask, p, 0.0)


def split_proj(p):
    offs = np.cumsum((0,) + PROJ_SPLITS)
    return tuple(p[..., int(offs[i]):int(offs[i + 1])] for i in range(len(PROJ_SPLITS)))


def to_heads(q):
    B, T, _ = q.shape
    return q.reshape(B, T, N_KV_HEADS, N_GROUP, N_HEAD_DIM).transpose(0, 2, 3, 1, 4)


def rwkv_mix(z, z_prev, s0, gate, mu_shift, w0, w_decay_up, a0, w_aaa_up, k_k, k_a, r_k, gn_w, gn_b):
    B, T, _ = z.shape
    f32 = jnp.float32
    zp = jnp.concatenate([z_prev[:, None].astype(z.dtype), z[:, :-1]], axis=1)
    zs = z + (zp - z) * mu_shift
    r, k, v, wd, ad = jnp.split(zs, [R_WIDTH, 2 * R_WIDTH, 3 * R_WIDTH, 3 * R_WIDTH + DECAY_LORA], axis=-1)
    w_log = -jax.nn.softplus(-(w0 + jnp.tanh(wd) @ w_decay_up).astype(f32)) - 0.5
    decay = jnp.exp(-jnp.exp(w_log))
    a = jax.nn.sigmoid((a0 + ad @ w_aaa_up).astype(f32))
    hs = (B, T, R_HEADS, R_HEAD_DIM)
    r, k, v, decay, a = (t.astype(f32).reshape(hs) for t in (r, k, v, decay, a))
    hshape = (R_HEADS, R_HEAD_DIM)
    kk = k * k_k.astype(f32).reshape(hshape)
    kk = kk * lax.rsqrt(jnp.maximum(jnp.sum(kk * kk, axis=-1, keepdims=True), 1e-24))
    k = k * (1.0 + (a - 1.0) * k_a.astype(f32).reshape(hshape))

    def step(S, inp):
        r_t, w_t, k_t, v_t, kk_t, a_t = inp
        sa = jnp.einsum('bhij,bhj->bhi', S, kk_t)
        S = (S * w_t[:, :, None, :] - sa[..., None] * (kk_t * a_t)[:, :, None, :]
             + v_t[..., None] * k_t[:, :, None, :])
        return S, jnp.einsum('bhij,bhj->bhi', S, r_t)

    s_fin, y = lax.scan(step, s0.astype(f32), tuple(jnp.moveaxis(t, 1, 0) for t in (r, decay, k, v, kk, a)))
    y = jnp.moveaxis(y, 0, 1)
    yc = y - jnp.mean(y, axis=-1, keepdims=True)
    y = yc * lax.rsqrt(jnp.mean(yc * yc, axis=-1, keepdims=True) + GN_EPS)
    y = y * gn_w.astype(f32).reshape(hshape) + gn_b.astype(f32).reshape(hshape)
    y = y + jnp.sum(r * k * r_k.astype(f32).reshape(hshape), axis=-1, keepdims=True) * v
    out = y.reshape(B, T, R_WIDTH).astype(z.dtype) * jax.nn.silu(gate)
    return out, s_fin, z[:, -1]


def compress(kv, w_pos, w_mix):
    B, L = kv.shape[:2]
    n_sub = CMP_BLOCK // CMP_STRIDE
    n_chunk = L // CMP_STRIDE
    n_cmp = n_chunk - n_sub + 1
    c = kv[:, :n_chunk * CMP_STRIDE].reshape(B, n_chunk, CMP_STRIDE, 2, N_KV_HEADS, N_HEAD_DIM)
    wp = w_pos.reshape(2, n_sub, CMP_STRIDE, N_HEAD_DIM)
    pooled = jnp.einsum('bjpekd,epd->bjekd', c[:, 0:n_cmp], wp[:, 0])
    for m in range(1, n_sub):
        pooled = pooled + jnp.einsum('bjpekd,epd->bjekd', c[:, m:m + n_cmp], wp[:, m])
    return jnp.einsum('bjekd,edf->bjekf', pooled, w_mix)


def compressed_attn(q, kc, qpos):
    n_cmp = kc.shape[1]
    kend = jnp.arange(n_cmp) * CMP_STRIDE + (CMP_BLOCK - 1)
    mask = kend[None, :] <= qpos[:, None]
    s = jnp.einsum('bkgtd,bjkd->bkgtj', q, kc[:, :, 0]) * ATTN_SCALE
    p = masked_softmax(s, mask)
    o = jnp.einsum('bkgtj,bjkd->bkgtd', p.astype(q.dtype), kc[:, :, 1])
    return o, jnp.sum(p, axis=2)


def select_blocks(imp, qpos, total_len):
    n_cmp = imp.shape[-1]
    n_sel = -(-total_len // SEL_BLOCK)
    cs = jnp.arange(n_cmp) * CMP_STRIDE
    ss = jnp.arange(n_sel) * SEL_BLOCK
    overlap = ((cs[:, None] < ss[None, :] + SEL_BLOCK) & (cs[:, None] + CMP_BLOCK > ss[None, :])).astype(jnp.float32)
    score = jnp.einsum('bktj,js->bkts', imp, overlap)
    cur = (qpos // SEL_BLOCK)[:, None]
    sid = jnp.arange(n_sel)[None, :]
    valid = sid <= cur
    forced = (sid == 0) | (sid == cur) | (sid == cur - 1)
    score = jnp.where(valid, score + jnp.where(forced, SEL_BIAS, 0.0), -SEL_BIAS)
    vals, idx = lax.top_k(score, min(SEL_TOPK, n_sel))
    return idx, vals > -0.5 * SEL_BIAS


def selected_attn(q, ks, vs, idx, valid, qpos):
    B, KV, T, K = idx.shape
    kpos = idx[..., None] * SEL_BLOCK + jnp.arange(SEL_BLOCK)
    mask = (valid[..., None] & (kpos <= qpos[:, None, None])).reshape(B, KV, 1, T, K * SEL_BLOCK)
    ks = ks.reshape(B, KV, T, K * SEL_BLOCK, N_HEAD_DIM)
    vs = vs.reshape(B, KV, T, K * SEL_BLOCK, N_HEAD_DIM)
    s = jnp.einsum('bkgtd,bktnd->bkgtn', q, ks) * ATTN_SCALE
    p = masked_softmax(s, mask)
    return jnp.einsum('bkgtn,bktnd->bkgtd', p.astype(q.dtype), vs)


def selected_prompt_attn(q, kv_sel, idx, valid, qpos):
    B, T = kv_sel.shape[:2]
    n_sel = T // SEL_BLOCK
    blocks = kv_sel.reshape(B, n_sel, SEL_BLOCK, 2, N_KV_HEADS, N_HEAD_DIM).transpose(0, 4, 1, 2, 3, 5)
    bk, bv = blocks[..., 0, :], blocks[..., 1, :]
    bi = jnp.arange(B)[:, None, None, None]
    gi = jnp.arange(N_KV_HEADS)[None, :, None, None]
    nq = T // SEL_QUERY_BLOCK

    def split(t, ax):
        return jnp.moveaxis(t.reshape(t.shape[:ax] + (nq, SEL_QUERY_BLOCK) + t.shape[ax + 1:]), ax, 0)

    def chunk(args):
        qc, ic, vc, pc = args
        return selected_attn(qc, bk[bi, gi, ic], bv[bi, gi, ic], ic, vc, pc)

    out = lax.map(chunk, (split(q, 3), split(idx, 2), split(valid, 2), qpos.reshape(nq, SEL_QUERY_BLOCK)))
    return jnp.moveaxis(out, 0, 3).reshape(B, N_KV_HEADS, N_GROUP, T, N_HEAD_DIM)


def selected_sample_attn(q, kv_new, cache, page_table, idx, valid, qpos):
    B, T = kv_new.shape[:2]
    bpp = PAGE_SIZE // SEL_BLOCK
    n_past = page_table.shape[1] * bpp
    pool = cache.reshape(cache.shape[0] * bpp, SEL_BLOCK, cache.shape[2], N_KV_HEADS, N_HEAD_DIM)
    n_tail = -(-T // SEL_BLOCK)
    tail = jnp.pad(kv_new, ((0, 0), (0, n_tail * SEL_BLOCK - T), (0, 0), (0, 0), (0, 0)))
    tail = tail.reshape(B, n_tail, SEL_BLOCK, 2, N_KV_HEADS, N_HEAD_DIM).transpose(0, 4, 1, 2, 3, 5)
    bi = jnp.arange(B)[:, None, None, None]
    gi = jnp.arange(N_KV_HEADS)[None, :, None, None]
    ip = jnp.minimum(idx, n_past - 1)
    phys = page_table[bi, ip // bpp] * bpp + ip % bpp
    past_rows = pool[phys, :, 2:4, gi]
    tail_rows = tail[bi, gi, jnp.clip(idx - n_past, 0, n_tail - 1)]
    rows = jnp.where((idx < n_past)[..., None, None, None], past_rows.astype(tail_rows.dtype), tail_rows)
    return selected_attn(q, rows[..., 0, :], rows[..., 1, :], idx, valid, qpos)


def window_attn(q, kw, qpos, kpos):
    mask = (kpos[None, :] <= qpos[:, None]) & (kpos[None, :] > qpos[:, None] - WINDOW) & (kpos[None, :] >= 0)
    s = jnp.einsum('bkgtd,bskd->bkgts', q, kw[:, :, 0]) * ATTN_SCALE
    p = masked_softmax(s, mask)
    return jnp.einsum('bkgts,bskd->bkgtd', p.astype(q.dtype), kw[:, :, 1])


def window_prompt_attn(q, kv_win):
    B, T = kv_win.shape[:2]
    padded = jnp.pad(kv_win, ((0, 0), (WINDOW, 0), (0, 0), (0, 0), (0, 0)))
    nq = T // WIN_QUERY_BLOCK

    def chunk(i):
        start = i * WIN_QUERY_BLOCK
        kw = lax.dynamic_slice_in_dim(padded, start, WINDOW + WIN_QUERY_BLOCK, axis=1)
        qc = lax.dynamic_slice_in_dim(q, start, WIN_QUERY_BLOCK, axis=3)
        qpos = start + jnp.arange(WIN_QUERY_BLOCK)
        kpos = start - WINDOW + jnp.arange(WINDOW + WIN_QUERY_BLOCK)
        return window_attn(qc, kw, qpos, kpos)

    out = lax.map(chunk, jnp.arange(nq))
    return jnp.moveaxis(out, 0, 3).reshape(B, N_KV_HEADS, N_GROUP, T, N_HEAD_DIM)


def mixer_out(y_r, o_c, o_s, o_w, gl, gate_n, w_out):
    B, T, _ = y_r.shape
    g = jax.nn.sigmoid(gl.astype(jnp.float32)).reshape(B, T, N_BRANCH, N_KV_HEADS, N_GROUP)
    g = g.transpose(2, 0, 3, 4, 1)[..., None]
    o = g[0] * o_c + g[1] * o_s + g[2] * o_w
    o = o.transpose(0, 3, 1, 2, 4).reshape(B, T, N_WIDTH).astype(y_r.dtype) * jax.nn.silu(gate_n)
    return jnp.concatenate([y_r, o], axis=-1) @ w_out


def prompt_layer(x, norm_g, w_in, mu_shift, w0, w_decay_up, a0, w_aaa_up, k_k, k_a, r_k, gn_w, gn_b,
                 w_cmp_pos, w_cmp_mix, w_out):
    B, T, _ = x.shape
    zr, gate_r, q, gate_n, kv, gl = split_proj(rms_norm(x, norm_g) @ w_in)
    y_r, wkv, shift = rwkv_mix(zr, jnp.zeros((B, R_SHIFT_WIDTH), zr.dtype),
                               jnp.zeros((B, R_HEADS, R_HEAD_DIM, R_HEAD_DIM), jnp.float32), gate_r,
                               mu_shift, w0, w_decay_up, a0, w_aaa_up, k_k, k_a, r_k, gn_w, gn_b)
    o_n = nsa_prompt(q, kv, compress_prompt(kv, w_cmp_pos, w_cmp_mix), gl, gate_n)
    y = x + jnp.concatenate([y_r, o_n], axis=-1) @ w_out
    kv = kv.reshape(B, T, KV_SLOTS, N_KV_HEADS, N_HEAD_DIM)
    return y, kv[:, :, 0:4], kv[:, T - min(WINDOW, T):, 4:6], wkv, shift


def sample_layer(x, cache, win_buf, wkv0, shift0, page_table, norm_g, w_in, mu_shift, w0, w_decay_up, a0,
                 w_aaa_up, k_k, k_a, r_k, gn_w, gn_b, w_cmp_pos, w_cmp_mix, w_out):
    B, T, _ = x.shape
    past = page_table.shape[1] * PAGE_SIZE
    zr, gate_r, q, gate_n, kv, gl = split_proj(rms_norm(x, norm_g) @ w_in)
    y_r, wkv, shift = rwkv_mix(zr, shift0, wkv0, gate_r,
                               mu_shift, w0, w_decay_up, a0, w_aaa_up, k_k, k_a, r_k, gn_w, gn_b)
    kv = kv.reshape(B, T, KV_SLOTS, N_KV_HEADS, N_HEAD_DIM)
    qh = to_heads(q)
    qpos = past + jnp.arange(T)
    past_cmp = cache[page_table, :, 0:2].reshape(B, past, 2, N_KV_HEADS, N_HEAD_DIM)
    full_cmp = jnp.concatenate([past_cmp.astype(kv.dtype), kv[:, :, 0:2]], axis=1)
    o_c, imp = compressed_attn(qh, compress(full_cmp, w_cmp_pos, w_cmp_mix), qpos)
    idx, valid = select_blocks(imp, qpos, past + T)
    o_s = selected_sample_attn(qh, kv[:, :, 2:4], cache, page_table, idx, valid, qpos)
    nb = win_buf.shape[1]
    keys_w = jnp.concatenate([win_buf.astype(kv.dtype), kv[:, :, 4:6]], axis=1)
    o_w = window_attn(qh, keys_w, qpos, past - nb + jnp.arange(nb + T))
    y = x + mixer_out(y_r, o_c, o_s, o_w, gl, gate_n, w_out)
    n_keep = min(WINDOW, nb + T)
    return y, kv[:, :, 0:4], keys_w[:, nb + T - n_keep:], wkv, shift


LANES = 128
NSA_TQ = 128
NSA_KC = 512
VMEM_LIMIT = 48 * 1024 * 1024


def _split3(x):
    hi = x.astype(jnp.bfloat16)
    r1 = x - hi.astype(jnp.float32)
    mid = r1.astype(jnp.bfloat16)
    lo = (r1 - mid.astype(jnp.float32)).astype(jnp.bfloat16)
    return hi, mid, lo


def _dot(a, b):
    return jnp.dot(a, b, preferred_element_type=jnp.float32)


def _dot_nt(a, b):
    return lax.dot_general(a, b, (((1,), (1,)), ((), ())), preferred_element_type=jnp.float32)


def _dot_exact_rhs(a, b01):
    hi, mid, lo = _split3(a)
    return _dot(hi, b01) + _dot(mid, b01) + _dot(lo, b01)


def _dot_x3(a, b):
    ah = a.astype(jnp.bfloat16)
    al = (a - ah.astype(jnp.float32)).astype(jnp.bfloat16)
    bh = b.astype(jnp.bfloat16)
    bl = (b - bh.astype(jnp.float32)).astype(jnp.bfloat16)
    return _dot(ah, bh) + _dot(ah, bl) + _dot(al, bh)


def _dot_nt_x3(a, b):
    ah = a.astype(jnp.bfloat16)
    al = (a - ah.astype(jnp.float32)).astype(jnp.bfloat16)
    bh = b.astype(jnp.bfloat16)
    bl = (b - bh.astype(jnp.float32)).astype(jnp.bfloat16)
    return _dot_nt(ah, bh) + _dot_nt(ah, bl) + _dot_nt(al, bh)


def _compress_prompt_body(kv_ref, wa_ref, wb_ref, pool_ref, wmix_ref, o_ref):
    x = kv_ref[0]
    xw = jnp.concatenate([x * wa_ref[...], x * wb_ref[...]], axis=0)
    pooled = _dot_exact_rhs_lhs01(pool_ref[...], xw)
    o_ref[0] = _dot_x3(pooled, wmix_ref[...])


def _dot_exact_rhs_lhs01(a01, b):
    hi, mid, lo = _split3(b)
    return _dot(a01, hi) + _dot(a01, mid) + _dot(a01, lo)


def compress_prompt(kv, w_pos, w_mix):
    B, T, _ = kv.shape
    n_sub = CMP_BLOCK // CMP_STRIDE
    assert n_sub == 2 and T % CMP_STRIDE == 0
    ncp = T // CMP_STRIDE
    cw = 2 * N_KV_HEADS * N_HEAD_DIM
    wp = jnp.broadcast_to(w_pos.reshape(2, 1, CMP_BLOCK, N_HEAD_DIM).transpose(2, 0, 1, 3),
                          (CMP_BLOCK, 2, N_KV_HEADS, N_HEAD_DIM)).reshape(CMP_BLOCK, cw)
    wa = jnp.tile(wp[:CMP_STRIDE], (ncp, 1))
    wb = jnp.tile(wp[CMP_STRIDE:], (ncp, 1))
    chunk = np.arange(T) // CMP_STRIDE
    pool = np.concatenate([chunk[None, :] == np.arange(ncp)[:, None],
                           chunk[None, :] == np.arange(ncp)[:, None] + 1], axis=1)
    pool = jnp.asarray(pool, jnp.bfloat16)
    wmix = jnp.zeros((cw, cw), jnp.float32)
    for e in range(2):
        for kh in range(N_KV_HEADS):
            o = (e * N_KV_HEADS + kh) * N_HEAD_DIM
            wmix = wmix.at[o:o + N_HEAD_DIM, o:o + N_HEAD_DIM].set(w_mix[e])
    return pl.pallas_call(
        _compress_prompt_body,
        grid=(B,),
        in_specs=[pl.BlockSpec((1, T, cw), lambda b: (b, 0, 0)),
                  pl.BlockSpec((T, cw), lambda b: (0, 0)),
                  pl.BlockSpec((T, cw), lambda b: (0, 0)),
                  pl.BlockSpec((ncp, 2 * T), lambda b: (0, 0)),
                  pl.BlockSpec((cw, cw), lambda b: (0, 0))],
        out_specs=pl.BlockSpec((1, ncp, cw), lambda b: (b, 0, 0)),
        out_shape=jax.ShapeDtypeStruct((B, ncp, cw), jnp.float32),
        compiler_params=pltpu.CompilerParams(dimension_semantics=("parallel",), vmem_limit_bytes=VMEM_LIMIT),
        name="compress_prompt",
    )(kv, wa, wb, pool, wmix)


def _masked_softmax_rows(s, mask):
    s = jnp.where(mask, s, MASK_NEG)
    e = jnp.exp(s - jnp.max(s, axis=-1, keepdims=True))
    p = e / jnp.sum(e, axis=-1, keepdims=True)
    return jnp.where(mask, p, 0.0)


def _nsa_prompt_body(q_ref, ksel_ref, kwin_ref, kc_ref, gl_ref, gn_ref, ovt_ref, ex_ref, o_ref):
    f32, bf16 = jnp.float32, jnp.bfloat16
    TQ, KC, G = NSA_TQ, NSA_KC, N_GROUP
    T = ksel_ref.shape[1]
    ncp = kc_ref.shape[1]
    n_sel = ovt_ref.shape[0]
    topk = min(SEL_TOPK, n_sel)
    qs = pl.program_id(1) * TQ
    qpos = qs + lax.broadcasted_iota(jnp.int32, (TQ, 1), 0)
    lo_half = lax.broadcasted_iota(jnp.int32, (TQ, LANES), 1) < N_HEAD_DIM

    def in_half(x, half):
        return jnp.where(lo_half if half == 0 else jnp.logical_not(lo_half), x, 0.0)

    def q_padded(h, half):
        tile = q_ref[0, :, (h // 2) * LANES:(h // 2 + 1) * LANES] * ATTN_SCALE
        if h % 2 != half:
            tile = pltpu.roll(tile, N_HEAD_DIM, axis=1)
        return in_half(tile, half)

    wstart = pl.multiple_of(jnp.maximum(qs - WINDOW, 0), TQ)
    kw = kwin_ref[0, pl.ds(wstart, WINDOW + TQ), 0:LANES].astype(bf16)
    vw = kwin_ref[0, pl.ds(wstart, WINDOW + TQ), LANES:2 * LANES].astype(bf16)
    kposw = wstart + lax.broadcasted_iota(jnp.int32, (1, WINDOW + TQ), 1)
    mask_w = (kposw <= qpos) & (kposw > qpos - WINDOW)

    kck = kc_ref[0, :, 0:LANES]
    kcv = kc_ref[0, :, LANES:2 * LANES].astype(bf16)
    kend = lax.broadcasted_iota(jnp.int32, (1, ncp), 1) * CMP_STRIDE + (CMP_BLOCK - 1)
    mask_c = kend <= qpos

    sid_t = lax.broadcasted_iota(jnp.int32, (n_sel, 1), 0)
    cur_t = (qs + lax.broadcasted_iota(jnp.int32, (1, TQ), 1)) // SEL_BLOCK
    valid_t = sid_t <= cur_t
    forced_t = (sid_t == 0) | (sid_t == cur_t) | (sid_t == cur_t - 1)
    n_chunks = qs // KC + 1

    gates = jax.nn.sigmoid(gl_ref[0])
    heads = [None] * N_HEADS
    for kvh in range(N_KV_HEADS):
        qg = jnp.concatenate([q_padded(kvh * G + g, kvh) for g in range(G)], axis=0)
        qb = qg.astype(bf16)

        s = _dot_nt_x3(qg, kck).reshape(G, TQ, ncp)
        p = _masked_softmax_rows(s, mask_c[None])
        o_c = _dot(p.reshape(G * TQ, ncp).astype(bf16), kcv)
        imp = jnp.sum(p, axis=0)

        hi, mid, lo = _split3(imp)
        ovt = ovt_ref[...]
        score = _dot_nt(ovt, hi) + _dot_nt(ovt, mid) + _dot_nt(ovt, lo)
        score = jnp.where(valid_t, score + jnp.where(forced_t, SEL_BIAS, 0.0), -SEL_BIAS)
        rank = jnp.zeros((n_sel, TQ), jnp.int32)
        for s2 in range(n_sel):
            row = score[s2:s2 + 1, :]
            ahead = (row > score) | ((row == score) & (s2 < sid_t))
            rank = rank + ahead.astype(jnp.int32)
        sel = jnp.where((rank < topk) & valid_t, 1.0, 0.0).T.astype(bf16)

        def chunk(c, carry, qb=qb, sel=sel):
            m, l, acc = carry
            k0 = pl.multiple_of(c * KC, KC)
            ks = ksel_ref[0, pl.ds(k0, KC), 0:LANES].astype(bf16)
            vs = ksel_ref[0, pl.ds(k0, KC), LANES:2 * LANES].astype(bf16)
            selx = _dot(sel, ex_ref[:, pl.ds(k0, KC)])
            kpos = k0 + lax.broadcasted_iota(jnp.int32, (1, KC), 1)
            keep = (selx > 0.5) & (kpos <= qpos)
            sc = jnp.where(keep[None], _dot_nt(qb, ks).reshape(G, TQ, KC), MASK_NEG)
            m_new = jnp.maximum(m, jnp.max(sc, axis=-1, keepdims=True))
            alpha = jnp.exp(m - m_new)
            e = jnp.exp(sc - m_new)
            l = alpha * l + jnp.sum(e, axis=-1, keepdims=True)
            pv = _dot(e.reshape(G * TQ, KC).astype(bf16), vs).reshape(G, TQ, LANES)
            return m_new, l, alpha * acc + pv

        m0 = jnp.full((G, TQ, 1), MASK_NEG, f32)
        l0 = jnp.zeros((G, TQ, 1), f32)
        a0 = jnp.zeros((G, TQ, LANES), f32)
        _, l, acc = lax.fori_loop(0, n_chunks, chunk, (m0, l0, a0))
        o_s = acc / l

        pw = _masked_softmax_rows(_dot_nt(qb, kw).reshape(G, TQ, WINDOW + TQ), mask_w[None])
        o_w = _dot(pw.reshape(G * TQ, WINDOW + TQ).astype(bf16), vw)

        o_c = o_c.reshape(G, TQ, LANES)
        o_w = o_w.reshape(G, TQ, LANES)
        for g in range(G):
            h = kvh * G + g
            o = (gates[:, h:h + 1] * o_c[g] + gates[:, N_HEADS + h:N_HEADS + h + 1] * o_s[g]
                 + gates[:, 2 * N_HEADS + h:2 * N_HEADS + h + 1] * o_w[g])
            if h % 2 != kvh:
                o = pltpu.roll(o, N_HEAD_DIM, axis=1)
            heads[h] = o

    for m in range(N_HEADS // 2):
        tile = jnp.where(lo_half, heads[2 * m], heads[2 * m + 1])
        gn = gn_ref[0, :, m * LANES:(m + 1) * LANES]
        o_ref[0, :, m * LANES:(m + 1) * LANES] = tile * (gn * jax.nn.sigmoid(gn))


def nsa_prompt(q, kv, kc, gl, gate_n):
    B, T, _ = q.shape
    TQ, KC = NSA_TQ, NSA_KC
    assert N_KV_HEADS == 2 and N_HEAD_DIM * 2 == LANES and T % KC == 0 and KC % TQ == 0 and T >= WINDOW + TQ
    assert SEL_BLOCK % CMP_STRIDE == 0 and WINDOW % TQ == 0
    ncp = T // CMP_STRIDE
    n_sel = T // SEL_BLOCK
    cs = np.arange(ncp) * CMP_STRIDE
    ss = np.arange(n_sel) * SEL_BLOCK
    overlap = (cs[:, None] < ss[None, :] + SEL_BLOCK) & (cs[:, None] + CMP_BLOCK > ss[None, :])
    overlap[ncp - 1] = False
    overlap_t = jnp.asarray(overlap.T, jnp.bfloat16)
    expand = jnp.asarray(np.arange(T)[None, :] // SEL_BLOCK == np.arange(n_sel)[:, None], jnp.bfloat16)
    kvw = 2 * N_KV_HEADS * N_HEAD_DIM
    return pl.pallas_call(
        _nsa_prompt_body,
        grid=(B, T // TQ),
        in_specs=[pl.BlockSpec((1, TQ, N_WIDTH), lambda b, i: (b, i, 0)),
                  pl.BlockSpec((1, T, kvw), lambda b, i: (b, 0, 1)),
                  pl.BlockSpec((1, T, kvw), lambda b, i: (b, 0, 2)),
                  pl.BlockSpec((1, ncp, kvw), lambda b, i: (b, 0, 0)),
                  pl.BlockSpec((1, TQ, N_BRANCH * N_HEADS), lambda b, i: (b, i, 0)),
                  pl.BlockSpec((1, TQ, N_WIDTH), lambda b, i: (b, i, 0)),
                  pl.BlockSpec((n_sel, ncp), lambda b, i: (0, 0)),
                  pl.BlockSpec((n_sel, T), lambda b, i: (0, 0))],
        out_specs=pl.BlockSpec((1, TQ, N_WIDTH), lambda b, i: (b, i, 0)),
        out_shape=jax.ShapeDtypeStruct((B, T, N_WIDTH), jnp.float32),
        compiler_params=pltpu.CompilerParams(dimension_semantics=("parallel", "arbitrary"),
                                             vmem_limit_bytes=VMEM_LIMIT),
        name="nsa_prompt",
    )(q, kv, kv, kc, gl, gate_n, overlap_t, expand)


def _final_norm_body(x_ref, g_ref, o_ref):
    x = x_ref[...]
    o_ref[...] = x * lax.rsqrt(jnp.mean(x * x, axis=-1, keepdims=True) + RMS_EPS) * g_ref[...]


def final_norm(x, g):
    B, T, D = x.shape
    x2 = x.reshape(B * T, D)
    tm = min(512, B * T)
    out = pl.pallas_call(
        _final_norm_body,
        grid=(B * T // tm,),
        in_specs=[pl.BlockSpec((tm, D), lambda i: (i, 0)), pl.BlockSpec((1, D), lambda i: (0, 0))],
        out_specs=pl.BlockSpec((tm, D), lambda i: (i, 0)),
        out_shape=jax.ShapeDtypeStruct((B * T, D), x.dtype),
        name="final_norm",
    )(x2, g.reshape(1, D))
    return out.reshape(B, T, D)


def kernel(x_prompt, x_sample, cache_kv, cache_kv_win, state_wkv, state_shift, page_table,
           norm_g, w_in, mu_shift, w0, w_decay_up, a0, w_aaa_up, k_k, k_a, r_k, gn_w, gn_b,
           w_cmp_pos, w_cmp_mix, w_out, final_g):
    y_p, y_s = x_prompt, x_sample
    kvp, kvs, wnp, wns, skp, sks, shp, shs = [], [], [], [], [], [], [], []
    for layer in range(DEPTH):
        lw = (norm_g[layer], w_in[layer], mu_shift[layer], w0[layer], w_decay_up[layer], a0[layer],
              w_aaa_up[layer], k_k[layer], k_a[layer], r_k[layer], gn_w[layer], gn_b[layer],
              w_cmp_pos[layer], w_cmp_mix[layer], w_out[layer])
        y_p, p_kv, p_win, p_wkv, p_shift = prompt_layer(y_p, *lw)
        y_s, s_kv, s_win, s_wkv, s_shift = sample_layer(y_s, cache_kv[layer], cache_kv_win[layer],
                                                        state_wkv[layer], state_shift[layer], page_table, *lw)
        kvp.append(p_kv); kvs.append(s_kv); wnp.append(p_win); wns.append(s_win)
        skp.append(p_wkv); sks.append(s_wkv); shp.append(p_shift); shs.append(s_shift)
    y_prompt = final_norm(y_p, final_g)
    y_sample = final_norm(y_s, final_g)
    return (y_prompt, y_sample, jnp.stack(kvp), jnp.stack(kvs), jnp.stack(wnp), jnp.stack(wns),
            jnp.stack(skp), jnp.stack(sks), jnp.stack(shp), jnp.stack(shs))
```

```python
import jax, jax.numpy as jnp
from jax import lax
import numpy as np
from jax.experimental import pallas as pl
from jax.experimental.pallas import tpu as pltpu

D_MODEL = 1024
BATCH = 16
SEQ = 2048
DEPTH = 1
DEC_BATCH = 32
DEC_SEQ = 4
PAST_LEN = 16384
PAGE_SIZE = 128

R_HEAD_DIM = 64
R_WIDTH = D_MODEL // 2
R_HEADS = R_WIDTH // R_HEAD_DIM
DECAY_LORA = 64
AAA_LORA = 64
R_SHIFT_WIDTH = 3 * R_WIDTH + DECAY_LORA + AAA_LORA
N_HEAD_DIM = 64
N_WIDTH = D_MODEL - R_WIDTH
N_HEADS = N_WIDTH // N_HEAD_DIM
N_KV_HEADS = 2
N_GROUP = N_HEADS // N_KV_HEADS
N_BRANCH = 3
KV_SLOTS = 2 * N_BRANCH
CMP_BLOCK = 32
CMP_STRIDE = 16
SEL_BLOCK = 64
SEL_TOPK = 16
WINDOW = 512
SEL_QUERY_BLOCK = 32
WIN_QUERY_BLOCK = 128
MIX_WIDTH = R_WIDTH + N_WIDTH
PROJ_SPLITS = (R_SHIFT_WIDTH, R_WIDTH, N_WIDTH, N_WIDTH, KV_SLOTS * N_KV_HEADS * N_HEAD_DIM, N_BRANCH * N_HEADS)
PROJ_WIDTH = sum(PROJ_SPLITS)
RMS_EPS = 1e-6
GN_EPS = 64e-5
MASK_NEG = -1e30
SEL_BIAS = 1e4
ATTN_SCALE = N_HEAD_DIM ** -0.5


def rms_norm(x, g):
    xf = x.astype(jnp.float32)
    y = xf * lax.rsqrt(jnp.mean(xf * xf, axis=-1, keepdims=True) + RMS_EPS)
    return (y * g.astype(jnp.float32)).astype(x.dtype)


def masked_softmax(s, mask):
    s = jnp.where(mask, s.astype(jnp.float32), MASK_NEG)
    p = jax.nn.softmax(s, axis=-1)
    return jnp.where(mask, p, 0.0)


def split_proj(p):
    offs = np.cumsum((0,) + PROJ_SPLITS)
    return tuple(p[..., int(offs[i]):int(offs[i + 1])] for i in range(len(PROJ_SPLITS)))


def to_heads(q):
    B, T, _ = q.shape
    return q.reshape(B, T, N_KV_HEADS, N_GROUP, N_HEAD_DIM).transpose(0, 2, 3, 1, 4)


def rwkv_mix(z, z_prev, s0, gate, mu_shift, w0, w_decay_up, a0, w_aaa_up, k_k, k_a, r_k, gn_w, gn_b):
    B, T, _ = z.shape
    f32 = jnp.float32
    zp = jnp.concatenate([z_prev[:, None].astype(z.dtype), z[:, :-1]], axis=1)
    zs = z + (zp - z) * mu_shift
    r, k, v, wd, ad = jnp.split(zs, [R_WIDTH, 2 * R_WIDTH, 3 * R_WIDTH, 3 * R_WIDTH + DECAY_LORA], axis=-1)
    w_log = -jax.nn.softplus(-(w0 + jnp.tanh(wd) @ w_decay_up).astype(f32)) - 0.5
    decay = jnp.exp(-jnp.exp(w_log))
    a = jax.nn.sigmoid((a0 + ad @ w_aaa_up).astype(f32))
    hs = (B, T, R_HEADS, R_HEAD_DIM)
    r, k, v, decay, a = (t.astype(f32).reshape(hs) for t in (r, k, v, decay, a))
    hshape = (R_HEADS, R_HEAD_DIM)
    kk = k * k_k.astype(f32).reshape(hshape)
    kk = kk * lax.rsqrt(jnp.maximum(jnp.sum(kk * kk, axis=-1, keepdims=True), 1e-24))
    k = k * (1.0 + (a - 1.0) * k_a.astype(f32).reshape(hshape))

    y, s_fin = rwkv_scan(r, decay, k, v, kk, kk * a, s0.astype(f32))
    yc = y - jnp.mean(y, axis=-1, keepdims=True)
    y = yc * lax.rsqrt(jnp.mean(yc * yc, axis=-1, keepdims=True) + GN_EPS)
    y = y * gn_w.astype(f32).reshape(hshape) + gn_b.astype(f32).reshape(hshape)
    y = y + jnp.sum(r * k * r_k.astype(f32).reshape(hshape), axis=-1, keepdims=True) * v
    out = y.reshape(B, T, R_WIDTH).astype(z.dtype) * jax.nn.silu(gate)
    return out, s_fin, z[:, -1]


def compress(kv, w_pos, w_mix):
    B, L = kv.shape[:2]
    n_sub = CMP_BLOCK // CMP_STRIDE
    n_chunk = L // CMP_STRIDE
    n_cmp = n_chunk - n_sub + 1
    c = kv[:, :n_chunk * CMP_STRIDE].reshape(B, n_chunk, CMP_STRIDE, 2, N_KV_HEADS, N_HEAD_DIM)
    wp = w_pos.reshape(2, n_sub, CMP_STRIDE, N_HEAD_DIM)
    pooled = jnp.einsum('bjpekd,epd->bjekd', c[:, 0:n_cmp], wp[:, 0])
    for m in range(1, n_sub):
        pooled = pooled + jnp.einsum('bjpekd,epd->bjekd', c[:, m:m + n_cmp], wp[:, m])
    return jnp.einsum('bjekd,edf->bjekf', pooled, w_mix)


def compressed_attn(q, kc, qpos):
    n_cmp = kc.shape[1]
    kend = jnp.arange(n_cmp) * CMP_STRIDE + (CMP_BLOCK - 1)
    mask = kend[None, :] <= qpos[:, None]
    s = jnp.einsum('bkgtd,bjkd->bkgtj', q, kc[:, :, 0]) * ATTN_SCALE
    p = masked_softmax(s, mask)
    o = jnp.einsum('bkgtj,bjkd->bkgtd', p.astype(q.dtype), kc[:, :, 1])
    return o, jnp.sum(p, axis=2)


def select_blocks(imp, qpos, total_len):
    n_cmp = imp.shape[-1]
    n_sel = -(-total_len // SEL_BLOCK)
    cs = jnp.arange(n_cmp) * CMP_STRIDE
    ss = jnp.arange(n_sel) * SEL_BLOCK
    overlap = ((cs[:, None] < ss[None, :] + SEL_BLOCK) & (cs[:, None] + CMP_BLOCK > ss[None, :])).astype(jnp.float32)
    score = jnp.einsum('bktj,js->bkts', imp, overlap)
    cur = (qpos // SEL_BLOCK)[:, None]
    sid = jnp.arange(n_sel)[None, :]
    valid = sid <= cur
    forced = (sid == 0) | (sid == cur) | (sid == cur - 1)
    score = jnp.where(valid, score + jnp.where(forced, SEL_BIAS, 0.0), -SEL_BIAS)
    vals, idx = lax.top_k(score, min(SEL_TOPK, n_sel))
    return idx, vals > -0.5 * SEL_BIAS


def selected_attn(q, ks, vs, idx, valid, qpos):
    B, KV, T, K = idx.shape
    kpos = idx[..., None] * SEL_BLOCK + jnp.arange(SEL_BLOCK)
    mask = (valid[..., None] & (kpos <= qpos[:, None, None])).reshape(B, KV, 1, T, K * SEL_BLOCK)
    ks = ks.reshape(B, KV, T, K * SEL_BLOCK, N_HEAD_DIM)
    vs = vs.reshape(B, KV, T, K * SEL_BLOCK, N_HEAD_DIM)
    s = jnp.einsum('bkgtd,bktnd->bkgtn', q, ks) * ATTN_SCALE
    p = masked_softmax(s, mask)
    return jnp.einsum('bkgtn,bktnd->bkgtd', p.astype(q.dtype), vs)


def selected_prompt_attn(q, kv_sel, idx, valid, qpos):
    B, T = kv_sel.shape[:2]
    n_sel = T // SEL_BLOCK
    blocks = kv_sel.reshape(B, n_sel, SEL_BLOCK, 2, N_KV_HEADS, N_HEAD_DIM).transpose(0, 4, 1, 2, 3, 5)
    bk, bv = blocks[..., 0, :], blocks[..., 1, :]
    bi = jnp.arange(B)[:, None, None, None]
    gi = jnp.arange(N_KV_HEADS)[None, :, None, None]
    nq = T // SEL_QUERY_BLOCK

    def split(t, ax):
        return jnp.moveaxis(t.reshape(t.shape[:ax] + (nq, SEL_QUERY_BLOCK) + t.shape[ax + 1:]), ax, 0)

    def chunk(args):
        qc, ic, vc, pc = args
        return selected_attn(qc, bk[bi, gi, ic], bv[bi, gi, ic], ic, vc, pc)

    out = lax.map(chunk, (split(q, 3), split(idx, 2), split(valid, 2), qpos.reshape(nq, SEL_QUERY_BLOCK)))
    return jnp.moveaxis(out, 0, 3).reshape(B, N_KV_HEADS, N_GROUP, T, N_HEAD_DIM)


def selected_sample_attn(q, kv_new, cache, page_table, idx, valid, qpos):
    B, T = kv_new.shape[:2]
    bpp = PAGE_SIZE // SEL_BLOCK
    n_past = page_table.shape[1] * bpp
    pool = cache.reshape(cache.shape[0] * bpp, SEL_BLOCK, cache.shape[2], N_KV_HEADS, N_HEAD_DIM)
    n_tail = -(-T // SEL_BLOCK)
    tail = jnp.pad(kv_new, ((0, 0), (0, n_tail * SEL_BLOCK - T), (0, 0), (0, 0), (0, 0)))
    tail = tail.reshape(B, n_tail, SEL_BLOCK, 2, N_KV_HEADS, N_HEAD_DIM).transpose(0, 4, 1, 2, 3, 5)
    bi = jnp.arange(B)[:, None, None, None]
    gi = jnp.arange(N_KV_HEADS)[None, :, None, None]
    ip = jnp.minimum(idx, n_past - 1)
    phys = page_table[bi, ip // bpp] * bpp + ip % bpp
    past_rows = pool[phys, :, 2:4, gi]
    tail_rows = tail[bi, gi, jnp.clip(idx - n_past, 0, n_tail - 1)]
    rows = jnp.where((idx < n_past)[..., None, None, None], past_rows.astype(tail_rows.dtype), tail_rows)
    return selected_attn(q, rows[..., 0, :], rows[..., 1, :], idx, valid, qpos)


def window_attn(q, kw, qpos, kpos):
    mask = (kpos[None, :] <= qpos[:, None]) & (kpos[None, :] > qpos[:, None] - WINDOW) & (kpos[None, :] >= 0)
    s = jnp.einsum('bkgtd,bskd->bkgts', q, kw[:, :, 0]) * ATTN_SCALE
    p = masked_softmax(s, mask)
    return jnp.einsum('bkgts,bskd->bkgtd', p.astype(q.dtype), kw[:, :, 1])


def window_prompt_attn(q, kv_win):
    B, T = kv_win.shape[:2]
    padded = jnp.pad(kv_win, ((0, 0), (WINDOW, 0), (0, 0), (0, 0), (0, 0)))
    nq = T // WIN_QUERY_BLOCK

    def chunk(i):
        start = i * WIN_QUERY_BLOCK
        kw = lax.dynamic_slice_in_dim(padded, start, WINDOW + WIN_QUERY_BLOCK, axis=1)
        qc = lax.dynamic_slice_in_dim(q, start, WIN_QUERY_BLOCK, axis=3)
        qpos = start + jnp.arange(WIN_QUERY_BLOCK)
        kpos = start - WINDOW + jnp.arange(WINDOW + WIN_QUERY_BLOCK)
        return window_attn(qc, kw, qpos, kpos)

    out = lax.map(chunk, jnp.arange(nq))
    return jnp.moveaxis(out, 0, 3).reshape(B, N_KV_HEADS, N_GROUP, T, N_HEAD_DIM)


def mixer_out(y_r, o_c, o_s, o_w, gl, gate_n, w_out):
    B, T, _ = y_r.shape
    g = jax.nn.sigmoid(gl.astype(jnp.float32)).reshape(B, T, N_BRANCH, N_KV_HEADS, N_GROUP)
    g = g.transpose(2, 0, 3, 4, 1)[..., None]
    o = g[0] * o_c + g[1] * o_s + g[2] * o_w
    o = o.transpose(0, 3, 1, 2, 4).reshape(B, T, N_WIDTH).astype(y_r.dtype) * jax.nn.silu(gate_n)
    return jnp.concatenate([y_r, o], axis=-1) @ w_out


def prompt_layer(x, norm_g, w_in, mu_shift, w0, w_decay_up, a0, w_aaa_up, k_k, k_a, r_k, gn_w, gn_b,
                 w_cmp_pos, w_cmp_mix, w_out):
    B, T, _ = x.shape
    zr, gate_r, q, gate_n, kv, gl = split_proj(rms_norm(x, norm_g) @ w_in)
    y_r, wkv, shift = rwkv_mix(zr, jnp.zeros((B, R_SHIFT_WIDTH), zr.dtype),
                               jnp.zeros((B, R_HEADS, R_HEAD_DIM, R_HEAD_DIM), jnp.float32), gate_r,
                               mu_shift, w0, w_decay_up, a0, w_aaa_up, k_k, k_a, r_k, gn_w, gn_b)
    o_n = nsa_prompt(q, kv, compress_prompt(kv, w_cmp_pos, w_cmp_mix), gl, gate_n)
    y = x + jnp.concatenate([y_r, o_n], axis=-1) @ w_out
    kv = kv.reshape(B, T, KV_SLOTS, N_KV_HEADS, N_HEAD_DIM)
    return y, kv[:, :, 0:4], kv[:, T - min(WINDOW, T):, 4:6], wkv, shift


def sample_layer(x, cache, win_buf, wkv0, shift0, page_table, norm_g, w_in, mu_shift, w0, w_decay_up, a0,
                 w_aaa_up, k_k, k_a, r_k, gn_w, gn_b, w_cmp_pos, w_cmp_mix, w_out):
    B, T, _ = x.shape
    past = page_table.shape[1] * PAGE_SIZE
    zr, gate_r, q, gate_n, kv, gl = split_proj(rms_norm(x, norm_g) @ w_in)
    y_r, wkv, shift = rwkv_mix(zr, shift0, wkv0, gate_r,
                               mu_shift, w0, w_decay_up, a0, w_aaa_up, k_k, k_a, r_k, gn_w, gn_b)
    kv = kv.reshape(B, T, KV_SLOTS, N_KV_HEADS, N_HEAD_DIM)
    qh = to_heads(q)
    qpos = past + jnp.arange(T)
    past_cmp = cache[page_table, :, 0:2].reshape(B, past, 2, N_KV_HEADS, N_HEAD_DIM)
    full_cmp = jnp.concatenate([past_cmp.astype(kv.dtype), kv[:, :, 0:2]], axis=1)
    o_c, imp = compressed_attn(qh, compress(full_cmp, w_cmp_pos, w_cmp_mix), qpos)
    idx, valid = select_blocks(imp, qpos, past + T)
    o_s = selected_sample_attn(qh, kv[:, :, 2:4], cache, page_table, idx, valid, qpos)
    nb = win_buf.shape[1]
    keys_w = jnp.concatenate([win_buf.astype(kv.dtype), kv[:, :, 4:6]], axis=1)
    o_w = window_attn(qh, keys_w, qpos, past - nb + jnp.arange(nb + T))
    y = x + mixer_out(y_r, o_c, o_s, o_w, gl, gate_n, w_out)
    n_keep = min(WINDOW, nb + T)
    return y, kv[:, :, 0:4], keys_w[:, nb + T - n_keep:], wkv, shift


LANES = 128
NSA_TQ = 128
NSA_KC = 512
VMEM_LIMIT = 48 * 1024 * 1024


def _split3(x):
    hi = x.astype(jnp.bfloat16)
    r1 = x - hi.astype(jnp.float32)
    mid = r1.astype(jnp.bfloat16)
    lo = (r1 - mid.astype(jnp.float32)).astype(jnp.bfloat16)
    return hi, mid, lo


def _dot(a, b):
    return jnp.dot(a, b, preferred_element_type=jnp.float32)


def _dot_nt(a, b):
    return lax.dot_general(a, b, (((1,), (1,)), ((), ())), preferred_element_type=jnp.float32)


def _dot_exact_rhs(a, b01):
    hi, mid, lo = _split3(a)
    return _dot(hi, b01) + _dot(mid, b01) + _dot(lo, b01)


def _dot_x3(a, b):
    ah = a.astype(jnp.bfloat16)
    al = (a - ah.astype(jnp.float32)).astype(jnp.bfloat16)
    bh = b.astype(jnp.bfloat16)
    bl = (b - bh.astype(jnp.float32)).astype(jnp.bfloat16)
    return _dot(ah, bh) + _dot(ah, bl) + _dot(al, bh)


def _dot_nt_x3(a, b):
    ah = a.astype(jnp.bfloat16)
    al = (a - ah.astype(jnp.float32)).astype(jnp.bfloat16)
    bh = b.astype(jnp.bfloat16)
    bl = (b - bh.astype(jnp.float32)).astype(jnp.bfloat16)
    return _dot_nt(ah, bh) + _dot_nt(ah, bl) + _dot_nt(al, bh)


def _compress_prompt_body(kv_ref, wa_ref, wb_ref, pool_ref, wmix_ref, o_ref):
    x = kv_ref[0]
    xw = jnp.concatenate([x * wa_ref[...], x * wb_ref[...]], axis=0)
    pooled = _dot_exact_rhs_lhs01(pool_ref[...], xw)
    o_ref[0] = _dot_x3(pooled, wmix_ref[...])


def _dot_exact_rhs_lhs01(a01, b):
    hi, mid, lo = _split3(b)
    return _dot(a01, hi) + _dot(a01, mid) + _dot(a01, lo)


def compress_prompt(kv, w_pos, w_mix):
    B, T, _ = kv.shape
    n_sub = CMP_BLOCK // CMP_STRIDE
    assert n_sub == 2 and T % CMP_STRIDE == 0
    ncp = T // CMP_STRIDE
    cw = 2 * N_KV_HEADS * N_HEAD_DIM
    wp = jnp.broadcast_to(w_pos.reshape(2, 1, CMP_BLOCK, N_HEAD_DIM).transpose(2, 0, 1, 3),
                          (CMP_BLOCK, 2, N_KV_HEADS, N_HEAD_DIM)).reshape(CMP_BLOCK, cw)
    wa = jnp.tile(wp[:CMP_STRIDE], (ncp, 1))
    wb = jnp.tile(wp[CMP_STRIDE:], (ncp, 1))
    chunk = np.arange(T) // CMP_STRIDE
    pool = np.concatenate([chunk[None, :] == np.arange(ncp)[:, None],
                           chunk[None, :] == np.arange(ncp)[:, None] + 1], axis=1)
    pool = jnp.asarray(pool, jnp.bfloat16)
    wmix = jnp.zeros((cw, cw), jnp.float32)
    for e in range(2):
        for kh in range(N_KV_HEADS):
            o = (e * N_KV_HEADS + kh) * N_HEAD_DIM
            wmix = wmix.at[o:o + N_HEAD_DIM, o:o + N_HEAD_DIM].set(w_mix[e])
    return pl.pallas_call(
        _compress_prompt_body,
        grid=(B,),
        in_specs=[pl.BlockSpec((1, T, cw), lambda b: (b, 0, 0)),
                  pl.BlockSpec((T, cw), lambda b: (0, 0)),
                  pl.BlockSpec((T, cw), lambda b: (0, 0)),
                  pl.BlockSpec((ncp, 2 * T), lambda b: (0, 0)),
                  pl.BlockSpec((cw, cw), lambda b: (0, 0))],
        out_specs=pl.BlockSpec((1, ncp, cw), lambda b: (b, 0, 0)),
        out_shape=jax.ShapeDtypeStruct((B, ncp, cw), jnp.float32),
        compiler_params=pltpu.CompilerParams(dimension_semantics=("parallel",), vmem_limit_bytes=VMEM_LIMIT),
        name="compress_prompt",
    )(kv, wa, wb, pool, wmix)


def _masked_softmax_rows(s, mask):
    s = jnp.where(mask, s, MASK_NEG)
    e = jnp.exp(s - jnp.max(s, axis=-1, keepdims=True))
    p = e / jnp.sum(e, axis=-1, keepdims=True)
    return jnp.where(mask, p, 0.0)


def _nsa_prompt_body(q_ref, ksel_ref, kwin_ref, kc_ref, gl_ref, gn_ref, ovt_ref, ex_ref, o_ref):
    f32, bf16 = jnp.float32, jnp.bfloat16
    TQ, KC, G = NSA_TQ, NSA_KC, N_GROUP
    T = ksel_ref.shape[1]
    ncp = kc_ref.shape[1]
    n_sel = ovt_ref.shape[0]
    topk = min(SEL_TOPK, n_sel)
    qs = pl.program_id(1) * TQ
    qpos = qs + lax.broadcasted_iota(jnp.int32, (TQ, 1), 0)
    lo_half = lax.broadcasted_iota(jnp.int32, (TQ, LANES), 1) < N_HEAD_DIM

    def in_half(x, half):
        return jnp.where(lo_half if half == 0 else jnp.logical_not(lo_half), x, 0.0)

    def q_padded(h, half):
        tile = q_ref[0, :, (h // 2) * LANES:(h // 2 + 1) * LANES] * ATTN_SCALE
        if h % 2 != half:
            tile = pltpu.roll(tile, N_HEAD_DIM, axis=1)
        return in_half(tile, half)

    wstart = pl.multiple_of(jnp.maximum(qs - WINDOW, 0), TQ)
    kw = kwin_ref[0, pl.ds(wstart, WINDOW + TQ), 0:LANES].astype(bf16)
    vw = kwin_ref[0, pl.ds(wstart, WINDOW + TQ), LANES:2 * LANES].astype(bf16)
    kposw = wstart + lax.broadcasted_iota(jnp.int32, (1, WINDOW + TQ), 1)
    mask_w = (kposw <= qpos) & (kposw > qpos - WINDOW)

    kck = kc_ref[0, :, 0:LANES]
    kcv = kc_ref[0, :, LANES:2 * LANES].astype(bf16)
    kend = lax.broadcasted_iota(jnp.int32, (1, ncp), 1) * CMP_STRIDE + (CMP_BLOCK - 1)
    mask_c = kend <= qpos

    sid_t = lax.broadcasted_iota(jnp.int32, (n_sel, 1), 0)
    cur_t = (qs + lax.broadcasted_iota(jnp.int32, (1, TQ), 1)) // SEL_BLOCK
    valid_t = sid_t <= cur_t
    forced_t = (sid_t == 0) | (sid_t == cur_t) | (sid_t == cur_t - 1)
    n_chunks = qs // KC + 1

    gates = jax.nn.sigmoid(gl_ref[0])
    heads = [None] * N_HEADS
    for kvh in range(N_KV_HEADS):
        qg = jnp.concatenate([q_padded(kvh * G + g, kvh) for g in range(G)], axis=0)
        qb = qg.astype(bf16)

        s = _dot_nt_x3(qg, kck).reshape(G, TQ, ncp)
        p = _masked_softmax_rows(s, mask_c[None])
        o_c = _dot(p.reshape(G * TQ, ncp).astype(bf16), kcv)
        imp = jnp.sum(p, axis=0)

        hi, mid, lo = _split3(imp)
        ovt = ovt_ref[...]
        score = _dot_nt(ovt, hi) + _dot_nt(ovt, mid) + _dot_nt(ovt, lo)
        score = jnp.where(valid_t, score + jnp.where(forced_t, SEL_BIAS, 0.0), -SEL_BIAS)
        rank = jnp.zeros((n_sel, TQ), jnp.int32)
        for s2 in range(n_sel):
            row = score[s2:s2 + 1, :]
            ahead = (row > score) | ((row == score) & (s2 < sid_t))
            rank = rank + ahead.astype(jnp.int32)
        sel = jnp.where((rank < topk) & valid_t, 1.0, 0.0).T.astype(bf16)

        def chunk(c, carry, qb=qb, sel=sel):
            m, l, acc = carry
            k0 = pl.multiple_of(c * KC, KC)
            ks = ksel_ref[0, pl.ds(k0, KC), 0:LANES].astype(bf16)
            vs = ksel_ref[0, pl.ds(k0, KC), LANES:2 * LANES].astype(bf16)
            selx = _dot(sel, ex_ref[:, pl.ds(k0, KC)])
            kpos = k0 + lax.broadcasted_iota(jnp.int32, (1, KC), 1)
            keep = (selx > 0.5) & (kpos <= qpos)
            sc = jnp.where(keep[None], _dot_nt(qb, ks).reshape(G, TQ, KC), MASK_NEG)
            m_new = jnp.maximum(m, jnp.max(sc, axis=-1, keepdims=True))
            alpha = jnp.exp(m - m_new)
            e = jnp.exp(sc - m_new)
            l = alpha * l + jnp.sum(e, axis=-1, keepdims=True)
            pv = _dot(e.reshape(G * TQ, KC).astype(bf16), vs).reshape(G, TQ, LANES)
            return m_new, l, alpha * acc + pv

        m0 = jnp.full((G, TQ, 1), MASK_NEG, f32)
        l0 = jnp.zeros((G, TQ, 1), f32)
        a0 = jnp.zeros((G, TQ, LANES), f32)
        _, l, acc = lax.fori_loop(0, n_chunks, chunk, (m0, l0, a0))
        o_s = acc / l

        pw = _masked_softmax_rows(_dot_nt(qb, kw).reshape(G, TQ, WINDOW + TQ), mask_w[None])
        o_w = _dot(pw.reshape(G * TQ, WINDOW + TQ).astype(bf16), vw)

        o_c = o_c.reshape(G, TQ, LANES)
        o_w = o_w.reshape(G, TQ, LANES)
        for g in range(G):
            h = kvh * G + g
            o = (gates[:, h:h + 1] * o_c[g] + gates[:, N_HEADS + h:N_HEADS + h + 1] * o_s[g]
                 + gates[:, 2 * N_HEADS + h:2 * N_HEADS + h + 1] * o_w[g])
            if h % 2 != kvh:
                o = pltpu.roll(o, N_HEAD_DIM, axis=1)
            heads[h] = o

    for m in range(N_HEADS // 2):
        tile = jnp.where(lo_half, heads[2 * m], heads[2 * m + 1])
        gn = gn_ref[0, :, m * LANES:(m + 1) * LANES]
        o_ref[0, :, m * LANES:(m + 1) * LANES] = tile * (gn * jax.nn.sigmoid(gn))


def nsa_prompt(q, kv, kc, gl, gate_n):
    B, T, _ = q.shape
    TQ, KC = NSA_TQ, NSA_KC
    assert N_KV_HEADS == 2 and N_HEAD_DIM * 2 == LANES and T % KC == 0 and KC % TQ == 0 and T >= WINDOW + TQ
    assert SEL_BLOCK % CMP_STRIDE == 0 and WINDOW % TQ == 0
    ncp = T // CMP_STRIDE
    n_sel = T // SEL_BLOCK
    cs = np.arange(ncp) * CMP_STRIDE
    ss = np.arange(n_sel) * SEL_BLOCK
    overlap = (cs[:, None] < ss[None, :] + SEL_BLOCK) & (cs[:, None] + CMP_BLOCK > ss[None, :])
    overlap[ncp - 1] = False
    overlap_t = jnp.asarray(overlap.T, jnp.bfloat16)
    expand = jnp.asarray(np.arange(T)[None, :] // SEL_BLOCK == np.arange(n_sel)[:, None], jnp.bfloat16)
    kvw = 2 * N_KV_HEADS * N_HEAD_DIM
    return pl.pallas_call(
        _nsa_prompt_body,
        grid=(B, T // TQ),
        in_specs=[pl.BlockSpec((1, TQ, N_WIDTH), lambda b, i: (b, i, 0)),
                  pl.BlockSpec((1, T, kvw), lambda b, i: (b, 0, 1)),
                  pl.BlockSpec((1, T, kvw), lambda b, i: (b, 0, 2)),
                  pl.BlockSpec((1, ncp, kvw), lambda b, i: (b, 0, 0)),
                  pl.BlockSpec((1, TQ, N_BRANCH * N_HEADS), lambda b, i: (b, i, 0)),
                  pl.BlockSpec((1, TQ, N_WIDTH), lambda b, i: (b, i, 0)),
                  pl.BlockSpec((n_sel, ncp), lambda b, i: (0, 0)),
                  pl.BlockSpec((n_sel, T), lambda b, i: (0, 0))],
        out_specs=pl.BlockSpec((1, TQ, N_WIDTH), lambda b, i: (b, i, 0)),
        out_shape=jax.ShapeDtypeStruct((B, T, N_WIDTH), jnp.float32),
        compiler_params=pltpu.CompilerParams(dimension_semantics=("parallel", "arbitrary"),
                                             vmem_limit_bytes=VMEM_LIMIT),
        name="nsa_prompt",
    )(q, kv, kv, kc, gl, gate_n, overlap_t, expand)


RWKV_TC = 32
RWKV_ROWS = 8


def _rwkv_scan_body(r_ref, w_ref, k_ref, v_ref, kk_ref, b_ref, s0_ref, y_ref, s_ref):
    n = s_ref.shape[0]

    @pl.when(pl.program_id(1) == 0)
    def _():
        s_ref[...] = s0_ref[...]

    def step(t, carry):
        r, w, k, kk, b = r_ref[t], w_ref[t], k_ref[t], kk_ref[t], b_ref[t]

        def rows(g, carry):
            for u in range(RWKV_ROWS):
                i = g * RWKV_ROWS + u
                s_i = s_ref[i]
                sa = jnp.sum(s_i * kk, axis=0, keepdims=True)
                s_i = s_i * w - sa * b + v_ref[t, pl.ds(i, 1), :] * k
                s_ref[i] = s_i
                y_ref[t, pl.ds(i, 1), :] = jnp.sum(s_i * r, axis=0, keepdims=True)
            return carry

        return lax.fori_loop(0, n // RWKV_ROWS, rows, carry)

    lax.fori_loop(0, r_ref.shape[0], step, 0)


def rwkv_scan(r, w, k, v, kk, b, s0):
    B, T, H, N = r.shape
    BH = B * H
    assert BH % LANES == 0 and N % RWKV_ROWS == 0
    tc = min(RWKV_TC, T)
    assert T % tc == 0

    def lanes_last(x):
        return x.transpose(1, 3, 0, 2).reshape(T, N, BH)

    seq = pl.BlockSpec((tc, N, LANES), lambda l, t: (t, 0, l))
    state = pl.BlockSpec((N, N, LANES), lambda l, t: (0, 0, l))
    y, s_fin = pl.pallas_call(
        _rwkv_scan_body,
        grid=(BH // LANES, T // tc),
        in_specs=[seq] * 6 + [state],
        out_specs=[seq, state],
        out_shape=[jax.ShapeDtypeStruct((T, N, BH), jnp.float32), jax.ShapeDtypeStruct((N, N, BH), jnp.float32)],
        compiler_params=pltpu.CompilerParams(dimension_semantics=("parallel", "arbitrary"),
                                             vmem_limit_bytes=VMEM_LIMIT),
        name="rwkv_scan",
    )(*(lanes_last(x) for x in (r, w, k, v, kk, b)), s0.transpose(2, 3, 0, 1).reshape(N, N, BH))
    return (y.reshape(T, N, B, H).transpose(2, 0, 3, 1),
            s_fin.reshape(N, N, B, H).transpose(2, 3, 0, 1))


def _final_norm_body(x_ref, g_ref, o_ref):
    x = x_ref[...]
    o_ref[...] = x * lax.rsqrt(jnp.mean(x * x, axis=-1, keepdims=True) + RMS_EPS) * g_ref[...]


def final_norm(x, g):
    B, T, D = x.shape
    x2 = x.reshape(B * T, D)
    tm = min(512, B * T)
    out = pl.pallas_call(
        _final_norm_body,
        grid=(B * T // tm,),
        in_specs=[pl.BlockSpec((tm, D), lambda i: (i, 0)), pl.BlockSpec((1, D), lambda i: (0, 0))],
        out_specs=pl.BlockSpec((tm, D), lambda i: (i, 0)),
        out_shape=jax.ShapeDtypeStruct((B * T, D), x.dtype),
        name="final_norm",
    )(x2, g.reshape(1, D))
    return out.reshape(B, T, D)


def kernel(x_prompt, x_sample, cache_kv, cache_kv_win, state_wkv, state_shift, page_table,
           norm_g, w_in, mu_shift, w0, w_decay_up, a0, w_aaa_up, k_k, k_a, r_k, gn_w, gn_b,
           w_cmp_pos, w_cmp_mix, w_out, final_g):
    y_p, y_s = x_prompt, x_sample
    kvp, kvs, wnp, wns, skp, sks, shp, shs = [], [], [], [], [], [], [], []
    for layer in range(DEPTH):
        lw = (norm_g[layer], w_in[layer], mu_shift[layer], w0[layer], w_decay_up[layer], a0[layer],
              w_aaa_up[layer], k_k[layer], k_a[layer], r_k[layer], gn_w[layer], gn_b[layer],
              w_cmp_pos[layer], w_cmp_mix[layer], w_out[layer])
        y_p, p_kv, p_win, p_wkv, p_shift = prompt_layer(y_p, *lw)
        y_s, s_kv, s_win, s_wkv, s_shift = sample_layer(y_s, cache_kv[layer], cache_kv_win[layer],
                                                        state_wkv[layer], state_shift[layer], page_table, *lw)
        kvp.append(p_kv); kvs.append(s_kv); wnp.append(p_win); wns.append(s_win)
        skp.append(p_wkv); sks.append(s_wkv); shp.append(p_shift); shs.append(s_shift)
    y_prompt = final_norm(y_p, final_g)
    y_sample = final_norm(y_s, final_g)
    return (y_prompt, y_sample, jnp.stack(kvp), jnp.stack(kvs), jnp.stack(wnp), jnp.stack(wns),
            jnp.stack(skp), jnp.stack(sks), jnp.stack(shp), jnp.stack(shs))
```

```python
import jax, jax.numpy as jnp
from jax import lax
import functools
import numpy as np
from jax.experimental import pallas as pl
from jax.experimental.pallas import tpu as pltpu

D_MODEL = 1024
BATCH = 16
SEQ = 2048
DEPTH = 1
DEC_BATCH = 32
DEC_SEQ = 4
PAST_LEN = 16384
PAGE_SIZE = 128

R_HEAD_DIM = 64
R_WIDTH = D_MODEL // 2
R_HEADS = R_WIDTH // R_HEAD_DIM
DECAY_LORA = 64
AAA_LORA = 64
R_SHIFT_WIDTH = 3 * R_WIDTH + DECAY_LORA + AAA_LORA
N_HEAD_DIM = 64
N_WIDTH = D_MODEL - R_WIDTH
N_HEADS = N_WIDTH // N_HEAD_DIM
N_KV_HEADS = 2
N_GROUP = N_HEADS // N_KV_HEADS
N_BRANCH = 3
KV_SLOTS = 2 * N_BRANCH
CMP_BLOCK = 32
CMP_STRIDE = 16
SEL_BLOCK = 64
SEL_TOPK = 16
WINDOW = 512
SEL_QUERY_BLOCK = 32
WIN_QUERY_BLOCK = 128
MIX_WIDTH = R_WIDTH + N_WIDTH
PROJ_SPLITS = (R_SHIFT_WIDTH, R_WIDTH, N_WIDTH, N_WIDTH, KV_SLOTS * N_KV_HEADS * N_HEAD_DIM, N_BRANCH * N_HEADS)
PROJ_WIDTH = sum(PROJ_SPLITS)
RMS_EPS = 1e-6
GN_EPS = 64e-5
MASK_NEG = -1e30
SEL_BIAS = 1e4
ATTN_SCALE = N_HEAD_DIM ** -0.5


def rms_norm(x, g):
    xf = x.astype(jnp.float32)
    y = xf * lax.rsqrt(jnp.mean(xf * xf, axis=-1, keepdims=True) + RMS_EPS)
    return (y * g.astype(jnp.float32)).astype(x.dtype)


def masked_softmax(s, mask):
    s = jnp.where(mask, s.astype(jnp.float32), MASK_NEG)
    p = jax.nn.softmax(s, axis=-1)
    return jnp.where(mask, p, 0.0)


def split_proj(p):
    offs = np.cumsum((0,) + PROJ_SPLITS)
    return tuple(p[..., int(offs[i]):int(offs[i + 1])] for i in range(len(PROJ_SPLITS)))


def to_heads(q):
    B, T, _ = q.shape
    return q.reshape(B, T, N_KV_HEADS, N_GROUP, N_HEAD_DIM).transpose(0, 2, 3, 1, 4)


def rwkv_mix(z, z_prev, s0, gate, mu_shift, w0, w_decay_up, a0, w_aaa_up, k_k, k_a, r_k, gn_w, gn_b):
    B, T, _ = z.shape
    f32 = jnp.float32
    zp = jnp.concatenate([z_prev[:, None].astype(z.dtype), z[:, :-1]], axis=1)
    zs = z + (zp - z) * mu_shift
    r, k, v, wd, ad = jnp.split(zs, [R_WIDTH, 2 * R_WIDTH, 3 * R_WIDTH, 3 * R_WIDTH + DECAY_LORA], axis=-1)
    w_log = -jax.nn.softplus(-(w0 + jnp.tanh(wd) @ w_decay_up).astype(f32)) - 0.5
    decay = jnp.exp(-jnp.exp(w_log))
    a = jax.nn.sigmoid((a0 + ad @ w_aaa_up).astype(f32))
    hs = (B, T, R_HEADS, R_HEAD_DIM)
    r, k, v, decay, a = (t.astype(f32).reshape(hs) for t in (r, k, v, decay, a))
    hshape = (R_HEADS, R_HEAD_DIM)
    kk = k * k_k.astype(f32).reshape(hshape)
    kk = kk * lax.rsqrt(jnp.maximum(jnp.sum(kk * kk, axis=-1, keepdims=True), 1e-24))
    k = k * (1.0 + (a - 1.0) * k_a.astype(f32).reshape(hshape))

    y, s_fin = rwkv_scan(r, decay, k, v, kk, kk * a, s0.astype(f32))
    yc = y - jnp.mean(y, axis=-1, keepdims=True)
    y = yc * lax.rsqrt(jnp.mean(yc * yc, axis=-1, keepdims=True) + GN_EPS)
    y = y * gn_w.astype(f32).reshape(hshape) + gn_b.astype(f32).reshape(hshape)
    y = y + jnp.sum(r * k * r_k.astype(f32).reshape(hshape), axis=-1, keepdims=True) * v
    out = y.reshape(B, T, R_WIDTH).astype(z.dtype) * jax.nn.silu(gate)
    return out, s_fin, z[:, -1]


def compress(kv, w_pos, w_mix):
    B, L = kv.shape[:2]
    n_sub = CMP_BLOCK // CMP_STRIDE
    n_chunk = L // CMP_STRIDE
    n_cmp = n_chunk - n_sub + 1
    c = kv[:, :n_chunk * CMP_STRIDE].reshape(B, n_chunk, CMP_STRIDE, 2, N_KV_HEADS, N_HEAD_DIM)
    wp = w_pos.reshape(2, n_sub, CMP_STRIDE, N_HEAD_DIM)
    pooled = jnp.einsum('bjpekd,epd->bjekd', c[:, 0:n_cmp], wp[:, 0])
    for m in range(1, n_sub):
        pooled = pooled + jnp.einsum('bjpekd,epd->bjekd', c[:, m:m + n_cmp], wp[:, m])
    return jnp.einsum('bjekd,edf->bjekf', pooled, w_mix)


def compressed_attn(q, kc, qpos):
    n_cmp = kc.shape[1]
    kend = jnp.arange(n_cmp) * CMP_STRIDE + (CMP_BLOCK - 1)
    mask = kend[None, :] <= qpos[:, None]
    s = jnp.einsum('bkgtd,bjkd->bkgtj', q, kc[:, :, 0]) * ATTN_SCALE
    p = masked_softmax(s, mask)
    o = jnp.einsum('bkgtj,bjkd->bkgtd', p.astype(q.dtype), kc[:, :, 1])
    return o, jnp.sum(p, axis=2)


def select_blocks(imp, qpos, total_len):
    n_cmp = imp.shape[-1]
    n_sel = -(-total_len // SEL_BLOCK)
    cs = jnp.arange(n_cmp) * CMP_STRIDE
    ss = jnp.arange(n_sel) * SEL_BLOCK
    overlap = ((cs[:, None] < ss[None, :] + SEL_BLOCK) & (cs[:, None] + CMP_BLOCK > ss[None, :])).astype(jnp.float32)
    score = jnp.einsum('bktj,js->bkts', imp, overlap)
    cur = (qpos // SEL_BLOCK)[:, None]
    sid = jnp.arange(n_sel)[None, :]
    valid = sid <= cur
    forced = (sid == 0) | (sid == cur) | (sid == cur - 1)
    score = jnp.where(valid, score + jnp.where(forced, SEL_BIAS, 0.0), -SEL_BIAS)
    vals, idx = lax.top_k(score, min(SEL_TOPK, n_sel))
    return idx, vals > -0.5 * SEL_BIAS


def selected_attn(q, ks, vs, idx, valid, qpos):
    B, KV, T, K = idx.shape
    kpos = idx[..., None] * SEL_BLOCK + jnp.arange(SEL_BLOCK)
    mask = (valid[..., None] & (kpos <= qpos[:, None, None])).reshape(B, KV, 1, T, K * SEL_BLOCK)
    ks = ks.reshape(B, KV, T, K * SEL_BLOCK, N_HEAD_DIM)
    vs = vs.reshape(B, KV, T, K * SEL_BLOCK, N_HEAD_DIM)
    s = jnp.einsum('bkgtd,bktnd->bkgtn', q, ks) * ATTN_SCALE
    p = masked_softmax(s, mask)
    return jnp.einsum('bkgtn,bktnd->bkgtd', p.astype(q.dtype), vs)


def selected_prompt_attn(q, kv_sel, idx, valid, qpos):
    B, T = kv_sel.shape[:2]
    n_sel = T // SEL_BLOCK
    blocks = kv_sel.reshape(B, n_sel, SEL_BLOCK, 2, N_KV_HEADS, N_HEAD_DIM).transpose(0, 4, 1, 2, 3, 5)
    bk, bv = blocks[..., 0, :], blocks[..., 1, :]
    bi = jnp.arange(B)[:, None, None, None]
    gi = jnp.arange(N_KV_HEADS)[None, :, None, None]
    nq = T // SEL_QUERY_BLOCK

    def split(t, ax):
        return jnp.moveaxis(t.reshape(t.shape[:ax] + (nq, SEL_QUERY_BLOCK) + t.shape[ax + 1:]), ax, 0)

    def chunk(args):
        qc, ic, vc, pc = args
        return selected_attn(qc, bk[bi, gi, ic], bv[bi, gi, ic], ic, vc, pc)

    out = lax.map(chunk, (split(q, 3), split(idx, 2), split(valid, 2), qpos.reshape(nq, SEL_QUERY_BLOCK)))
    return jnp.moveaxis(out, 0, 3).reshape(B, N_KV_HEADS, N_GROUP, T, N_HEAD_DIM)


def selected_sample_attn(q, kv_new, cache, page_table, idx, valid, qpos):
    B, T = kv_new.shape[:2]
    bpp = PAGE_SIZE // SEL_BLOCK
    n_past = page_table.shape[1] * bpp
    pool = cache.reshape(cache.shape[0] * bpp, SEL_BLOCK, cache.shape[2], N_KV_HEADS, N_HEAD_DIM)
    n_tail = -(-T // SEL_BLOCK)
    tail = jnp.pad(kv_new, ((0, 0), (0, n_tail * SEL_BLOCK - T), (0, 0), (0, 0), (0, 0)))
    tail = tail.reshape(B, n_tail, SEL_BLOCK, 2, N_KV_HEADS, N_HEAD_DIM).transpose(0, 4, 1, 2, 3, 5)
    bi = jnp.arange(B)[:, None, None, None]
    gi = jnp.arange(N_KV_HEADS)[None, :, None, None]
    ip = jnp.minimum(idx, n_past - 1)
    phys = page_table[bi, ip // bpp] * bpp + ip % bpp
    past_rows = pool[phys, :, 2:4, gi]
    tail_rows = tail[bi, gi, jnp.clip(idx - n_past, 0, n_tail - 1)]
    rows = jnp.where((idx < n_past)[..., None, None, None], past_rows.astype(tail_rows.dtype), tail_rows)
    return selected_attn(q, rows[..., 0, :], rows[..., 1, :], idx, valid, qpos)


def window_attn(q, kw, qpos, kpos):
    mask = (kpos[None, :] <= qpos[:, None]) & (kpos[None, :] > qpos[:, None] - WINDOW) & (kpos[None, :] >= 0)
    s = jnp.einsum('bkgtd,bskd->bkgts', q, kw[:, :, 0]) * ATTN_SCALE
    p = masked_softmax(s, mask)
    return jnp.einsum('bkgts,bskd->bkgtd', p.astype(q.dtype), kw[:, :, 1])


def window_prompt_attn(q, kv_win):
    B, T = kv_win.shape[:2]
    padded = jnp.pad(kv_win, ((0, 0), (WINDOW, 0), (0, 0), (0, 0), (0, 0)))
    nq = T // WIN_QUERY_BLOCK

    def chunk(i):
        start = i * WIN_QUERY_BLOCK
        kw = lax.dynamic_slice_in_dim(padded, start, WINDOW + WIN_QUERY_BLOCK, axis=1)
        qc = lax.dynamic_slice_in_dim(q, start, WIN_QUERY_BLOCK, axis=3)
        qpos = start + jnp.arange(WIN_QUERY_BLOCK)
        kpos = start - WINDOW + jnp.arange(WINDOW + WIN_QUERY_BLOCK)
        return window_attn(qc, kw, qpos, kpos)

    out = lax.map(chunk, jnp.arange(nq))
    return jnp.moveaxis(out, 0, 3).reshape(B, N_KV_HEADS, N_GROUP, T, N_HEAD_DIM)


def mixer_out(y_r, o_c, o_s, o_w, gl, gate_n, w_out):
    B, T, _ = y_r.shape
    g = jax.nn.sigmoid(gl.astype(jnp.float32)).reshape(B, T, N_BRANCH, N_KV_HEADS, N_GROUP)
    g = g.transpose(2, 0, 3, 4, 1)[..., None]
    o = g[0] * o_c + g[1] * o_s + g[2] * o_w
    o = o.transpose(0, 3, 1, 2, 4).reshape(B, T, N_WIDTH).astype(y_r.dtype) * jax.nn.silu(gate_n)
    return jnp.concatenate([y_r, o], axis=-1) @ w_out


def prompt_layer(x, norm_g, w_in, mu_shift, w0, w_decay_up, a0, w_aaa_up, k_k, k_a, r_k, gn_w, gn_b,
                 w_cmp_pos, w_cmp_mix, w_out):
    B, T, _ = x.shape
    zr, gate_r, q, gate_n, kv, gl = split_proj(rms_norm(x, norm_g) @ w_in)
    y_r, wkv, shift = rwkv_mix(zr, jnp.zeros((B, R_SHIFT_WIDTH), zr.dtype),
                               jnp.zeros((B, R_HEADS, R_HEAD_DIM, R_HEAD_DIM), jnp.float32), gate_r,
                               mu_shift, w0, w_decay_up, a0, w_aaa_up, k_k, k_a, r_k, gn_w, gn_b)
    o_n = nsa_prompt(q, kv, compress_prompt(kv, w_cmp_pos, w_cmp_mix), gl, gate_n)
    y = x + jnp.concatenate([y_r, o_n], axis=-1) @ w_out
    kv = kv.reshape(B, T, KV_SLOTS, N_KV_HEADS, N_HEAD_DIM)
    return y, kv[:, :, 0:4], kv[:, T - min(WINDOW, T):, 4:6], wkv, shift


def sample_layer(x, cache, win_buf, wkv0, shift0, page_table, norm_g, w_in, mu_shift, w0, w_decay_up, a0,
                 w_aaa_up, k_k, k_a, r_k, gn_w, gn_b, w_cmp_pos, w_cmp_mix, w_out):
    B, T, _ = x.shape
    past = page_table.shape[1] * PAGE_SIZE
    zr, gate_r, q, gate_n, kv, gl = split_proj(rms_norm(x, norm_g) @ w_in)
    y_r, wkv, shift = rwkv_mix(zr, shift0, wkv0, gate_r,
                               mu_shift, w0, w_decay_up, a0, w_aaa_up, k_k, k_a, r_k, gn_w, gn_b)
    cache2 = cache.reshape(cache.shape[0], PAGE_SIZE, 2 * BRANCH_W)
    nb = win_buf.shape[1]
    kc = sample_compress(cache2, page_table, w_cmp_pos, w_cmp_mix)
    o_n = nsa_sample(q, kv, kc, gl, gate_n, cache2, page_table, win_buf.reshape(B, nb, BRANCH_W))
    y = x + jnp.concatenate([y_r, o_n], axis=-1) @ w_out
    kv = kv.reshape(B, T, KV_SLOTS, N_KV_HEADS, N_HEAD_DIM)
    keys_w = jnp.concatenate([win_buf.astype(kv.dtype), kv[:, :, 4:6]], axis=1)
    n_keep = min(WINDOW, nb + T)
    return y, kv[:, :, 0:4], keys_w[:, nb + T - n_keep:], wkv, shift


LANES = 128
NSA_TQ = 128
NSA_KC = 512
VMEM_LIMIT = 48 * 1024 * 1024


def _split3(x):
    hi = x.astype(jnp.bfloat16)
    r1 = x - hi.astype(jnp.float32)
    mid = r1.astype(jnp.bfloat16)
    lo = (r1 - mid.astype(jnp.float32)).astype(jnp.bfloat16)
    return hi, mid, lo


def _dot(a, b):
    return jnp.dot(a, b, preferred_element_type=jnp.float32)


def _dot_nt(a, b):
    return lax.dot_general(a, b, (((1,), (1,)), ((), ())), preferred_element_type=jnp.float32)


def _dot_exact_rhs(a, b01):
    hi, mid, lo = _split3(a)
    return _dot(hi, b01) + _dot(mid, b01) + _dot(lo, b01)


def _dot_x3(a, b):
    ah = a.astype(jnp.bfloat16)
    al = (a - ah.astype(jnp.float32)).astype(jnp.bfloat16)
    bh = b.astype(jnp.bfloat16)
    bl = (b - bh.astype(jnp.float32)).astype(jnp.bfloat16)
    return _dot(ah, bh) + _dot(ah, bl) + _dot(al, bh)


def _dot_nt_x3(a, b):
    ah = a.astype(jnp.bfloat16)
    al = (a - ah.astype(jnp.float32)).astype(jnp.bfloat16)
    bh = b.astype(jnp.bfloat16)
    bl = (b - bh.astype(jnp.float32)).astype(jnp.bfloat16)
    return _dot_nt(ah, bh) + _dot_nt(ah, bl) + _dot_nt(al, bh)


def _compress_prompt_body(kv_ref, wa_ref, wb_ref, pool_ref, wmix_ref, o_ref):
    x = kv_ref[0]
    xw = jnp.concatenate([x * wa_ref[...], x * wb_ref[...]], axis=0)
    pooled = _dot_exact_rhs_lhs01(pool_ref[...], xw)
    o_ref[0] = _dot_x3(pooled, wmix_ref[...])


def _dot_exact_rhs_lhs01(a01, b):
    hi, mid, lo = _split3(b)
    return _dot(a01, hi) + _dot(a01, mid) + _dot(a01, lo)


def compress_prompt(kv, w_pos, w_mix):
    B, T, _ = kv.shape
    n_sub = CMP_BLOCK // CMP_STRIDE
    assert n_sub == 2 and T % CMP_STRIDE == 0
    ncp = T // CMP_STRIDE
    cw = 2 * N_KV_HEADS * N_HEAD_DIM
    wp = jnp.broadcast_to(w_pos.reshape(2, 1, CMP_BLOCK, N_HEAD_DIM).transpose(2, 0, 1, 3),
                          (CMP_BLOCK, 2, N_KV_HEADS, N_HEAD_DIM)).reshape(CMP_BLOCK, cw)
    wa = jnp.tile(wp[:CMP_STRIDE], (ncp, 1))
    wb = jnp.tile(wp[CMP_STRIDE:], (ncp, 1))
    chunk = np.arange(T) // CMP_STRIDE
    pool = np.concatenate([chunk[None, :] == np.arange(ncp)[:, None],
                           chunk[None, :] == np.arange(ncp)[:, None] + 1], axis=1)
    pool = jnp.asarray(pool, jnp.bfloat16)
    wmix = jnp.zeros((cw, cw), jnp.float32)
    for e in range(2):
        for kh in range(N_KV_HEADS):
            o = (e * N_KV_HEADS + kh) * N_HEAD_DIM
            wmix = wmix.at[o:o + N_HEAD_DIM, o:o + N_HEAD_DIM].set(w_mix[e])
    return pl.pallas_call(
        _compress_prompt_body,
        grid=(B,),
        in_specs=[pl.BlockSpec((1, T, cw), lambda b: (b, 0, 0)),
                  pl.BlockSpec((T, cw), lambda b: (0, 0)),
                  pl.BlockSpec((T, cw), lambda b: (0, 0)),
                  pl.BlockSpec((ncp, 2 * T), lambda b: (0, 0)),
                  pl.BlockSpec((cw, cw), lambda b: (0, 0))],
        out_specs=pl.BlockSpec((1, ncp, cw), lambda b: (b, 0, 0)),
        out_shape=jax.ShapeDtypeStruct((B, ncp, cw), jnp.float32),
        compiler_params=pltpu.CompilerParams(dimension_semantics=("parallel",), vmem_limit_bytes=VMEM_LIMIT),
        name="compress_prompt",
    )(kv, wa, wb, pool, wmix)


def _masked_softmax_rows(s, mask):
    s = jnp.where(mask, s, MASK_NEG)
    e = jnp.exp(s - jnp.max(s, axis=-1, keepdims=True))
    p = e / jnp.sum(e, axis=-1, keepdims=True)
    return jnp.where(mask, p, 0.0)


def _nsa_prompt_body(q_ref, ksel_ref, kwin_ref, kc_ref, gl_ref, gn_ref, ovt_ref, ex_ref, o_ref):
    f32, bf16 = jnp.float32, jnp.bfloat16
    TQ, KC, G = NSA_TQ, NSA_KC, N_GROUP
    T = ksel_ref.shape[1]
    ncp = kc_ref.shape[1]
    n_sel = ovt_ref.shape[0]
    topk = min(SEL_TOPK, n_sel)
    qs = pl.program_id(1) * TQ
    qpos = qs + lax.broadcasted_iota(jnp.int32, (TQ, 1), 0)
    lo_half = lax.broadcasted_iota(jnp.int32, (TQ, LANES), 1) < N_HEAD_DIM

    def in_half(x, half):
        return jnp.where(lo_half if half == 0 else jnp.logical_not(lo_half), x, 0.0)

    def q_padded(h, half):
        tile = q_ref[0, :, (h // 2) * LANES:(h // 2 + 1) * LANES] * ATTN_SCALE
        if h % 2 != half:
            tile = pltpu.roll(tile, N_HEAD_DIM, axis=1)
        return in_half(tile, half)

    wstart = pl.multiple_of(jnp.maximum(qs - WINDOW, 0), TQ)
    kw = kwin_ref[0, pl.ds(wstart, WINDOW + TQ), 0:LANES].astype(bf16)
    vw = kwin_ref[0, pl.ds(wstart, WINDOW + TQ), LANES:2 * LANES].astype(bf16)
    kposw = wstart + lax.broadcasted_iota(jnp.int32, (1, WINDOW + TQ), 1)
    mask_w = (kposw <= qpos) & (kposw > qpos - WINDOW)

    kck = kc_ref[0, :, 0:LANES]
    kcv = kc_ref[0, :, LANES:2 * LANES].astype(bf16)
    kend = lax.broadcasted_iota(jnp.int32, (1, ncp), 1) * CMP_STRIDE + (CMP_BLOCK - 1)
    mask_c = kend <= qpos

    sid_t = lax.broadcasted_iota(jnp.int32, (n_sel, 1), 0)
    cur_t = (qs + lax.broadcasted_iota(jnp.int32, (1, TQ), 1)) // SEL_BLOCK
    valid_t = sid_t <= cur_t
    forced_t = (sid_t == 0) | (sid_t == cur_t) | (sid_t == cur_t - 1)
    n_chunks = qs // KC + 1

    gates = jax.nn.sigmoid(gl_ref[0])
    heads = [None] * N_HEADS
    for kvh in range(N_KV_HEADS):
        qg = jnp.concatenate([q_padded(kvh * G + g, kvh) for g in range(G)], axis=0)
        qb = qg.astype(bf16)

        s = _dot_nt_x3(qg, kck).reshape(G, TQ, ncp)
        p = _masked_softmax_rows(s, mask_c[None])
        o_c = _dot(p.reshape(G * TQ, ncp).astype(bf16), kcv)
        imp = jnp.sum(p, axis=0)

        hi, mid, lo = _split3(imp)
        ovt = ovt_ref[...]
        score = _dot_nt(ovt, hi) + _dot_nt(ovt, mid) + _dot_nt(ovt, lo)
        score = jnp.where(valid_t, score + jnp.where(forced_t, SEL_BIAS, 0.0), -SEL_BIAS)
        rank = jnp.zeros((n_sel, TQ), jnp.int32)
        for s2 in range(n_sel):
            row = score[s2:s2 + 1, :]
            ahead = (row > score) | ((row == score) & (s2 < sid_t))
            rank = rank + ahead.astype(jnp.int32)
        sel = jnp.where((rank < topk) & valid_t, 1.0, 0.0).T.astype(bf16)

        def chunk(c, carry, qb=qb, sel=sel):
            m, l, acc = carry
            k0 = pl.multiple_of(c * KC, KC)
            ks = ksel_ref[0, pl.ds(k0, KC), 0:LANES].astype(bf16)
            vs = ksel_ref[0, pl.ds(k0, KC), LANES:2 * LANES].astype(bf16)
            selx = _dot(sel, ex_ref[:, pl.ds(k0, KC)])
            kpos = k0 + lax.broadcasted_iota(jnp.int32, (1, KC), 1)
            keep = (selx > 0.5) & (kpos <= qpos)
            sc = jnp.where(keep[None], _dot_nt(qb, ks).reshape(G, TQ, KC), MASK_NEG)
            m_new = jnp.maximum(m, jnp.max(sc, axis=-1, keepdims=True))
            alpha = jnp.exp(m - m_new)
            e = jnp.exp(sc - m_new)
            l = alpha * l + jnp.sum(e, axis=-1, keepdims=True)
            pv = _dot(e.reshape(G * TQ, KC).astype(bf16), vs).reshape(G, TQ, LANES)
            return m_new, l, alpha * acc + pv

        m0 = jnp.full((G, TQ, 1), MASK_NEG, f32)
        l0 = jnp.zeros((G, TQ, 1), f32)
        a0 = jnp.zeros((G, TQ, LANES), f32)
        _, l, acc = lax.fori_loop(0, n_chunks, chunk, (m0, l0, a0))
        o_s = acc / l

        pw = _masked_softmax_rows(_dot_nt(qb, kw).reshape(G, TQ, WINDOW + TQ), mask_w[None])
        o_w = _dot(pw.reshape(G * TQ, WINDOW + TQ).astype(bf16), vw)

        o_c = o_c.reshape(G, TQ, LANES)
        o_w = o_w.reshape(G, TQ, LANES)
        for g in range(G):
            h = kvh * G + g
            o = (gates[:, h:h + 1] * o_c[g] + gates[:, N_HEADS + h:N_HEADS + h + 1] * o_s[g]
                 + gates[:, 2 * N_HEADS + h:2 * N_HEADS + h + 1] * o_w[g])
            if h % 2 != kvh:
                o = pltpu.roll(o, N_HEAD_DIM, axis=1)
            heads[h] = o

    for m in range(N_HEADS // 2):
        tile = jnp.where(lo_half, heads[2 * m], heads[2 * m + 1])
        gn = gn_ref[0, :, m * LANES:(m + 1) * LANES]
        o_ref[0, :, m * LANES:(m + 1) * LANES] = tile * (gn * jax.nn.sigmoid(gn))


def nsa_prompt(q, kv, kc, gl, gate_n):
    B, T, _ = q.shape
    TQ, KC = NSA_TQ, NSA_KC
    assert N_KV_HEADS == 2 and N_HEAD_DIM * 2 == LANES and T % KC == 0 and KC % TQ == 0 and T >= WINDOW + TQ
    assert SEL_BLOCK % CMP_STRIDE == 0 and WINDOW % TQ == 0
    ncp = T // CMP_STRIDE
    n_sel = T // SEL_BLOCK
    cs = np.arange(ncp) * CMP_STRIDE
    ss = np.arange(n_sel) * SEL_BLOCK
    overlap = (cs[:, None] < ss[None, :] + SEL_BLOCK) & (cs[:, None] + CMP_BLOCK > ss[None, :])
    overlap[ncp - 1] = False
    overlap_t = jnp.asarray(overlap.T, jnp.bfloat16)
    expand = jnp.asarray(np.arange(T)[None, :] // SEL_BLOCK == np.arange(n_sel)[:, None], jnp.bfloat16)
    kvw = 2 * N_KV_HEADS * N_HEAD_DIM
    return pl.pallas_call(
        _nsa_prompt_body,
        grid=(B, T // TQ),
        in_specs=[pl.BlockSpec((1, TQ, N_WIDTH), lambda b, i: (b, i, 0)),
                  pl.BlockSpec((1, T, kvw), lambda b, i: (b, 0, 1)),
                  pl.BlockSpec((1, T, kvw), lambda b, i: (b, 0, 2)),
                  pl.BlockSpec((1, ncp, kvw), lambda b, i: (b, 0, 0)),
                  pl.BlockSpec((1, TQ, N_BRANCH * N_HEADS), lambda b, i: (b, i, 0)),
                  pl.BlockSpec((1, TQ, N_WIDTH), lambda b, i: (b, i, 0)),
                  pl.BlockSpec((n_sel, ncp), lambda b, i: (0, 0)),
                  pl.BlockSpec((n_sel, T), lambda b, i: (0, 0))],
        out_specs=pl.BlockSpec((1, TQ, N_WIDTH), lambda b, i: (b, i, 0)),
        out_shape=jax.ShapeDtypeStruct((B, T, N_WIDTH), jnp.float32),
        compiler_params=pltpu.CompilerParams(dimension_semantics=("parallel", "arbitrary"),
                                             vmem_limit_bytes=VMEM_LIMIT),
        name="nsa_prompt",
    )(q, kv, kv, kc, gl, gate_n, overlap_t, expand)


SAMPLE_PG = 8
BRANCH_W = 2 * N_KV_HEADS * N_HEAD_DIM


def _page_spec(k, col):
    return pl.BlockSpec((1, PAGE_SIZE, BRANCH_W), lambda b, g, pt: (pt[b, g * SAMPLE_PG + k], 0, col))


def _sample_compress_body(pt_ref, *refs):
    pages = refs[:SAMPLE_PG]
    wa_ref, wb_ref, pool_ref, wmix_ref, o_ref = refs[SAMPLE_PG:]
    g = pl.program_id(1)
    rows = pool_ref.shape[0]
    x = jnp.concatenate([p[0] for p in pages], axis=0)
    a = _dot_exact_rhs_lhs01(pool_ref[...], x * wa_ref[...])
    bc = _dot_exact_rhs_lhs01(pool_ref[...], x * wb_ref[...])
    last = lax.broadcasted_iota(jnp.int32, (rows, 1), 0) == rows - 1
    r0 = pl.multiple_of(g * rows, rows)
    o_ref[0, pl.ds(r0, rows), :] = a + jnp.where(last, 0.0, pltpu.roll(bc, rows - 1, axis=0))

    @pl.when(g > 0)
    def _():
        o_ref[0, pl.ds(r0 - 1, 1), :] += bc[0:1, :]

    @pl.when(g == pl.num_programs(1) - 1)
    def _():
        o_ref[0] = _dot_x3(o_ref[0], wmix_ref[...])


def _cmp_weights(w_pos, w_mix, n_rows):
    wp = jnp.broadcast_to(w_pos.reshape(2, 1, CMP_BLOCK, N_HEAD_DIM).transpose(2, 0, 1, 3),
                          (CMP_BLOCK, 2, N_KV_HEADS, N_HEAD_DIM)).reshape(CMP_BLOCK, BRANCH_W)
    wa = jnp.tile(wp[:CMP_STRIDE], (n_rows // CMP_STRIDE, 1))
    wb = jnp.tile(wp[CMP_STRIDE:], (n_rows // CMP_STRIDE, 1))
    wmix = jnp.zeros((BRANCH_W, BRANCH_W), jnp.float32)
    for e in range(2):
        for kh in range(N_KV_HEADS):
            o = (e * N_KV_HEADS + kh) * N_HEAD_DIM
            wmix = wmix.at[o:o + N_HEAD_DIM, o:o + N_HEAD_DIM].set(w_mix[e])
    return wa, wb, wmix


def sample_compress(cache, page_table, w_pos, w_mix):
    B, n_pages = page_table.shape
    PG = SAMPLE_PG
    assert n_pages % PG == 0 and CMP_BLOCK == 2 * CMP_STRIDE and PAGE_SIZE % CMP_STRIDE == 0
    rows = PG * PAGE_SIZE // CMP_STRIDE
    ncp = n_pages * PAGE_SIZE // CMP_STRIDE
    wa, wb, wmix = _cmp_weights(w_pos, w_mix, PG * PAGE_SIZE)
    pool = jnp.asarray(np.arange(PG * PAGE_SIZE)[None, :] // CMP_STRIDE == np.arange(rows)[:, None], jnp.bfloat16)
    const = lambda shape: pl.BlockSpec(shape, lambda b, g, pt: (0,) * len(shape))
    return pl.pallas_call(
        _sample_compress_body,
        grid_spec=pltpu.PrefetchScalarGridSpec(
            num_scalar_prefetch=1, grid=(B, n_pages // PG),
            in_specs=[_page_spec(k, 0) for k in range(PG)] + [const(wa.shape), const(wb.shape), const(pool.shape),
                                                              const(wmix.shape)],
            out_specs=pl.BlockSpec((1, ncp, BRANCH_W), lambda b, g, pt: (b, 0, 0))),
        out_shape=jax.ShapeDtypeStruct((B, ncp, BRANCH_W), jnp.float32),
        compiler_params=pltpu.CompilerParams(dimension_semantics=("parallel", "arbitrary"),
                                             vmem_limit_bytes=VMEM_LIMIT),
        name="sample_compress",
    )(page_table, *([cache] * PG), wa, wb, pool, wmix)


def _online_chunk(m, l, acc, s, keep, v):
    s = jnp.where(keep, s, MASK_NEG)
    m_new = jnp.maximum(m, jnp.max(s, axis=-1, keepdims=True))
    alpha = jnp.exp(m - m_new)
    e = jnp.where(keep, jnp.exp(s - m_new), 0.0)
    return (m_new, alpha * l + jnp.sum(e, axis=-1, keepdims=True),
            alpha * acc + _dot(e.astype(jnp.bfloat16), v))


def _sample_attend_body(pt_ref, *refs, past):
    f32, bf16 = jnp.float32, jnp.bfloat16
    PG, G = SAMPLE_PG, N_GROUP
    pages = refs[:PG]
    (kc_ref, q_ref, kvn_ref, gl_ref, gn_ref, win_ref, ex_ref, ov_ref, o_ref,
     qf_ref, sel_ref, m_ref, l_ref, acc_ref, oc_ref) = refs[PG:]
    T = q_ref.shape[1]
    R = N_HEADS * T
    ncp = kc_ref.shape[1]
    nsp = ov_ref.shape[1]
    KC = PG * PAGE_SIZE
    n_past = past // SEL_BLOCK
    g = pl.program_id(1)
    lo_half = lax.broadcasted_iota(jnp.int32, (T, LANES), 1) < N_HEAD_DIM
    row = lax.broadcasted_iota(jnp.int32, (R, 1), 0)
    qpos = past + row % T

    @pl.when(g == 0)
    def _():
        pieces = []
        for h in range(N_HEADS):
            kvh = h // G
            tile = q_ref[0, :, (h // 2) * LANES:(h // 2 + 1) * LANES] * ATTN_SCALE
            if h % 2 != kvh:
                tile = pltpu.roll(tile, N_HEAD_DIM, axis=1)
            pieces.append(jnp.where(lo_half if kvh == 0 else jnp.logical_not(lo_half), tile, 0.0))
        q32 = jnp.concatenate(pieces, axis=0)
        qf_ref[...] = q32

        kend = lax.broadcasted_iota(jnp.int32, (1, ncp), 1) * CMP_STRIDE + (CMP_BLOCK - 1)
        p = _masked_softmax_rows(_dot_nt_x3(q32, kc_ref[0, :, 0:LANES]), kend <= qpos)
        oc_ref[...] = _dot(p.astype(bf16), kc_ref[0, :, LANES:2 * LANES].astype(bf16))
        imp = jnp.concatenate(
            [sum(p[(kvh * G + gg) * T:(kvh * G + gg + 1) * T] for gg in range(G)) for kvh in range(N_KV_HEADS)],
            axis=0)

        sid = lax.broadcasted_iota(jnp.int32, (1, nsp), 1)
        sid_f = sid.astype(f32)
        cur = (past + lax.broadcasted_iota(jnp.int32, (N_KV_HEADS * T, 1), 0) % T) // SEL_BLOCK
        valid = sid <= cur
        forced = (sid == 0) | (sid == cur) | (sid == cur - 1)
        score = _dot_exact_rhs(imp, ov_ref[...])
        score = jnp.where(valid, score + jnp.where(forced, SEL_BIAS, 0.0), -SEL_BIAS)

        def extract(_, carry):
            sc, chosen = carry
            best = jnp.max(sc, axis=-1, keepdims=True)
            first = jnp.min(jnp.where(sc == best, sid_f, float(nsp)), axis=-1, keepdims=True)
            hit = sid_f == first
            return jnp.where(hit, -3e38, sc), jnp.where(hit, 1.0, chosen)

        _, chosen = lax.fori_loop(0, min(SEL_TOPK, n_past + 1), extract, (score, jnp.zeros_like(score)))
        sel = jnp.where(valid, chosen, 0.0)
        sel_ref[...] = jnp.concatenate(
            [sel[(h // G) * T:(h // G + 1) * T] for h in range(N_HEADS)], axis=0).astype(bf16)
        m_ref[...] = jnp.full(m_ref.shape, MASK_NEG, f32)
        l_ref[...] = jnp.zeros(l_ref.shape, f32)
        acc_ref[...] = jnp.zeros(acc_ref.shape, f32)

    qb = qf_ref[...].astype(bf16)
    ks = jnp.concatenate([p[0, :, 0:LANES] for p in pages], axis=0).astype(bf16)
    vs = jnp.concatenate([p[0, :, LANES:2 * LANES] for p in pages], axis=0).astype(bf16)
    kpos = g * KC + lax.broadcasted_iota(jnp.int32, (1, KC), 1)
    keep = (_dot(sel_ref[...], ex_ref[0]) > 0.5) & (kpos <= qpos)
    m, l, acc = _online_chunk(m_ref[...], l_ref[...], acc_ref[...], _dot_nt(qb, ks), keep, vs)
    m_ref[...] = m
    l_ref[...] = l
    acc_ref[...] = acc

    @pl.when(g == pl.num_programs(1) - 1)
    def _():
        def new_rows(c0):
            return jnp.concatenate([kvn_ref[0, :, c0:c0 + LANES], jnp.zeros((LANES - T, LANES), f32)],
                                   axis=0).astype(bf16)

        npos = past + lax.broadcasted_iota(jnp.int32, (1, LANES), 1)
        is_new = npos < past + T
        keep_t = is_new & (npos <= qpos) & (sel_ref[:, n_past:n_past + 1].astype(f32) > 0.5)
        _, l_s, acc_s = _online_chunk(m, l, acc, _dot_nt(qb, new_rows(BRANCH_W)), keep_t, new_rows(BRANCH_W + LANES))
        o_s = acc_s / l_s

        nb = win_ref.shape[1]
        bpos = past - nb + lax.broadcasted_iota(jnp.int32, (1, nb), 1)
        keep_b = (bpos <= qpos) & (bpos > qpos - WINDOW) & (bpos >= 0)
        mw, lw, aw = _online_chunk(jnp.full((R, 1), MASK_NEG, f32), jnp.zeros((R, 1), f32), jnp.zeros((R, LANES), f32),
                                   _dot_nt(qb, win_ref[0, :, 0:LANES].astype(bf16)), keep_b,
                                   win_ref[0, :, LANES:2 * LANES].astype(bf16))
        keep_n = is_new & (npos <= qpos) & (npos > qpos - WINDOW)
        _, lw, aw = _online_chunk(mw, lw, aw, _dot_nt(qb, new_rows(2 * BRANCH_W)), keep_n,
                                  new_rows(2 * BRANCH_W + LANES))
        o_w = aw / lw

        gates = jax.nn.sigmoid(gl_ref[0])
        gcol = lambda br: jnp.concatenate([gates[:, br * N_HEADS + h:br * N_HEADS + h + 1] for h in range(N_HEADS)],
                                          axis=0)
        o = gcol(0) * oc_ref[...] + gcol(1) * o_s + gcol(2) * o_w
        for mt in range(N_HEADS // 2):
            halves = []
            for h in (2 * mt, 2 * mt + 1):
                piece = o[h * T:(h + 1) * T]
                if h % 2 != h // G:
                    piece = pltpu.roll(piece, N_HEAD_DIM, axis=1)
                halves.append(piece)
            gn = gn_ref[0, :, mt * LANES:(mt + 1) * LANES]
            o_ref[0, :, mt * LANES:(mt + 1) * LANES] = jnp.where(lo_half, halves[0], halves[1]) * (gn * jax.nn.sigmoid(gn))


def nsa_sample(q, kv_new, kc, gl, gate_n, cache, page_table, win_buf):
    B, T, _ = q.shape
    n_pages = page_table.shape[1]
    past = n_pages * PAGE_SIZE
    PG = SAMPLE_PG
    KC = PG * PAGE_SIZE
    ng = n_pages // PG
    assert n_pages % PG == 0 and past % SEL_BLOCK == 0 and T <= SEL_BLOCK and T <= LANES
    assert (past + T) // CMP_STRIDE == past // CMP_STRIDE and KC % SEL_BLOCK == 0
    ncp = past // CMP_STRIDE
    n_sel = past // SEL_BLOCK + 1
    nsp = -(-n_sel // LANES) * LANES
    cs = np.arange(ncp) * CMP_STRIDE
    ss = np.arange(nsp) * SEL_BLOCK
    overlap = (cs[:, None] < ss[None, :] + SEL_BLOCK) & (cs[:, None] + CMP_BLOCK > ss[None, :])
    overlap[ncp - 1] = False
    overlap[:, n_sel:] = False
    overlap = jnp.asarray(overlap, jnp.bfloat16)
    expand = jnp.asarray((np.arange(past) // SEL_BLOCK).reshape(ng, 1, KC) == np.arange(nsp)[None, :, None],
                         jnp.bfloat16)
    R = N_HEADS * T
    per_b = lambda shape: pl.BlockSpec((1,) + shape, lambda b, g, pt: (b, 0, 0))
    return pl.pallas_call(
        functools.partial(_sample_attend_body, past=past),
        grid_spec=pltpu.PrefetchScalarGridSpec(
            num_scalar_prefetch=1, grid=(B, ng),
            in_specs=[_page_spec(k, 1) for k in range(PG)] + [
                per_b((ncp, BRANCH_W)), per_b((T, N_WIDTH)), per_b((T, KV_SLOTS * N_KV_HEADS * N_HEAD_DIM)),
                per_b((T, N_BRANCH * N_HEADS)), per_b((T, N_WIDTH)), per_b((win_buf.shape[1], BRANCH_W)),
                pl.BlockSpec((1, nsp, KC), lambda b, g, pt: (g, 0, 0)),
                pl.BlockSpec((ncp, nsp), lambda b, g, pt: (0, 0))],
            out_specs=per_b((T, N_WIDTH)),
            scratch_shapes=[pltpu.VMEM((R, LANES), jnp.float32), pltpu.VMEM((R, nsp), jnp.bfloat16),
                            pltpu.VMEM((R, 1), jnp.float32), pltpu.VMEM((R, 1), jnp.float32),
                            pltpu.VMEM((R, LANES), jnp.float32), pltpu.VMEM((R, LANES), jnp.float32)]),
        out_shape=jax.ShapeDtypeStruct((B, T, N_WIDTH), jnp.float32),
        compiler_params=pltpu.CompilerParams(dimension_semantics=("parallel", "arbitrary"),
                                             vmem_limit_bytes=VMEM_LIMIT),
        name="nsa_sample",
    )(page_table, *([cache] * PG), kc, q, kv_new, gl, gate_n, win_buf, expand, overlap)


RWKV_TC = 32
RWKV_ROWS = 8


def _rwkv_scan_body(r_ref, w_ref, k_ref, v_ref, kk_ref, b_ref, s0_ref, y_ref, s_ref):
    n = s_ref.shape[0]

    @pl.when(pl.program_id(1) == 0)
    def _():
        s_ref[...] = s0_ref[...]

    def step(t, carry):
        r, w, k, kk, b = r_ref[t], w_ref[t], k_ref[t], kk_ref[t], b_ref[t]

        def rows(g, carry):
            for u in range(RWKV_ROWS):
                i = g * RWKV_ROWS + u
                s_i = s_ref[i]
                sa = jnp.sum(s_i * kk, axis=0, keepdims=True)
                s_i = s_i * w - sa * b + v_ref[t, pl.ds(i, 1), :] * k
                s_ref[i] = s_i
                y_ref[t, pl.ds(i, 1), :] = jnp.sum(s_i * r, axis=0, keepdims=True)
            return carry

        return lax.fori_loop(0, n // RWKV_ROWS, rows, carry)

    lax.fori_loop(0, r_ref.shape[0], step, 0)


def rwkv_scan(r, w, k, v, kk, b, s0):
    B, T, H, N = r.shape
    BH = B * H
    assert BH % LANES == 0 and N % RWKV_ROWS == 0
    tc = min(RWKV_TC, T)
    assert T % tc == 0

    def lanes_last(x):
        return x.transpose(1, 3, 0, 2).reshape(T, N, BH)

    seq = pl.BlockSpec((tc, N, LANES), lambda l, t: (t, 0, l))
    state = pl.BlockSpec((N, N, LANES), lambda l, t: (0, 0, l))
    y, s_fin = pl.pallas_call(
        _rwkv_scan_body,
        grid=(BH // LANES, T // tc),
        in_specs=[seq] * 6 + [state],
        out_specs=[seq, state],
        out_shape=[jax.ShapeDtypeStruct((T, N, BH), jnp.float32), jax.ShapeDtypeStruct((N, N, BH), jnp.float32)],
        compiler_params=pltpu.CompilerParams(dimension_semantics=("parallel", "arbitrary"),
                                             vmem_limit_bytes=VMEM_LIMIT),
        name="rwkv_scan",
    )(*(lanes_last(x) for x in (r, w, k, v, kk, b)), s0.transpose(2, 3, 0, 1).reshape(N, N, BH))
    return (y.reshape(T, N, B, H).transpose(2, 0, 3, 1),
            s_fin.reshape(N, N, B, H).transpose(2, 3, 0, 1))


def _final_norm_body(x_ref, g_ref, o_ref):
    x = x_ref[...]
    o_ref[...] = x * lax.rsqrt(jnp.mean(x * x, axis=-1, keepdims=True) + RMS_EPS) * g_ref[...]


def final_norm(x, g):
    B, T, D = x.shape
    x2 = x.reshape(B * T, D)
    tm = min(512, B * T)
    out = pl.pallas_call(
        _final_norm_body,
        grid=(B * T // tm,),
        in_specs=[pl.BlockSpec((tm, D), lambda i: (i, 0)), pl.BlockSpec((1, D), lambda i: (0, 0))],
        out_specs=pl.BlockSpec((tm, D), lambda i: (i, 0)),
        out_shape=jax.ShapeDtypeStruct((B * T, D), x.dtype),
        name="final_norm",
    )(x2, g.reshape(1, D))
    return out.reshape(B, T, D)


def kernel(x_prompt, x_sample, cache_kv, cache_kv_win, state_wkv, state_shift, page_table,
           norm_g, w_in, mu_shift, w0, w_decay_up, a0, w_aaa_up, k_k, k_a, r_k, gn_w, gn_b,
           w_cmp_pos, w_cmp_mix, w_out, final_g):
    y_p, y_s = x_prompt, x_sample
    kvp, kvs, wnp, wns, skp, sks, shp, shs = [], [], [], [], [], [], [], []
    for layer in range(DEPTH):
        lw = (norm_g[layer], w_in[layer], mu_shift[layer], w0[layer], w_decay_up[layer], a0[layer],
              w_aaa_up[layer], k_k[layer], k_a[layer], r_k[layer], gn_w[layer], gn_b[layer],
              w_cmp_pos[layer], w_cmp_mix[layer], w_out[layer])
        y_p, p_kv, p_win, p_wkv, p_shift = prompt_layer(y_p, *lw)
        y_s, s_kv, s_win, s_wkv, s_shift = sample_layer(y_s, cache_kv[layer], cache_kv_win[layer],
                                                        state_wkv[layer], state_shift[layer], page_table, *lw)
        kvp.append(p_kv); kvs.append(s_kv); wnp.append(p_win); wns.append(s_win)
        skp.append(p_wkv); sks.append(s_wkv); shp.append(p_shift); shs.append(s_shift)
    y_prompt = final_norm(y_p, final_g)
    y_sample = final_norm(y_s, final_g)
    return (y_prompt, y_sample, jnp.stack(kvp), jnp.stack(kvs), jnp.stack(wnp), jnp.stack(wns),
            jnp.stack(skp), jnp.stack(sks), jnp.stack(shp), jnp.stack(shs))
```

```python
import jax, jax.numpy as jnp
from jax import lax
import functools
import numpy as np
from jax.experimental import pallas as pl
from jax.experimental.pallas import tpu as pltpu

D_MODEL = 1024
BATCH = 16
SEQ = 2048
DEPTH = 1
DEC_BATCH = 32
DEC_SEQ = 4
PAST_LEN = 16384
PAGE_SIZE = 128

R_HEAD_DIM = 64
R_WIDTH = D_MODEL // 2
R_HEADS = R_WIDTH // R_HEAD_DIM
DECAY_LORA = 64
AAA_LORA = 64
R_SHIFT_WIDTH = 3 * R_WIDTH + DECAY_LORA + AAA_LORA
N_HEAD_DIM = 64
N_WIDTH = D_MODEL - R_WIDTH
N_HEADS = N_WIDTH // N_HEAD_DIM
N_KV_HEADS = 2
N_GROUP = N_HEADS // N_KV_HEADS
N_BRANCH = 3
KV_SLOTS = 2 * N_BRANCH
CMP_BLOCK = 32
CMP_STRIDE = 16
SEL_BLOCK = 64
SEL_TOPK = 16
WINDOW = 512
SEL_QUERY_BLOCK = 32
WIN_QUERY_BLOCK = 128
MIX_WIDTH = R_WIDTH + N_WIDTH
PROJ_SPLITS = (R_SHIFT_WIDTH, R_WIDTH, N_WIDTH, N_WIDTH, KV_SLOTS * N_KV_HEADS * N_HEAD_DIM, N_BRANCH * N_HEADS)
PROJ_WIDTH = sum(PROJ_SPLITS)
RMS_EPS = 1e-6
GN_EPS = 64e-5
MASK_NEG = -1e30
SEL_BIAS = 1e4
ATTN_SCALE = N_HEAD_DIM ** -0.5


def rms_norm(x, g):
    xf = x.astype(jnp.float32)
    y = xf * lax.rsqrt(jnp.mean(xf * xf, axis=-1, keepdims=True) + RMS_EPS)
    return (y * g.astype(jnp.float32)).astype(x.dtype)


def masked_softmax(s, mask):
    s = jnp.where(mask, s.astype(jnp.float32), MASK_NEG)
    p = jax.nn.softmax(s, axis=-1)
    return jnp.where(mask, p, 0.0)


def split_proj(p):
    offs = np.cumsum((0,) + PROJ_SPLITS)
    return tuple(p[..., int(offs[i]):int(offs[i + 1])] for i in range(len(PROJ_SPLITS)))


def to_heads(q):
    B, T, _ = q.shape
    return q.reshape(B, T, N_KV_HEADS, N_GROUP, N_HEAD_DIM).transpose(0, 2, 3, 1, 4)


def rwkv_mix(z, z_prev, s0, gate, mu_shift, w0, w_decay_up, a0, w_aaa_up, k_k, k_a, r_k, gn_w, gn_b):
    B, T, _ = z.shape
    f32 = jnp.float32
    zp = jnp.concatenate([z_prev[:, None].astype(z.dtype), z[:, :-1]], axis=1)
    zs = z + (zp - z) * mu_shift
    r, k, v, wd, ad = jnp.split(zs, [R_WIDTH, 2 * R_WIDTH, 3 * R_WIDTH, 3 * R_WIDTH + DECAY_LORA], axis=-1)
    w_log = -jax.nn.softplus(-(w0 + jnp.tanh(wd) @ w_decay_up).astype(f32)) - 0.5
    decay = jnp.exp(-jnp.exp(w_log))
    a = jax.nn.sigmoid((a0 + ad @ w_aaa_up).astype(f32))
    hs = (B, T, R_HEADS, R_HEAD_DIM)
    r, k, v, decay, a = (t.astype(f32).reshape(hs) for t in (r, k, v, decay, a))
    hshape = (R_HEADS, R_HEAD_DIM)
    kk = k * k_k.astype(f32).reshape(hshape)
    kk = kk * lax.rsqrt(jnp.maximum(jnp.sum(kk * kk, axis=-1, keepdims=True), 1e-24))
    k = k * (1.0 + (a - 1.0) * k_a.astype(f32).reshape(hshape))

    y, s_fin = rwkv_scan(r, decay, k, v, kk, kk * a, s0.astype(f32))
    yc = y - jnp.mean(y, axis=-1, keepdims=True)
    y = yc * lax.rsqrt(jnp.mean(yc * yc, axis=-1, keepdims=True) + GN_EPS)
    y = y * gn_w.astype(f32).reshape(hshape) + gn_b.astype(f32).reshape(hshape)
    y = y + jnp.sum(r * k * r_k.astype(f32).reshape(hshape), axis=-1, keepdims=True) * v
    out = y.reshape(B, T, R_WIDTH).astype(z.dtype) * jax.nn.silu(gate)
    return out, s_fin, z[:, -1]


def compress(kv, w_pos, w_mix):
    B, L = kv.shape[:2]
    n_sub = CMP_BLOCK // CMP_STRIDE
    n_chunk = L // CMP_STRIDE
    n_cmp = n_chunk - n_sub + 1
    c = kv[:, :n_chunk * CMP_STRIDE].reshape(B, n_chunk, CMP_STRIDE, 2, N_KV_HEADS, N_HEAD_DIM)
    wp = w_pos.reshape(2, n_sub, CMP_STRIDE, N_HEAD_DIM)
    pooled = jnp.einsum('bjpekd,epd->bjekd', c[:, 0:n_cmp], wp[:, 0])
    for m in range(1, n_sub):
        pooled = pooled + jnp.einsum('bjpekd,epd->bjekd', c[:, m:m + n_cmp], wp[:, m])
    return jnp.einsum('bjekd,edf->bjekf', pooled, w_mix)


def compressed_attn(q, kc, qpos):
    n_cmp = kc.shape[1]
    kend = jnp.arange(n_cmp) * CMP_STRIDE + (CMP_BLOCK - 1)
    mask = kend[None, :] <= qpos[:, None]
    s = jnp.einsum('bkgtd,bjkd->bkgtj', q, kc[:, :, 0]) * ATTN_SCALE
    p = masked_softmax(s, mask)
    o = jnp.einsum('bkgtj,bjkd->bkgtd', p.astype(q.dtype), kc[:, :, 1])
    return o, jnp.sum(p, axis=2)


def select_blocks(imp, qpos, total_len):
    n_cmp = imp.shape[-1]
    n_sel = -(-total_len // SEL_BLOCK)
    cs = jnp.arange(n_cmp) * CMP_STRIDE
    ss = jnp.arange(n_sel) * SEL_BLOCK
    overlap = ((cs[:, None] < ss[None, :] + SEL_BLOCK) & (cs[:, None] + CMP_BLOCK > ss[None, :])).astype(jnp.float32)
    score = jnp.einsum('bktj,js->bkts', imp, overlap)
    cur = (qpos // SEL_BLOCK)[:, None]
    sid = jnp.arange(n_sel)[None, :]
    valid = sid <= cur
    forced = (sid == 0) | (sid == cur) | (sid == cur - 1)
    score = jnp.where(valid, score + jnp.where(forced, SEL_BIAS, 0.0), -SEL_BIAS)
    vals, idx = lax.top_k(score, min(SEL_TOPK, n_sel))
    return idx, vals > -0.5 * SEL_BIAS


def selected_attn(q, ks, vs, idx, valid, qpos):
    B, KV, T, K = idx.shape
    kpos = idx[..., None] * SEL_BLOCK + jnp.arange(SEL_BLOCK)
    mask = (valid[..., None] & (kpos <= qpos[:, None, None])).reshape(B, KV, 1, T, K * SEL_BLOCK)
    ks = ks.reshape(B, KV, T, K * SEL_BLOCK, N_HEAD_DIM)
    vs = vs.reshape(B, KV, T, K * SEL_BLOCK, N_HEAD_DIM)
    s = jnp.einsum('bkgtd,bktnd->bkgtn', q, ks) * ATTN_SCALE
    p = masked_softmax(s, mask)
    return jnp.einsum('bkgtn,bktnd->bkgtd', p.astype(q.dtype), vs)


def selected_prompt_attn(q, kv_sel, idx, valid, qpos):
    B, T = kv_sel.shape[:2]
    n_sel = T // SEL_BLOCK
    blocks = kv_sel.reshape(B, n_sel, SEL_BLOCK, 2, N_KV_HEADS, N_HEAD_DIM).transpose(0, 4, 1, 2, 3, 5)
    bk, bv = blocks[..., 0, :], blocks[..., 1, :]
    bi = jnp.arange(B)[:, None, None, None]
    gi = jnp.arange(N_KV_HEADS)[None, :, None, None]
    nq = T // SEL_QUERY_BLOCK

    def split(t, ax):
        return jnp.moveaxis(t.reshape(t.shape[:ax] + (nq, SEL_QUERY_BLOCK) + t.shape[ax + 1:]), ax, 0)

    def chunk(args):
        qc, ic, vc, pc = args
        return selected_attn(qc, bk[bi, gi, ic], bv[bi, gi, ic], ic, vc, pc)

    out = lax.map(chunk, (split(q, 3), split(idx, 2), split(valid, 2), qpos.reshape(nq, SEL_QUERY_BLOCK)))
    return jnp.moveaxis(out, 0, 3).reshape(B, N_KV_HEADS, N_GROUP, T, N_HEAD_DIM)


def selected_sample_attn(q, kv_new, cache, page_table, idx, valid, qpos):
    B, T = kv_new.shape[:2]
    bpp = PAGE_SIZE // SEL_BLOCK
    n_past = page_table.shape[1] * bpp
    pool = cache.reshape(cache.shape[0] * bpp, SEL_BLOCK, cache.shape[2], N_KV_HEADS, N_HEAD_DIM)
    n_tail = -(-T // SEL_BLOCK)
    tail = jnp.pad(kv_new, ((0, 0), (0, n_tail * SEL_BLOCK - T), (0, 0), (0, 0), (0, 0)))
    tail = tail.reshape(B, n_tail, SEL_BLOCK, 2, N_KV_HEADS, N_HEAD_DIM).transpose(0, 4, 1, 2, 3, 5)
    bi = jnp.arange(B)[:, None, None, None]
    gi = jnp.arange(N_KV_HEADS)[None, :, None, None]
    ip = jnp.minimum(idx, n_past - 1)
    phys = page_table[bi, ip // bpp] * bpp + ip % bpp
    past_rows = pool[phys, :, 2:4, gi]
    tail_rows = tail[bi, gi, jnp.clip(idx - n_past, 0, n_tail - 1)]
    rows = jnp.where((idx < n_past)[..., None, None, None], past_rows.astype(tail_rows.dtype), tail_rows)
    return selected_attn(q, rows[..., 0, :], rows[..., 1, :], idx, valid, qpos)


def window_attn(q, kw, qpos, kpos):
    mask = (kpos[None, :] <= qpos[:, None]) & (kpos[None, :] > qpos[:, None] - WINDOW) & (kpos[None, :] >= 0)
    s = jnp.einsum('bkgtd,bskd->bkgts', q, kw[:, :, 0]) * ATTN_SCALE
    p = masked_softmax(s, mask)
    return jnp.einsum('bkgts,bskd->bkgtd', p.astype(q.dtype), kw[:, :, 1])


def window_prompt_attn(q, kv_win):
    B, T = kv_win.shape[:2]
    padded = jnp.pad(kv_win, ((0, 0), (WINDOW, 0), (0, 0), (0, 0), (0, 0)))
    nq = T // WIN_QUERY_BLOCK

    def chunk(i):
        start = i * WIN_QUERY_BLOCK
        kw = lax.dynamic_slice_in_dim(padded, start, WINDOW + WIN_QUERY_BLOCK, axis=1)
        qc = lax.dynamic_slice_in_dim(q, start, WIN_QUERY_BLOCK, axis=3)
        qpos = start + jnp.arange(WIN_QUERY_BLOCK)
        kpos = start - WINDOW + jnp.arange(WINDOW + WIN_QUERY_BLOCK)
        return window_attn(qc, kw, qpos, kpos)

    out = lax.map(chunk, jnp.arange(nq))
    return jnp.moveaxis(out, 0, 3).reshape(B, N_KV_HEADS, N_GROUP, T, N_HEAD_DIM)


def mixer_out(y_r, o_c, o_s, o_w, gl, gate_n, w_out):
    B, T, _ = y_r.shape
    g = jax.nn.sigmoid(gl.astype(jnp.float32)).reshape(B, T, N_BRANCH, N_KV_HEADS, N_GROUP)
    g = g.transpose(2, 0, 3, 4, 1)[..., None]
    o = g[0] * o_c + g[1] * o_s + g[2] * o_w
    o = o.transpose(0, 3, 1, 2, 4).reshape(B, T, N_WIDTH).astype(y_r.dtype) * jax.nn.silu(gate_n)
    return jnp.concatenate([y_r, o], axis=-1) @ w_out


def prompt_layer(x, norm_g, w_in, mu_shift, w0, w_decay_up, a0, w_aaa_up, k_k, k_a, r_k, gn_w, gn_b,
                 w_cmp_pos, w_cmp_mix, w_out):
    B, T, _ = x.shape
    zr, gate_r, q, gate_n, kv, gl = split_proj(rms_norm(x, norm_g) @ w_in)
    y_r, wkv, shift = rwkv_mix(zr, jnp.zeros((B, R_SHIFT_WIDTH), zr.dtype),
                               jnp.zeros((B, R_HEADS, R_HEAD_DIM, R_HEAD_DIM), jnp.float32), gate_r,
                               mu_shift, w0, w_decay_up, a0, w_aaa_up, k_k, k_a, r_k, gn_w, gn_b)
    o_n = nsa_prompt(q, kv, compress_prompt(kv, w_cmp_pos, w_cmp_mix), gl, gate_n)
    y = x + jnp.concatenate([y_r, o_n], axis=-1) @ w_out
    kv = kv.reshape(B, T, KV_SLOTS, N_KV_HEADS, N_HEAD_DIM)
    return y, kv[:, :, 0:4], kv[:, T - min(WINDOW, T):, 4:6], wkv, shift


def sample_layer(x, cache, win_buf, wkv0, shift0, page_table, norm_g, w_in, mu_shift, w0, w_decay_up, a0,
                 w_aaa_up, k_k, k_a, r_k, gn_w, gn_b, w_cmp_pos, w_cmp_mix, w_out):
    B, T, _ = x.shape
    past = page_table.shape[1] * PAGE_SIZE
    zr, gate_r, q, gate_n, kv, gl = split_proj(rms_norm(x, norm_g) @ w_in)
    y_r, wkv, shift = rwkv_mix(zr, shift0, wkv0, gate_r,
                               mu_shift, w0, w_decay_up, a0, w_aaa_up, k_k, k_a, r_k, gn_w, gn_b)
    cache_t = cache.reshape(cache.shape[0], PAGE_SIZE, 2 * BRANCH_W).transpose(0, 2, 1)
    nb = win_buf.shape[1]
    win_t = win_buf.reshape(B, nb, BRANCH_W).transpose(0, 2, 1)
    kc_a, kc_b = sample_compress(cache_t, page_table, w_cmp_pos, w_cmp_mix)
    o_n = nsa_sample(q, kv, kc_a, kc_b, gl, gate_n, cache_t, page_table, win_t)
    y = x + jnp.concatenate([y_r, o_n], axis=-1) @ w_out
    kv = kv.reshape(B, T, KV_SLOTS, N_KV_HEADS, N_HEAD_DIM)
    keys_w = jnp.concatenate([win_buf.astype(kv.dtype), kv[:, :, 4:6]], axis=1)
    n_keep = min(WINDOW, nb + T)
    return y, kv[:, :, 0:4], keys_w[:, nb + T - n_keep:], wkv, shift


LANES = 128
NSA_TQ = 128
NSA_KC = 512
VMEM_LIMIT = 48 * 1024 * 1024


def _split3(x):
    hi = x.astype(jnp.bfloat16)
    r1 = x - hi.astype(jnp.float32)
    mid = r1.astype(jnp.bfloat16)
    lo = (r1 - mid.astype(jnp.float32)).astype(jnp.bfloat16)
    return hi, mid, lo


def _dot(a, b):
    return jnp.dot(a, b, preferred_element_type=jnp.float32)


def _dot_nt(a, b):
    return lax.dot_general(a, b, (((1,), (1,)), ((), ())), preferred_element_type=jnp.float32)


def _dot_exact_rhs(a, b01):
    hi, mid, lo = _split3(a)
    return _dot(hi, b01) + _dot(mid, b01) + _dot(lo, b01)


def _dot_x3(a, b):
    ah = a.astype(jnp.bfloat16)
    al = (a - ah.astype(jnp.float32)).astype(jnp.bfloat16)
    bh = b.astype(jnp.bfloat16)
    bl = (b - bh.astype(jnp.float32)).astype(jnp.bfloat16)
    return _dot(ah, bh) + _dot(ah, bl) + _dot(al, bh)


def _dot_nt_x3(a, b):
    ah = a.astype(jnp.bfloat16)
    al = (a - ah.astype(jnp.float32)).astype(jnp.bfloat16)
    bh = b.astype(jnp.bfloat16)
    bl = (b - bh.astype(jnp.float32)).astype(jnp.bfloat16)
    return _dot_nt(ah, bh) + _dot_nt(ah, bl) + _dot_nt(al, bh)


def _compress_prompt_body(kv_ref, wa_ref, wb_ref, pool_ref, wmix_ref, o_ref):
    x = kv_ref[0]
    xw = jnp.concatenate([x * wa_ref[...], x * wb_ref[...]], axis=0)
    pooled = _dot_exact_rhs_lhs01(pool_ref[...], xw)
    o_ref[0] = _dot_x3(pooled, wmix_ref[...])


def _dot_exact_rhs_lhs01(a01, b):
    hi, mid, lo = _split3(b)
    return _dot(a01, hi) + _dot(a01, mid) + _dot(a01, lo)


def compress_prompt(kv, w_pos, w_mix):
    B, T, _ = kv.shape
    n_sub = CMP_BLOCK // CMP_STRIDE
    assert n_sub == 2 and T % CMP_STRIDE == 0
    ncp = T // CMP_STRIDE
    cw = 2 * N_KV_HEADS * N_HEAD_DIM
    wp = jnp.broadcast_to(w_pos.reshape(2, 1, CMP_BLOCK, N_HEAD_DIM).transpose(2, 0, 1, 3),
                          (CMP_BLOCK, 2, N_KV_HEADS, N_HEAD_DIM)).reshape(CMP_BLOCK, cw)
    wa = jnp.tile(wp[:CMP_STRIDE], (ncp, 1))
    wb = jnp.tile(wp[CMP_STRIDE:], (ncp, 1))
    chunk = np.arange(T) // CMP_STRIDE
    pool = np.concatenate([chunk[None, :] == np.arange(ncp)[:, None],
                           chunk[None, :] == np.arange(ncp)[:, None] + 1], axis=1)
    pool = jnp.asarray(pool, jnp.bfloat16)
    wmix = jnp.zeros((cw, cw), jnp.float32)
    for e in range(2):
        for kh in range(N_KV_HEADS):
            o = (e * N_KV_HEADS + kh) * N_HEAD_DIM
            wmix = wmix.at[o:o + N_HEAD_DIM, o:o + N_HEAD_DIM].set(w_mix[e])
    return pl.pallas_call(
        _compress_prompt_body,
        grid=(B,),
        in_specs=[pl.BlockSpec((1, T, cw), lambda b: (b, 0, 0)),
                  pl.BlockSpec((T, cw), lambda b: (0, 0)),
                  pl.BlockSpec((T, cw), lambda b: (0, 0)),
                  pl.BlockSpec((ncp, 2 * T), lambda b: (0, 0)),
                  pl.BlockSpec((cw, cw), lambda b: (0, 0))],
        out_specs=pl.BlockSpec((1, ncp, cw), lambda b: (b, 0, 0)),
        out_shape=jax.ShapeDtypeStruct((B, ncp, cw), jnp.float32),
        compiler_params=pltpu.CompilerParams(dimension_semantics=("parallel",), vmem_limit_bytes=VMEM_LIMIT),
        name="compress_prompt",
    )(kv, wa, wb, pool, wmix)


def _masked_softmax_rows(s, mask):
    s = jnp.where(mask, s, MASK_NEG)
    e = jnp.exp(s - jnp.max(s, axis=-1, keepdims=True))
    p = e / jnp.sum(e, axis=-1, keepdims=True)
    return jnp.where(mask, p, 0.0)


def _nsa_prompt_body(q_ref, ksel_ref, kwin_ref, kc_ref, gl_ref, gn_ref, ovt_ref, ex_ref, o_ref):
    f32, bf16 = jnp.float32, jnp.bfloat16
    TQ, KC, G = NSA_TQ, NSA_KC, N_GROUP
    T = ksel_ref.shape[1]
    ncp = kc_ref.shape[1]
    n_sel = ovt_ref.shape[0]
    topk = min(SEL_TOPK, n_sel)
    qs = pl.program_id(1) * TQ
    qpos = qs + lax.broadcasted_iota(jnp.int32, (TQ, 1), 0)
    lo_half = lax.broadcasted_iota(jnp.int32, (TQ, LANES), 1) < N_HEAD_DIM

    def in_half(x, half):
        return jnp.where(lo_half if half == 0 else jnp.logical_not(lo_half), x, 0.0)

    def q_padded(h, half):
        tile = q_ref[0, :, (h // 2) * LANES:(h // 2 + 1) * LANES] * ATTN_SCALE
        if h % 2 != half:
            tile = pltpu.roll(tile, N_HEAD_DIM, axis=1)
        return in_half(tile, half)

    wstart = pl.multiple_of(jnp.maximum(qs - WINDOW, 0), TQ)
    kw = kwin_ref[0, pl.ds(wstart, WINDOW + TQ), 0:LANES].astype(bf16)
    vw = kwin_ref[0, pl.ds(wstart, WINDOW + TQ), LANES:2 * LANES].astype(bf16)
    kposw = wstart + lax.broadcasted_iota(jnp.int32, (1, WINDOW + TQ), 1)
    mask_w = (kposw <= qpos) & (kposw > qpos - WINDOW)

    kck = kc_ref[0, :, 0:LANES]
    kcv = kc_ref[0, :, LANES:2 * LANES].astype(bf16)
    kend = lax.broadcasted_iota(jnp.int32, (1, ncp), 1) * CMP_STRIDE + (CMP_BLOCK - 1)
    mask_c = kend <= qpos

    sid_t = lax.broadcasted_iota(jnp.int32, (n_sel, 1), 0)
    cur_t = (qs + lax.broadcasted_iota(jnp.int32, (1, TQ), 1)) // SEL_BLOCK
    valid_t = sid_t <= cur_t
    forced_t = (sid_t == 0) | (sid_t == cur_t) | (sid_t == cur_t - 1)
    n_chunks = qs // KC + 1

    gates = jax.nn.sigmoid(gl_ref[0])
    heads = [None] * N_HEADS
    for kvh in range(N_KV_HEADS):
        qg = jnp.concatenate([q_padded(kvh * G + g, kvh) for g in range(G)], axis=0)
        qb = qg.astype(bf16)

        s = _dot_nt_x3(qg, kck).reshape(G, TQ, ncp)
        p = _masked_softmax_rows(s, mask_c[None])
        o_c = _dot(p.reshape(G * TQ, ncp).astype(bf16), kcv)
        imp = jnp.sum(p, axis=0)

        hi, mid, lo = _split3(imp)
        ovt = ovt_ref[...]
        score = _dot_nt(ovt, hi) + _dot_nt(ovt, mid) + _dot_nt(ovt, lo)
        score = jnp.where(valid_t, score + jnp.where(forced_t, SEL_BIAS, 0.0), -SEL_BIAS)
        rank = jnp.zeros((n_sel, TQ), jnp.int32)
        for s2 in range(n_sel):
            row = score[s2:s2 + 1, :]
            ahead = (row > score) | ((row == score) & (s2 < sid_t))
            rank = rank + ahead.astype(jnp.int32)
        sel = jnp.where((rank < topk) & valid_t, 1.0, 0.0).T.astype(bf16)

        def chunk(c, carry, qb=qb, sel=sel):
            m, l, acc = carry
            k0 = pl.multiple_of(c * KC, KC)
            ks = ksel_ref[0, pl.ds(k0, KC), 0:LANES].astype(bf16)
            vs = ksel_ref[0, pl.ds(k0, KC), LANES:2 * LANES].astype(bf16)
            selx = _dot(sel, ex_ref[:, pl.ds(k0, KC)])
            kpos = k0 + lax.broadcasted_iota(jnp.int32, (1, KC), 1)
            keep = (selx > 0.5) & (kpos <= qpos)
            sc = jnp.where(keep[None], _dot_nt(qb, ks).reshape(G, TQ, KC), MASK_NEG)
            m_new = jnp.maximum(m, jnp.max(sc, axis=-1, keepdims=True))
            alpha = jnp.exp(m - m_new)
            e = jnp.exp(sc - m_new)
            l = alpha * l + jnp.sum(e, axis=-1, keepdims=True)
            pv = _dot(e.reshape(G * TQ, KC).astype(bf16), vs).reshape(G, TQ, LANES)
            return m_new, l, alpha * acc + pv

        m0 = jnp.full((G, TQ, 1), MASK_NEG, f32)
        l0 = jnp.zeros((G, TQ, 1), f32)
        a0 = jnp.zeros((G, TQ, LANES), f32)
        _, l, acc = lax.fori_loop(0, n_chunks, chunk, (m0, l0, a0))
        o_s = acc / l

        pw = _masked_softmax_rows(_dot_nt(qb, kw).reshape(G, TQ, WINDOW + TQ), mask_w[None])
        o_w = _dot(pw.reshape(G * TQ, WINDOW + TQ).astype(bf16), vw)

        o_c = o_c.reshape(G, TQ, LANES)
        o_w = o_w.reshape(G, TQ, LANES)
        for g in range(G):
            h = kvh * G + g
            o = (gates[:, h:h + 1] * o_c[g] + gates[:, N_HEADS + h:N_HEADS + h + 1] * o_s[g]
                 + gates[:, 2 * N_HEADS + h:2 * N_HEADS + h + 1] * o_w[g])
            if h % 2 != kvh:
                o = pltpu.roll(o, N_HEAD_DIM, axis=1)
            heads[h] = o

    for m in range(N_HEADS // 2):
        tile = jnp.where(lo_half, heads[2 * m], heads[2 * m + 1])
        gn = gn_ref[0, :, m * LANES:(m + 1) * LANES]
        o_ref[0, :, m * LANES:(m + 1) * LANES] = tile * (gn * jax.nn.sigmoid(gn))


def nsa_prompt(q, kv, kc, gl, gate_n):
    B, T, _ = q.shape
    TQ, KC = NSA_TQ, NSA_KC
    assert N_KV_HEADS == 2 and N_HEAD_DIM * 2 == LANES and T % KC == 0 and KC % TQ == 0 and T >= WINDOW + TQ
    assert SEL_BLOCK % CMP_STRIDE == 0 and WINDOW % TQ == 0
    ncp = T // CMP_STRIDE
    n_sel = T // SEL_BLOCK
    cs = np.arange(ncp) * CMP_STRIDE
    ss = np.arange(n_sel) * SEL_BLOCK
    overlap = (cs[:, None] < ss[None, :] + SEL_BLOCK) & (cs[:, None] + CMP_BLOCK > ss[None, :])
    overlap[ncp - 1] = False
    overlap_t = jnp.asarray(overlap.T, jnp.bfloat16)
    expand = jnp.asarray(np.arange(T)[None, :] // SEL_BLOCK == np.arange(n_sel)[:, None], jnp.bfloat16)
    kvw = 2 * N_KV_HEADS * N_HEAD_DIM
    return pl.pallas_call(
        _nsa_prompt_body,
        grid=(B, T // TQ),
        in_specs=[pl.BlockSpec((1, TQ, N_WIDTH), lambda b, i: (b, i, 0)),
                  pl.BlockSpec((1, T, kvw), lambda b, i: (b, 0, 1)),
                  pl.BlockSpec((1, T, kvw), lambda b, i: (b, 0, 2)),
                  pl.BlockSpec((1, ncp, kvw), lambda b, i: (b, 0, 0)),
                  pl.BlockSpec((1, TQ, N_BRANCH * N_HEADS), lambda b, i: (b, i, 0)),
                  pl.BlockSpec((1, TQ, N_WIDTH), lambda b, i: (b, i, 0)),
                  pl.BlockSpec((n_sel, ncp), lambda b, i: (0, 0)),
                  pl.BlockSpec((n_sel, T), lambda b, i: (0, 0))],
        out_specs=pl.BlockSpec((1, TQ, N_WIDTH), lambda b, i: (b, i, 0)),
        out_shape=jax.ShapeDtypeStruct((B, T, N_WIDTH), jnp.float32),
        compiler_params=pltpu.CompilerParams(dimension_semantics=("parallel", "arbitrary"),
                                             vmem_limit_bytes=VMEM_LIMIT),
        name="nsa_prompt",
    )(q, kv, kv, kc, gl, gate_n, overlap_t, expand)


SAMPLE_PG = 16
BRANCH_W = 2 * N_KV_HEADS * N_HEAD_DIM


def _page_spec(k, branch):
    return pl.BlockSpec((1, BRANCH_W, PAGE_SIZE), lambda b, g, pt: (pt[b, g * SAMPLE_PG + k], branch, 0))


def _sample_compress_body(pt_ref, *refs):
    pages = refs[:SAMPLE_PG]
    wa_ref, wb_ref, pool_ref, wmix_ref, oa_ref, ob_ref = refs[SAMPLE_PG:]
    x = jnp.concatenate([p[0] for p in pages], axis=1)
    oa_ref[0] = _dot_x3(wmix_ref[...], _dot_exact_rhs(x * wa_ref[...], pool_ref[...]))
    ob_ref[0] = _dot_x3(wmix_ref[...], _dot_exact_rhs(x * wb_ref[...], pool_ref[...]))


def _cmp_weights(w_pos, w_mix, n_rows):
    wp = jnp.broadcast_to(w_pos.reshape(2, 1, CMP_BLOCK, N_HEAD_DIM).transpose(2, 0, 1, 3),
                          (CMP_BLOCK, 2, N_KV_HEADS, N_HEAD_DIM)).reshape(CMP_BLOCK, BRANCH_W)
    wa = jnp.tile(wp[:CMP_STRIDE], (n_rows // CMP_STRIDE, 1))
    wb = jnp.tile(wp[CMP_STRIDE:], (n_rows // CMP_STRIDE, 1))
    wmix = jnp.zeros((BRANCH_W, BRANCH_W), jnp.float32)
    for e in range(2):
        for kh in range(N_KV_HEADS):
            o = (e * N_KV_HEADS + kh) * N_HEAD_DIM
            wmix = wmix.at[o:o + N_HEAD_DIM, o:o + N_HEAD_DIM].set(w_mix[e])
    return wa, wb, wmix


def sample_compress(cache_t, page_table, w_pos, w_mix):
    B, n_pages = page_table.shape
    PG = SAMPLE_PG
    assert n_pages % PG == 0 and CMP_BLOCK == 2 * CMP_STRIDE and PAGE_SIZE % CMP_STRIDE == 0
    cols = PG * PAGE_SIZE // CMP_STRIDE
    assert cols % LANES == 0
    ncp = n_pages * PAGE_SIZE // CMP_STRIDE
    wa, wb, wmix = _cmp_weights(w_pos, w_mix, PG * PAGE_SIZE)
    pool = jnp.asarray(np.arange(PG * PAGE_SIZE)[:, None] // CMP_STRIDE == np.arange(cols)[None, :], jnp.bfloat16)
    const = lambda shape: pl.BlockSpec(shape, lambda b, g, pt: (0,) * len(shape))
    out = pl.BlockSpec((1, BRANCH_W, cols), lambda b, g, pt: (b, 0, g))
    return pl.pallas_call(
        _sample_compress_body,
        grid_spec=pltpu.PrefetchScalarGridSpec(
            num_scalar_prefetch=1, grid=(B, n_pages // PG),
            in_specs=[_page_spec(k, 0) for k in range(PG)] + [const(wa.T.shape), const(wb.T.shape), const(pool.shape),
                                                              const(wmix.shape)],
            out_specs=[out, out]),
        out_shape=[jax.ShapeDtypeStruct((B, BRANCH_W, ncp), jnp.float32)] * 2,
        compiler_params=pltpu.CompilerParams(dimension_semantics=("parallel", "arbitrary"),
                                             vmem_limit_bytes=VMEM_LIMIT),
        name="sample_compress",
    )(page_table, *([cache_t] * PG), wa.T, wb.T, pool, wmix.T)


def _online_chunk(m, l, acc, s, keep, v, v_channels_major=False):
    s = jnp.where(keep, s, MASK_NEG)
    m_new = jnp.maximum(m, jnp.max(s, axis=-1, keepdims=True))
    alpha = jnp.exp(m - m_new)
    e = jnp.where(keep, jnp.exp(s - m_new), 0.0)
    eb = e.astype(jnp.bfloat16)
    pv = _dot_nt(eb, v) if v_channels_major else _dot(eb, v)
    return m_new, alpha * l + jnp.sum(e, axis=-1, keepdims=True), alpha * acc + pv


def _sample_attend_body(pt_ref, *refs, past):
    f32, bf16 = jnp.float32, jnp.bfloat16
    PG, G = SAMPLE_PG, N_GROUP
    pages = refs[:PG]
    (ka_ref, kb_ref, q_ref, kvn_ref, gl_ref, gn_ref, win_ref, ex_ref, ov_ref, o_ref,
     qf_ref, sel_ref, m_ref, l_ref, acc_ref, oc_ref) = refs[PG:]
    T = q_ref.shape[1]
    R = N_HEADS * T
    ncp = ka_ref.shape[2]
    nsp = ov_ref.shape[1]
    KC = PG * PAGE_SIZE
    n_past = past // SEL_BLOCK
    g = pl.program_id(1)
    lo_half = lax.broadcasted_iota(jnp.int32, (T, LANES), 1) < N_HEAD_DIM
    row = lax.broadcasted_iota(jnp.int32, (R, 1), 0)
    qpos = past + row % T

    @pl.when(g == 0)
    def _():
        pieces = []
        for h in range(N_HEADS):
            kvh = h // G
            tile = q_ref[0, :, (h // 2) * LANES:(h // 2 + 1) * LANES] * ATTN_SCALE
            if h % 2 != kvh:
                tile = pltpu.roll(tile, N_HEAD_DIM, axis=1)
            pieces.append(jnp.where(lo_half if kvh == 0 else jnp.logical_not(lo_half), tile, 0.0))
        q32 = jnp.concatenate(pieces, axis=0)
        qf_ref[...] = q32

        kc = ka_ref[0] + pltpu.roll(kb_ref[0], ncp - 1, axis=1)
        kend = lax.broadcasted_iota(jnp.int32, (1, ncp), 1) * CMP_STRIDE + (CMP_BLOCK - 1)
        p = _masked_softmax_rows(_dot_x3(q32, kc[0:LANES]), kend <= qpos)
        oc_ref[...] = _dot_nt(p.astype(bf16), kc[LANES:2 * LANES].astype(bf16))
        imp = jnp.concatenate(
            [sum(p[(kvh * G + gg) * T:(kvh * G + gg + 1) * T] for gg in range(G)) for kvh in range(N_KV_HEADS)],
            axis=0)

        sid = lax.broadcasted_iota(jnp.int32, (1, nsp), 1)
        sid_f = sid.astype(f32)
        cur = (past + lax.broadcasted_iota(jnp.int32, (N_KV_HEADS * T, 1), 0) % T) // SEL_BLOCK
        valid = sid <= cur
        forced = (sid == 0) | (sid == cur) | (sid == cur - 1)
        score = _dot_exact_rhs(imp, ov_ref[...])
        score = jnp.where(valid, score + jnp.where(forced, SEL_BIAS, 0.0), -SEL_BIAS)

        def extract(_, carry):
            sc, chosen = carry
            best = jnp.max(sc, axis=-1, keepdims=True)
            first = jnp.min(jnp.where(sc == best, sid_f, float(nsp)), axis=-1, keepdims=True)
            hit = sid_f == first
            return jnp.where(hit, -3e38, sc), jnp.where(hit, 1.0, chosen)

        _, chosen = lax.fori_loop(0, min(SEL_TOPK, n_past + 1), extract, (score, jnp.zeros_like(score)))
        sel = jnp.where(valid, chosen, 0.0)
        sel_ref[...] = jnp.concatenate(
            [sel[(h // G) * T:(h // G + 1) * T] for h in range(N_HEADS)], axis=0).astype(bf16)
        m_ref[...] = jnp.full(m_ref.shape, MASK_NEG, f32)
        l_ref[...] = jnp.zeros(l_ref.shape, f32)
        acc_ref[...] = jnp.zeros(acc_ref.shape, f32)

    qb = qf_ref[...].astype(bf16)
    ks = jnp.concatenate([p[0, 0:LANES, :] for p in pages], axis=1).astype(bf16)
    vs = jnp.concatenate([p[0, LANES:2 * LANES, :] for p in pages], axis=1).astype(bf16)
    kpos = g * KC + lax.broadcasted_iota(jnp.int32, (1, KC), 1)
    keep = (_dot(sel_ref[...], ex_ref[0]) > 0.5) & (kpos <= qpos)
    m, l, acc = _online_chunk(m_ref[...], l_ref[...], acc_ref[...], _dot(qb, ks), keep, vs, True)
    m_ref[...] = m
    l_ref[...] = l
    acc_ref[...] = acc

    @pl.when(g == pl.num_programs(1) - 1)
    def _():
        def new_rows(c0):
            return jnp.concatenate([kvn_ref[0, :, c0:c0 + LANES], jnp.zeros((LANES - T, LANES), f32)],
                                   axis=0).astype(bf16)

        npos = past + lax.broadcasted_iota(jnp.int32, (1, LANES), 1)
        is_new = npos < past + T
        keep_t = is_new & (npos <= qpos) & (sel_ref[:, n_past:n_past + 1].astype(f32) > 0.5)
        _, l_s, acc_s = _online_chunk(m, l, acc, _dot_nt(qb, new_rows(BRANCH_W)), keep_t, new_rows(BRANCH_W + LANES))
        o_s = acc_s / l_s

        nb = win_ref.shape[2]
        bpos = past - nb + lax.broadcasted_iota(jnp.int32, (1, nb), 1)
        keep_b = (bpos <= qpos) & (bpos > qpos - WINDOW) & (bpos >= 0)
        mw, lw, aw = _online_chunk(jnp.full((R, 1), MASK_NEG, f32), jnp.zeros((R, 1), f32), jnp.zeros((R, LANES), f32),
                                   _dot(qb, win_ref[0, 0:LANES, :].astype(bf16)), keep_b,
                                   win_ref[0, LANES:2 * LANES, :].astype(bf16), True)
        keep_n = is_new & (npos <= qpos) & (npos > qpos - WINDOW)
        _, lw, aw = _online_chunk(mw, lw, aw, _dot_nt(qb, new_rows(2 * BRANCH_W)), keep_n,
                                  new_rows(2 * BRANCH_W + LANES))
        o_w = aw / lw

        gates = jax.nn.sigmoid(gl_ref[0])
        gcol = lambda br: jnp.concatenate([gates[:, br * N_HEADS + h:br * N_HEADS + h + 1] for h in range(N_HEADS)],
                                          axis=0)
        o = gcol(0) * oc_ref[...] + gcol(1) * o_s + gcol(2) * o_w
        for mt in range(N_HEADS // 2):
            halves = []
            for h in (2 * mt, 2 * mt + 1):
                piece = o[h * T:(h + 1) * T]
                if h % 2 != h // G:
                    piece = pltpu.roll(piece, N_HEAD_DIM, axis=1)
                halves.append(piece)
            gn = gn_ref[0, :, mt * LANES:(mt + 1) * LANES]
            o_ref[0, :, mt * LANES:(mt + 1) * LANES] = jnp.where(lo_half, halves[0], halves[1]) * (gn * jax.nn.sigmoid(gn))


def nsa_sample(q, kv_new, kc_a, kc_b, gl, gate_n, cache_t, page_table, win_t):
    B, T, _ = q.shape
    n_pages = page_table.shape[1]
    past = n_pages * PAGE_SIZE
    PG = SAMPLE_PG
    KC = PG * PAGE_SIZE
    ng = n_pages // PG
    assert n_pages % PG == 0 and past % SEL_BLOCK == 0 and T <= SEL_BLOCK and T <= LANES
    assert (past + T) // CMP_STRIDE == past // CMP_STRIDE and KC % SEL_BLOCK == 0
    ncp = past // CMP_STRIDE
    n_sel = past // SEL_BLOCK + 1
    nsp = -(-n_sel // LANES) * LANES
    cs = np.arange(ncp) * CMP_STRIDE
    ss = np.arange(nsp) * SEL_BLOCK
    overlap = (cs[:, None] < ss[None, :] + SEL_BLOCK) & (cs[:, None] + CMP_BLOCK > ss[None, :])
    overlap[ncp - 1] = False
    overlap[:, n_sel:] = False
    overlap = jnp.asarray(overlap, jnp.bfloat16)
    expand = jnp.asarray((np.arange(past) // SEL_BLOCK).reshape(ng, 1, KC) == np.arange(nsp)[None, :, None],
                         jnp.bfloat16)
    R = N_HEADS * T
    per_b = lambda shape: pl.BlockSpec((1,) + shape, lambda b, g, pt: (b, 0, 0))
    return pl.pallas_call(
        functools.partial(_sample_attend_body, past=past),
        grid_spec=pltpu.PrefetchScalarGridSpec(
            num_scalar_prefetch=1, grid=(B, ng),
            in_specs=[_page_spec(k, 1) for k in range(PG)] + [
                per_b((BRANCH_W, ncp)), per_b((BRANCH_W, ncp)), per_b((T, N_WIDTH)),
                per_b((T, KV_SLOTS * N_KV_HEADS * N_HEAD_DIM)),
                per_b((T, N_BRANCH * N_HEADS)), per_b((T, N_WIDTH)), per_b((BRANCH_W, win_t.shape[2])),
                pl.BlockSpec((1, nsp, KC), lambda b, g, pt: (g, 0, 0)),
                pl.BlockSpec((ncp, nsp), lambda b, g, pt: (0, 0))],
            out_specs=per_b((T, N_WIDTH)),
            scratch_shapes=[pltpu.VMEM((R, LANES), jnp.float32), pltpu.VMEM((R, nsp), jnp.bfloat16),
                            pltpu.VMEM((R, 1), jnp.float32), pltpu.VMEM((R, 1), jnp.float32),
                            pltpu.VMEM((R, LANES), jnp.float32), pltpu.VMEM((R, LANES), jnp.float32)]),
        out_shape=jax.ShapeDtypeStruct((B, T, N_WIDTH), jnp.float32),
        compiler_params=pltpu.CompilerParams(dimension_semantics=("parallel", "arbitrary"),
                                             vmem_limit_bytes=VMEM_LIMIT),
        name="nsa_sample",
    )(page_table, *([cache_t] * PG), kc_a, kc_b, q, kv_new, gl, gate_n, win_t, expand, overlap)


RWKV_TC = 32
RWKV_ROWS = 8


def _rwkv_scan_body(r_ref, w_ref, k_ref, v_ref, kk_ref, b_ref, s0_ref, y_ref, s_ref):
    n = s_ref.shape[0]

    @pl.when(pl.program_id(1) == 0)
    def _():
        s_ref[...] = s0_ref[...]

    def step(t, carry):
        r, w, k, kk, b = r_ref[t], w_ref[t], k_ref[t], kk_ref[t], b_ref[t]

        def rows(g, carry):
            for u in range(RWKV_ROWS):
                i = g * RWKV_ROWS + u
                s_i = s_ref[i]
                sa = jnp.sum(s_i * kk, axis=0, keepdims=True)
                s_i = s_i * w - sa * b + v_ref[t, pl.ds(i, 1), :] * k
                s_ref[i] = s_i
                y_ref[t, pl.ds(i, 1), :] = jnp.sum(s_i * r, axis=0, keepdims=True)
            return carry

        return lax.fori_loop(0, n // RWKV_ROWS, rows, carry)

    lax.fori_loop(0, r_ref.shape[0], step, 0)


def rwkv_scan(r, w, k, v, kk, b, s0):
    B, T, H, N = r.shape
    BH = B * H
    assert BH % LANES == 0 and N % RWKV_ROWS == 0
    tc = min(RWKV_TC, T)
    assert T % tc == 0

    def lanes_last(x):
        return x.transpose(1, 3, 0, 2).reshape(T, N, BH)

    seq = pl.BlockSpec((tc, N, LANES), lambda l, t: (t, 0, l))
    state = pl.BlockSpec((N, N, LANES), lambda l, t: (0, 0, l))
    y, s_fin = pl.pallas_call(
        _rwkv_scan_body,
        grid=(BH // LANES, T // tc),
        in_specs=[seq] * 6 + [state],
        out_specs=[seq, state],
        out_shape=[jax.ShapeDtypeStruct((T, N, BH), jnp.float32), jax.ShapeDtypeStruct((N, N, BH), jnp.float32)],
        compiler_params=pltpu.CompilerParams(dimension_semantics=("parallel", "arbitrary"),
                                             vmem_limit_bytes=VMEM_LIMIT),
        name="rwkv_scan",
    )(*(lanes_last(x) for x in (r, w, k, v, kk, b)), s0.transpose(2, 3, 0, 1).reshape(N, N, BH))
    return (y.reshape(T, N, B, H).transpose(2, 0, 3, 1),
            s_fin.reshape(N, N, B, H).transpose(2, 3, 0, 1))


def _final_norm_body(x_ref, g_ref, o_ref):
    x = x_ref[...]
    o_ref[...] = x * lax.rsqrt(jnp.mean(x * x, axis=-1, keepdims=True) + RMS_EPS) * g_ref[...]


def final_norm(x, g):
    B, T, D = x.shape
    x2 = x.reshape(B * T, D)
    tm = min(512, B * T)
    out = pl.pallas_call(
        _final_norm_body,
        grid=(B * T // tm,),
        in_specs=[pl.BlockSpec((tm, D), lambda i: (i, 0)), pl.BlockSpec((1, D), lambda i: (0, 0))],
        out_specs=pl.BlockSpec((tm, D), lambda i: (i, 0)),
        out_shape=jax.ShapeDtypeStruct((B * T, D), x.dtype),
        name="final_norm",
    )(x2, g.reshape(1, D))
    return out.reshape(B, T, D)


def kernel(x_prompt, x_sample, cache_kv, cache_kv_win, state_wkv, state_shift, page_table,
           norm_g, w_in, mu_shift, w0, w_decay_up, a0, w_aaa_up, k_k, k_a, r_k, gn_w, gn_b,
           w_cmp_pos, w_cmp_mix, w_out, final_g):
    y_p, y_s = x_prompt, x_sample
    kvp, kvs, wnp, wns, skp, sks, shp, shs = [], [], [], [], [], [], [], []
    for layer in range(DEPTH):
        lw = (norm_g[layer], w_in[layer], mu_shift[layer], w0[layer], w_decay_up[layer], a0[layer],
              w_aaa_up[layer], k_k[layer], k_a[layer], r_k[layer], gn_w[layer], gn_b[layer],
              w_cmp_pos[layer], w_cmp_mix[layer], w_out[layer])
        y_p, p_kv, p_win, p_wkv, p_shift = prompt_layer(y_p, *lw)
        y_s, s_kv, s_win, s_wkv, s_shift = sample_layer(y_s, cache_kv[layer], cache_kv_win[layer],
                                                        state_wkv[layer], state_shift[layer], page_table, *lw)
        kvp.append(p_kv); kvs.append(s_kv); wnp.append(p_win); wns.append(s_win)
        skp.append(p_wkv); sks.append(s_wkv); shp.append(p_shift); shs.append(s_shift)
    y_prompt = final_norm(y_p, final_g)
    y_sample = final_norm(y_s, final_g)
    return (y_prompt, y_sample, jnp.stack(kvp), jnp.stack(kvs), jnp.stack(wnp), jnp.stack(wns),
            jnp.stack(skp), jnp.stack(sks), jnp.stack(shp), jnp.stack(shs))
```

```python
import jax, jax.numpy as jnp
from jax import lax
import functools
import numpy as np
from jax.experimental import pallas as pl
from jax.experimental.pallas import tpu as pltpu

D_MODEL = 1024
BATCH = 16
SEQ = 2048
DEPTH = 1
DEC_BATCH = 32
DEC_SEQ = 4
PAST_LEN = 16384
PAGE_SIZE = 128

R_HEAD_DIM = 64
R_WIDTH = D_MODEL // 2
R_HEADS = R_WIDTH // R_HEAD_DIM
DECAY_LORA = 64
AAA_LORA = 64
R_SHIFT_WIDTH = 3 * R_WIDTH + DECAY_LORA + AAA_LORA
N_HEAD_DIM = 64
N_WIDTH = D_MODEL - R_WIDTH
N_HEADS = N_WIDTH // N_HEAD_DIM
N_KV_HEADS = 2
N_GROUP = N_HEADS // N_KV_HEADS
N_BRANCH = 3
KV_SLOTS = 2 * N_BRANCH
CMP_BLOCK = 32
CMP_STRIDE = 16
SEL_BLOCK = 64
SEL_TOPK = 16
WINDOW = 512
SEL_QUERY_BLOCK = 32
WIN_QUERY_BLOCK = 128
MIX_WIDTH = R_WIDTH + N_WIDTH
PROJ_SPLITS = (R_SHIFT_WIDTH, R_WIDTH, N_WIDTH, N_WIDTH, KV_SLOTS * N_KV_HEADS * N_HEAD_DIM, N_BRANCH * N_HEADS)
PROJ_WIDTH = sum(PROJ_SPLITS)
RMS_EPS = 1e-6
GN_EPS = 64e-5
MASK_NEG = -1e30
SEL_BIAS = 1e4
ATTN_SCALE = N_HEAD_DIM ** -0.5


def rms_norm(x, g):
    xf = x.astype(jnp.float32)
    y = xf * lax.rsqrt(jnp.mean(xf * xf, axis=-1, keepdims=True) + RMS_EPS)
    return (y * g.astype(jnp.float32)).astype(x.dtype)


def masked_softmax(s, mask):
    s = jnp.where(mask, s.astype(jnp.float32), MASK_NEG)
    p = jax.nn.softmax(s, axis=-1)
    return jnp.where(mask, p, 0.0)


def split_proj(p):
    offs = np.cumsum((0,) + PROJ_SPLITS)
    return tuple(p[..., int(offs[i]):int(offs[i + 1])] for i in range(len(PROJ_SPLITS)))


def to_heads(q):
    B, T, _ = q.shape
    return q.reshape(B, T, N_KV_HEADS, N_GROUP, N_HEAD_DIM).transpose(0, 2, 3, 1, 4)


def rwkv_mix(z, z_prev, s0, gate, mu_shift, w0, w_decay_up, a0, w_aaa_up, k_k, k_a, r_k, gn_w, gn_b):
    B, T, _ = z.shape
    f32 = jnp.float32
    zp = jnp.concatenate([z_prev[:, None].astype(z.dtype), z[:, :-1]], axis=1)
    zs = z + (zp - z) * mu_shift
    r, k, v, wd, ad = jnp.split(zs, [R_WIDTH, 2 * R_WIDTH, 3 * R_WIDTH, 3 * R_WIDTH + DECAY_LORA], axis=-1)
    w_log = -jax.nn.softplus(-(w0 + jnp.tanh(wd) @ w_decay_up).astype(f32)) - 0.5
    decay = jnp.exp(-jnp.exp(w_log))
    a = jax.nn.sigmoid((a0 + ad @ w_aaa_up).astype(f32))
    hs = (B, T, R_HEADS, R_HEAD_DIM)
    r, k, v, decay, a = (t.astype(f32).reshape(hs) for t in (r, k, v, decay, a))
    hshape = (R_HEADS, R_HEAD_DIM)
    kk = k * k_k.astype(f32).reshape(hshape)
    kk = kk * lax.rsqrt(jnp.maximum(jnp.sum(kk * kk, axis=-1, keepdims=True), 1e-24))
    k = k * (1.0 + (a - 1.0) * k_a.astype(f32).reshape(hshape))

    y, s_fin = rwkv_scan(r, decay, k, v, kk, kk * a, s0.astype(f32))
    yc = y - jnp.mean(y, axis=-1, keepdims=True)
    y = yc * lax.rsqrt(jnp.mean(yc * yc, axis=-1, keepdims=True) + GN_EPS)
    y = y * gn_w.astype(f32).reshape(hshape) + gn_b.astype(f32).reshape(hshape)
    y = y + jnp.sum(r * k * r_k.astype(f32).reshape(hshape), axis=-1, keepdims=True) * v
    out = y.reshape(B, T, R_WIDTH).astype(z.dtype) * jax.nn.silu(gate)
    return out, s_fin, z[:, -1]


def compress(kv, w_pos, w_mix):
    B, L = kv.shape[:2]
    n_sub = CMP_BLOCK // CMP_STRIDE
    n_chunk = L // CMP_STRIDE
    n_cmp = n_chunk - n_sub + 1
    c = kv[:, :n_chunk * CMP_STRIDE].reshape(B, n_chunk, CMP_STRIDE, 2, N_KV_HEADS, N_HEAD_DIM)
    wp = w_pos.reshape(2, n_sub, CMP_STRIDE, N_HEAD_DIM)
    pooled = jnp.einsum('bjpekd,epd->bjekd', c[:, 0:n_cmp], wp[:, 0])
    for m in range(1, n_sub):
        pooled = pooled + jnp.einsum('bjpekd,epd->bjekd', c[:, m:m + n_cmp], wp[:, m])
    return jnp.einsum('bjekd,edf->bjekf', pooled, w_mix)


def compressed_attn(q, kc, qpos):
    n_cmp = kc.shape[1]
    kend = jnp.arange(n_cmp) * CMP_STRIDE + (CMP_BLOCK - 1)
    mask = kend[None, :] <= qpos[:, None]
    s = jnp.einsum('bkgtd,bjkd->bkgtj', q, kc[:, :, 0]) * ATTN_SCALE
    p = masked_softmax(s, mask)
    o = jnp.einsum('bkgtj,bjkd->bkgtd', p.astype(q.dtype), kc[:, :, 1])
    return o, jnp.sum(p, axis=2)


def select_blocks(imp, qpos, total_len):
    n_cmp = imp.shape[-1]
    n_sel = -(-total_len // SEL_BLOCK)
    cs = jnp.arange(n_cmp) * CMP_STRIDE
    ss = jnp.arange(n_sel) * SEL_BLOCK
    overlap = ((cs[:, None] < ss[None, :] + SEL_BLOCK) & (cs[:, None] + CMP_BLOCK > ss[None, :])).astype(jnp.float32)
    score = jnp.einsum('bktj,js->bkts', imp, overlap)
    cur = (qpos // SEL_BLOCK)[:, None]
    sid = jnp.arange(n_sel)[None, :]
    valid = sid <= cur
    forced = (sid == 0) | (sid == cur) | (sid == cur - 1)
    score = jnp.where(valid, score + jnp.where(forced, SEL_BIAS, 0.0), -SEL_BIAS)
    vals, idx = lax.top_k(score, min(SEL_TOPK, n_sel))
    return idx, vals > -0.5 * SEL_BIAS


def selected_attn(q, ks, vs, idx, valid, qpos):
    B, KV, T, K = idx.shape
    kpos = idx[..., None] * SEL_BLOCK + jnp.arange(SEL_BLOCK)
    mask = (valid[..., None] & (kpos <= qpos[:, None, None])).reshape(B, KV, 1, T, K * SEL_BLOCK)
    ks = ks.reshape(B, KV, T, K * SEL_BLOCK, N_HEAD_DIM)
    vs = vs.reshape(B, KV, T, K * SEL_BLOCK, N_HEAD_DIM)
    s = jnp.einsum('bkgtd,bktnd->bkgtn', q, ks) * ATTN_SCALE
    p = masked_softmax(s, mask)
    return jnp.einsum('bkgtn,bktnd->bkgtd', p.astype(q.dtype), vs)


def selected_prompt_attn(q, kv_sel, idx, valid, qpos):
    B, T = kv_sel.shape[:2]
    n_sel = T // SEL_BLOCK
    blocks = kv_sel.reshape(B, n_sel, SEL_BLOCK, 2, N_KV_HEADS, N_HEAD_DIM).transpose(0, 4, 1, 2, 3, 5)
    bk, bv = blocks[..., 0, :], blocks[..., 1, :]
    bi = jnp.arange(B)[:, None, None, None]
    gi = jnp.arange(N_KV_HEADS)[None, :, None, None]
    nq = T // SEL_QUERY_BLOCK

    def split(t, ax):
        return jnp.moveaxis(t.reshape(t.shape[:ax] + (nq, SEL_QUERY_BLOCK) + t.shape[ax + 1:]), ax, 0)

    def chunk(args):
        qc, ic, vc, pc = args
        return selected_attn(qc, bk[bi, gi, ic], bv[bi, gi, ic], ic, vc, pc)

    out = lax.map(chunk, (split(q, 3), split(idx, 2), split(valid, 2), qpos.reshape(nq, SEL_QUERY_BLOCK)))
    return jnp.moveaxis(out, 0, 3).reshape(B, N_KV_HEADS, N_GROUP, T, N_HEAD_DIM)


def selected_sample_attn(q, kv_new, cache, page_table, idx, valid, qpos):
    B, T = kv_new.shape[:2]
    bpp = PAGE_SIZE // SEL_BLOCK
    n_past = page_table.shape[1] * bpp
    pool = cache.reshape(cache.shape[0] * bpp, SEL_BLOCK, cache.shape[2], N_KV_HEADS, N_HEAD_DIM)
    n_tail = -(-T // SEL_BLOCK)
    tail = jnp.pad(kv_new, ((0, 0), (0, n_tail * SEL_BLOCK - T), (0, 0), (0, 0), (0, 0)))
    tail = tail.reshape(B, n_tail, SEL_BLOCK, 2, N_KV_HEADS, N_HEAD_DIM).transpose(0, 4, 1, 2, 3, 5)
    bi = jnp.arange(B)[:, None, None, None]
    gi = jnp.arange(N_KV_HEADS)[None, :, None, None]
    ip = jnp.minimum(idx, n_past - 1)
    phys = page_table[bi, ip // bpp] * bpp + ip % bpp
    past_rows = pool[phys, :, 2:4, gi]
    tail_rows = tail[bi, gi, jnp.clip(idx - n_past, 0, n_tail - 1)]
    rows = jnp.where((idx < n_past)[..., None, None, None], past_rows.astype(tail_rows.dtype), tail_rows)
    return selected_attn(q, rows[..., 0, :], rows[..., 1, :], idx, valid, qpos)


def window_attn(q, kw, qpos, kpos):
    mask = (kpos[None, :] <= qpos[:, None]) & (kpos[None, :] > qpos[:, None] - WINDOW) & (kpos[None, :] >= 0)
    s = jnp.einsum('bkgtd,bskd->bkgts', q, kw[:, :, 0]) * ATTN_SCALE
    p = masked_softmax(s, mask)
    return jnp.einsum('bkgts,bskd->bkgtd', p.astype(q.dtype), kw[:, :, 1])


def window_prompt_attn(q, kv_win):
    B, T = kv_win.shape[:2]
    padded = jnp.pad(kv_win, ((0, 0), (WINDOW, 0), (0, 0), (0, 0), (0, 0)))
    nq = T // WIN_QUERY_BLOCK

    def chunk(i):
        start = i * WIN_QUERY_BLOCK
        kw = lax.dynamic_slice_in_dim(padded, start, WINDOW + WIN_QUERY_BLOCK, axis=1)
        qc = lax.dynamic_slice_in_dim(q, start, WIN_QUERY_BLOCK, axis=3)
        qpos = start + jnp.arange(WIN_QUERY_BLOCK)
        kpos = start - WINDOW + jnp.arange(WINDOW + WIN_QUERY_BLOCK)
        return window_attn(qc, kw, qpos, kpos)

    out = lax.map(chunk, jnp.arange(nq))
    return jnp.moveaxis(out, 0, 3).reshape(B, N_KV_HEADS, N_GROUP, T, N_HEAD_DIM)


def mixer_out(y_r, o_c, o_s, o_w, gl, gate_n, w_out):
    B, T, _ = y_r.shape
    g = jax.nn.sigmoid(gl.astype(jnp.float32)).reshape(B, T, N_BRANCH, N_KV_HEADS, N_GROUP)
    g = g.transpose(2, 0, 3, 4, 1)[..., None]
    o = g[0] * o_c + g[1] * o_s + g[2] * o_w
    o = o.transpose(0, 3, 1, 2, 4).reshape(B, T, N_WIDTH).astype(y_r.dtype) * jax.nn.silu(gate_n)
    return jnp.concatenate([y_r, o], axis=-1) @ w_out


def prompt_layer(x, norm_g, w_in, mu_shift, w0, w_decay_up, a0, w_aaa_up, k_k, k_a, r_k, gn_w, gn_b,
                 w_cmp_pos, w_cmp_mix, w_out):
    B, T, _ = x.shape
    zr, gate_r, q, gate_n, kv, gl = split_proj(rms_norm(x, norm_g) @ w_in)
    y_r, wkv, shift = rwkv_mix(zr, jnp.zeros((B, R_SHIFT_WIDTH), zr.dtype),
                               jnp.zeros((B, R_HEADS, R_HEAD_DIM, R_HEAD_DIM), jnp.float32), gate_r,
                               mu_shift, w0, w_decay_up, a0, w_aaa_up, k_k, k_a, r_k, gn_w, gn_b)
    o_n = nsa_prompt(q, kv, compress_prompt(kv, w_cmp_pos, w_cmp_mix), gl, gate_n)
    y = x + jnp.concatenate([y_r, o_n], axis=-1) @ w_out
    kv = kv.reshape(B, T, KV_SLOTS, N_KV_HEADS, N_HEAD_DIM)
    return y, kv[:, :, 0:4], kv[:, T - min(WINDOW, T):, 4:6], wkv, shift


def sample_layer(x, cache, win_buf, wkv0, shift0, page_table, norm_g, w_in, mu_shift, w0, w_decay_up, a0,
                 w_aaa_up, k_k, k_a, r_k, gn_w, gn_b, w_cmp_pos, w_cmp_mix, w_out):
    B, T, _ = x.shape
    past = page_table.shape[1] * PAGE_SIZE
    zr, gate_r, q, gate_n, kv, gl = split_proj(rms_norm(x, norm_g) @ w_in)
    y_r, wkv, shift = rwkv_mix(zr, shift0, wkv0, gate_r,
                               mu_shift, w0, w_decay_up, a0, w_aaa_up, k_k, k_a, r_k, gn_w, gn_b)
    cache_t = cache.reshape(cache.shape[0], PAGE_SIZE, 2 * BRANCH_W).transpose(0, 2, 1)
    nb = win_buf.shape[1]
    win_t = win_buf.reshape(B, nb, BRANCH_W).transpose(0, 2, 1)
    kc_a, kc_b = sample_compress(cache_t, page_table, w_cmp_pos, w_cmp_mix)
    o_n = nsa_sample(q, kv, kc_a, kc_b, gl, gate_n, cache_t, page_table, win_t)
    y = x + jnp.concatenate([y_r, o_n], axis=-1) @ w_out
    kv = kv.reshape(B, T, KV_SLOTS, N_KV_HEADS, N_HEAD_DIM)
    keys_w = jnp.concatenate([win_buf.astype(kv.dtype), kv[:, :, 4:6]], axis=1)
    n_keep = min(WINDOW, nb + T)
    return y, kv[:, :, 0:4], keys_w[:, nb + T - n_keep:], wkv, shift


LANES = 128
NSA_TQ = 128
NSA_KC = 512
VMEM_LIMIT = 48 * 1024 * 1024


def _split3(x):
    hi = x.astype(jnp.bfloat16)
    r1 = x - hi.astype(jnp.float32)
    mid = r1.astype(jnp.bfloat16)
    lo = (r1 - mid.astype(jnp.float32)).astype(jnp.bfloat16)
    return hi, mid, lo


def _dot(a, b):
    return jnp.dot(a, b, preferred_element_type=jnp.float32)


def _dot_nt(a, b):
    return lax.dot_general(a, b, (((1,), (1,)), ((), ())), preferred_element_type=jnp.float32)


def _dot_exact_rhs(a, b01):
    hi, mid, lo = _split3(a)
    return _dot(hi, b01) + _dot(mid, b01) + _dot(lo, b01)


def _dot_x3(a, b):
    ah = a.astype(jnp.bfloat16)
    al = (a - ah.astype(jnp.float32)).astype(jnp.bfloat16)
    bh = b.astype(jnp.bfloat16)
    bl = (b - bh.astype(jnp.float32)).astype(jnp.bfloat16)
    return _dot(ah, bh) + _dot(ah, bl) + _dot(al, bh)


def _dot_nt_x3(a, b):
    ah = a.astype(jnp.bfloat16)
    al = (a - ah.astype(jnp.float32)).astype(jnp.bfloat16)
    bh = b.astype(jnp.bfloat16)
    bl = (b - bh.astype(jnp.float32)).astype(jnp.bfloat16)
    return _dot_nt(ah, bh) + _dot_nt(ah, bl) + _dot_nt(al, bh)


def _compress_prompt_body(kv_ref, wa_ref, wb_ref, pool_ref, wmix_ref, o_ref):
    x = kv_ref[0]
    xw = jnp.concatenate([x * wa_ref[...], x * wb_ref[...]], axis=0)
    pooled = _dot_exact_rhs_lhs01(pool_ref[...], xw)
    o_ref[0] = _dot_x3(pooled, wmix_ref[...])


def _dot_exact_rhs_lhs01(a01, b):
    hi, mid, lo = _split3(b)
    return _dot(a01, hi) + _dot(a01, mid) + _dot(a01, lo)


def compress_prompt(kv, w_pos, w_mix):
    B, T, _ = kv.shape
    n_sub = CMP_BLOCK // CMP_STRIDE
    assert n_sub == 2 and T % CMP_STRIDE == 0
    ncp = T // CMP_STRIDE
    cw = 2 * N_KV_HEADS * N_HEAD_DIM
    wp = jnp.broadcast_to(w_pos.reshape(2, 1, CMP_BLOCK, N_HEAD_DIM).transpose(2, 0, 1, 3),
                          (CMP_BLOCK, 2, N_KV_HEADS, N_HEAD_DIM)).reshape(CMP_BLOCK, cw)
    wa = jnp.tile(wp[:CMP_STRIDE], (ncp, 1))
    wb = jnp.tile(wp[CMP_STRIDE:], (ncp, 1))
    chunk = np.arange(T) // CMP_STRIDE
    pool = np.concatenate([chunk[None, :] == np.arange(ncp)[:, None],
                           chunk[None, :] == np.arange(ncp)[:, None] + 1], axis=1)
    pool = jnp.asarray(pool, jnp.bfloat16)
    wmix = jnp.zeros((cw, cw), jnp.float32)
    for e in range(2):
        for kh in range(N_KV_HEADS):
            o = (e * N_KV_HEADS + kh) * N_HEAD_DIM
            wmix = wmix.at[o:o + N_HEAD_DIM, o:o + N_HEAD_DIM].set(w_mix[e])
    return pl.pallas_call(
        _compress_prompt_body,
        grid=(B,),
        in_specs=[pl.BlockSpec((1, T, cw), lambda b: (b, 0, 0)),
                  pl.BlockSpec((T, cw), lambda b: (0, 0)),
                  pl.BlockSpec((T, cw), lambda b: (0, 0)),
                  pl.BlockSpec((ncp, 2 * T), lambda b: (0, 0)),
                  pl.BlockSpec((cw, cw), lambda b: (0, 0))],
        out_specs=pl.BlockSpec((1, ncp, cw), lambda b: (b, 0, 0)),
        out_shape=jax.ShapeDtypeStruct((B, ncp, cw), jnp.float32),
        compiler_params=pltpu.CompilerParams(dimension_semantics=("parallel",), vmem_limit_bytes=VMEM_LIMIT),
        name="compress_prompt",
    )(kv, wa, wb, pool, wmix)


def _masked_softmax_rows(s, mask):
    s = jnp.where(mask, s, MASK_NEG)
    e = jnp.exp(s - jnp.max(s, axis=-1, keepdims=True))
    p = e / jnp.sum(e, axis=-1, keepdims=True)
    return jnp.where(mask, p, 0.0)


def _nsa_prompt_body(q_ref, ksel_ref, kwin_ref, kc_ref, gl_ref, gn_ref, ovt_ref, ex_ref, o_ref):
    f32, bf16 = jnp.float32, jnp.bfloat16
    TQ, KC, G = NSA_TQ, NSA_KC, N_GROUP
    T = ksel_ref.shape[1]
    ncp = kc_ref.shape[1]
    n_sel = ovt_ref.shape[0]
    topk = min(SEL_TOPK, n_sel)
    qs = pl.program_id(1) * TQ
    qpos = qs + lax.broadcasted_iota(jnp.int32, (TQ, 1), 0)
    lo_half = lax.broadcasted_iota(jnp.int32, (TQ, LANES), 1) < N_HEAD_DIM

    def in_half(x, half):
        return jnp.where(lo_half if half == 0 else jnp.logical_not(lo_half), x, 0.0)

    def q_padded(h, half):
        tile = q_ref[0, :, (h // 2) * LANES:(h // 2 + 1) * LANES] * ATTN_SCALE
        if h % 2 != half:
            tile = pltpu.roll(tile, N_HEAD_DIM, axis=1)
        return in_half(tile, half)

    wstart = pl.multiple_of(jnp.maximum(qs - WINDOW, 0), TQ)
    kw = kwin_ref[0, pl.ds(wstart, WINDOW + TQ), 0:LANES].astype(bf16)
    vw = kwin_ref[0, pl.ds(wstart, WINDOW + TQ), LANES:2 * LANES].astype(bf16)
    kposw = wstart + lax.broadcasted_iota(jnp.int32, (1, WINDOW + TQ), 1)
    mask_w = (kposw <= qpos) & (kposw > qpos - WINDOW)

    kck = kc_ref[0, :, 0:LANES]
    kcv = kc_ref[0, :, LANES:2 * LANES].astype(bf16)
    kend = lax.broadcasted_iota(jnp.int32, (1, ncp), 1) * CMP_STRIDE + (CMP_BLOCK - 1)
    mask_c = kend <= qpos

    sid_t = lax.broadcasted_iota(jnp.int32, (n_sel, 1), 0)
    cur_t = (qs + lax.broadcasted_iota(jnp.int32, (1, TQ), 1)) // SEL_BLOCK
    valid_t = sid_t <= cur_t
    forced_t = (sid_t == 0) | (sid_t == cur_t) | (sid_t == cur_t - 1)
    n_chunks = qs // KC + 1

    gates = jax.nn.sigmoid(gl_ref[0])
    heads = [None] * N_HEADS
    for kvh in range(N_KV_HEADS):
        qg = jnp.concatenate([q_padded(kvh * G + g, kvh) for g in range(G)], axis=0)
        qb = qg.astype(bf16)

        s = _dot_nt_x3(qg, kck).reshape(G, TQ, ncp)
        p = _masked_softmax_rows(s, mask_c[None])
        o_c = _dot(p.reshape(G * TQ, ncp).astype(bf16), kcv)
        imp = jnp.sum(p, axis=0)

        hi, mid, lo = _split3(imp)
        ovt = ovt_ref[...]
        score = _dot_nt(ovt, hi) + _dot_nt(ovt, mid) + _dot_nt(ovt, lo)
        score = jnp.where(valid_t, score + jnp.where(forced_t, SEL_BIAS, 0.0), -SEL_BIAS)
        rank = jnp.zeros((n_sel, TQ), jnp.int32)
        for s2 in range(n_sel):
            row = score[s2:s2 + 1, :]
            ahead = (row > score) | ((row == score) & (s2 < sid_t))
            rank = rank + ahead.astype(jnp.int32)
        sel = jnp.where((rank < topk) & valid_t, 1.0, 0.0).T.astype(bf16)

        def chunk(c, carry, qb=qb, sel=sel):
            m, l, acc = carry
            k0 = pl.multiple_of(c * KC, KC)
            ks = ksel_ref[0, pl.ds(k0, KC), 0:LANES].astype(bf16)
            vs = ksel_ref[0, pl.ds(k0, KC), LANES:2 * LANES].astype(bf16)
            selx = _dot(sel, ex_ref[:, pl.ds(k0, KC)])
            kpos = k0 + lax.broadcasted_iota(jnp.int32, (1, KC), 1)
            keep = (selx > 0.5) & (kpos <= qpos)
            sc = jnp.where(keep[None], _dot_nt(qb, ks).reshape(G, TQ, KC), MASK_NEG)
            m_new = jnp.maximum(m, jnp.max(sc, axis=-1, keepdims=True))
            alpha = jnp.exp(m - m_new)
            e = jnp.exp(sc - m_new)
            l = alpha * l + jnp.sum(e, axis=-1, keepdims=True)
            pv = _dot(e.reshape(G * TQ, KC).astype(bf16), vs).reshape(G, TQ, LANES)
            return m_new, l, alpha * acc + pv

        m0 = jnp.full((G, TQ, 1), MASK_NEG, f32)
        l0 = jnp.zeros((G, TQ, 1), f32)
        a0 = jnp.zeros((G, TQ, LANES), f32)
        _, l, acc = lax.fori_loop(0, n_chunks, chunk, (m0, l0, a0))
        o_s = acc / l

        sw = jnp.where(mask_w[None], _dot_nt(qb, kw).reshape(G, TQ, WINDOW + TQ), MASK_NEG)
        ew = jnp.exp(sw - jnp.max(sw, axis=-1, keepdims=True))
        o_w = (_dot(ew.reshape(G * TQ, WINDOW + TQ).astype(bf16), vw).reshape(G, TQ, LANES)
               / jnp.sum(ew, axis=-1, keepdims=True))

        o_c = o_c.reshape(G, TQ, LANES)
        for g in range(G):
            h = kvh * G + g
            o = (gates[:, h:h + 1] * o_c[g] + gates[:, N_HEADS + h:N_HEADS + h + 1] * o_s[g]
                 + gates[:, 2 * N_HEADS + h:2 * N_HEADS + h + 1] * o_w[g])
            if h % 2 != kvh:
                o = pltpu.roll(o, N_HEAD_DIM, axis=1)
            heads[h] = o

    for m in range(N_HEADS // 2):
        tile = jnp.where(lo_half, heads[2 * m], heads[2 * m + 1])
        gn = gn_ref[0, :, m * LANES:(m + 1) * LANES]
        o_ref[0, :, m * LANES:(m + 1) * LANES] = tile * (gn * jax.nn.sigmoid(gn))


def nsa_prompt(q, kv, kc, gl, gate_n):
    B, T, _ = q.shape
    TQ, KC = NSA_TQ, NSA_KC
    assert N_KV_HEADS == 2 and N_HEAD_DIM * 2 == LANES and T % KC == 0 and KC % TQ == 0 and T >= WINDOW + TQ
    assert SEL_BLOCK % CMP_STRIDE == 0 and WINDOW % TQ == 0
    ncp = T // CMP_STRIDE
    n_sel = T // SEL_BLOCK
    cs = np.arange(ncp) * CMP_STRIDE
    ss = np.arange(n_sel) * SEL_BLOCK
    overlap = (cs[:, None] < ss[None, :] + SEL_BLOCK) & (cs[:, None] + CMP_BLOCK > ss[None, :])
    overlap[ncp - 1] = False
    overlap_t = jnp.asarray(overlap.T, jnp.bfloat16)
    expand = jnp.asarray(np.arange(T)[None, :] // SEL_BLOCK == np.arange(n_sel)[:, None], jnp.bfloat16)
    kvw = 2 * N_KV_HEADS * N_HEAD_DIM
    return pl.pallas_call(
        _nsa_prompt_body,
        grid=(B, T // TQ),
        in_specs=[pl.BlockSpec((1, TQ, N_WIDTH), lambda b, i: (b, i, 0)),
                  pl.BlockSpec((1, T, kvw), lambda b, i: (b, 0, 1)),
                  pl.BlockSpec((1, T, kvw), lambda b, i: (b, 0, 2)),
                  pl.BlockSpec((1, ncp, kvw), lambda b, i: (b, 0, 0)),
                  pl.BlockSpec((1, TQ, N_BRANCH * N_HEADS), lambda b, i: (b, i, 0)),
                  pl.BlockSpec((1, TQ, N_WIDTH), lambda b, i: (b, i, 0)),
                  pl.BlockSpec((n_sel, ncp), lambda b, i: (0, 0)),
                  pl.BlockSpec((n_sel, T), lambda b, i: (0, 0))],
        out_specs=pl.BlockSpec((1, TQ, N_WIDTH), lambda b, i: (b, i, 0)),
        out_shape=jax.ShapeDtypeStruct((B, T, N_WIDTH), jnp.float32),
        compiler_params=pltpu.CompilerParams(dimension_semantics=("parallel", "arbitrary"),
                                             vmem_limit_bytes=VMEM_LIMIT),
        name="nsa_prompt",
    )(q, kv, kv, kc, gl, gate_n, overlap_t, expand)


SAMPLE_PG = 16
BRANCH_W = 2 * N_KV_HEADS * N_HEAD_DIM


def _page_spec(k, branch):
    return pl.BlockSpec((1, BRANCH_W, PAGE_SIZE), lambda b, g, pt: (pt[b, g * SAMPLE_PG + k], branch, 0))


def _sample_compress_body(pt_ref, *refs):
    pages = refs[:SAMPLE_PG]
    wa_ref, wb_ref, pool_ref, wmix_ref, oa_ref, ob_ref = refs[SAMPLE_PG:]
    x = jnp.concatenate([p[0] for p in pages], axis=1)

    def pooled(w_ref):
        xw = x * w_ref[...]
        hi = xw.astype(jnp.bfloat16)
        lo = (xw - hi.astype(jnp.float32)).astype(jnp.bfloat16)
        return _dot(hi, pool_ref[...]) + _dot(lo, pool_ref[...])

    oa_ref[0] = _dot_x3(wmix_ref[...], pooled(wa_ref))
    ob_ref[0] = _dot_x3(wmix_ref[...], pooled(wb_ref))


def _cmp_weights(w_pos, w_mix, n_rows):
    wp = jnp.broadcast_to(w_pos.reshape(2, 1, CMP_BLOCK, N_HEAD_DIM).transpose(2, 0, 1, 3),
                          (CMP_BLOCK, 2, N_KV_HEADS, N_HEAD_DIM)).reshape(CMP_BLOCK, BRANCH_W)
    wa = jnp.tile(wp[:CMP_STRIDE], (n_rows // CMP_STRIDE, 1))
    wb = jnp.tile(wp[CMP_STRIDE:], (n_rows // CMP_STRIDE, 1))
    wmix = jnp.zeros((BRANCH_W, BRANCH_W), jnp.float32)
    for e in range(2):
        for kh in range(N_KV_HEADS):
            o = (e * N_KV_HEADS + kh) * N_HEAD_DIM
            wmix = wmix.at[o:o + N_HEAD_DIM, o:o + N_HEAD_DIM].set(w_mix[e])
    return wa, wb, wmix


def sample_compress(cache_t, page_table, w_pos, w_mix):
    B, n_pages = page_table.shape
    PG = SAMPLE_PG
    assert n_pages % PG == 0 and CMP_BLOCK == 2 * CMP_STRIDE and PAGE_SIZE % CMP_STRIDE == 0
    cols = PG * PAGE_SIZE // CMP_STRIDE
    assert cols % LANES == 0
    ncp = n_pages * PAGE_SIZE // CMP_STRIDE
    wa, wb, wmix = _cmp_weights(w_pos, w_mix, PG * PAGE_SIZE)
    pool = jnp.asarray(np.arange(PG * PAGE_SIZE)[:, None] // CMP_STRIDE == np.arange(cols)[None, :], jnp.bfloat16)
    const = lambda shape: pl.BlockSpec(shape, lambda b, g, pt: (0,) * len(shape))
    out = pl.BlockSpec((1, BRANCH_W, cols), lambda b, g, pt: (b, 0, g))
    return pl.pallas_call(
        _sample_compress_body,
        grid_spec=pltpu.PrefetchScalarGridSpec(
            num_scalar_prefetch=1, grid=(B, n_pages // PG),
            in_specs=[_page_spec(k, 0) for k in range(PG)] + [const(wa.T.shape), const(wb.T.shape), const(pool.shape),
                                                              const(wmix.shape)],
            out_specs=[out, out]),
        out_shape=[jax.ShapeDtypeStruct((B, BRANCH_W, ncp), jnp.float32)] * 2,
        compiler_params=pltpu.CompilerParams(dimension_semantics=("parallel", "arbitrary"),
                                             vmem_limit_bytes=VMEM_LIMIT),
        name="sample_compress",
    )(page_table, *([cache_t] * PG), wa.T, wb.T, pool, wmix.T)


def _online_chunk(m, l, acc, s, keep, v, v_channels_major=False):
    s = jnp.where(keep, s, MASK_NEG)
    m_new = jnp.maximum(m, jnp.max(s, axis=-1, keepdims=True))
    alpha = jnp.exp(m - m_new)
    e = jnp.where(keep, jnp.exp(s - m_new), 0.0)
    eb = e.astype(jnp.bfloat16)
    pv = _dot_nt(eb, v) if v_channels_major else _dot(eb, v)
    return m_new, alpha * l + jnp.sum(e, axis=-1, keepdims=True), alpha * acc + pv


def _sample_attend_body(pt_ref, *refs, past):
    f32, bf16 = jnp.float32, jnp.bfloat16
    PG, G = SAMPLE_PG, N_GROUP
    pages = refs[:PG]
    (ka_ref, kb_ref, q_ref, kvn_ref, gl_ref, gn_ref, win_ref, ex_ref, ov_ref, o_ref,
     qf_ref, sel_ref, m_ref, l_ref, acc_ref, oc_ref) = refs[PG:]
    T = q_ref.shape[1]
    R = N_HEADS * T
    ncp = ka_ref.shape[2]
    nsp = ov_ref.shape[1]
    KC = PG * PAGE_SIZE
    n_past = past // SEL_BLOCK
    g = pl.program_id(1)
    lo_half = lax.broadcasted_iota(jnp.int32, (T, LANES), 1) < N_HEAD_DIM
    row = lax.broadcasted_iota(jnp.int32, (R, 1), 0)
    qpos = past + row % T

    @pl.when(g == 0)
    def _():
        pieces = []
        for h in range(N_HEADS):
            kvh = h // G
            tile = q_ref[0, :, (h // 2) * LANES:(h // 2 + 1) * LANES] * ATTN_SCALE
            if h % 2 != kvh:
                tile = pltpu.roll(tile, N_HEAD_DIM, axis=1)
            pieces.append(jnp.where(lo_half if kvh == 0 else jnp.logical_not(lo_half), tile, 0.0))
        q32 = jnp.concatenate(pieces, axis=0)
        qf_ref[...] = q32

        kc = ka_ref[0] + pltpu.roll(kb_ref[0], ncp - 1, axis=1)
        kend = lax.broadcasted_iota(jnp.int32, (1, ncp), 1) * CMP_STRIDE + (CMP_BLOCK - 1)
        p = _masked_softmax_rows(_dot_x3(q32, kc[0:LANES]), kend <= qpos)
        oc_ref[...] = _dot_nt(p.astype(bf16), kc[LANES:2 * LANES].astype(bf16))
        imp = jnp.concatenate(
            [sum(p[(kvh * G + gg) * T:(kvh * G + gg + 1) * T] for gg in range(G)) for kvh in range(N_KV_HEADS)],
            axis=0)

        sid = lax.broadcasted_iota(jnp.int32, (1, nsp), 1)
        sid_f = sid.astype(f32)
        cur = (past + lax.broadcasted_iota(jnp.int32, (N_KV_HEADS * T, 1), 0) % T) // SEL_BLOCK
        valid = sid <= cur
        forced = (sid == 0) | (sid == cur) | (sid == cur - 1)
        score = _dot_exact_rhs(imp, ov_ref[...])
        score = jnp.where(valid, score + jnp.where(forced, SEL_BIAS, 0.0), -SEL_BIAS)

        def extract(_, carry):
            sc, chosen = carry
            best = jnp.max(sc, axis=-1, keepdims=True)
            first = jnp.min(jnp.where(sc == best, sid_f, float(nsp)), axis=-1, keepdims=True)
            hit = sid_f == first
            return jnp.where(hit, -3e38, sc), jnp.where(hit, 1.0, chosen)

        _, chosen = lax.fori_loop(0, min(SEL_TOPK, n_past + 1), extract, (score, jnp.zeros_like(score)))
        sel = jnp.where(valid, chosen, 0.0)
        sel_ref[...] = jnp.concatenate(
            [sel[(h // G) * T:(h // G + 1) * T] for h in range(N_HEADS)], axis=0).astype(bf16)
        m_ref[...] = jnp.full(m_ref.shape, MASK_NEG, f32)
        l_ref[...] = jnp.zeros(l_ref.shape, f32)
        acc_ref[...] = jnp.zeros(acc_ref.shape, f32)

    qb = qf_ref[...].astype(bf16)
    ks = jnp.concatenate([p[0, 0:LANES, :] for p in pages], axis=1).astype(bf16)
    vs = jnp.concatenate([p[0, LANES:2 * LANES, :] for p in pages], axis=1).astype(bf16)
    kpos = g * KC + lax.broadcasted_iota(jnp.int32, (1, KC), 1)
    keep = (_dot(sel_ref[...], ex_ref[0]) > 0.5) & (kpos <= qpos)
    m, l, acc = _online_chunk(m_ref[...], l_ref[...], acc_ref[...], _dot(qb, ks), keep, vs, True)
    m_ref[...] = m
    l_ref[...] = l
    acc_ref[...] = acc

    @pl.when(g == pl.num_programs(1) - 1)
    def _():
        def new_rows(c0):
            return jnp.concatenate([kvn_ref[0, :, c0:c0 + LANES], jnp.zeros((LANES - T, LANES), f32)],
                                   axis=0).astype(bf16)

        npos = past + lax.broadcasted_iota(jnp.int32, (1, LANES), 1)
        is_new = npos < past + T
        keep_t = is_new & (npos <= qpos) & (sel_ref[:, n_past:n_past + 1].astype(f32) > 0.5)
        _, l_s, acc_s = _online_chunk(m, l, acc, _dot_nt(qb, new_rows(BRANCH_W)), keep_t, new_rows(BRANCH_W + LANES))
        o_s = acc_s / l_s

        nb = win_ref.shape[2]
        bpos = past - nb + lax.broadcasted_iota(jnp.int32, (1, nb), 1)
        keep_b = (bpos <= qpos) & (bpos > qpos - WINDOW) & (bpos >= 0)
        mw, lw, aw = _online_chunk(jnp.full((R, 1), MASK_NEG, f32), jnp.zeros((R, 1), f32), jnp.zeros((R, LANES), f32),
                                   _dot(qb, win_ref[0, 0:LANES, :].astype(bf16)), keep_b,
                                   win_ref[0, LANES:2 * LANES, :].astype(bf16), True)
        keep_n = is_new & (npos <= qpos) & (npos > qpos - WINDOW)
        _, lw, aw = _online_chunk(mw, lw, aw, _dot_nt(qb, new_rows(2 * BRANCH_W)), keep_n,
                                  new_rows(2 * BRANCH_W + LANES))
        o_w = aw / lw

        gates = jax.nn.sigmoid(gl_ref[0])
        gcol = lambda br: jnp.concatenate([gates[:, br * N_HEADS + h:br * N_HEADS + h + 1] for h in range(N_HEADS)],
                                          axis=0)
        o = gcol(0) * oc_ref[...] + gcol(1) * o_s + gcol(2) * o_w
        for mt in range(N_HEADS // 2):
            halves = []
            for h in (2 * mt, 2 * mt + 1):
                piece = o[h * T:(h + 1) * T]
                if h % 2 != h // G:
                    piece = pltpu.roll(piece, N_HEAD_DIM, axis=1)
                halves.append(piece)
            gn = gn_ref[0, :, mt * LANES:(mt + 1) * LANES]
            o_ref[0, :, mt * LANES:(mt + 1) * LANES] = jnp.where(lo_half, halves[0], halves[1]) * (gn * jax.nn.sigmoid(gn))


def nsa_sample(q, kv_new, kc_a, kc_b, gl, gate_n, cache_t, page_table, win_t):
    B, T, _ = q.shape
    n_pages = page_table.shape[1]
    past = n_pages * PAGE_SIZE
    PG = SAMPLE_PG
    KC = PG * PAGE_SIZE
    ng = n_pages // PG
    assert n_pages % PG == 0 and past % SEL_BLOCK == 0 and T <= SEL_BLOCK and T <= LANES
    assert (past + T) // CMP_STRIDE == past // CMP_STRIDE and KC % SEL_BLOCK == 0
    ncp = past // CMP_STRIDE
    n_sel = past // SEL_BLOCK + 1
    nsp = -(-n_sel // LANES) * LANES
    cs = np.arange(ncp) * CMP_STRIDE
    ss = np.arange(nsp) * SEL_BLOCK
    overlap = (cs[:, None] < ss[None, :] + SEL_BLOCK) & (cs[:, None] + CMP_BLOCK > ss[None, :])
    overlap[ncp - 1] = False
    overlap[:, n_sel:] = False
    overlap = jnp.asarray(overlap, jnp.bfloat16)
    expand = jnp.asarray((np.arange(past) // SEL_BLOCK).reshape(ng, 1, KC) == np.arange(nsp)[None, :, None],
                         jnp.bfloat16)
    R = N_HEADS * T
    per_b = lambda shape: pl.BlockSpec((1,) + shape, lambda b, g, pt: (b, 0, 0))
    return pl.pallas_call(
        functools.partial(_sample_attend_body, past=past),
        grid_spec=pltpu.PrefetchScalarGridSpec(
            num_scalar_prefetch=1, grid=(B, ng),
            in_specs=[_page_spec(k, 1) for k in range(PG)] + [
                per_b((BRANCH_W, ncp)), per_b((BRANCH_W, ncp)), per_b((T, N_WIDTH)),
                per_b((T, KV_SLOTS * N_KV_HEADS * N_HEAD_DIM)),
                per_b((T, N_BRANCH * N_HEADS)), per_b((T, N_WIDTH)), per_b((BRANCH_W, win_t.shape[2])),
                pl.BlockSpec((1, nsp, KC), lambda b, g, pt: (g, 0, 0)),
                pl.BlockSpec((ncp, nsp), lambda b, g, pt: (0, 0))],
            out_specs=per_b((T, N_WIDTH)),
            scratch_shapes=[pltpu.VMEM((R, LANES), jnp.float32), pltpu.VMEM((R, nsp), jnp.bfloat16),
                            pltpu.VMEM((R, 1), jnp.float32), pltpu.VMEM((R, 1), jnp.float32),
                            pltpu.VMEM((R, LANES), jnp.float32), pltpu.VMEM((R, LANES), jnp.float32)]),
        out_shape=jax.ShapeDtypeStruct((B, T, N_WIDTH), jnp.float32),
        compiler_params=pltpu.CompilerParams(dimension_semantics=("parallel", "arbitrary"),
                                             vmem_limit_bytes=VMEM_LIMIT),
        name="nsa_sample",
    )(page_table, *([cache_t] * PG), kc_a, kc_b, q, kv_new, gl, gate_n, win_t, expand, overlap)


RWKV_TC = 32
RWKV_ROWS = 8


def _rwkv_scan_body(r_ref, w_ref, k_ref, v_ref, kk_ref, b_ref, s0_ref, y_ref, s_ref):
    n = s_ref.shape[0]

    @pl.when(pl.program_id(1) == 0)
    def _():
        s_ref[...] = s0_ref[...]

    def step(t, carry):
        r, w, k, kk, b = r_ref[t], w_ref[t], k_ref[t], kk_ref[t], b_ref[t]

        def rows(g, carry):
            for u in range(RWKV_ROWS):
                i = g * RWKV_ROWS + u
                s_i = s_ref[i]
                sa = jnp.sum(s_i * kk, axis=0, keepdims=True)
                s_i = s_i * w - sa * b + v_ref[t, pl.ds(i, 1), :] * k
                s_ref[i] = s_i
                y_ref[t, pl.ds(i, 1), :] = jnp.sum(s_i * r, axis=0, keepdims=True)
            return carry

        return lax.fori_loop(0, n // RWKV_ROWS, rows, carry)

    lax.fori_loop(0, r_ref.shape[0], step, 0)


def rwkv_scan(r, w, k, v, kk, b, s0):
    B, T, H, N = r.shape
    BH = B * H
    assert BH % LANES == 0 and N % RWKV_ROWS == 0
    tc = min(RWKV_TC, T)
    assert T % tc == 0

    def lanes_last(x):
        return x.transpose(1, 3, 0, 2).reshape(T, N, BH)

    seq = pl.BlockSpec((tc, N, LANES), lambda l, t: (t, 0, l))
    state = pl.BlockSpec((N, N, LANES), lambda l, t: (0, 0, l))
    y, s_fin = pl.pallas_call(
        _rwkv_scan_body,
        grid=(BH // LANES, T // tc),
        in_specs=[seq] * 6 + [state],
        out_specs=[seq, state],
        out_shape=[jax.ShapeDtypeStruct((T, N, BH), jnp.float32), jax.ShapeDtypeStruct((N, N, BH), jnp.float32)],
        compiler_params=pltpu.CompilerParams(dimension_semantics=("parallel", "arbitrary"),
                                             vmem_limit_bytes=VMEM_LIMIT),
        name="rwkv_scan",
    )(*(lanes_last(x) for x in (r, w, k, v, kk, b)), s0.transpose(2, 3, 0, 1).reshape(N, N, BH))
    return (y.reshape(T, N, B, H).transpose(2, 0, 3, 1),
            s_fin.reshape(N, N, B, H).transpose(2, 3, 0, 1))


def _final_norm_body(x_ref, g_ref, o_ref):
    x = x_ref[...]
    o_ref[...] = x * lax.rsqrt(jnp.mean(x * x, axis=-1, keepdims=True) + RMS_EPS) * g_ref[...]


def final_norm(x, g):
    B, T, D = x.shape
    x2 = x.reshape(B * T, D)
    tm = min(512, B * T)
    out = pl.pallas_call(
        _final_norm_body,
        grid=(B * T // tm,),
        in_specs=[pl.BlockSpec((tm, D), lambda i: (i, 0)), pl.BlockSpec((1, D), lambda i: (0, 0))],
        out_specs=pl.BlockSpec((tm, D), lambda i: (i, 0)),
        out_shape=jax.ShapeDtypeStruct((B * T, D), x.dtype),
        name="final_norm",
    )(x2, g.reshape(1, D))
    return out.reshape(B, T, D)


def kernel(x_prompt, x_sample, cache_kv, cache_kv_win, state_wkv, state_shift, page_table,
           norm_g, w_in, mu_shift, w0, w_decay_up, a0, w_aaa_up, k_k, k_a, r_k, gn_w, gn_b,
           w_cmp_pos, w_cmp_mix, w_out, final_g):
    y_p, y_s = x_prompt, x_sample
    kvp, kvs, wnp, wns, skp, sks, shp, shs = [], [], [], [], [], [], [], []
    for layer in range(DEPTH):
        lw = (norm_g[layer], w_in[layer], mu_shift[layer], w0[layer], w_decay_up[layer], a0[layer],
              w_aaa_up[layer], k_k[layer], k_a[layer], r_k[layer], gn_w[layer], gn_b[layer],
              w_cmp_pos[layer], w_cmp_mix[layer], w_out[layer])
        y_p, p_kv, p_win, p_wkv, p_shift = prompt_layer(y_p, *lw)
        y_s, s_kv, s_win, s_wkv, s_shift = sample_layer(y_s, cache_kv[layer], cache_kv_win[layer],
                                                        state_wkv[layer], state_shift[layer], page_table, *lw)
        kvp.append(p_kv); kvs.append(s_kv); wnp.append(p_win); wns.append(s_win)
        skp.append(p_wkv); sks.append(s_wkv); shp.append(p_shift); shs.append(s_shift)
    y_prompt = final_norm(y_p, final_g)
    y_sample = final_norm(y_s, final_g)
    return (y_prompt, y_sample, jnp.stack(kvp), jnp.stack(kvs), jnp.stack(wnp), jnp.stack(wns),
            jnp.stack(skp), jnp.stack(sks), jnp.stack(shp), jnp.stack(shs))
```

```python
import functools

import jax, jax.numpy as jnp
from jax import lax
import numpy as np
from jax.experimental import pallas as pl
from jax.experimental.pallas import tpu as pltpu

D_MODEL = 1024
BATCH = 16
SEQ = 2048
DEPTH = 1
DEC_BATCH = 32
DEC_SEQ = 4
PAST_LEN = 16384
PAGE_SIZE = 128

R_HEAD_DIM = 64
R_WIDTH = D_MODEL // 2
R_HEADS = R_WIDTH // R_HEAD_DIM
DECAY_LORA = 64
AAA_LORA = 64
R_SHIFT_WIDTH = 3 * R_WIDTH + DECAY_LORA + AAA_LORA
N_HEAD_DIM = 64
N_WIDTH = D_MODEL - R_WIDTH
N_HEADS = N_WIDTH // N_HEAD_DIM
N_KV_HEADS = 2
N_GROUP = N_HEADS // N_KV_HEADS
N_BRANCH = 3
KV_SLOTS = 2 * N_BRANCH
CMP_BLOCK = 32
CMP_STRIDE = 16
SEL_BLOCK = 64
SEL_TOPK = 16
WINDOW = 512
MIX_WIDTH = R_WIDTH + N_WIDTH
KV_WIDTH = KV_SLOTS * N_KV_HEADS * N_HEAD_DIM
PROJ_SPLITS = (R_SHIFT_WIDTH, R_WIDTH, N_WIDTH, N_WIDTH, KV_WIDTH, N_BRANCH * N_HEADS)
PROJ_OFFS = tuple(int(o) for o in np.cumsum((0,) + PROJ_SPLITS))
PROJ_WIDTH = sum(PROJ_SPLITS)
RMS_EPS = 1e-6
GN_EPS = 64e-5
MASK_NEG = -1e30
SEL_BIAS = 1e4
ATTN_SCALE = N_HEAD_DIM ** -0.5

LANES = 128
BRANCH_W = 2 * N_KV_HEADS * N_HEAD_DIM
NSA_TQ = 128
NSA_KC = 512
PROJ_TM = 256
OUT_TM = 256
SAMPLE_PG = 16
RWKV_TC = 32
RWKV_ROWS = 8
VMEM_LIMIT = 56 * 1024 * 1024


def _split3(x):
    hi = x.astype(jnp.bfloat16)
    r1 = x - hi.astype(jnp.float32)
    mid = r1.astype(jnp.bfloat16)
    lo = (r1 - mid.astype(jnp.float32)).astype(jnp.bfloat16)
    return hi, mid, lo


def _dot(a, b):
    return jnp.dot(a, b, preferred_element_type=jnp.float32)


def _dot_nt(a, b):
    return lax.dot_general(a, b, (((1,), (1,)), ((), ())), preferred_element_type=jnp.float32)


def _dot_exact_rhs(a, b01):
    hi, mid, lo = _split3(a)
    return _dot(hi, b01) + _dot(mid, b01) + _dot(lo, b01)


def _dot_x3(a, b):
    ah = a.astype(jnp.bfloat16)
    al = (a - ah.astype(jnp.float32)).astype(jnp.bfloat16)
    bh = b.astype(jnp.bfloat16)
    bl = (b - bh.astype(jnp.float32)).astype(jnp.bfloat16)
    return _dot(ah, bh) + _dot(ah, bl) + _dot(al, bh)


def _silu(x):
    return x * jax.nn.sigmoid(x)


def _softplus(x):
    return jnp.maximum(x, 0.0) + jnp.log1p(jnp.exp(-jnp.abs(x)))


def _masked_softmax_rows(s, mask):
    s = jnp.where(mask, s, MASK_NEG)
    e = jnp.exp(s - jnp.max(s, axis=-1, keepdims=True))
    p = e / jnp.sum(e, axis=-1, keepdims=True)
    return jnp.where(mask, p, 0.0)


def _const_spec(shape, n_grid):
    zeros = (0,) * len(shape)
    if n_grid == 1:
        return pl.BlockSpec(shape, lambda i: zeros)
    return pl.BlockSpec(shape, lambda b, i: zeros)


def _rwkv_prep(zs, w0, a0, k_k, k_a, wdec_t, waaa_t):
    bf16 = jnp.bfloat16
    n = zs.shape[1]
    r = zs[0:R_WIDTH]
    k = zs[R_WIDTH:2 * R_WIDTH]
    v = zs[2 * R_WIDTH:3 * R_WIDTH]
    wd = zs[3 * R_WIDTH:3 * R_WIDTH + DECAY_LORA]
    ad = zs[3 * R_WIDTH + DECAY_LORA:R_SHIFT_WIDTH]
    w_log = -_softplus(-(w0 + _dot(wdec_t, jnp.tanh(wd).astype(bf16)))) - 0.5
    decay = jnp.exp(-jnp.exp(w_log))
    a = jax.nn.sigmoid(a0 + _dot(waaa_t, ad.astype(bf16)))
    kk = (k * k_k).reshape(R_HEADS, R_HEAD_DIM, n)
    kk = (kk * lax.rsqrt(jnp.maximum(jnp.sum(kk * kk, axis=1, keepdims=True), 1e-24))).reshape(R_WIDTH, n)
    k = k * (1.0 + (a - 1.0) * k_a)
    return r, decay, k, v, kk, kk * a


def _prompt_proj_body(x_ref, g_ref, wz_ref, wkv_ref, wn_ref, mu_ref, w0_ref, a0_ref, kk_ref, ka_ref, wdec_ref,
                      waaa_ref, r_o, w_o, k_o, v_o, kkn_o, b_o, kvcs_o, kvw_o, gr_o, q_o, gn_o, gl_o, zl_o, carry_ref):
    bf16 = jnp.bfloat16
    tm = x_ref.shape[1]
    x = x_ref[0]
    xn = (x * lax.rsqrt(jnp.mean(x * x, axis=-1, keepdims=True) + RMS_EPS) * g_ref[...]).astype(bf16)

    z = _dot_nt(wz_ref[...], xn)

    @pl.when(pl.program_id(1) == 0)
    def _():
        carry_ref[...] = jnp.zeros(carry_ref.shape, jnp.float32)

    first = lax.broadcasted_iota(jnp.int32, (1, tm), 1) == 0
    zp = jnp.where(first, carry_ref[:, LANES - 1:LANES], pltpu.roll(z, 1, axis=1))
    carry_ref[...] = z[:, tm - LANES:tm]
    zl_o[0] = z[:, tm - LANES:tm]
    zs = z + (zp - z) * mu_ref[...]
    outs = _rwkv_prep(zs, w0_ref[...], a0_ref[...], kk_ref[...], ka_ref[...], wdec_ref[...], waaa_ref[...])
    for o_ref, val in zip((r_o, w_o, k_o, v_o, kkn_o, b_o), outs):
        o_ref[0] = val

    kv = _dot_nt(wkv_ref[...], xn)
    kvcs_o[0] = kv[0:2 * BRANCH_W]
    kvw_o[0] = kv[2 * BRANCH_W:3 * BRANCH_W]

    nrm = _dot(xn, wn_ref[...])
    gr_o[0] = nrm[:, 0:R_WIDTH]
    q_o[0] = nrm[:, R_WIDTH:R_WIDTH + N_WIDTH]
    gn_o[0] = nrm[:, R_WIDTH + N_WIDTH:R_WIDTH + 2 * N_WIDTH]
    gl_o[0] = nrm[:, R_WIDTH + 2 * N_WIDTH:]


def prompt_in_proj(x, norm_g, w_in, mu_shift, w0, w_decay_up, a0, w_aaa_up, k_k, k_a):
    B, T, D = x.shape
    tm = PROJ_TM
    assert T % tm == 0 and tm % LANES == 0
    f32, bf16 = jnp.float32, jnp.bfloat16
    o = PROJ_OFFS
    wb = w_in.astype(bf16)
    wz_t = wb[:, o[0]:o[1]].T
    wkv_t = wb[:, o[4]:o[5]].T
    gl_pad = LANES - PROJ_SPLITS[5]
    wn = jnp.concatenate([wb[:, o[1]:o[4]], wb[:, o[5]:o[6]], jnp.zeros((D, gl_pad), bf16)], axis=1)
    col = lambda p: p.astype(f32).reshape(-1, 1)
    consts = [norm_g.astype(f32).reshape(1, D), wz_t, wkv_t, wn, col(mu_shift), col(w0), col(a0), col(k_k), col(k_a),
              w_decay_up.astype(bf16).T, w_aaa_up.astype(bf16).T]
    cm = lambda c: pl.BlockSpec((1, c, tm), lambda b, i: (b, 0, i))
    tok = lambda c: pl.BlockSpec((1, tm, c), lambda b, i: (b, i, 0))
    cms = lambda c: jax.ShapeDtypeStruct((B, c, T), f32)
    toks = lambda c: jax.ShapeDtypeStruct((B, T, c), f32)
    return pl.pallas_call(
        _prompt_proj_body,
        grid=(B, T // tm),
        in_specs=[tok(D)] + [_const_spec(c.shape, 2) for c in consts],
        out_specs=[cm(R_WIDTH)] * 6 + [cm(2 * BRANCH_W), cm(BRANCH_W), tok(R_WIDTH), tok(N_WIDTH), tok(N_WIDTH),
                                       tok(LANES), pl.BlockSpec((1, R_SHIFT_WIDTH, LANES), lambda b, i: (b, 0, 0))],
        out_shape=[cms(R_WIDTH)] * 6 + [cms(2 * BRANCH_W), cms(BRANCH_W), toks(R_WIDTH), toks(N_WIDTH), toks(N_WIDTH),
                                        toks(LANES), jax.ShapeDtypeStruct((B, R_SHIFT_WIDTH, LANES), f32)],
        scratch_shapes=[pltpu.VMEM((R_SHIFT_WIDTH, LANES), f32)],
        compiler_params=pltpu.CompilerParams(dimension_semantics=("parallel", "arbitrary"),
                                             vmem_limit_bytes=VMEM_LIMIT),
        name="prompt_in_proj",
    )(x, *consts)


def _rwkv_scan_body(r_ref, w_ref, k_ref, v_ref, kk_ref, b_ref, s0_ref, rk_ref, gw_ref, gb_ref, y_ref, s_ref):
    n = s_ref.shape[0]

    @pl.when(pl.program_id(1) == 0)
    def _():
        s_ref[...] = s0_ref[...]

    def step(t, carry):
        r, w, k, kk, b = r_ref[t], w_ref[t], k_ref[t], kk_ref[t], b_ref[t]

        def rows(g, carry):
            for u in range(RWKV_ROWS):
                i = g * RWKV_ROWS + u
                s_i = s_ref[i]
                sa = jnp.sum(s_i * kk, axis=0, keepdims=True)
                s_i = s_i * w - sa * b + v_ref[t, pl.ds(i, 1), :] * k
                s_ref[i] = s_i
                y_ref[t, pl.ds(i, 1), :] = jnp.sum(s_i * r, axis=0, keepdims=True)
            return carry

        lax.fori_loop(0, n // RWKV_ROWS, rows, 0)
        y = y_ref[t]
        yc = y - jnp.mean(y, axis=0, keepdims=True)
        y = yc * lax.rsqrt(jnp.mean(yc * yc, axis=0, keepdims=True) + GN_EPS)
        y = y * gw_ref[...] + gb_ref[...]
        y_ref[t] = y + jnp.sum(r * k * rk_ref[...], axis=0, keepdims=True) * v_ref[t]
        return carry

    lax.fori_loop(0, r_ref.shape[0], step, 0)


def rwkv_scan_lanes(seqs, s0, r_k, gn_w, gn_b, n_batch):
    T, N, BH = seqs[0].shape
    assert BH % LANES == 0 and N % RWKV_ROWS == 0
    tc = min(RWKV_TC, T)
    assert T % tc == 0
    per_lane = lambda p: jnp.tile(p.astype(jnp.float32).reshape(BH // n_batch, N).T, (1, n_batch))
    seq = pl.BlockSpec((tc, N, LANES), lambda l, t: (t, 0, l))
    state = pl.BlockSpec((N, N, LANES), lambda l, t: (0, 0, l))
    param = pl.BlockSpec((N, LANES), lambda l, t: (0, l))
    return pl.pallas_call(
        _rwkv_scan_body,
        grid=(BH // LANES, T // tc),
        in_specs=[seq] * 6 + [state] + [param] * 3,
        out_specs=[seq, state],
        out_shape=[jax.ShapeDtypeStruct((T, N, BH), jnp.float32), jax.ShapeDtypeStruct((N, N, BH), jnp.float32)],
        compiler_params=pltpu.CompilerParams(dimension_semantics=("parallel", "arbitrary"),
                                             vmem_limit_bytes=VMEM_LIMIT),
        name="rwkv_scan",
    )(*seqs, s0, per_lane(r_k), per_lane(gn_w), per_lane(gn_b))


def _lanes_from_cm(a, T):
    return a.reshape(-1, R_HEAD_DIM, T).transpose(2, 1, 0)


def _y_from_lanes(y, B):
    T, N, _ = y.shape
    return y.reshape(T, N, B, R_HEADS).transpose(2, 0, 3, 1).reshape(B, T, R_WIDTH)


def _state_to_lanes(s):
    B, H, N, _ = s.shape
    return s.astype(jnp.float32).transpose(2, 3, 0, 1).reshape(N, N, B * H)


def _state_from_lanes(s, B):
    N = s.shape[0]
    return s.reshape(N, N, B, R_HEADS).transpose(2, 3, 0, 1)


def _cmp_weights(w_pos, w_mix, n_rows):
    wp = jnp.broadcast_to(w_pos.astype(jnp.float32).reshape(2, 1, CMP_BLOCK, N_HEAD_DIM).transpose(2, 0, 1, 3),
                          (CMP_BLOCK, 2, N_KV_HEADS, N_HEAD_DIM)).reshape(CMP_BLOCK, BRANCH_W)
    wa = jnp.tile(wp[:CMP_STRIDE], (n_rows // CMP_STRIDE, 1)).T
    wb = jnp.tile(wp[CMP_STRIDE:], (n_rows // CMP_STRIDE, 1)).T
    wmix = jnp.zeros((BRANCH_W, BRANCH_W), jnp.float32)
    for e in range(2):
        for kh in range(N_KV_HEADS):
            o = (e * N_KV_HEADS + kh) * N_HEAD_DIM
            wmix = wmix.at[o:o + N_HEAD_DIM, o:o + N_HEAD_DIM].set(w_mix[e].astype(jnp.float32))
    return wa, wb, wmix.T


def _compress_prompt_body(kv_ref, wa_ref, wb_ref, pool_ref, wmix_ref, o_ref):
    x = kv_ref[0]
    xw = jnp.concatenate([x * wa_ref[...], x * wb_ref[...]], axis=1)
    o_ref[0] = _dot_x3(wmix_ref[...], _dot_exact_rhs(xw, pool_ref[...]))


def compress_prompt(kv_cs, w_pos, w_mix):
    B, _, T = kv_cs.shape
    assert CMP_BLOCK == 2 * CMP_STRIDE and T % CMP_STRIDE == 0
    ncp = T // CMP_STRIDE
    wa, wb, wmix_t = _cmp_weights(w_pos, w_mix, T)
    chunk = np.arange(T) // CMP_STRIDE
    pool = np.concatenate([chunk[:, None] == np.arange(ncp)[None, :],
                           chunk[:, None] == np.arange(ncp)[None, :] + 1], axis=0)
    pool = jnp.asarray(pool, jnp.bfloat16)
    return pl.pallas_call(
        _compress_prompt_body,
        grid=(B,),
        in_specs=[pl.BlockSpec((1, BRANCH_W, T), lambda b: (b, 0, 0))] + [_const_spec(c.shape, 1)
                                                                          for c in (wa, wb, pool, wmix_t)],
        out_specs=pl.BlockSpec((1, BRANCH_W, ncp), lambda b: (b, 0, 0)),
        out_shape=jax.ShapeDtypeStruct((B, BRANCH_W, ncp), jnp.float32),
        compiler_params=pltpu.CompilerParams(dimension_semantics=("parallel",), vmem_limit_bytes=VMEM_LIMIT),
        name="compress_prompt",
    )(kv_cs, wa, wb, pool, wmix_t)


def _nsa_prompt_body(q_ref, ksel_ref, kwin_ref, kc_ref, gl_ref, gn_ref, ovt_ref, ex_ref, o_ref):
    f32, bf16 = jnp.float32, jnp.bfloat16
    TQ, KC, G = NSA_TQ, NSA_KC, N_GROUP
    ncp = kc_ref.shape[2]
    n_sel = ovt_ref.shape[0]
    topk = min(SEL_TOPK, n_sel)
    qs = pl.program_id(1) * TQ
    qpos = qs + lax.broadcasted_iota(jnp.int32, (TQ, 1), 0)
    lo_half = lax.broadcasted_iota(jnp.int32, (TQ, LANES), 1) < N_HEAD_DIM

    def q_padded(h, half):
        tile = q_ref[0, :, (h // 2) * LANES:(h // 2 + 1) * LANES] * ATTN_SCALE
        if h % 2 != half:
            tile = pltpu.roll(tile, N_HEAD_DIM, axis=1)
        return jnp.where(lo_half if half == 0 else jnp.logical_not(lo_half), tile, 0.0)

    wstart = pl.multiple_of(jnp.maximum(qs - WINDOW, 0), TQ)
    kw = kwin_ref[0, 0:LANES, pl.ds(wstart, WINDOW + TQ)].astype(bf16)
    vw = kwin_ref[0, LANES:2 * LANES, pl.ds(wstart, WINDOW + TQ)].astype(bf16)
    kposw = wstart + lax.broadcasted_iota(jnp.int32, (1, WINDOW + TQ), 1)
    mask_w = (kposw <= qpos) & (kposw > qpos - WINDOW)

    kck = kc_ref[0, 0:LANES, :]
    kcv = kc_ref[0, LANES:2 * LANES, :].astype(bf16)
    kend = lax.broadcasted_iota(jnp.int32, (1, ncp), 1) * CMP_STRIDE + (CMP_BLOCK - 1)
    mask_c = kend <= qpos

    sid_t = lax.broadcasted_iota(jnp.int32, (n_sel, 1), 0)
    cur_t = (qs + lax.broadcasted_iota(jnp.int32, (1, TQ), 1)) // SEL_BLOCK
    valid_t = sid_t <= cur_t
    forced_t = (sid_t == 0) | (sid_t == cur_t) | (sid_t == cur_t - 1)
    n_chunks = qs // KC + 1

    gates = jax.nn.sigmoid(gl_ref[0])
    heads = [None] * N_HEADS
    for kvh in range(N_KV_HEADS):
        qg = jnp.concatenate([q_padded(kvh * G + g, kvh) for g in range(G)], axis=0)
        qb = qg.astype(bf16)

        p = _masked_softmax_rows(_dot_x3(qg, kck).reshape(G, TQ, ncp), mask_c[None])
        o_c = _dot_nt(p.reshape(G * TQ, ncp).astype(bf16), kcv).reshape(G, TQ, LANES)
        imp = jnp.sum(p, axis=0)

        hi, mid, lo = _split3(imp)
        ovt = ovt_ref[...]
        score = _dot_nt(ovt, hi) + _dot_nt(ovt, mid) + _dot_nt(ovt, lo)
        score = jnp.where(valid_t, score + jnp.where(forced_t, SEL_BIAS, 0.0), -SEL_BIAS)
        rank = jnp.zeros((n_sel, TQ), jnp.int32)
        for s2 in range(n_sel):
            row = score[s2:s2 + 1, :]
            ahead = (row > score) | ((row == score) & (s2 < sid_t))
            rank = rank + ahead.astype(jnp.int32)
        sel = jnp.where((rank < topk) & valid_t, 1.0, 0.0).T.astype(bf16)

        def chunk(c, carry, qb=qb, sel=sel):
            m, l, acc = carry
            k0 = pl.multiple_of(c * KC, KC)
            ks = ksel_ref[0, 0:LANES, pl.ds(k0, KC)].astype(bf16)
            vs = ksel_ref[0, LANES:2 * LANES, pl.ds(k0, KC)].astype(bf16)
            selx = _dot(sel, ex_ref[:, pl.ds(k0, KC)])
            kpos = k0 + lax.broadcasted_iota(jnp.int32, (1, KC), 1)
            keep = (selx > 0.5) & (kpos <= qpos)
            sc = jnp.where(keep[None], _dot(qb, ks).reshape(G, TQ, KC), MASK_NEG)
            m_new = jnp.maximum(m, jnp.max(sc, axis=-1, keepdims=True))
            alpha = jnp.exp(m - m_new)
            e = jnp.exp(sc - m_new)
            l = alpha * l + jnp.sum(e, axis=-1, keepdims=True)
            pv = _dot_nt(e.reshape(G * TQ, KC).astype(bf16), vs).reshape(G, TQ, LANES)
            return m_new, l, alpha * acc + pv

        m0 = jnp.full((G, TQ, 1), MASK_NEG, f32)
        l0 = jnp.zeros((G, TQ, 1), f32)
        a0 = jnp.zeros((G, TQ, LANES), f32)
        _, l, acc = lax.fori_loop(0, n_chunks, chunk, (m0, l0, a0))
        o_s = acc / l

        sw = jnp.where(mask_w[None], _dot(qb, kw).reshape(G, TQ, WINDOW + TQ), MASK_NEG)
        ew = jnp.exp(sw - jnp.max(sw, axis=-1, keepdims=True))
        o_w = (_dot_nt(ew.reshape(G * TQ, WINDOW + TQ).astype(bf16), vw).reshape(G, TQ, LANES)
               / jnp.sum(ew, axis=-1, keepdims=True))

        for g in range(G):
            h = kvh * G + g
            o = (gates[:, h:h + 1] * o_c[g] + gates[:, N_HEADS + h:N_HEADS + h + 1] * o_s[g]
                 + gates[:, 2 * N_HEADS + h:2 * N_HEADS + h + 1] * o_w[g])
            if h % 2 != kvh:
                o = pltpu.roll(o, N_HEAD_DIM, axis=1)
            heads[h] = o

    for m in range(N_HEADS // 2):
        tile = jnp.where(lo_half, heads[2 * m], heads[2 * m + 1])
        o_ref[0, :, m * LANES:(m + 1) * LANES] = tile * _silu(gn_ref[0, :, m * LANES:(m + 1) * LANES])


def nsa_prompt(q, kv_cs, kv_w, kc, gl, gate_n):
    B, T, _ = q.shape
    TQ, KC = NSA_TQ, NSA_KC
    assert N_KV_HEADS == 2 and N_HEAD_DIM * 2 == LANES and T % KC == 0 and KC % TQ == 0 and T >= WINDOW + TQ
    assert SEL_BLOCK % CMP_STRIDE == 0 and WINDOW % TQ == 0 and TQ % LANES == 0
    ncp = T // CMP_STRIDE
    n_sel = T // SEL_BLOCK
    cs = np.arange(ncp) * CMP_STRIDE
    ss = np.arange(n_sel) * SEL_BLOCK
    overlap = (cs[:, None] < ss[None, :] + SEL_BLOCK) & (cs[:, None] + CMP_BLOCK > ss[None, :])
    overlap[ncp - 1] = False
    overlap_t = jnp.asarray(overlap.T, jnp.bfloat16)
    expand = jnp.asarray(np.arange(T)[None, :] // SEL_BLOCK == np.arange(n_sel)[:, None], jnp.bfloat16)
    return pl.pallas_call(
        _nsa_prompt_body,
        grid=(B, T // TQ),
        in_specs=[pl.BlockSpec((1, TQ, N_WIDTH), lambda b, i: (b, i, 0)),
                  pl.BlockSpec((1, BRANCH_W, T), lambda b, i: (b, 1, 0)),
                  pl.BlockSpec((1, BRANCH_W, T), lambda b, i: (b, 0, 0)),
                  pl.BlockSpec((1, BRANCH_W, ncp), lambda b, i: (b, 0, 0)),
                  pl.BlockSpec((1, TQ, LANES), lambda b, i: (b, i, 0)),
                  pl.BlockSpec((1, TQ, N_WIDTH), lambda b, i: (b, i, 0)),
                  _const_spec((n_sel, ncp), 2), _const_spec((n_sel, T), 2)],
        out_specs=pl.BlockSpec((1, TQ, N_WIDTH), lambda b, i: (b, i, 0)),
        out_shape=jax.ShapeDtypeStruct((B, T, N_WIDTH), jnp.float32),
        compiler_params=pltpu.CompilerParams(dimension_semantics=("parallel", "arbitrary"),
                                             vmem_limit_bytes=VMEM_LIMIT),
        name="nsa_prompt",
    )(q, kv_cs, kv_w, kc, gl, gate_n, overlap_t, expand)


def _page_spec(k, branch):
    return pl.BlockSpec((1, BRANCH_W, PAGE_SIZE), lambda b, g, pt: (pt[b, g * SAMPLE_PG + k], branch, 0))


def _sample_compress_body(pt_ref, *refs):
    pages = refs[:SAMPLE_PG]
    wa_ref, wb_ref, pool_ref, wmix_ref, oa_ref, ob_ref = refs[SAMPLE_PG:]
    x = jnp.concatenate([p[0] for p in pages], axis=1)

    def pooled(w_ref):
        xw = x * w_ref[...]
        hi = xw.astype(jnp.bfloat16)
        lo = (xw - hi.astype(jnp.float32)).astype(jnp.bfloat16)
        return _dot(hi, pool_ref[...]) + _dot(lo, pool_ref[...])

    oa_ref[0] = _dot_x3(wmix_ref[...], pooled(wa_ref))
    ob_ref[0] = _dot_x3(wmix_ref[...], pooled(wb_ref))


def sample_compress(cache_t, page_table, w_pos, w_mix):
    B, n_pages = page_table.shape
    PG = SAMPLE_PG
    assert n_pages % PG == 0 and CMP_BLOCK == 2 * CMP_STRIDE and PAGE_SIZE % CMP_STRIDE == 0
    cols = PG * PAGE_SIZE // CMP_STRIDE
    assert cols % LANES == 0
    ncp = n_pages * PAGE_SIZE // CMP_STRIDE
    wa, wb, wmix_t = _cmp_weights(w_pos, w_mix, PG * PAGE_SIZE)
    pool = jnp.asarray(np.arange(PG * PAGE_SIZE)[:, None] // CMP_STRIDE == np.arange(cols)[None, :], jnp.bfloat16)
    const = lambda shape: pl.BlockSpec(shape, lambda b, g, pt: (0,) * len(shape))
    out = pl.BlockSpec((1, BRANCH_W, cols), lambda b, g, pt: (b, 0, g))
    return pl.pallas_call(
        _sample_compress_body,
        grid_spec=pltpu.PrefetchScalarGridSpec(
            num_scalar_prefetch=1, grid=(B, n_pages // PG),
            in_specs=[_page_spec(k, 0) for k in range(PG)] + [const(wa.shape), const(wb.shape), const(pool.shape),
                                                              const(wmix_t.shape)],
            out_specs=[out, out]),
        out_shape=[jax.ShapeDtypeStruct((B, BRANCH_W, ncp), jnp.float32)] * 2,
        compiler_params=pltpu.CompilerParams(dimension_semantics=("parallel", "arbitrary"),
                                             vmem_limit_bytes=VMEM_LIMIT),
        name="sample_compress",
    )(page_table, *([cache_t] * PG), wa, wb, pool, wmix_t)


def _online_chunk(m, l, acc, s, keep, v, v_channels_major=False):
    s = jnp.where(keep, s, MASK_NEG)
    m_new = jnp.maximum(m, jnp.max(s, axis=-1, keepdims=True))
    alpha = jnp.exp(m - m_new)
    e = jnp.where(keep, jnp.exp(s - m_new), 0.0)
    eb = e.astype(jnp.bfloat16)
    pv = _dot_nt(eb, v) if v_channels_major else _dot(eb, v)
    return m_new, alpha * l + jnp.sum(e, axis=-1, keepdims=True), alpha * acc + pv


def _sample_attend_body(pt_ref, *refs, past):
    f32, bf16 = jnp.float32, jnp.bfloat16
    PG, G = SAMPLE_PG, N_GROUP
    pages = refs[:PG]
    (ka_ref, kb_ref, q_ref, kvn_ref, gl_ref, gn_ref, win_ref, ex_ref, ov_ref, o_ref,
     qf_ref, sel_ref, m_ref, l_ref, acc_ref, oc_ref) = refs[PG:]
    T = q_ref.shape[1]
    R = N_HEADS * T
    ncp = ka_ref.shape[2]
    nsp = ov_ref.shape[1]
    KC = PG * PAGE_SIZE
    n_past = past // SEL_BLOCK
    g = pl.program_id(1)
    lo_half = lax.broadcasted_iota(jnp.int32, (T, LANES), 1) < N_HEAD_DIM
    row = lax.broadcasted_iota(jnp.int32, (R, 1), 0)
    qpos = past + row % T

    @pl.when(g == 0)
    def _():
        pieces = []
        for h in range(N_HEADS):
            kvh = h // G
            tile = q_ref[0, :, (h // 2) * LANES:(h // 2 + 1) * LANES] * ATTN_SCALE
            if h % 2 != kvh:
                tile = pltpu.roll(tile, N_HEAD_DIM, axis=1)
            pieces.append(jnp.where(lo_half if kvh == 0 else jnp.logical_not(lo_half), tile, 0.0))
        q32 = jnp.concatenate(pieces, axis=0)
        qf_ref[...] = q32

        kc = ka_ref[0] + pltpu.roll(kb_ref[0], ncp - 1, axis=1)
        kend = lax.broadcasted_iota(jnp.int32, (1, ncp), 1) * CMP_STRIDE + (CMP_BLOCK - 1)
        p = _masked_softmax_rows(_dot_x3(q32, kc[0:LANES]), kend <= qpos)
        oc_ref[...] = _dot_nt(p.astype(bf16), kc[LANES:2 * LANES].astype(bf16))
        imp = jnp.concatenate(
            [sum(p[(kvh * G + gg) * T:(kvh * G + gg + 1) * T] for gg in range(G)) for kvh in range(N_KV_HEADS)],
            axis=0)

        sid = lax.broadcasted_iota(jnp.int32, (1, nsp), 1)
        sid_f = sid.astype(f32)
        cur = (past + lax.broadcasted_iota(jnp.int32, (N_KV_HEADS * T, 1), 0) % T) // SEL_BLOCK
        valid = sid <= cur
        forced = (sid == 0) | (sid == cur) | (sid == cur - 1)
        score = _dot_exact_rhs(imp, ov_ref[...])
        score = jnp.where(valid, score + jnp.where(forced, SEL_BIAS, 0.0), -SEL_BIAS)

        def extract(_, carry):
            sc, chosen = carry
            best = jnp.max(sc, axis=-1, keepdims=True)
            first = jnp.min(jnp.where(sc == best, sid_f, float(nsp)), axis=-1, keepdims=True)
            hit = sid_f == first
            return jnp.where(hit, -3e38, sc), jnp.where(hit, 1.0, chosen)

        _, chosen = lax.fori_loop(0, min(SEL_TOPK, n_past + 1), extract, (score, jnp.zeros_like(score)))
        sel = jnp.where(valid, chosen, 0.0)
        sel_ref[...] = jnp.concatenate(
            [sel[(h // G) * T:(h // G + 1) * T] for h in range(N_HEADS)], axis=0).astype(bf16)
        m_ref[...] = jnp.full(m_ref.shape, MASK_NEG, f32)
        l_ref[...] = jnp.zeros(l_ref.shape, f32)
        acc_ref[...] = jnp.zeros(acc_ref.shape, f32)

    qb = qf_ref[...].astype(bf16)
    ks = jnp.concatenate([p[0, 0:LANES, :] for p in pages], axis=1).astype(bf16)
    vs = jnp.concatenate([p[0, LANES:2 * LANES, :] for p in pages], axis=1).astype(bf16)
    kpos = g * KC + lax.broadcasted_iota(jnp.int32, (1, KC), 1)
    keep = (_dot(sel_ref[...], ex_ref[0]) > 0.5) & (kpos <= qpos)
    m, l, acc = _online_chunk(m_ref[...], l_ref[...], acc_ref[...], _dot(qb, ks), keep, vs, True)
    m_ref[...] = m
    l_ref[...] = l
    acc_ref[...] = acc

    @pl.when(g == pl.num_programs(1) - 1)
    def _():
        def new_rows(c0):
            return jnp.concatenate([kvn_ref[0, :, c0:c0 + LANES], jnp.zeros((LANES - T, LANES), f32)],
                                   axis=0).astype(bf16)

        npos = past + lax.broadcasted_iota(jnp.int32, (1, LANES), 1)
        is_new = npos < past + T
        keep_t = is_new & (npos <= qpos) & (sel_ref[:, n_past:n_past + 1].astype(f32) > 0.5)
        _, l_s, acc_s = _online_chunk(m, l, acc, _dot_nt(qb, new_rows(BRANCH_W)), keep_t, new_rows(BRANCH_W + LANES))
        o_s = acc_s / l_s

        nb = win_ref.shape[2]
        bpos = past - nb + lax.broadcasted_iota(jnp.int32, (1, nb), 1)
        keep_b = (bpos <= qpos) & (bpos > qpos - WINDOW) & (bpos >= 0)
        mw, lw, aw = _online_chunk(jnp.full((R, 1), MASK_NEG, f32), jnp.zeros((R, 1), f32), jnp.zeros((R, LANES), f32),
                                   _dot(qb, win_ref[0, 0:LANES, :].astype(bf16)), keep_b,
                                   win_ref[0, LANES:2 * LANES, :].astype(bf16), True)
        keep_n = is_new & (npos <= qpos) & (npos > qpos - WINDOW)
        _, lw, aw = _online_chunk(mw, lw, aw, _dot_nt(qb, new_rows(2 * BRANCH_W)), keep_n,
                                  new_rows(2 * BRANCH_W + LANES))
        o_w = aw / lw

        gates = jax.nn.sigmoid(gl_ref[0])
        gcol = lambda br: jnp.concatenate([gates[:, br * N_HEADS + h:br * N_HEADS + h + 1] for h in range(N_HEADS)],
                                          axis=0)
        o = gcol(0) * oc_ref[...] + gcol(1) * o_s + gcol(2) * o_w
        for mt in range(N_HEADS // 2):
            halves = []
            for h in (2 * mt, 2 * mt + 1):
                piece = o[h * T:(h + 1) * T]
                if h % 2 != h // G:
                    piece = pltpu.roll(piece, N_HEAD_DIM, axis=1)
                halves.append(piece)
            o_ref[0, :, mt * LANES:(mt + 1) * LANES] = (jnp.where(lo_half, halves[0], halves[1])
                                                        * _silu(gn_ref[0, :, mt * LANES:(mt + 1) * LANES]))


def nsa_sample(q, kv_new, kc_a, kc_b, gl, gate_n, cache_t, page_table, win_t):
    B, T, _ = q.shape
    n_pages = page_table.shape[1]
    past = n_pages * PAGE_SIZE
    PG = SAMPLE_PG
    KC = PG * PAGE_SIZE
    ng = n_pages // PG
    assert n_pages % PG == 0 and past % SEL_BLOCK == 0 and T <= SEL_BLOCK and T <= LANES
    assert (past + T) // CMP_STRIDE == past // CMP_STRIDE and KC % SEL_BLOCK == 0
    ncp = past // CMP_STRIDE
    n_sel = past // SEL_BLOCK + 1
    nsp = -(-n_sel // LANES) * LANES
    cs = np.arange(ncp) * CMP_STRIDE
    ss = np.arange(nsp) * SEL_BLOCK
    overlap = (cs[:, None] < ss[None, :] + SEL_BLOCK) & (cs[:, None] + CMP_BLOCK > ss[None, :])
    overlap[ncp - 1] = False
    overlap[:, n_sel:] = False
    overlap = jnp.asarray(overlap, jnp.bfloat16)
    expand = jnp.asarray((np.arange(past) // SEL_BLOCK).reshape(ng, 1, KC) == np.arange(nsp)[None, :, None],
                         jnp.bfloat16)
    R = N_HEADS * T
    per_b = lambda shape: pl.BlockSpec((1,) + shape, lambda b, g, pt: (b, 0, 0))
    return pl.pallas_call(
        functools.partial(_sample_attend_body, past=past),
        grid_spec=pltpu.PrefetchScalarGridSpec(
            num_scalar_prefetch=1, grid=(B, ng),
            in_specs=[_page_spec(k, 1) for k in range(PG)] + [
                per_b((BRANCH_W, ncp)), per_b((BRANCH_W, ncp)), per_b((T, N_WIDTH)), per_b((T, KV_WIDTH)),
                per_b((T, N_BRANCH * N_HEADS)), per_b((T, N_WIDTH)), per_b((BRANCH_W, win_t.shape[2])),
                pl.BlockSpec((1, nsp, KC), lambda b, g, pt: (g, 0, 0)),
                pl.BlockSpec((ncp, nsp), lambda b, g, pt: (0, 0))],
            out_specs=per_b((T, N_WIDTH)),
            scratch_shapes=[pltpu.VMEM((R, LANES), jnp.float32), pltpu.VMEM((R, nsp), jnp.bfloat16),
                            pltpu.VMEM((R, 1), jnp.float32), pltpu.VMEM((R, 1), jnp.float32),
                            pltpu.VMEM((R, LANES), jnp.float32), pltpu.VMEM((R, LANES), jnp.float32)]),
        out_shape=jax.ShapeDtypeStruct((B, T, N_WIDTH), jnp.float32),
        compiler_params=pltpu.CompilerParams(dimension_semantics=("parallel", "arbitrary"),
                                             vmem_limit_bytes=VMEM_LIMIT),
        name="nsa_sample",
    )(page_table, *([cache_t] * PG), kc_a, kc_b, q, kv_new, gl, gate_n, win_t, expand, overlap)


def _out_proj_body(yr_ref, gr_ref, on_ref, x_ref, w_ref, g_ref, o_ref, *, final_norm):
    bf16 = jnp.bfloat16
    a = (yr_ref[...] * _silu(gr_ref[...])).astype(bf16)
    y = x_ref[...] + _dot(a, w_ref[0:R_WIDTH, :]) + _dot(on_ref[...].astype(bf16), w_ref[R_WIDTH:MIX_WIDTH, :])
    if final_norm:
        y = y * lax.rsqrt(jnp.mean(y * y, axis=-1, keepdims=True) + RMS_EPS) * g_ref[...]
    o_ref[...] = y


def out_proj(y_r, gate_r, o_n, x, w_out, final_g):
    M, D = x.shape
    tm = min(OUT_TM, M)
    assert M % tm == 0
    g = (jnp.ones((D,), jnp.float32) if final_g is None else final_g.astype(jnp.float32)).reshape(1, D)
    rows = lambda c: pl.BlockSpec((tm, c), lambda i: (i, 0))
    return pl.pallas_call(
        functools.partial(_out_proj_body, final_norm=final_g is not None),
        grid=(M // tm,),
        in_specs=[rows(R_WIDTH), rows(R_WIDTH), rows(N_WIDTH), rows(D), _const_spec((MIX_WIDTH, D), 1),
                  _const_spec((1, D), 1)],
        out_specs=rows(D),
        out_shape=jax.ShapeDtypeStruct((M, D), jnp.float32),
        compiler_params=pltpu.CompilerParams(dimension_semantics=("parallel",), vmem_limit_bytes=VMEM_LIMIT),
        name="out_proj",
    )(y_r, gate_r, o_n, x, w_out.astype(jnp.bfloat16), g)


def rms_norm(x, g):
    xf = x.astype(jnp.float32)
    y = xf * lax.rsqrt(jnp.mean(xf * xf, axis=-1, keepdims=True) + RMS_EPS)
    return (y * g.astype(jnp.float32)).astype(x.dtype)


def prompt_layer(x, final_g, norm_g, w_in, mu_shift, w0, w_decay_up, a0, w_aaa_up, k_k, k_a, r_k, gn_w, gn_b,
                 w_cmp_pos, w_cmp_mix, w_out):
    B, T, D = x.shape
    (r, w, k, v, kk, b, kv_cs, kv_w, gate_r, q, gate_n, gl, z_last) = prompt_in_proj(
        x, norm_g, w_in, mu_shift, w0, w_decay_up, a0, w_aaa_up, k_k, k_a)
    s0 = jnp.zeros((R_HEAD_DIM, R_HEAD_DIM, B * R_HEADS), jnp.float32)
    y, s_fin = rwkv_scan_lanes([_lanes_from_cm(a, T) for a in (r, w, k, v, kk, b)], s0, r_k, gn_w, gn_b, B)
    o_n = nsa_prompt(q, kv_cs, kv_w, compress_prompt(kv_cs, w_cmp_pos, w_cmp_mix), gl, gate_n)
    y = out_proj(_y_from_lanes(y, B).reshape(B * T, R_WIDTH), gate_r.reshape(B * T, R_WIDTH),
                 o_n.reshape(B * T, N_WIDTH), x.reshape(B * T, D), w_out, final_g).reshape(B, T, D)
    heads = (N_KV_HEADS, N_HEAD_DIM)
    kv_rows = kv_cs.reshape((B, 4) + heads + (T,)).transpose(0, 4, 1, 2, 3)
    n_keep = min(WINDOW, T)
    win_rows = kv_w[:, :, T - n_keep:].reshape((B, 2) + heads + (n_keep,)).transpose(0, 4, 1, 2, 3)
    return y, kv_rows, win_rows, _state_from_lanes(s_fin, B), z_last[:, :, LANES - 1]


def sample_layer(x, final_g, cache, win_buf, wkv0, shift0, page_table, norm_g, w_in, mu_shift, w0, w_decay_up, a0,
                 w_aaa_up, k_k, k_a, r_k, gn_w, gn_b, w_cmp_pos, w_cmp_mix, w_out):
    B, T, D = x.shape
    f32 = jnp.float32
    p = rms_norm(x, norm_g) @ w_in
    z, gate_r, q, gate_n, kv, gl = (p[..., PROJ_OFFS[i]:PROJ_OFFS[i + 1]] for i in range(len(PROJ_SPLITS)))
    zp = jnp.concatenate([shift0[:, None].astype(z.dtype), z[:, :-1]], axis=1)
    zs = (z + (zp - z) * mu_shift).reshape(B * T, R_SHIFT_WIDTH).T
    col = lambda c: c.astype(f32).reshape(-1, 1)
    seqs = _rwkv_prep(zs, col(w0), col(a0), col(k_k), col(k_a), w_decay_up.astype(jnp.bfloat16).T,
                      w_aaa_up.astype(jnp.bfloat16).T)
    seqs = [a.reshape(R_HEADS, R_HEAD_DIM, B, T).transpose(3, 1, 2, 0).reshape(T, R_HEAD_DIM, B * R_HEADS)
            for a in seqs]
    y, s_fin = rwkv_scan_lanes(seqs, _state_to_lanes(wkv0), r_k, gn_w, gn_b, B)

    cache_t = cache.reshape(cache.shape[0], PAGE_SIZE, 2 * BRANCH_W).transpose(0, 2, 1)
    nb = win_buf.shape[1]
    win_t = win_buf.reshape(B, nb, BRANCH_W).transpose(0, 2, 1)
    kc_a, kc_b = sample_compress(cache_t, page_table, w_cmp_pos, w_cmp_mix)
    o_n = nsa_sample(q, kv, kc_a, kc_b, gl, gate_n, cache_t, page_table, win_t)
    y = out_proj(_y_from_lanes(y, B).reshape(B * T, R_WIDTH), gate_r.reshape(B * T, R_WIDTH),
                 o_n.reshape(B * T, N_WIDTH), x.reshape(B * T, D), w_out, final_g).reshape(B, T, D)
    kv = kv.reshape(B, T, KV_SLOTS, N_KV_HEADS, N_HEAD_DIM)
    keys_w = jnp.concatenate([win_buf.astype(kv.dtype), kv[:, :, 4:6]], axis=1)
    n_keep = min(WINDOW, nb + T)
    return y, kv[:, :, 0:4], keys_w[:, nb + T - n_keep:], _state_from_lanes(s_fin, B), z[:, -1]


def kernel(x_prompt, x_sample, cache_kv, cache_kv_win, state_wkv, state_shift, page_table,
           norm_g, w_in, mu_shift, w0, w_decay_up, a0, w_aaa_up, k_k, k_a, r_k, gn_w, gn_b,
           w_cmp_pos, w_cmp_mix, w_out, final_g):
    y_p, y_s = x_prompt, x_sample
    kvp, kvs, wnp, wns, skp, sks, shp, shs = [], [], [], [], [], [], [], []
    for layer in range(DEPTH):
        lw = (norm_g[layer], w_in[layer], mu_shift[layer], w0[layer], w_decay_up[layer], a0[layer],
              w_aaa_up[layer], k_k[layer], k_a[layer], r_k[layer], gn_w[layer], gn_b[layer],
              w_cmp_pos[layer], w_cmp_mix[layer], w_out[layer])
        fg = final_g if layer == DEPTH - 1 else None
        y_p, p_kv, p_win, p_wkv, p_shift = prompt_layer(y_p, fg, *lw)
        y_s, s_kv, s_win, s_wkv, s_shift = sample_layer(y_s, fg, cache_kv[layer], cache_kv_win[layer],
                                                        state_wkv[layer], state_shift[layer], page_table, *lw)
        kvp.append(p_kv); kvs.append(s_kv); wnp.append(p_win); wns.append(s_win)
        skp.append(p_wkv); sks.append(s_wkv); shp.append(p_shift); shs.append(s_shift)
    return (y_p, y_s, jnp.stack(kvp), jnp.stack(kvs), jnp.stack(wnp), jnp.stack(wns),
            jnp.stack(skp), jnp.stack(sks), jnp.stack(shp), jnp.stack(shs))
```

```python
import functools

import jax, jax.numpy as jnp
from jax import lax
import numpy as np
from jax.experimental import pallas as pl
from jax.experimental.pallas import tpu as pltpu

D_MODEL = 1024
BATCH = 16
SEQ = 2048
DEPTH = 1
DEC_BATCH = 32
DEC_SEQ = 4
PAST_LEN = 16384
PAGE_SIZE = 128

R_HEAD_DIM = 64
R_WIDTH = D_MODEL // 2
R_HEADS = R_WIDTH // R_HEAD_DIM
DECAY_LORA = 64
AAA_LORA = 64
R_SHIFT_WIDTH = 3 * R_WIDTH + DECAY_LORA + AAA_LORA
N_HEAD_DIM = 64
N_WIDTH = D_MODEL - R_WIDTH
N_HEADS = N_WIDTH // N_HEAD_DIM
N_KV_HEADS = 2
N_GROUP = N_HEADS // N_KV_HEADS
N_BRANCH = 3
KV_SLOTS = 2 * N_BRANCH
CMP_BLOCK = 32
CMP_STRIDE = 16
SEL_BLOCK = 64
SEL_TOPK = 16
WINDOW = 512
MIX_WIDTH = R_WIDTH + N_WIDTH
KV_WIDTH = KV_SLOTS * N_KV_HEADS * N_HEAD_DIM
PROJ_SPLITS = (R_SHIFT_WIDTH, R_WIDTH, N_WIDTH, N_WIDTH, KV_WIDTH, N_BRANCH * N_HEADS)
PROJ_OFFS = tuple(int(o) for o in np.cumsum((0,) + PROJ_SPLITS))
PROJ_WIDTH = sum(PROJ_SPLITS)
RMS_EPS = 1e-6
GN_EPS = 64e-5
MASK_NEG = -1e30
SEL_NEG = -(2.0 ** 100)
SEL_BIAS = 1e4
ATTN_SCALE = N_HEAD_DIM ** -0.5

LANES = 128
BRANCH_W = 2 * N_KV_HEADS * N_HEAD_DIM
NSA_TQ = 128
NSA_KC = 512
PROJ_TM = 256
OUT_TM = 256
SAMPLE_PG = 16
RWKV_TC = 32
VMEM_LIMIT = 56 * 1024 * 1024


def _split3(x):
    hi = x.astype(jnp.bfloat16)
    r1 = x - hi.astype(jnp.float32)
    mid = r1.astype(jnp.bfloat16)
    lo = (r1 - mid.astype(jnp.float32)).astype(jnp.bfloat16)
    return hi, mid, lo


def _dot(a, b):
    return jnp.dot(a, b, preferred_element_type=jnp.float32)


def _dot_nt(a, b):
    return lax.dot_general(a, b, (((1,), (1,)), ((), ())), preferred_element_type=jnp.float32)


def _dot_exact_rhs(a, b01):
    hi, mid, lo = _split3(a)
    return _dot(hi, b01) + _dot(mid, b01) + _dot(lo, b01)


def _dot_x3(a, b):
    ah = a.astype(jnp.bfloat16)
    al = (a - ah.astype(jnp.float32)).astype(jnp.bfloat16)
    bh = b.astype(jnp.bfloat16)
    bl = (b - bh.astype(jnp.float32)).astype(jnp.bfloat16)
    return _dot(ah, bh) + _dot(ah, bl) + _dot(al, bh)


def _silu(x):
    return x * jax.nn.sigmoid(x)


def _softplus(x):
    return jnp.maximum(x, 0.0) + jnp.log1p(jnp.exp(-jnp.abs(x)))


def _masked_softmax_rows(s, mask):
    s = jnp.where(mask, s, MASK_NEG)
    e = jnp.exp(s - jnp.max(s, axis=-1, keepdims=True))
    p = e / jnp.sum(e, axis=-1, keepdims=True)
    return jnp.where(mask, p, 0.0)


def _const_spec(shape, n_grid):
    zeros = (0,) * len(shape)
    if n_grid == 1:
        return pl.BlockSpec(shape, lambda i: zeros)
    return pl.BlockSpec(shape, lambda b, i: zeros)


def _rwkv_prep(zs, w0, a0, k_k, k_a, wdec_t, waaa_t):
    bf16 = jnp.bfloat16
    n = zs.shape[1]
    r = zs[0:R_WIDTH]
    k = zs[R_WIDTH:2 * R_WIDTH]
    v = zs[2 * R_WIDTH:3 * R_WIDTH]
    wd = zs[3 * R_WIDTH:3 * R_WIDTH + DECAY_LORA]
    ad = zs[3 * R_WIDTH + DECAY_LORA:R_SHIFT_WIDTH]
    w_log = -_softplus(-(w0 + _dot(wdec_t, jnp.tanh(wd).astype(bf16)))) - 0.5
    decay = jnp.exp(-jnp.exp(w_log))
    a = jax.nn.sigmoid(a0 + _dot(waaa_t, ad.astype(bf16)))
    kk = (k * k_k).reshape(R_HEADS, R_HEAD_DIM, n)
    kk = (kk * lax.rsqrt(jnp.maximum(jnp.sum(kk * kk, axis=1, keepdims=True), 1e-24))).reshape(R_WIDTH, n)
    k = k * (1.0 + (a - 1.0) * k_a)
    return r, decay, k, v, kk, kk * a


def _prompt_proj_body(x_ref, g_ref, wz_ref, wkv_ref, wn_ref, mu_ref, w0_ref, a0_ref, kk_ref, ka_ref, wdec_ref,
                      waaa_ref, r_o, w_o, k_o, v_o, kkn_o, b_o, kvcs_o, kvw_o, gr_o, q_o, gn_o, gl_o, zl_o, carry_ref):
    bf16 = jnp.bfloat16
    tm = x_ref.shape[1]
    x = x_ref[0]
    xn = (x * lax.rsqrt(jnp.mean(x * x, axis=-1, keepdims=True) + RMS_EPS) * g_ref[...]).astype(bf16)

    z = _dot_nt(wz_ref[...], xn)
    kv = _dot_nt(wkv_ref[...], xn)
    nrm = _dot(xn, wn_ref[...])
    kvcs_o[0] = kv[0:2 * BRANCH_W]
    kvw_o[0] = kv[2 * BRANCH_W:3 * BRANCH_W]
    gr_o[0] = nrm[:, 0:R_WIDTH]
    q_o[0] = nrm[:, R_WIDTH:R_WIDTH + N_WIDTH]
    gn_o[0] = nrm[:, R_WIDTH + N_WIDTH:R_WIDTH + 2 * N_WIDTH]
    gl_o[0] = nrm[:, R_WIDTH + 2 * N_WIDTH:]

    @pl.when(pl.program_id(1) == 0)
    def _():
        carry_ref[...] = jnp.zeros(carry_ref.shape, jnp.float32)

    first = lax.broadcasted_iota(jnp.int32, (1, tm), 1) == 0
    zp = jnp.where(first, carry_ref[:, LANES - 1:LANES], pltpu.roll(z, 1, axis=1))
    carry_ref[...] = z[:, tm - LANES:tm]
    zl_o[0] = z[:, tm - LANES:tm]
    zs = z + (zp - z) * mu_ref[...]
    outs = _rwkv_prep(zs, w0_ref[...], a0_ref[...], kk_ref[...], ka_ref[...], wdec_ref[...], waaa_ref[...])
    for o_ref, val in zip((r_o, w_o, k_o, v_o, kkn_o, b_o), outs):
        o_ref[0] = val


def prompt_in_proj(x, norm_g, w_in, mu_shift, w0, w_decay_up, a0, w_aaa_up, k_k, k_a):
    B, T, D = x.shape
    tm = PROJ_TM
    assert T % tm == 0 and tm % LANES == 0
    f32, bf16 = jnp.float32, jnp.bfloat16
    o = PROJ_OFFS
    wb = w_in.astype(bf16)
    wz_t = wb[:, o[0]:o[1]].T
    wkv_t = wb[:, o[4]:o[5]].T
    gl_pad = LANES - PROJ_SPLITS[5]
    wn = jnp.concatenate([wb[:, o[1]:o[4]], wb[:, o[5]:o[6]], jnp.zeros((D, gl_pad), bf16)], axis=1)
    col = lambda p: p.astype(f32).reshape(-1, 1)
    consts = [norm_g.astype(f32).reshape(1, D), wz_t, wkv_t, wn, col(mu_shift), col(w0), col(a0), col(k_k), col(k_a),
              w_decay_up.astype(bf16).T, w_aaa_up.astype(bf16).T]
    cm = lambda c: pl.BlockSpec((1, c, tm), lambda b, i: (b, 0, i))
    tok = lambda c: pl.BlockSpec((1, tm, c), lambda b, i: (b, i, 0))
    cms = lambda c: jax.ShapeDtypeStruct((B, c, T), f32)
    toks = lambda c: jax.ShapeDtypeStruct((B, T, c), f32)
    return pl.pallas_call(
        _prompt_proj_body,
        grid=(B, T // tm),
        in_specs=[tok(D)] + [_const_spec(c.shape, 2) for c in consts],
        out_specs=[cm(R_WIDTH)] * 6 + [cm(2 * BRANCH_W), cm(BRANCH_W), tok(R_WIDTH), tok(N_WIDTH), tok(N_WIDTH),
                                       tok(LANES), pl.BlockSpec((1, R_SHIFT_WIDTH, LANES), lambda b, i: (b, 0, 0))],
        out_shape=[cms(R_WIDTH)] * 6 + [cms(2 * BRANCH_W), cms(BRANCH_W), toks(R_WIDTH), toks(N_WIDTH), toks(N_WIDTH),
                                        toks(LANES), jax.ShapeDtypeStruct((B, R_SHIFT_WIDTH, LANES), f32)],
        scratch_shapes=[pltpu.VMEM((R_SHIFT_WIDTH, LANES), f32)],
        compiler_params=pltpu.CompilerParams(dimension_semantics=("parallel", "arbitrary"),
                                             vmem_limit_bytes=VMEM_LIMIT),
        name="prompt_in_proj",
    )(x, *consts)


def _rwkv_scan_body(r_ref, w_ref, k_ref, v_ref, kk_ref, b_ref, s0_ref, rk_ref, gw_ref, gb_ref, y_ref, s_ref):
    n = s_ref.shape[0]

    @pl.when(pl.program_id(1) == 0)
    def _():
        s_ref[...] = s0_ref[...]

    def step(t, carry):
        r, w, k, kk, b = r_ref[t], w_ref[t], k_ref[t], kk_ref[t], b_ref[t]

        for i in range(n):
            s_i = s_ref[i]
            sa = jnp.sum(s_i * kk, axis=0, keepdims=True)
            s_i = s_i * w - sa * b + v_ref[t, i:i + 1, :] * k
            s_ref[i] = s_i
            y_ref[t, i:i + 1, :] = jnp.sum(s_i * r, axis=0, keepdims=True)
        y = y_ref[t]
        yc = y - jnp.mean(y, axis=0, keepdims=True)
        y = yc * lax.rsqrt(jnp.mean(yc * yc, axis=0, keepdims=True) + GN_EPS)
        y = y * gw_ref[...] + gb_ref[...]
        y_ref[t] = y + jnp.sum(r * k * rk_ref[...], axis=0, keepdims=True) * v_ref[t]
        return carry

    lax.fori_loop(0, r_ref.shape[0], step, 0)


def rwkv_scan_lanes(seqs, s0, r_k, gn_w, gn_b, n_batch):
    T, N, BH = seqs[0].shape
    assert BH % LANES == 0
    tc = min(RWKV_TC, T)
    assert T % tc == 0
    per_lane = lambda p: jnp.tile(p.astype(jnp.float32).reshape(BH // n_batch, N).T, (1, n_batch))
    seq = pl.BlockSpec((tc, N, LANES), lambda l, t: (t, 0, l))
    state = pl.BlockSpec((N, N, LANES), lambda l, t: (0, 0, l))
    param = pl.BlockSpec((N, LANES), lambda l, t: (0, l))
    return pl.pallas_call(
        _rwkv_scan_body,
        grid=(BH // LANES, T // tc),
        in_specs=[seq] * 6 + [state] + [param] * 3,
        out_specs=[seq, state],
        out_shape=[jax.ShapeDtypeStruct((T, N, BH), jnp.float32), jax.ShapeDtypeStruct((N, N, BH), jnp.float32)],
        compiler_params=pltpu.CompilerParams(dimension_semantics=("parallel", "arbitrary"),
                                             vmem_limit_bytes=VMEM_LIMIT),
        name="rwkv_scan",
    )(*seqs, s0, per_lane(r_k), per_lane(gn_w), per_lane(gn_b))


def _lanes_from_cm(a, T):
    return a.reshape(-1, R_HEAD_DIM, T).transpose(2, 1, 0)


def _y_from_lanes(y, B):
    T, N, _ = y.shape
    return y.reshape(T, N, B, R_HEADS).transpose(2, 0, 3, 1).reshape(B, T, R_WIDTH)


def _state_to_lanes(s):
    B, H, N, _ = s.shape
    return s.astype(jnp.float32).transpose(2, 3, 0, 1).reshape(N, N, B * H)


def _state_from_lanes(s, B):
    N = s.shape[0]
    return s.reshape(N, N, B, R_HEADS).transpose(2, 3, 0, 1)


def _cmp_weights(w_pos, w_mix, n_rows):
    wp = jnp.broadcast_to(w_pos.astype(jnp.float32).reshape(2, 1, CMP_BLOCK, N_HEAD_DIM).transpose(2, 0, 1, 3),
                          (CMP_BLOCK, 2, N_KV_HEADS, N_HEAD_DIM)).reshape(CMP_BLOCK, BRANCH_W)
    wa = jnp.tile(wp[:CMP_STRIDE], (n_rows // CMP_STRIDE, 1)).T
    wb = jnp.tile(wp[CMP_STRIDE:], (n_rows // CMP_STRIDE, 1)).T
    wmix = jnp.zeros((BRANCH_W, BRANCH_W), jnp.float32)
    for e in range(2):
        for kh in range(N_KV_HEADS):
            o = (e * N_KV_HEADS + kh) * N_HEAD_DIM
            wmix = wmix.at[o:o + N_HEAD_DIM, o:o + N_HEAD_DIM].set(w_mix[e].astype(jnp.float32))
    return wa, wb, wmix.T


def _compress_prompt_body(kv_ref, wa_ref, wb_ref, pool_ref, wmix_ref, o_ref):
    x = kv_ref[0]
    xw = jnp.concatenate([x * wa_ref[...], x * wb_ref[...]], axis=1)
    o_ref[0] = _dot_x3(wmix_ref[...], _dot_exact_rhs(xw, pool_ref[...]))


def compress_prompt(kv_cs, w_pos, w_mix):
    B, _, T = kv_cs.shape
    assert CMP_BLOCK == 2 * CMP_STRIDE and T % CMP_STRIDE == 0
    ncp = T // CMP_STRIDE
    wa, wb, wmix_t = _cmp_weights(w_pos, w_mix, T)
    chunk = np.arange(T) // CMP_STRIDE
    pool = np.concatenate([chunk[:, None] == np.arange(ncp)[None, :],
                           chunk[:, None] == np.arange(ncp)[None, :] + 1], axis=0)
    pool = jnp.asarray(pool, jnp.bfloat16)
    return pl.pallas_call(
        _compress_prompt_body,
        grid=(B,),
        in_specs=[pl.BlockSpec((1, BRANCH_W, T), lambda b: (b, 0, 0))] + [_const_spec(c.shape, 1)
                                                                          for c in (wa, wb, pool, wmix_t)],
        out_specs=pl.BlockSpec((1, BRANCH_W, ncp), lambda b: (b, 0, 0)),
        out_shape=jax.ShapeDtypeStruct((B, BRANCH_W, ncp), jnp.float32),
        compiler_params=pltpu.CompilerParams(dimension_semantics=("parallel",), vmem_limit_bytes=VMEM_LIMIT),
        name="compress_prompt",
    )(kv_cs, wa, wb, pool, wmix_t)


def _nsa_prompt_body(q_ref, ksel_ref, kwin_ref, kc_ref, gl_ref, gn_ref, ovt_ref, ex_ref, o_ref):
    f32, bf16 = jnp.float32, jnp.bfloat16
    TQ, KC, G = NSA_TQ, NSA_KC, N_GROUP
    ncp = kc_ref.shape[2]
    n_sel = ovt_ref.shape[0]
    topk = min(SEL_TOPK, n_sel)
    qs = pl.program_id(1) * TQ
    qpos = qs + lax.broadcasted_iota(jnp.int32, (TQ, 1), 0)
    lo_half = lax.broadcasted_iota(jnp.int32, (TQ, LANES), 1) < N_HEAD_DIM

    def q_padded(h, half):
        tile = q_ref[0, :, (h // 2) * LANES:(h // 2 + 1) * LANES] * ATTN_SCALE
        if h % 2 != half:
            tile = pltpu.roll(tile, N_HEAD_DIM, axis=1)
        return jnp.where(lo_half if half == 0 else jnp.logical_not(lo_half), tile, 0.0)

    wstart = pl.multiple_of(jnp.maximum(qs - WINDOW, 0), TQ)
    kw = kwin_ref[0, 0:LANES, pl.ds(wstart, WINDOW + TQ)].astype(bf16)
    vw = kwin_ref[0, LANES:2 * LANES, pl.ds(wstart, WINDOW + TQ)].astype(bf16)
    kposw = wstart + lax.broadcasted_iota(jnp.int32, (1, WINDOW + TQ), 1)
    mask_w = (kposw <= qpos) & (kposw > qpos - WINDOW)

    kck = kc_ref[0, 0:LANES, :]
    kcv = kc_ref[0, LANES:2 * LANES, :].astype(bf16)
    kend = lax.broadcasted_iota(jnp.int32, (1, ncp), 1) * CMP_STRIDE + (CMP_BLOCK - 1)
    mask_c = kend <= qpos

    sid_t = lax.broadcasted_iota(jnp.int32, (n_sel, 1), 0)
    cur_t = (qs + lax.broadcasted_iota(jnp.int32, (1, TQ), 1)) // SEL_BLOCK
    valid_t = sid_t <= cur_t
    forced_t = (sid_t == 0) | (sid_t == cur_t) | (sid_t == cur_t - 1)
    n_chunks = qs // KC + 1

    H = N_HEADS
    qg = jnp.concatenate([q_padded(h, h // G) for h in range(H)], axis=0)
    qb = qg.astype(bf16)

    p = _masked_softmax_rows(_dot_x3(qg, kck).reshape(H, TQ, ncp), mask_c[None])
    o_c = _dot_nt(p.reshape(H * TQ, ncp).astype(bf16), kcv).reshape(H, TQ, LANES)

    ovt = ovt_ref[...]
    selb = []
    for kvh in range(N_KV_HEADS):
        hi, mid, lo = _split3(jnp.sum(p[kvh * G:(kvh + 1) * G], axis=0))
        score = _dot_nt(ovt, hi) + _dot_nt(ovt, mid) + _dot_nt(ovt, lo)
        score = jnp.where(valid_t, score + jnp.where(forced_t, SEL_BIAS, 0.0), -SEL_BIAS)
        rank = jnp.zeros((n_sel, TQ), jnp.int32)
        for s2 in range(n_sel):
            row = score[s2:s2 + 1, :]
            ahead = (row > score) | ((row == score) & (s2 < sid_t))
            rank = rank + ahead.astype(jnp.int32)
        selb_t = jnp.concatenate([jnp.where((rank < topk) & valid_t, 0.0, SEL_NEG),
                                  jnp.zeros((LANES - n_sel, TQ), f32)], axis=0)
        selb += [selb_t.T.astype(bf16)] * G
    qa = jnp.concatenate([qb, jnp.concatenate(selb, axis=0)], axis=1)

    def chunk(c, carry, causal):
        m, l, acc = carry
        k0 = pl.multiple_of(c * KC, KC)
        ka = jnp.concatenate([ksel_ref[0, 0:LANES, pl.ds(k0, KC)].astype(bf16), ex_ref[:, pl.ds(k0, KC)]],
                             axis=0)
        va = jnp.concatenate([ksel_ref[0, LANES:2 * LANES, pl.ds(k0, KC)].astype(bf16), ones_kc], axis=0)
        sc = _dot(qa, ka).reshape(H, TQ, KC)
        if causal:
            kpos = k0 + lax.broadcasted_iota(jnp.int32, (1, KC), 1)
            sc = jnp.where((kpos <= qpos)[None], sc, SEL_NEG)
        m_new = jnp.maximum(m, jnp.max(sc, axis=-1, keepdims=True))
        alpha = jnp.exp(m - m_new)
        e = jnp.exp(sc - m_new).astype(bf16)
        pv = _dot_nt(e.reshape(H * TQ, KC), va).reshape(H, TQ, 2 * LANES)
        return m_new, alpha * l + pv[:, :, LANES:], alpha * acc + pv[:, :, :LANES]

    ones_kc = jnp.ones((LANES, KC), bf16)
    carry = (jnp.full((H, TQ, 1), SEL_NEG, f32), jnp.zeros((H, TQ, LANES), f32), jnp.zeros((H, TQ, LANES), f32))
    carry = lax.fori_loop(0, n_chunks - 1, functools.partial(chunk, causal=False), carry)
    _, l, acc = chunk(n_chunks - 1, carry, causal=True)
    o_s = acc / l

    sw = jnp.where(mask_w[None], _dot(qb, kw).reshape(H, TQ, WINDOW + TQ), MASK_NEG)
    ew = jnp.exp(sw - jnp.max(sw, axis=-1, keepdims=True)).astype(bf16)
    pw = _dot_nt(ew.reshape(H * TQ, WINDOW + TQ),
                 jnp.concatenate([vw, jnp.ones((LANES, WINDOW + TQ), bf16)], axis=0)).reshape(H, TQ, 2 * LANES)
    o_w = pw[:, :, :LANES] / pw[:, :, LANES:]

    gates = jax.nn.sigmoid(gl_ref[0])
    heads = []
    for h in range(H):
        o = (gates[:, h:h + 1] * o_c[h] + gates[:, H + h:H + h + 1] * o_s[h]
             + gates[:, 2 * H + h:2 * H + h + 1] * o_w[h])
        heads.append(pltpu.roll(o, N_HEAD_DIM, axis=1) if h % 2 != h // G else o)

    for m in range(N_HEADS // 2):
        tile = jnp.where(lo_half, heads[2 * m], heads[2 * m + 1])
        o_ref[0, :, m * LANES:(m + 1) * LANES] = tile * _silu(gn_ref[0, :, m * LANES:(m + 1) * LANES])


def nsa_prompt(q, kv_cs, kv_w, kc, gl, gate_n):
    B, T, _ = q.shape
    TQ, KC = NSA_TQ, NSA_KC
    assert N_KV_HEADS == 2 and N_HEAD_DIM * 2 == LANES and T % KC == 0 and KC % TQ == 0 and T >= WINDOW + TQ
    assert SEL_BLOCK % CMP_STRIDE == 0 and WINDOW % TQ == 0 and TQ % LANES == 0 and T // SEL_BLOCK <= LANES
    ncp = T // CMP_STRIDE
    n_sel = T // SEL_BLOCK
    cs = np.arange(ncp) * CMP_STRIDE
    ss = np.arange(n_sel) * SEL_BLOCK
    overlap = (cs[:, None] < ss[None, :] + SEL_BLOCK) & (cs[:, None] + CMP_BLOCK > ss[None, :])
    overlap[ncp - 1] = False
    overlap_t = jnp.asarray(overlap.T, jnp.bfloat16)
    expand = jnp.asarray(np.arange(T)[None, :] // SEL_BLOCK == np.arange(LANES)[:, None], jnp.bfloat16)
    return pl.pallas_call(
        _nsa_prompt_body,
        grid=(B, T // TQ),
        in_specs=[pl.BlockSpec((1, TQ, N_WIDTH), lambda b, i: (b, i, 0)),
                  pl.BlockSpec((1, BRANCH_W, T), lambda b, i: (b, 1, 0)),
                  pl.BlockSpec((1, BRANCH_W, T), lambda b, i: (b, 0, 0)),
                  pl.BlockSpec((1, BRANCH_W, ncp), lambda b, i: (b, 0, 0)),
                  pl.BlockSpec((1, TQ, LANES), lambda b, i: (b, i, 0)),
                  pl.BlockSpec((1, TQ, N_WIDTH), lambda b, i: (b, i, 0)),
                  _const_spec((n_sel, ncp), 2), _const_spec((LANES, T), 2)],
        out_specs=pl.BlockSpec((1, TQ, N_WIDTH), lambda b, i: (b, i, 0)),
        out_shape=jax.ShapeDtypeStruct((B, T, N_WIDTH), jnp.float32),
        compiler_params=pltpu.CompilerParams(dimension_semantics=("parallel", "arbitrary"),
                                             vmem_limit_bytes=VMEM_LIMIT),
        name="nsa_prompt",
    )(q, kv_cs, kv_w, kc, gl, gate_n, overlap_t, expand)


def _page_spec(k, branch):
    return pl.BlockSpec((1, BRANCH_W, PAGE_SIZE), lambda b, g, pt: (pt[b, g * SAMPLE_PG + k], branch, 0))


def _sample_compress_body(pt_ref, *refs):
    pages = refs[:SAMPLE_PG]
    wa_ref, wb_ref, pool_ref, wmix_ref, oa_ref, ob_ref = refs[SAMPLE_PG:]
    x = jnp.concatenate([p[0] for p in pages], axis=1)

    def pooled(w_ref):
        xw = x * w_ref[...]
        hi = xw.astype(jnp.bfloat16)
        lo = (xw - hi.astype(jnp.float32)).astype(jnp.bfloat16)
        return _dot(hi, pool_ref[...]) + _dot(lo, pool_ref[...])

    oa_ref[0] = _dot_x3(wmix_ref[...], pooled(wa_ref))
    ob_ref[0] = _dot_x3(wmix_ref[...], pooled(wb_ref))


def sample_compress(cache_t, page_table, w_pos, w_mix):
    B, n_pages = page_table.shape
    PG = SAMPLE_PG
    assert n_pages % PG == 0 and CMP_BLOCK == 2 * CMP_STRIDE and PAGE_SIZE % CMP_STRIDE == 0
    cols = PG * PAGE_SIZE // CMP_STRIDE
    assert cols % LANES == 0
    ncp = n_pages * PAGE_SIZE // CMP_STRIDE
    wa, wb, wmix_t = _cmp_weights(w_pos, w_mix, PG * PAGE_SIZE)
    pool = jnp.asarray(np.arange(PG * PAGE_SIZE)[:, None] // CMP_STRIDE == np.arange(cols)[None, :], jnp.bfloat16)
    const = lambda shape: pl.BlockSpec(shape, lambda b, g, pt: (0,) * len(shape))
    out = pl.BlockSpec((1, BRANCH_W, cols), lambda b, g, pt: (b, 0, g))
    return pl.pallas_call(
        _sample_compress_body,
        grid_spec=pltpu.PrefetchScalarGridSpec(
            num_scalar_prefetch=1, grid=(B, n_pages // PG),
            in_specs=[_page_spec(k, 0) for k in range(PG)] + [const(wa.shape), const(wb.shape), const(pool.shape),
                                                              const(wmix_t.shape)],
            out_specs=[out, out]),
        out_shape=[jax.ShapeDtypeStruct((B, BRANCH_W, ncp), jnp.float32)] * 2,
        compiler_params=pltpu.CompilerParams(dimension_semantics=("parallel", "arbitrary"),
                                             vmem_limit_bytes=VMEM_LIMIT),
        name="sample_compress",
    )(page_table, *([cache_t] * PG), wa, wb, pool, wmix_t)


def _online_chunk(m, l, acc, s, keep, v, v_channels_major=False):
    s = jnp.where(keep, s, MASK_NEG)
    m_new = jnp.maximum(m, jnp.max(s, axis=-1, keepdims=True))
    alpha = jnp.exp(m - m_new)
    e = jnp.where(keep, jnp.exp(s - m_new), 0.0)
    eb = e.astype(jnp.bfloat16)
    pv = _dot_nt(eb, v) if v_channels_major else _dot(eb, v)
    return m_new, alpha * l + jnp.sum(e, axis=-1, keepdims=True), alpha * acc + pv


def _sample_attend_body(pt_ref, *refs, past):
    f32, bf16 = jnp.float32, jnp.bfloat16
    PG, G = SAMPLE_PG, N_GROUP
    pages = refs[:PG]
    (ka_ref, kb_ref, q_ref, kvn_ref, gl_ref, gn_ref, win_ref, ex_ref, ov_ref, o_ref,
     qf_ref, sel_ref, m_ref, l_ref, acc_ref, oc_ref) = refs[PG:]
    T = q_ref.shape[1]
    R = N_HEADS * T
    ncp = ka_ref.shape[2]
    nsp = ov_ref.shape[1]
    KC = PG * PAGE_SIZE
    n_past = past // SEL_BLOCK
    g = pl.program_id(1)
    lo_half = lax.broadcasted_iota(jnp.int32, (T, LANES), 1) < N_HEAD_DIM
    row = lax.broadcasted_iota(jnp.int32, (R, 1), 0)
    qpos = past + row % T

    @pl.when(g == 0)
    def _():
        pieces = []
        for h in range(N_HEADS):
            kvh = h // G
            tile = q_ref[0, :, (h // 2) * LANES:(h // 2 + 1) * LANES] * ATTN_SCALE
            if h % 2 != kvh:
                tile = pltpu.roll(tile, N_HEAD_DIM, axis=1)
            pieces.append(jnp.where(lo_half if kvh == 0 else jnp.logical_not(lo_half), tile, 0.0))
        q32 = jnp.concatenate(pieces, axis=0)
        qf_ref[...] = q32

        kc = ka_ref[0] + pltpu.roll(kb_ref[0], ncp - 1, axis=1)
        kend = lax.broadcasted_iota(jnp.int32, (1, ncp), 1) * CMP_STRIDE + (CMP_BLOCK - 1)
        p = _masked_softmax_rows(_dot_x3(q32, kc[0:LANES]), kend <= qpos)
        oc_ref[...] = _dot_nt(p.astype(bf16), kc[LANES:2 * LANES].astype(bf16))
        imp = jnp.concatenate(
            [sum(p[(kvh * G + gg) * T:(kvh * G + gg + 1) * T] for gg in range(G)) for kvh in range(N_KV_HEADS)],
            axis=0)

        sid = lax.broadcasted_iota(jnp.int32, (1, nsp), 1)
        sid_f = sid.astype(f32)
        cur = (past + lax.broadcasted_iota(jnp.int32, (N_KV_HEADS * T, 1), 0) % T) // SEL_BLOCK
        valid = sid <= cur
        forced = (sid == 0) | (sid == cur) | (sid == cur - 1)
        score = _dot_exact_rhs(imp, ov_ref[...])
        score = jnp.where(valid, score + jnp.where(forced, SEL_BIAS, 0.0), -SEL_BIAS)

        def extract(_, carry):
            sc, chosen = carry
            best = jnp.max(sc, axis=-1, keepdims=True)
            first = jnp.min(jnp.where(sc == best, sid_f, float(nsp)), axis=-1, keepdims=True)
            hit = sid_f == first
            return jnp.where(hit, -3e38, sc), jnp.where(hit, 1.0, chosen)

        _, chosen = lax.fori_loop(0, min(SEL_TOPK, n_past + 1), extract, (score, jnp.zeros_like(score)))
        sel = jnp.where(valid, chosen, 0.0)
        sel_ref[...] = jnp.concatenate(
            [sel[(h // G) * T:(h // G + 1) * T] for h in range(N_HEADS)], axis=0).astype(bf16)
        m_ref[...] = jnp.full(m_ref.shape, MASK_NEG, f32)
        l_ref[...] = jnp.zeros(l_ref.shape, f32)
        acc_ref[...] = jnp.zeros(acc_ref.shape, f32)

    qb = qf_ref[...].astype(bf16)
    ks = jnp.concatenate([p[0, 0:LANES, :] for p in pages], axis=1).astype(bf16)
    vs = jnp.concatenate([p[0, LANES:2 * LANES, :] for p in pages], axis=1).astype(bf16)
    kpos = g * KC + lax.broadcasted_iota(jnp.int32, (1, KC), 1)
    keep = (_dot(sel_ref[...], ex_ref[0]) > 0.5) & (kpos <= qpos)
    m, l, acc = _online_chunk(m_ref[...], l_ref[...], acc_ref[...], _dot(qb, ks), keep, vs, True)
    m_ref[...] = m
    l_ref[...] = l
    acc_ref[...] = acc

    @pl.when(g == pl.num_programs(1) - 1)
    def _():
        def new_rows(c0):
            return jnp.concatenate([kvn_ref[0, :, c0:c0 + LANES], jnp.zeros((LANES - T, LANES), f32)],
                                   axis=0).astype(bf16)

        npos = past + lax.broadcasted_iota(jnp.int32, (1, LANES), 1)
        is_new = npos < past + T
        keep_t = is_new & (npos <= qpos) & (sel_ref[:, n_past:n_past + 1].astype(f32) > 0.5)
        _, l_s, acc_s = _online_chunk(m, l, acc, _dot_nt(qb, new_rows(BRANCH_W)), keep_t, new_rows(BRANCH_W + LANES))
        o_s = acc_s / l_s

        nb = win_ref.shape[2]
        bpos = past - nb + lax.broadcasted_iota(jnp.int32, (1, nb), 1)
        keep_b = (bpos <= qpos) & (bpos > qpos - WINDOW) & (bpos >= 0)
        mw, lw, aw = _online_chunk(jnp.full((R, 1), MASK_NEG, f32), jnp.zeros((R, 1), f32), jnp.zeros((R, LANES), f32),
                                   _dot(qb, win_ref[0, 0:LANES, :].astype(bf16)), keep_b,
                                   win_ref[0, LANES:2 * LANES, :].astype(bf16), True)
        keep_n = is_new & (npos <= qpos) & (npos > qpos - WINDOW)
        _, lw, aw = _online_chunk(mw, lw, aw, _dot_nt(qb, new_rows(2 * BRANCH_W)), keep_n,
                                  new_rows(2 * BRANCH_W + LANES))
        o_w = aw / lw

        gates = jax.nn.sigmoid(gl_ref[0])
        gcol = lambda br: jnp.concatenate([gates[:, br * N_HEADS + h:br * N_HEADS + h + 1] for h in range(N_HEADS)],
                                          axis=0)
        o = gcol(0) * oc_ref[...] + gcol(1) * o_s + gcol(2) * o_w
        for mt in range(N_HEADS // 2):
            halves = []
            for h in (2 * mt, 2 * mt + 1):
                piece = o[h * T:(h + 1) * T]
                if h % 2 != h // G:
                    piece = pltpu.roll(piece, N_HEAD_DIM, axis=1)
                halves.append(piece)
            o_ref[0, :, mt * LANES:(mt + 1) * LANES] = (jnp.where(lo_half, halves[0], halves[1])
                                                        * _silu(gn_ref[0, :, mt * LANES:(mt + 1) * LANES]))


def nsa_sample(q, kv_new, kc_a, kc_b, gl, gate_n, cache_t, page_table, win_t):
    B, T, _ = q.shape
    n_pages = page_table.shape[1]
    past = n_pages * PAGE_SIZE
    PG = SAMPLE_PG
    KC = PG * PAGE_SIZE
    ng = n_pages // PG
    assert n_pages % PG == 0 and past % SEL_BLOCK == 0 and T <= SEL_BLOCK and T <= LANES
    assert (past + T) // CMP_STRIDE == past // CMP_STRIDE and KC % SEL_BLOCK == 0
    ncp = past // CMP_STRIDE
    n_sel = past // SEL_BLOCK + 1
    nsp = -(-n_sel // LANES) * LANES
    cs = np.arange(ncp) * CMP_STRIDE
    ss = np.arange(nsp) * SEL_BLOCK
    overlap = (cs[:, None] < ss[None, :] + SEL_BLOCK) & (cs[:, None] + CMP_BLOCK > ss[None, :])
    overlap[ncp - 1] = False
    overlap[:, n_sel:] = False
    overlap = jnp.asarray(overlap, jnp.bfloat16)
    expand = jnp.asarray((np.arange(past) // SEL_BLOCK).reshape(ng, 1, KC) == np.arange(nsp)[None, :, None],
                         jnp.bfloat16)
    R = N_HEADS * T
    per_b = lambda shape: pl.BlockSpec((1,) + shape, lambda b, g, pt: (b, 0, 0))
    return pl.pallas_call(
        functools.partial(_sample_attend_body, past=past),
        grid_spec=pltpu.PrefetchScalarGridSpec(
            num_scalar_prefetch=1, grid=(B, ng),
            in_specs=[_page_spec(k, 1) for k in range(PG)] + [
                per_b((BRANCH_W, ncp)), per_b((BRANCH_W, ncp)), per_b((T, N_WIDTH)), per_b((T, KV_WIDTH)),
                per_b((T, N_BRANCH * N_HEADS)), per_b((T, N_WIDTH)), per_b((BRANCH_W, win_t.shape[2])),
                pl.BlockSpec((1, nsp, KC), lambda b, g, pt: (g, 0, 0)),
                pl.BlockSpec((ncp, nsp), lambda b, g, pt: (0, 0))],
            out_specs=per_b((T, N_WIDTH)),
            scratch_shapes=[pltpu.VMEM((R, LANES), jnp.float32), pltpu.VMEM((R, nsp), jnp.bfloat16),
                            pltpu.VMEM((R, 1), jnp.float32), pltpu.VMEM((R, 1), jnp.float32),
                            pltpu.VMEM((R, LANES), jnp.float32), pltpu.VMEM((R, LANES), jnp.float32)]),
        out_shape=jax.ShapeDtypeStruct((B, T, N_WIDTH), jnp.float32),
        compiler_params=pltpu.CompilerParams(dimension_semantics=("parallel", "arbitrary"),
                                             vmem_limit_bytes=VMEM_LIMIT),
        name="nsa_sample",
    )(page_table, *([cache_t] * PG), kc_a, kc_b, q, kv_new, gl, gate_n, win_t, expand, overlap)


def _out_proj_body(yr_ref, gr_ref, on_ref, x_ref, w_ref, g_ref, o_ref, *, final_norm):
    bf16 = jnp.bfloat16
    a = (yr_ref[...] * _silu(gr_ref[...])).astype(bf16)
    y = x_ref[...] + _dot(a, w_ref[0:R_WIDTH, :]) + _dot(on_ref[...].astype(bf16), w_ref[R_WIDTH:MIX_WIDTH, :])
    if final_norm:
        y = y * lax.rsqrt(jnp.mean(y * y, axis=-1, keepdims=True) + RMS_EPS) * g_ref[...]
    o_ref[...] = y


def out_proj(y_r, gate_r, o_n, x, w_out, final_g):
    M, D = x.shape
    tm = min(OUT_TM, M)
    assert M % tm == 0
    g = (jnp.ones((D,), jnp.float32) if final_g is None else final_g.astype(jnp.float32)).reshape(1, D)
    rows = lambda c: pl.BlockSpec((tm, c), lambda i: (i, 0))
    return pl.pallas_call(
        functools.partial(_out_proj_body, final_norm=final_g is not None),
        grid=(M // tm,),
        in_specs=[rows(R_WIDTH), rows(R_WIDTH), rows(N_WIDTH), rows(D), _const_spec((MIX_WIDTH, D), 1),
                  _const_spec((1, D), 1)],
        out_specs=rows(D),
        out_shape=jax.ShapeDtypeStruct((M, D), jnp.float32),
        compiler_params=pltpu.CompilerParams(dimension_semantics=("parallel",), vmem_limit_bytes=VMEM_LIMIT),
        name="out_proj",
    )(y_r, gate_r, o_n, x, w_out.astype(jnp.bfloat16), g)


def rms_norm(x, g):
    xf = x.astype(jnp.float32)
    y = xf * lax.rsqrt(jnp.mean(xf * xf, axis=-1, keepdims=True) + RMS_EPS)
    return (y * g.astype(jnp.float32)).astype(x.dtype)


def prompt_layer(x, final_g, norm_g, w_in, mu_shift, w0, w_decay_up, a0, w_aaa_up, k_k, k_a, r_k, gn_w, gn_b,
                 w_cmp_pos, w_cmp_mix, w_out):
    B, T, D = x.shape
    (r, w, k, v, kk, b, kv_cs, kv_w, gate_r, q, gate_n, gl, z_last) = prompt_in_proj(
        x, norm_g, w_in, mu_shift, w0, w_decay_up, a0, w_aaa_up, k_k, k_a)
    s0 = jnp.zeros((R_HEAD_DIM, R_HEAD_DIM, B * R_HEADS), jnp.float32)
    y, s_fin = rwkv_scan_lanes([_lanes_from_cm(a, T) for a in (r, w, k, v, kk, b)], s0, r_k, gn_w, gn_b, B)
    o_n = nsa_prompt(q, kv_cs, kv_w, compress_prompt(kv_cs, w_cmp_pos, w_cmp_mix), gl, gate_n)
    y = out_proj(_y_from_lanes(y, B).reshape(B * T, R_WIDTH), gate_r.reshape(B * T, R_WIDTH),
                 o_n.reshape(B * T, N_WIDTH), x.reshape(B * T, D), w_out, final_g).reshape(B, T, D)
    heads = (N_KV_HEADS, N_HEAD_DIM)
    kv_rows = kv_cs.reshape((B, 4) + heads + (T,)).transpose(0, 4, 1, 2, 3)
    n_keep = min(WINDOW, T)
    win_rows = kv_w[:, :, T - n_keep:].reshape((B, 2) + heads + (n_keep,)).transpose(0, 4, 1, 2, 3)
    return y, kv_rows, win_rows, _state_from_lanes(s_fin, B), z_last[:, :, LANES - 1]


def sample_layer(x, final_g, cache, win_buf, wkv0, shift0, page_table, norm_g, w_in, mu_shift, w0, w_decay_up, a0,
                 w_aaa_up, k_k, k_a, r_k, gn_w, gn_b, w_cmp_pos, w_cmp_mix, w_out):
    B, T, D = x.shape
    f32 = jnp.float32
    p = rms_norm(x, norm_g) @ w_in
    z, gate_r, q, gate_n, kv, gl = (p[..., PROJ_OFFS[i]:PROJ_OFFS[i + 1]] for i in range(len(PROJ_SPLITS)))
    zp = jnp.concatenate([shift0[:, None].astype(z.dtype), z[:, :-1]], axis=1)
    zs = (z + (zp - z) * mu_shift).reshape(B * T, R_SHIFT_WIDTH).T
    col = lambda c: c.astype(f32).reshape(-1, 1)
    seqs = _rwkv_prep(zs, col(w0), col(a0), col(k_k), col(k_a), w_decay_up.astype(jnp.bfloat16).T,
                      w_aaa_up.astype(jnp.bfloat16).T)
    seqs = [a.reshape(R_HEADS, R_HEAD_DIM, B, T).transpose(3, 1, 2, 0).reshape(T, R_HEAD_DIM, B * R_HEADS)
            for a in seqs]
    y, s_fin = rwkv_scan_lanes(seqs, _state_to_lanes(wkv0), r_k, gn_w, gn_b, B)

    cache_t = cache.reshape(cache.shape[0], PAGE_SIZE, 2 * BRANCH_W).transpose(0, 2, 1)
    nb = win_buf.shape[1]
    win_t = win_buf.reshape(B, nb, BRANCH_W).transpose(0, 2, 1)
    kc_a, kc_b = sample_compress(cache_t, page_table, w_cmp_pos, w_cmp_mix)
    o_n = nsa_sample(q, kv, kc_a, kc_b, gl, gate_n, cache_t, page_table, win_t)
    y = out_proj(_y_from_lanes(y, B).reshape(B * T, R_WIDTH), gate_r.reshape(B * T, R_WIDTH),
                 o_n.reshape(B * T, N_WIDTH), x.reshape(B * T, D), w_out, final_g).reshape(B, T, D)
    kv = kv.reshape(B, T, KV_SLOTS, N_KV_HEADS, N_HEAD_DIM)
    keys_w = jnp.concatenate([win_buf.astype(kv.dtype), kv[:, :, 4:6]], axis=1)
    n_keep = min(WINDOW, nb + T)
    return y, kv[:, :, 0:4], keys_w[:, nb + T - n_keep:], _state_from_lanes(s_fin, B), z[:, -1]


def kernel(x_prompt, x_sample, cache_kv, cache_kv_win, state_wkv, state_shift, page_table,
           norm_g, w_in, mu_shift, w0, w_decay_up, a0, w_aaa_up, k_k, k_a, r_k, gn_w, gn_b,
           w_cmp_pos, w_cmp_mix, w_out, final_g):
    y_p, y_s = x_prompt, x_sample
    kvp, kvs, wnp, wns, skp, sks, shp, shs = [], [], [], [], [], [], [], []
    for layer in range(DEPTH):
        lw = (norm_g[layer], w_in[layer], mu_shift[layer], w0[layer], w_decay_up[layer], a0[layer],
              w_aaa_up[layer], k_k[layer], k_a[layer], r_k[layer], gn_w[layer], gn_b[layer],
              w_cmp_pos[layer], w_cmp_mix[layer], w_out[layer])
        fg = final_g if layer == DEPTH - 1 else None
        y_p, p_kv, p_win, p_wkv, p_shift = prompt_layer(y_p, fg, *lw)
        y_s, s_kv, s_win, s_wkv, s_shift = sample_layer(y_s, fg, cache_kv[layer], cache_kv_win[layer],
                                                        state_wkv[layer], state_shift[layer], page_table, *lw)
        kvp.append(p_kv); kvs.append(s_kv); wnp.append(p_win); wns.append(s_win)
        skp.append(p_wkv); sks.append(s_wkv); shp.append(p_shift); shs.append(s_shift)
    return (y_p, y_s, jnp.stack(kvp), jnp.stack(kvs), jnp.stack(wnp), jnp.stack(wns),
            jnp.stack(skp), jnp.stack(sks), jnp.stack(shp), jnp.stack(shs))
```

```python
import functools

import jax, jax.numpy as jnp
from jax import lax
import numpy as np
from jax.experimental import pallas as pl
from jax.experimental.pallas import tpu as pltpu

D_MODEL = 1024
BATCH = 16
SEQ = 2048
DEPTH = 1
DEC_BATCH = 32
DEC_SEQ = 4
PAST_LEN = 16384
PAGE_SIZE = 128

R_HEAD_DIM = 64
R_WIDTH = D_MODEL // 2
R_HEADS = R_WIDTH // R_HEAD_DIM
DECAY_LORA = 64
AAA_LORA = 64
R_SHIFT_WIDTH = 3 * R_WIDTH + DECAY_LORA + AAA_LORA
N_HEAD_DIM = 64
N_WIDTH = D_MODEL - R_WIDTH
N_HEADS = N_WIDTH // N_HEAD_DIM
N_KV_HEADS = 2
N_GROUP = N_HEADS // N_KV_HEADS
N_BRANCH = 3
KV_SLOTS = 2 * N_BRANCH
CMP_BLOCK = 32
CMP_STRIDE = 16
SEL_BLOCK = 64
SEL_TOPK = 16
WINDOW = 512
MIX_WIDTH = R_WIDTH + N_WIDTH
KV_WIDTH = KV_SLOTS * N_KV_HEADS * N_HEAD_DIM
PROJ_SPLITS = (R_SHIFT_WIDTH, R_WIDTH, N_WIDTH, N_WIDTH, KV_WIDTH, N_BRANCH * N_HEADS)
PROJ_OFFS = tuple(int(o) for o in np.cumsum((0,) + PROJ_SPLITS))
PROJ_WIDTH = sum(PROJ_SPLITS)
RMS_EPS = 1e-6
GN_EPS = 64e-5
MASK_NEG = -1e30
SEL_NEG = -(2.0 ** 100)
SEL_BIAS = 1e4
ATTN_SCALE = N_HEAD_DIM ** -0.5

LANES = 128
BRANCH_W = 2 * N_KV_HEADS * N_HEAD_DIM
NSA_TQ = 128
NSA_KC = 512
PROJ_TM = 256
OUT_TM = 256
SAMPLE_PG = 16
RWKV_TC = 32
VMEM_LIMIT = 56 * 1024 * 1024


def _split3(x):
    hi = x.astype(jnp.bfloat16)
    r1 = x - hi.astype(jnp.float32)
    mid = r1.astype(jnp.bfloat16)
    lo = (r1 - mid.astype(jnp.float32)).astype(jnp.bfloat16)
    return hi, mid, lo


def _dot(a, b):
    return jnp.dot(a, b, preferred_element_type=jnp.float32)


def _dot_nt(a, b):
    return lax.dot_general(a, b, (((1,), (1,)), ((), ())), preferred_element_type=jnp.float32)


def _dot_exact_rhs(a, b01):
    hi, mid, lo = _split3(a)
    return _dot(hi, b01) + _dot(mid, b01) + _dot(lo, b01)


def _dot_nt_x3(a, b):
    ah = a.astype(jnp.bfloat16)
    al = (a - ah.astype(jnp.float32)).astype(jnp.bfloat16)
    bh = b.astype(jnp.bfloat16)
    bl = (b - bh.astype(jnp.float32)).astype(jnp.bfloat16)
    return _dot_nt(ah, bh) + _dot_nt(ah, bl) + _dot_nt(al, bh)


def _dot_x3(a, b):
    ah = a.astype(jnp.bfloat16)
    al = (a - ah.astype(jnp.float32)).astype(jnp.bfloat16)
    bh = b.astype(jnp.bfloat16)
    bl = (b - bh.astype(jnp.float32)).astype(jnp.bfloat16)
    return _dot(ah, bh) + _dot(ah, bl) + _dot(al, bh)


def _silu(x):
    return x * jax.nn.sigmoid(x)


def _softplus(x):
    return jnp.maximum(x, 0.0) + jnp.log1p(jnp.exp(-jnp.abs(x)))


def _masked_softmax_rows(s, mask):
    s = jnp.where(mask, s, MASK_NEG)
    e = jnp.exp(s - jnp.max(s, axis=-1, keepdims=True))
    p = e / jnp.sum(e, axis=-1, keepdims=True)
    return jnp.where(mask, p, 0.0)


def _const_spec(shape, n_grid):
    zeros = (0,) * len(shape)
    if n_grid == 1:
        return pl.BlockSpec(shape, lambda i: zeros)
    return pl.BlockSpec(shape, lambda b, i: zeros)


def _rwkv_prep(zs, w0, a0, k_k, k_a, wdec_t, waaa_t):
    bf16 = jnp.bfloat16
    n = zs.shape[1]
    r = zs[0:R_WIDTH]
    k = zs[R_WIDTH:2 * R_WIDTH]
    v = zs[2 * R_WIDTH:3 * R_WIDTH]
    wd = zs[3 * R_WIDTH:3 * R_WIDTH + DECAY_LORA]
    ad = zs[3 * R_WIDTH + DECAY_LORA:R_SHIFT_WIDTH]
    w_log = -_softplus(-(w0 + _dot(wdec_t, jnp.tanh(wd).astype(bf16)))) - 0.5
    decay = jnp.exp(-jnp.exp(w_log))
    a = jax.nn.sigmoid(a0 + _dot(waaa_t, ad.astype(bf16)))
    kk = (k * k_k).reshape(R_HEADS, R_HEAD_DIM, n)
    kk = (kk * lax.rsqrt(jnp.maximum(jnp.sum(kk * kk, axis=1, keepdims=True), 1e-24))).reshape(R_WIDTH, n)
    k = k * (1.0 + (a - 1.0) * k_a)
    return r, decay, k, v, kk, kk * a


def _prompt_proj_body(x_ref, g_ref, wz_ref, wkv_ref, wn_ref, mu_ref, w0_ref, a0_ref, kk_ref, ka_ref, wdec_ref,
                      waaa_ref, r_o, w_o, k_o, v_o, kkn_o, b_o, kvcs_o, kvw_o, gr_o, q_o, gn_o, gl_o, zl_o, carry_ref):
    bf16 = jnp.bfloat16
    tm = x_ref.shape[1]
    x = x_ref[0]
    xn = (x * lax.rsqrt(jnp.mean(x * x, axis=-1, keepdims=True) + RMS_EPS) * g_ref[...]).astype(bf16)

    z = _dot_nt(wz_ref[...], xn)
    kv = _dot_nt(wkv_ref[...], xn)
    nrm = _dot(xn, wn_ref[...])
    kvcs_o[0] = kv[0:2 * BRANCH_W]
    kvw_o[0] = kv[2 * BRANCH_W:3 * BRANCH_W]
    gr_o[0] = nrm[:, 0:R_WIDTH]
    q_o[0] = nrm[:, R_WIDTH:R_WIDTH + N_WIDTH]
    gn_o[0] = nrm[:, R_WIDTH + N_WIDTH:R_WIDTH + 2 * N_WIDTH]
    gl_o[0] = nrm[:, R_WIDTH + 2 * N_WIDTH:]

    @pl.when(pl.program_id(1) == 0)
    def _():
        carry_ref[...] = jnp.zeros(carry_ref.shape, jnp.float32)

    first = lax.broadcasted_iota(jnp.int32, (1, tm), 1) == 0
    zp = jnp.where(first, carry_ref[:, LANES - 1:LANES], pltpu.roll(z, 1, axis=1))
    carry_ref[...] = z[:, tm - LANES:tm]
    zl_o[0] = z[:, tm - LANES:tm]
    zs = z + (zp - z) * mu_ref[...]
    outs = _rwkv_prep(zs, w0_ref[...], a0_ref[...], kk_ref[...], ka_ref[...], wdec_ref[...], waaa_ref[...])
    for o_ref, val in zip((r_o, w_o, k_o, v_o, kkn_o, b_o), outs):
        o_ref[0] = val


def prompt_in_proj(x, norm_g, w_in, mu_shift, w0, w_decay_up, a0, w_aaa_up, k_k, k_a):
    B, T, D = x.shape
    tm = PROJ_TM
    assert T % tm == 0 and tm % LANES == 0
    f32 = jnp.float32
    pw = _proj_weights(norm_g, w_in, mu_shift, w0, w_decay_up, a0, w_aaa_up, k_k, k_a)
    consts = [pw["g"], pw["wz_t"], pw["wkv"].T, pw["wn"]] + pw["cols"] + pw["lora"]
    cm = lambda c: pl.BlockSpec((1, c, tm), lambda b, i: (b, 0, i))
    tok = lambda c: pl.BlockSpec((1, tm, c), lambda b, i: (b, i, 0))
    cms = lambda c: jax.ShapeDtypeStruct((B, c, T), f32)
    toks = lambda c: jax.ShapeDtypeStruct((B, T, c), f32)
    return pl.pallas_call(
        _prompt_proj_body,
        grid=(B, T // tm),
        in_specs=[tok(D)] + [_const_spec(c.shape, 2) for c in consts],
        out_specs=[cm(R_WIDTH)] * 6 + [cm(2 * BRANCH_W), cm(BRANCH_W), tok(R_WIDTH), tok(N_WIDTH), tok(N_WIDTH),
                                       tok(LANES), pl.BlockSpec((1, R_SHIFT_WIDTH, LANES), lambda b, i: (b, 0, 0))],
        out_shape=[cms(R_WIDTH)] * 6 + [cms(2 * BRANCH_W), cms(BRANCH_W), toks(R_WIDTH), toks(N_WIDTH), toks(N_WIDTH),
                                        toks(LANES), jax.ShapeDtypeStruct((B, R_SHIFT_WIDTH, LANES), f32)],
        scratch_shapes=[pltpu.VMEM((R_SHIFT_WIDTH, LANES), f32)],
        compiler_params=pltpu.CompilerParams(dimension_semantics=("parallel", "arbitrary"),
                                             vmem_limit_bytes=VMEM_LIMIT),
        name="prompt_in_proj",
    )(x, *consts)


def _proj_weights(norm_g, w_in, mu_shift, w0, w_decay_up, a0, w_aaa_up, k_k, k_a):
    f32, bf16 = jnp.float32, jnp.bfloat16
    D = w_in.shape[0]
    o = PROJ_OFFS
    wb = w_in.astype(bf16)
    gl_pad = LANES - PROJ_SPLITS[5]
    wn = jnp.concatenate([wb[:, o[1]:o[4]], wb[:, o[5]:o[6]], jnp.zeros((D, gl_pad), bf16)], axis=1)
    col = lambda p: p.astype(f32).reshape(-1, 1)
    return dict(g=norm_g.astype(f32).reshape(1, D), wz_t=wb[:, o[0]:o[1]].T, wkv=wb[:, o[4]:o[5]], wn=wn,
                cols=[col(mu_shift), col(w0), col(a0), col(k_k), col(k_a)],
                lora=[w_decay_up.astype(bf16).T, w_aaa_up.astype(bf16).T])


def _sample_proj_body(x_ref, prev_ref, g_ref, wz_ref, wkv_ref, wn_ref, mu_ref, w0_ref, a0_ref, kk_ref, ka_ref, wdec_ref,
                      waaa_ref, r_o, w_o, k_o, v_o, kkn_o, b_o, z_o, kv_o, gr_o, q_o, gn_o, gl_o, *, steps):
    bf16 = jnp.bfloat16
    m = x_ref.shape[0]
    x = x_ref[...]
    xn = (x * lax.rsqrt(jnp.mean(x * x, axis=-1, keepdims=True) + RMS_EPS) * g_ref[...]).astype(bf16)
    z = _dot_nt(wz_ref[...], xn)
    kv_o[...] = _dot(xn, wkv_ref[...])
    nrm = _dot(xn, wn_ref[...])
    gr_o[...] = nrm[:, 0:R_WIDTH]
    q_o[...] = nrm[:, R_WIDTH:R_WIDTH + N_WIDTH]
    gn_o[...] = nrm[:, R_WIDTH + N_WIDTH:R_WIDTH + 2 * N_WIDTH]
    gl_o[...] = nrm[:, R_WIDTH + 2 * N_WIDTH:]
    z_o[...] = z
    first = lax.broadcasted_iota(jnp.int32, (1, m), 1) % steps == 0
    zp = jnp.where(first, prev_ref[...], pltpu.roll(z, 1, axis=1))
    zs = z + (zp - z) * mu_ref[...]
    outs = _rwkv_prep(zs, w0_ref[...], a0_ref[...], kk_ref[...], ka_ref[...], wdec_ref[...], waaa_ref[...])
    for o_ref, val in zip((r_o, w_o, k_o, v_o, kkn_o, b_o), outs):
        o_ref[...] = val


def sample_in_proj(x, shift0, norm_g, w_in, mu_shift, w0, w_decay_up, a0, w_aaa_up, k_k, k_a):
    B, T, D = x.shape
    M = B * T
    f32 = jnp.float32
    pw = _proj_weights(norm_g, w_in, mu_shift, w0, w_decay_up, a0, w_aaa_up, k_k, k_a)
    prev = jnp.repeat(shift0.astype(f32).T, T, axis=1)
    args = [x.reshape(M, D), prev, pw["g"], pw["wz_t"], pw["wkv"], pw["wn"]] + pw["cols"] + pw["lora"]
    cm = lambda c: jax.ShapeDtypeStruct((c, M), f32)
    tok = lambda c: jax.ShapeDtypeStruct((M, c), f32)
    out_shape = [cm(R_WIDTH)] * 6 + [cm(R_SHIFT_WIDTH), tok(KV_WIDTH), tok(R_WIDTH), tok(N_WIDTH), tok(N_WIDTH),
                                     tok(LANES)]
    return pl.pallas_call(
        functools.partial(_sample_proj_body, steps=T),
        grid=(1,),
        in_specs=[_const_spec(a.shape, 1) for a in args],
        out_specs=[_const_spec(s.shape, 1) for s in out_shape],
        out_shape=out_shape,
        compiler_params=pltpu.CompilerParams(dimension_semantics=("arbitrary",), vmem_limit_bytes=VMEM_LIMIT),
        name="sample_in_proj",
    )(*args)


def _rwkv_scan_body(r_ref, w_ref, k_ref, v_ref, kk_ref, b_ref, s0_ref, rk_ref, gw_ref, gb_ref, y_ref, s_ref):
    n = s_ref.shape[0]

    @pl.when(pl.program_id(1) == 0)
    def _():
        s_ref[...] = s0_ref[...]

    def step(t, carry):
        r, w, k, kk, b = r_ref[t], w_ref[t], k_ref[t], kk_ref[t], b_ref[t]

        for i in range(n):
            s_i = s_ref[i]
            sa = jnp.sum(s_i * kk, axis=0, keepdims=True)
            s_i = s_i * w - sa * b + v_ref[t, i:i + 1, :] * k
            s_ref[i] = s_i
            y_ref[t, i:i + 1, :] = jnp.sum(s_i * r, axis=0, keepdims=True)
        y = y_ref[t]
        yc = y - jnp.mean(y, axis=0, keepdims=True)
        y = yc * lax.rsqrt(jnp.mean(yc * yc, axis=0, keepdims=True) + GN_EPS)
        y = y * gw_ref[...] + gb_ref[...]
        y_ref[t] = y + jnp.sum(r * k * rk_ref[...], axis=0, keepdims=True) * v_ref[t]
        return carry

    lax.fori_loop(0, r_ref.shape[0], step, 0)


def rwkv_scan_lanes(seqs, s0, r_k, gn_w, gn_b, n_batch):
    T, N, BH = seqs[0].shape
    assert BH % LANES == 0
    tc = min(RWKV_TC, T)
    assert T % tc == 0
    per_lane = lambda p: jnp.tile(p.astype(jnp.float32).reshape(BH // n_batch, N).T, (1, n_batch))
    seq = pl.BlockSpec((tc, N, LANES), lambda l, t: (t, 0, l))
    state = pl.BlockSpec((N, N, LANES), lambda l, t: (0, 0, l))
    param = pl.BlockSpec((N, LANES), lambda l, t: (0, l))
    return pl.pallas_call(
        _rwkv_scan_body,
        grid=(BH // LANES, T // tc),
        in_specs=[seq] * 6 + [state] + [param] * 3,
        out_specs=[seq, state],
        out_shape=[jax.ShapeDtypeStruct((T, N, BH), jnp.float32), jax.ShapeDtypeStruct((N, N, BH), jnp.float32)],
        compiler_params=pltpu.CompilerParams(dimension_semantics=("parallel", "arbitrary"),
                                             vmem_limit_bytes=VMEM_LIMIT),
        name="rwkv_scan",
    )(*seqs, s0, per_lane(r_k), per_lane(gn_w), per_lane(gn_b))


def _lanes_from_cm(a, T):
    return a.reshape(-1, R_HEAD_DIM, T).transpose(2, 1, 0)


def _y_from_lanes(y, B):
    T, N, _ = y.shape
    return y.reshape(T, N, B, R_HEADS).transpose(2, 0, 3, 1).reshape(B, T, R_WIDTH)


def _state_to_lanes(s):
    B, H, N, _ = s.shape
    return s.astype(jnp.float32).transpose(2, 3, 0, 1).reshape(N, N, B * H)


def _state_from_lanes(s, B):
    N = s.shape[0]
    return s.reshape(N, N, B, R_HEADS).transpose(2, 3, 0, 1)


def _cmp_weights(w_pos, w_mix, n_rows):
    wp = jnp.broadcast_to(w_pos.astype(jnp.float32).reshape(2, 1, CMP_BLOCK, N_HEAD_DIM).transpose(2, 0, 1, 3),
                          (CMP_BLOCK, 2, N_KV_HEADS, N_HEAD_DIM)).reshape(CMP_BLOCK, BRANCH_W)
    wa = jnp.tile(wp[:CMP_STRIDE], (n_rows // CMP_STRIDE, 1)).T
    wb = jnp.tile(wp[CMP_STRIDE:], (n_rows // CMP_STRIDE, 1)).T
    wmix = jnp.zeros((BRANCH_W, BRANCH_W), jnp.float32)
    for e in range(2):
        for kh in range(N_KV_HEADS):
            o = (e * N_KV_HEADS + kh) * N_HEAD_DIM
            wmix = wmix.at[o:o + N_HEAD_DIM, o:o + N_HEAD_DIM].set(w_mix[e].astype(jnp.float32))
    return wa, wb, wmix.T


def _compress_prompt_body(kv_ref, wa_ref, wb_ref, pool_ref, wmix_ref, o_ref):
    x = kv_ref[0]
    xw = jnp.concatenate([x * wa_ref[...], x * wb_ref[...]], axis=1)
    o_ref[0] = _dot_x3(wmix_ref[...], _dot_exact_rhs(xw, pool_ref[...]))


def compress_prompt(kv_cs, w_pos, w_mix):
    B, _, T = kv_cs.shape
    assert CMP_BLOCK == 2 * CMP_STRIDE and T % CMP_STRIDE == 0
    ncp = T // CMP_STRIDE
    wa, wb, wmix_t = _cmp_weights(w_pos, w_mix, T)
    chunk = np.arange(T) // CMP_STRIDE
    pool = np.concatenate([chunk[:, None] == np.arange(ncp)[None, :],
                           chunk[:, None] == np.arange(ncp)[None, :] + 1], axis=0)
    pool = jnp.asarray(pool, jnp.bfloat16)
    return pl.pallas_call(
        _compress_prompt_body,
        grid=(B,),
        in_specs=[pl.BlockSpec((1, BRANCH_W, T), lambda b: (b, 0, 0))] + [_const_spec(c.shape, 1)
                                                                          for c in (wa, wb, pool, wmix_t)],
        out_specs=pl.BlockSpec((1, BRANCH_W, ncp), lambda b: (b, 0, 0)),
        out_shape=jax.ShapeDtypeStruct((B, BRANCH_W, ncp), jnp.float32),
        compiler_params=pltpu.CompilerParams(dimension_semantics=("parallel",), vmem_limit_bytes=VMEM_LIMIT),
        name="compress_prompt",
    )(kv_cs, wa, wb, pool, wmix_t)


def _nsa_prompt_body(q_ref, ksel_ref, kwin_ref, kc_ref, gl_ref, gn_ref, ovt_ref, ex_ref, o_ref):
    f32, bf16 = jnp.float32, jnp.bfloat16
    TQ, KC, G = NSA_TQ, NSA_KC, N_GROUP
    ncp = kc_ref.shape[2]
    n_sel = ovt_ref.shape[0]
    topk = min(SEL_TOPK, n_sel)
    qs = pl.program_id(1) * TQ
    qpos = qs + lax.broadcasted_iota(jnp.int32, (TQ, 1), 0)
    lo_half = lax.broadcasted_iota(jnp.int32, (TQ, LANES), 1) < N_HEAD_DIM

    def q_padded(h, half):
        tile = q_ref[0, :, (h // 2) * LANES:(h // 2 + 1) * LANES] * ATTN_SCALE
        if h % 2 != half:
            tile = pltpu.roll(tile, N_HEAD_DIM, axis=1)
        return jnp.where(lo_half if half == 0 else jnp.logical_not(lo_half), tile, 0.0)

    wstart = pl.multiple_of(jnp.maximum(qs - WINDOW, 0), TQ)
    kw = kwin_ref[0, 0:LANES, pl.ds(wstart, WINDOW + TQ)].astype(bf16)
    vw = kwin_ref[0, LANES:2 * LANES, pl.ds(wstart, WINDOW + TQ)].astype(bf16)
    kposw = wstart + lax.broadcasted_iota(jnp.int32, (1, WINDOW + TQ), 1)
    mask_w = (kposw <= qpos) & (kposw > qpos - WINDOW)

    kck = kc_ref[0, 0:LANES, :]
    kcv = kc_ref[0, LANES:2 * LANES, :].astype(bf16)
    kend = lax.broadcasted_iota(jnp.int32, (1, ncp), 1) * CMP_STRIDE + (CMP_BLOCK - 1)
    mask_c = kend <= qpos

    sid_t = lax.broadcasted_iota(jnp.int32, (n_sel, 1), 0)
    cur_t = (qs + lax.broadcasted_iota(jnp.int32, (1, TQ), 1)) // SEL_BLOCK
    valid_t = sid_t <= cur_t
    forced_t = (sid_t == 0) | (sid_t == cur_t) | (sid_t == cur_t - 1)
    n_chunks = qs // KC + 1

    H = N_HEADS
    qg = jnp.concatenate([q_padded(h, h // G) for h in range(H)], axis=0)
    qb = qg.astype(bf16)

    p = _masked_softmax_rows(_dot_x3(qg, kck).reshape(H, TQ, ncp), mask_c[None])
    o_c = _dot_nt(p.reshape(H * TQ, ncp).astype(bf16), kcv).reshape(H, TQ, LANES)

    ovt = ovt_ref[...]
    selb = []
    for kvh in range(N_KV_HEADS):
        hi, mid, lo = _split3(jnp.sum(p[kvh * G:(kvh + 1) * G], axis=0))
        score = _dot_nt(ovt, hi) + _dot_nt(ovt, mid) + _dot_nt(ovt, lo)
        score = jnp.where(valid_t, score + jnp.where(forced_t, SEL_BIAS, 0.0), -SEL_BIAS)
        rank = jnp.zeros((n_sel, TQ), jnp.int32)
        for s2 in range(n_sel):
            row = score[s2:s2 + 1, :]
            ahead = (row > score) | ((row == score) & (s2 < sid_t))
            rank = rank + ahead.astype(jnp.int32)
        selb_t = jnp.concatenate([jnp.where((rank < topk) & valid_t, 0.0, SEL_NEG),
                                  jnp.zeros((LANES - n_sel, TQ), f32)], axis=0)
        selb += [selb_t.T.astype(bf16)] * G
    qa = jnp.concatenate([qb, jnp.concatenate(selb, axis=0)], axis=1)

    def chunk(c, carry, causal):
        m, l, acc = carry
        k0 = pl.multiple_of(c * KC, KC)
        ka = jnp.concatenate([ksel_ref[0, 0:LANES, pl.ds(k0, KC)].astype(bf16), ex_ref[:, pl.ds(k0, KC)]],
                             axis=0)
        va = jnp.concatenate([ksel_ref[0, LANES:2 * LANES, pl.ds(k0, KC)].astype(bf16), ones_kc], axis=0)
        sc = _dot(qa, ka).reshape(H, TQ, KC)
        if causal:
            kpos = k0 + lax.broadcasted_iota(jnp.int32, (1, KC), 1)
            sc = jnp.where((kpos <= qpos)[None], sc, SEL_NEG)
        m_new = jnp.maximum(m, jnp.max(sc, axis=-1, keepdims=True))
        alpha = jnp.exp(m - m_new)
        e = jnp.exp(sc - m_new).astype(bf16)
        pv = _dot_nt(e.reshape(H * TQ, KC), va).reshape(H, TQ, 2 * LANES)
        return m_new, alpha * l + pv[:, :, LANES:], alpha * acc + pv[:, :, :LANES]

    ones_kc = jnp.ones((LANES, KC), bf16)
    carry = (jnp.full((H, TQ, 1), SEL_NEG, f32), jnp.zeros((H, TQ, LANES), f32), jnp.zeros((H, TQ, LANES), f32))
    carry = lax.fori_loop(0, n_chunks - 1, functools.partial(chunk, causal=False), carry)
    _, l, acc = chunk(n_chunks - 1, carry, causal=True)
    o_s = acc / l

    sw = jnp.where(mask_w[None], _dot(qb, kw).reshape(H, TQ, WINDOW + TQ), MASK_NEG)
    ew = jnp.exp(sw - jnp.max(sw, axis=-1, keepdims=True)).astype(bf16)
    pw = _dot_nt(ew.reshape(H * TQ, WINDOW + TQ),
                 jnp.concatenate([vw, jnp.ones((LANES, WINDOW + TQ), bf16)], axis=0)).reshape(H, TQ, 2 * LANES)
    o_w = pw[:, :, :LANES] / pw[:, :, LANES:]

    gates = jax.nn.sigmoid(gl_ref[0])
    heads = []
    for h in range(H):
        o = (gates[:, h:h + 1] * o_c[h] + gates[:, H + h:H + h + 1] * o_s[h]
             + gates[:, 2 * H + h:2 * H + h + 1] * o_w[h])
        heads.append(pltpu.roll(o, N_HEAD_DIM, axis=1) if h % 2 != h // G else o)

    for m in range(N_HEADS // 2):
        tile = jnp.where(lo_half, heads[2 * m], heads[2 * m + 1])
        o_ref[0, :, m * LANES:(m + 1) * LANES] = tile * _silu(gn_ref[0, :, m * LANES:(m + 1) * LANES])


def nsa_prompt(q, kv_cs, kv_w, kc, gl, gate_n):
    B, T, _ = q.shape
    TQ, KC = NSA_TQ, NSA_KC
    assert N_KV_HEADS == 2 and N_HEAD_DIM * 2 == LANES and T % KC == 0 and KC % TQ == 0 and T >= WINDOW + TQ
    assert SEL_BLOCK % CMP_STRIDE == 0 and WINDOW % TQ == 0 and TQ % LANES == 0 and T // SEL_BLOCK <= LANES
    ncp = T // CMP_STRIDE
    n_sel = T // SEL_BLOCK
    cs = np.arange(ncp) * CMP_STRIDE
    ss = np.arange(n_sel) * SEL_BLOCK
    overlap = (cs[:, None] < ss[None, :] + SEL_BLOCK) & (cs[:, None] + CMP_BLOCK > ss[None, :])
    overlap[ncp - 1] = False
    overlap_t = jnp.asarray(overlap.T, jnp.bfloat16)
    expand = jnp.asarray(np.arange(T)[None, :] // SEL_BLOCK == np.arange(LANES)[:, None], jnp.bfloat16)
    return pl.pallas_call(
        _nsa_prompt_body,
        grid=(B, T // TQ),
        in_specs=[pl.BlockSpec((1, TQ, N_WIDTH), lambda b, i: (b, i, 0)),
                  pl.BlockSpec((1, BRANCH_W, T), lambda b, i: (b, 1, 0)),
                  pl.BlockSpec((1, BRANCH_W, T), lambda b, i: (b, 0, 0)),
                  pl.BlockSpec((1, BRANCH_W, ncp), lambda b, i: (b, 0, 0)),
                  pl.BlockSpec((1, TQ, LANES), lambda b, i: (b, i, 0)),
                  pl.BlockSpec((1, TQ, N_WIDTH), lambda b, i: (b, i, 0)),
                  _const_spec((n_sel, ncp), 2), _const_spec((LANES, T), 2)],
        out_specs=pl.BlockSpec((1, TQ, N_WIDTH), lambda b, i: (b, i, 0)),
        out_shape=jax.ShapeDtypeStruct((B, T, N_WIDTH), jnp.float32),
        compiler_params=pltpu.CompilerParams(dimension_semantics=("parallel", "arbitrary"),
                                             vmem_limit_bytes=VMEM_LIMIT),
        name="nsa_prompt",
    )(q, kv_cs, kv_w, kc, gl, gate_n, overlap_t, expand)


def _page_spec(k, branch):
    return pl.BlockSpec((1, BRANCH_W, PAGE_SIZE), lambda b, g, pt: (pt[b, g * SAMPLE_PG + k], branch, 0))


def _sample_compress_body(pt_ref, *refs):
    pages = refs[:SAMPLE_PG]
    wa_ref, wb_ref, pool_ref, wmix_ref, oa_ref, ob_ref = refs[SAMPLE_PG:]
    x = jnp.concatenate([p[0] for p in pages], axis=1)

    def pooled(w_ref):
        xw = x * w_ref[...]
        hi = xw.astype(jnp.bfloat16)
        lo = (xw - hi.astype(jnp.float32)).astype(jnp.bfloat16)
        return _dot_nt(pool_ref[...], hi) + _dot_nt(pool_ref[...], lo)

    oa_ref[0] = _dot_x3(pooled(wa_ref), wmix_ref[...])
    ob_ref[0] = _dot_x3(pooled(wb_ref), wmix_ref[...])


def sample_compress(cache_t, page_table, w_pos, w_mix):
    B, n_pages = page_table.shape
    PG = SAMPLE_PG
    assert n_pages % PG == 0 and CMP_BLOCK == 2 * CMP_STRIDE and PAGE_SIZE % CMP_STRIDE == 0
    rows = PG * PAGE_SIZE // CMP_STRIDE
    ncp = n_pages * PAGE_SIZE // CMP_STRIDE
    wa, wb, wmix_t = _cmp_weights(w_pos, w_mix, PG * PAGE_SIZE)
    wmix = wmix_t.T
    pool = jnp.asarray(np.arange(PG * PAGE_SIZE)[None, :] // CMP_STRIDE == np.arange(rows)[:, None], jnp.bfloat16)
    const = lambda shape: pl.BlockSpec(shape, lambda b, g, pt: (0,) * len(shape))
    out = pl.BlockSpec((1, rows, BRANCH_W), lambda b, g, pt: (b, g, 0))
    return pl.pallas_call(
        _sample_compress_body,
        grid_spec=pltpu.PrefetchScalarGridSpec(
            num_scalar_prefetch=1, grid=(B, n_pages // PG),
            in_specs=[_page_spec(k, 0) for k in range(PG)] + [const(wa.shape), const(wb.shape), const(pool.shape),
                                                              const(wmix.shape)],
            out_specs=[out, out]),
        out_shape=[jax.ShapeDtypeStruct((B, ncp, BRANCH_W), jnp.float32)] * 2,
        compiler_params=pltpu.CompilerParams(dimension_semantics=("parallel", "arbitrary"),
                                             vmem_limit_bytes=VMEM_LIMIT),
        name="sample_compress",
    )(page_table, *([cache_t] * PG), wa, wb, pool, wmix)


def _online_chunk(m, l, acc, s, keep, v, v_channels_major=False):
    s = jnp.where(keep, s, MASK_NEG)
    m_new = jnp.maximum(m, jnp.max(s, axis=-1, keepdims=True))
    alpha = jnp.exp(m - m_new)
    e = jnp.where(keep, jnp.exp(s - m_new), 0.0)
    eb = e.astype(jnp.bfloat16)
    pv = _dot_nt(eb, v) if v_channels_major else _dot(eb, v)
    return m_new, alpha * l + jnp.sum(e, axis=-1, keepdims=True), alpha * acc + pv


def _sample_attend_body(pt_ref, *refs, past):
    f32, bf16 = jnp.float32, jnp.bfloat16
    PG, G = SAMPLE_PG, N_GROUP
    pages = refs[:PG]
    (ka_ref, kb_ref, q_ref, kvn_ref, gl_ref, gn_ref, win_ref, ex_ref, ov_ref, o_ref,
     qf_ref, sel_ref, m_ref, l_ref, acc_ref, oc_ref) = refs[PG:]
    T = q_ref.shape[1]
    R = N_HEADS * T
    ncp = ka_ref.shape[1]
    nsp = ov_ref.shape[1]
    KC = PG * PAGE_SIZE
    n_past = past // SEL_BLOCK
    g = pl.program_id(1)
    lo_half = lax.broadcasted_iota(jnp.int32, (T, LANES), 1) < N_HEAD_DIM
    row = lax.broadcasted_iota(jnp.int32, (R, 1), 0)
    qpos = past + row % T

    @pl.when(g == 0)
    def _():
        pieces = []
        for h in range(N_HEADS):
            kvh = h // G
            tile = q_ref[0, :, (h // 2) * LANES:(h // 2 + 1) * LANES] * ATTN_SCALE
            if h % 2 != kvh:
                tile = pltpu.roll(tile, N_HEAD_DIM, axis=1)
            pieces.append(jnp.where(lo_half if kvh == 0 else jnp.logical_not(lo_half), tile, 0.0))
        q32 = jnp.concatenate(pieces, axis=0)
        qf_ref[...] = q32

        kc = ka_ref[0] + pltpu.roll(kb_ref[0], ncp - 1, axis=0)
        kend = lax.broadcasted_iota(jnp.int32, (1, ncp), 1) * CMP_STRIDE + (CMP_BLOCK - 1)
        p = _masked_softmax_rows(_dot_nt_x3(q32, kc[:, 0:LANES]), kend <= qpos)
        oc_ref[...] = _dot(p.astype(bf16), kc[:, LANES:2 * LANES].astype(bf16))
        imp = jnp.concatenate(
            [sum(p[(kvh * G + gg) * T:(kvh * G + gg + 1) * T] for gg in range(G)) for kvh in range(N_KV_HEADS)],
            axis=0)

        sid = lax.broadcasted_iota(jnp.int32, (1, nsp), 1)
        sid_f = sid.astype(f32)
        cur = (past + lax.broadcasted_iota(jnp.int32, (N_KV_HEADS * T, 1), 0) % T) // SEL_BLOCK
        valid = sid <= cur
        forced = (sid == 0) | (sid == cur) | (sid == cur - 1)
        score = _dot_exact_rhs(imp, ov_ref[...])
        score = jnp.where(valid, score + jnp.where(forced, SEL_BIAS, 0.0), -SEL_BIAS)

        def extract(_, carry):
            sc, chosen = carry
            best = jnp.max(sc, axis=-1, keepdims=True)
            first = jnp.min(jnp.where(sc == best, sid_f, float(nsp)), axis=-1, keepdims=True)
            hit = sid_f == first
            return jnp.where(hit, -3e38, sc), jnp.where(hit, 1.0, chosen)

        _, chosen = lax.fori_loop(0, min(SEL_TOPK, n_past + 1), extract, (score, jnp.zeros_like(score)))
        sel = jnp.where(valid, chosen, 0.0)
        sel_ref[...] = jnp.concatenate(
            [sel[(h // G) * T:(h // G + 1) * T] for h in range(N_HEADS)], axis=0).astype(bf16)
        m_ref[...] = jnp.full(m_ref.shape, MASK_NEG, f32)
        l_ref[...] = jnp.zeros(l_ref.shape, f32)
        acc_ref[...] = jnp.zeros(acc_ref.shape, f32)

    qb = qf_ref[...].astype(bf16)
    ks = jnp.concatenate([p[0, 0:LANES, :] for p in pages], axis=1).astype(bf16)
    vs = jnp.concatenate([p[0, LANES:2 * LANES, :] for p in pages], axis=1).astype(bf16)
    kpos = g * KC + lax.broadcasted_iota(jnp.int32, (1, KC), 1)
    keep = (_dot(sel_ref[...], ex_ref[0]) > 0.5) & (kpos <= qpos)
    m, l, acc = _online_chunk(m_ref[...], l_ref[...], acc_ref[...], _dot(qb, ks), keep, vs, True)
    m_ref[...] = m
    l_ref[...] = l
    acc_ref[...] = acc

    @pl.when(g == pl.num_programs(1) - 1)
    def _():
        def new_rows(c0):
            return jnp.concatenate([kvn_ref[0, :, c0:c0 + LANES], jnp.zeros((LANES - T, LANES), f32)],
                                   axis=0).astype(bf16)

        npos = past + lax.broadcasted_iota(jnp.int32, (1, LANES), 1)
        is_new = npos < past + T
        keep_t = is_new & (npos <= qpos) & (sel_ref[:, n_past:n_past + 1].astype(f32) > 0.5)
        _, l_s, acc_s = _online_chunk(m, l, acc, _dot_nt(qb, new_rows(BRANCH_W)), keep_t, new_rows(BRANCH_W + LANES))
        o_s = acc_s / l_s

        nb = win_ref.shape[2]
        bpos = past - nb + lax.broadcasted_iota(jnp.int32, (1, nb), 1)
        keep_b = (bpos <= qpos) & (bpos > qpos - WINDOW) & (bpos >= 0)
        mw, lw, aw = _online_chunk(jnp.full((R, 1), MASK_NEG, f32), jnp.zeros((R, 1), f32), jnp.zeros((R, LANES), f32),
                                   _dot(qb, win_ref[0, 0:LANES, :].astype(bf16)), keep_b,
                                   win_ref[0, LANES:2 * LANES, :].astype(bf16), True)
        keep_n = is_new & (npos <= qpos) & (npos > qpos - WINDOW)
        _, lw, aw = _online_chunk(mw, lw, aw, _dot_nt(qb, new_rows(2 * BRANCH_W)), keep_n,
                                  new_rows(2 * BRANCH_W + LANES))
        o_w = aw / lw

        gates = jax.nn.sigmoid(gl_ref[0])
        gcol = lambda br: jnp.concatenate([gates[:, br * N_HEADS + h:br * N_HEADS + h + 1] for h in range(N_HEADS)],
                                          axis=0)
        o = gcol(0) * oc_ref[...] + gcol(1) * o_s + gcol(2) * o_w
        for mt in range(N_HEADS // 2):
            halves = []
            for h in (2 * mt, 2 * mt + 1):
                piece = o[h * T:(h + 1) * T]
                if h % 2 != h // G:
                    piece = pltpu.roll(piece, N_HEAD_DIM, axis=1)
                halves.append(piece)
            o_ref[0, :, mt * LANES:(mt + 1) * LANES] = (jnp.where(lo_half, halves[0], halves[1])
                                                        * _silu(gn_ref[0, :, mt * LANES:(mt + 1) * LANES]))


def nsa_sample(q, kv_new, kc_a, kc_b, gl, gate_n, cache_t, page_table, win_t):
    B, T, _ = q.shape
    n_pages = page_table.shape[1]
    past = n_pages * PAGE_SIZE
    PG = SAMPLE_PG
    KC = PG * PAGE_SIZE
    ng = n_pages // PG
    assert n_pages % PG == 0 and past % SEL_BLOCK == 0 and T <= SEL_BLOCK and T <= LANES
    assert (past + T) // CMP_STRIDE == past // CMP_STRIDE and KC % SEL_BLOCK == 0
    ncp = past // CMP_STRIDE
    n_sel = past // SEL_BLOCK + 1
    nsp = -(-n_sel // LANES) * LANES
    cs = np.arange(ncp) * CMP_STRIDE
    ss = np.arange(nsp) * SEL_BLOCK
    overlap = (cs[:, None] < ss[None, :] + SEL_BLOCK) & (cs[:, None] + CMP_BLOCK > ss[None, :])
    overlap[ncp - 1] = False
    overlap[:, n_sel:] = False
    overlap = jnp.asarray(overlap, jnp.bfloat16)
    expand = jnp.asarray((np.arange(past) // SEL_BLOCK).reshape(ng, 1, KC) == np.arange(nsp)[None, :, None],
                         jnp.bfloat16)
    R = N_HEADS * T
    per_b = lambda shape: pl.BlockSpec((1,) + shape, lambda b, g, pt: (b, 0, 0))
    return pl.pallas_call(
        functools.partial(_sample_attend_body, past=past),
        grid_spec=pltpu.PrefetchScalarGridSpec(
            num_scalar_prefetch=1, grid=(B, ng),
            in_specs=[_page_spec(k, 1) for k in range(PG)] + [
                per_b((ncp, BRANCH_W)), per_b((ncp, BRANCH_W)), per_b((T, N_WIDTH)), per_b((T, KV_WIDTH)),
                per_b((T, LANES)), per_b((T, N_WIDTH)), per_b((BRANCH_W, win_t.shape[2])),
                pl.BlockSpec((1, nsp, KC), lambda b, g, pt: (g, 0, 0)),
                pl.BlockSpec((ncp, nsp), lambda b, g, pt: (0, 0))],
            out_specs=per_b((T, N_WIDTH)),
            scratch_shapes=[pltpu.VMEM((R, LANES), jnp.float32), pltpu.VMEM((R, nsp), jnp.bfloat16),
                            pltpu.VMEM((R, 1), jnp.float32), pltpu.VMEM((R, 1), jnp.float32),
                            pltpu.VMEM((R, LANES), jnp.float32), pltpu.VMEM((R, LANES), jnp.float32)]),
        out_shape=jax.ShapeDtypeStruct((B, T, N_WIDTH), jnp.float32),
        compiler_params=pltpu.CompilerParams(dimension_semantics=("parallel", "arbitrary"),
                                             vmem_limit_bytes=VMEM_LIMIT),
        name="nsa_sample",
    )(page_table, *([cache_t] * PG), kc_a, kc_b, q, kv_new, gl, gate_n, win_t, expand, overlap)


def _out_proj_body(yr_ref, gr_ref, on_ref, x_ref, w_ref, g_ref, o_ref, *, final_norm):
    bf16 = jnp.bfloat16
    a = (yr_ref[...] * _silu(gr_ref[...])).astype(bf16)
    y = x_ref[...] + _dot(a, w_ref[0:R_WIDTH, :]) + _dot(on_ref[...].astype(bf16), w_ref[R_WIDTH:MIX_WIDTH, :])
    if final_norm:
        y = y * lax.rsqrt(jnp.mean(y * y, axis=-1, keepdims=True) + RMS_EPS) * g_ref[...]
    o_ref[...] = y


def out_proj(y_r, gate_r, o_n, x, w_out, final_g):
    M, D = x.shape
    tm = min(OUT_TM, M)
    assert M % tm == 0
    g = (jnp.ones((D,), jnp.float32) if final_g is None else final_g.astype(jnp.float32)).reshape(1, D)
    rows = lambda c: pl.BlockSpec((tm, c), lambda i: (i, 0))
    return pl.pallas_call(
        functools.partial(_out_proj_body, final_norm=final_g is not None),
        grid=(M // tm,),
        in_specs=[rows(R_WIDTH), rows(R_WIDTH), rows(N_WIDTH), rows(D), _const_spec((MIX_WIDTH, D), 1),
                  _const_spec((1, D), 1)],
        out_specs=rows(D),
        out_shape=jax.ShapeDtypeStruct((M, D), jnp.float32),
        compiler_params=pltpu.CompilerParams(dimension_semantics=("parallel",), vmem_limit_bytes=VMEM_LIMIT),
        name="out_proj",
    )(y_r, gate_r, o_n, x, w_out.astype(jnp.bfloat16), g)


def prompt_mixers(x, norm_g, w_in, mu_shift, w0, w_decay_up, a0, w_aaa_up, k_k, k_a, r_k, gn_w, gn_b,
                  w_cmp_pos, w_cmp_mix):
    B, T, D = x.shape
    (r, w, k, v, kk, b, kv_cs, kv_w, gate_r, q, gate_n, gl, z_last) = prompt_in_proj(
        x, norm_g, w_in, mu_shift, w0, w_decay_up, a0, w_aaa_up, k_k, k_a)
    seqs = [_lanes_from_cm(a, T) for a in (r, w, k, v, kk, b)]
    o_n = nsa_prompt(q, kv_cs, kv_w, compress_prompt(kv_cs, w_cmp_pos, w_cmp_mix), gl, gate_n)
    seqs, o_n = lax.optimization_barrier((seqs, o_n))
    s0 = jnp.zeros((R_HEAD_DIM, R_HEAD_DIM, B * R_HEADS), jnp.float32)
    y, s_fin = rwkv_scan_lanes(seqs, s0, r_k, gn_w, gn_b, B)
    heads = (N_KV_HEADS, N_HEAD_DIM)
    kv_rows = kv_cs.reshape((B, 4) + heads + (T,)).transpose(0, 4, 1, 2, 3)
    n_keep = min(WINDOW, T)
    win_rows = kv_w[:, :, T - n_keep:].reshape((B, 2) + heads + (n_keep,)).transpose(0, 4, 1, 2, 3)
    outs = (kv_rows, win_rows, _state_from_lanes(s_fin, B), z_last[:, :, LANES - 1])
    return (_y_from_lanes(y, B), gate_r, o_n), outs


def sample_mixers(x, cache, win_buf, wkv0, shift0, page_table, norm_g, w_in, mu_shift, w0, w_decay_up, a0,
                  w_aaa_up, k_k, k_a, r_k, gn_w, gn_b, w_cmp_pos, w_cmp_mix):
    B, T, D = x.shape
    (r, w, k, v, kk, b, z, kv, gate_r, q, gate_n, gl) = sample_in_proj(
        x, shift0, norm_g, w_in, mu_shift, w0, w_decay_up, a0, w_aaa_up, k_k, k_a)
    seqs = [a.reshape(R_HEADS, R_HEAD_DIM, B, T).transpose(3, 1, 2, 0).reshape(T, R_HEAD_DIM, B * R_HEADS)
            for a in (r, w, k, v, kk, b)]
    y, s_fin = rwkv_scan_lanes(seqs, _state_to_lanes(wkv0), r_k, gn_w, gn_b, B)

    cache_t = cache.reshape(cache.shape[0], PAGE_SIZE, 2 * BRANCH_W).transpose(0, 2, 1)
    nb = win_buf.shape[1]
    win_t = win_buf.reshape(B, nb, BRANCH_W).transpose(0, 2, 1)
    kc_a, kc_b = sample_compress(cache_t, page_table, w_cmp_pos, w_cmp_mix)
    kv = kv.reshape(B, T, KV_WIDTH)
    o_n = nsa_sample(q.reshape(B, T, N_WIDTH), kv, kc_a, kc_b, gl.reshape(B, T, LANES),
                     gate_n.reshape(B, T, N_WIDTH), cache_t, page_table, win_t)
    kv = kv.reshape(B, T, KV_SLOTS, N_KV_HEADS, N_HEAD_DIM)
    keys_w = jnp.concatenate([win_buf.astype(kv.dtype), kv[:, :, 4:6]], axis=1)
    n_keep = min(WINDOW, nb + T)
    shift = z.reshape(R_SHIFT_WIDTH, B, T)[:, :, T - 1].T
    outs = (kv[:, :, 0:4], keys_w[:, nb + T - n_keep:], _state_from_lanes(s_fin, B), shift)
    return (_y_from_lanes(y, B), gate_r.reshape(B, T, R_WIDTH), o_n), outs


def _finish(mixed, x, w_out, final_g):
    B, T, D = x.shape
    y_r, gate_r, o_n = (a.reshape(B * T, -1) for a in mixed)
    return out_proj(y_r, gate_r, o_n, x.reshape(B * T, D), w_out, final_g).reshape(B, T, D)


def kernel(x_prompt, x_sample, cache_kv, cache_kv_win, state_wkv, state_shift, page_table,
           norm_g, w_in, mu_shift, w0, w_decay_up, a0, w_aaa_up, k_k, k_a, r_k, gn_w, gn_b,
           w_cmp_pos, w_cmp_mix, w_out, final_g):
    y_p, y_s = x_prompt, x_sample
    kvp, kvs, wnp, wns, skp, sks, shp, shs = [], [], [], [], [], [], [], []
    for layer in range(DEPTH):
        lw = (norm_g[layer], w_in[layer], mu_shift[layer], w0[layer], w_decay_up[layer], a0[layer],
              w_aaa_up[layer], k_k[layer], k_a[layer], r_k[layer], gn_w[layer], gn_b[layer],
              w_cmp_pos[layer], w_cmp_mix[layer])
        fg = final_g if layer == DEPTH - 1 else None
        mixed_p, (p_kv, p_win, p_wkv, p_shift) = prompt_mixers(y_p, *lw)
        mixed_s, (s_kv, s_win, s_wkv, s_shift) = sample_mixers(y_s, cache_kv[layer], cache_kv_win[layer],
                                                               state_wkv[layer], state_shift[layer], page_table, *lw)
        mixed_p, mixed_s = lax.optimization_barrier((mixed_p, mixed_s))
        y_p = _finish(mixed_p, y_p, w_out[layer], fg)
        y_s = _finish(mixed_s, y_s, w_out[layer], fg)
        kvp.append(p_kv); kvs.append(s_kv); wnp.append(p_win); wns.append(s_win)
        skp.append(p_wkv); sks.append(s_wkv); shp.append(p_shift); shs.append(s_shift)
    return (y_p, y_s, jnp.stack(kvp), jnp.stack(kvs), jnp.stack(wnp), jnp.stack(wns),
            jnp.stack(skp), jnp.stack(sks), jnp.stack(shp), jnp.stack(shs))
```

```python
import functools

import jax, jax.numpy as jnp
from jax import lax
import numpy as np
from jax.experimental import pallas as pl
from jax.experimental.pallas import tpu as pltpu

D_MODEL = 1024
BATCH = 16
SEQ = 2048
DEPTH = 1
DEC_BATCH = 32
DEC_SEQ = 4
PAST_LEN = 16384
PAGE_SIZE = 128

R_HEAD_DIM = 64
R_WIDTH = D_MODEL // 2
R_HEADS = R_WIDTH // R_HEAD_DIM
DECAY_LORA = 64
AAA_LORA = 64
R_SHIFT_WIDTH = 3 * R_WIDTH + DECAY_LORA + AAA_LORA
N_HEAD_DIM = 64
N_WIDTH = D_MODEL - R_WIDTH
N_HEADS = N_WIDTH // N_HEAD_DIM
N_KV_HEADS = 2
N_GROUP = N_HEADS // N_KV_HEADS
N_BRANCH = 3
KV_SLOTS = 2 * N_BRANCH
CMP_BLOCK = 32
CMP_STRIDE = 16
SEL_BLOCK = 64
SEL_TOPK = 16
WINDOW = 512
MIX_WIDTH = R_WIDTH + N_WIDTH
KV_WIDTH = KV_SLOTS * N_KV_HEADS * N_HEAD_DIM
PROJ_SPLITS = (R_SHIFT_WIDTH, R_WIDTH, N_WIDTH, N_WIDTH, KV_WIDTH, N_BRANCH * N_HEADS)
PROJ_OFFS = tuple(int(o) for o in np.cumsum((0,) + PROJ_SPLITS))
PROJ_WIDTH = sum(PROJ_SPLITS)
RMS_EPS = 1e-6
GN_EPS = 64e-5
MASK_NEG = -1e30
SEL_NEG = -(2.0 ** 100)
SEL_BIAS = 1e4
ATTN_SCALE = N_HEAD_DIM ** -0.5

LANES = 128
BRANCH_W = 2 * N_KV_HEADS * N_HEAD_DIM
NSA_TQ = 256
NSA_KC = 512
PROJ_TM = 256
OUT_TM = 256
SAMPLE_PG = 16
RWKV_TC = 32
VMEM_LIMIT = 56 * 1024 * 1024


def _split3(x):
    hi = x.astype(jnp.bfloat16)
    r1 = x - hi.astype(jnp.float32)
    mid = r1.astype(jnp.bfloat16)
    lo = (r1 - mid.astype(jnp.float32)).astype(jnp.bfloat16)
    return hi, mid, lo


def _dot(a, b):
    return jnp.dot(a, b, preferred_element_type=jnp.float32)


def _dot_nt(a, b):
    return lax.dot_general(a, b, (((1,), (1,)), ((), ())), preferred_element_type=jnp.float32)


def _dot_exact_rhs(a, b01):
    hi, mid, lo = _split3(a)
    return _dot(hi, b01) + _dot(mid, b01) + _dot(lo, b01)


def _dot_x3(a, b):
    ah = a.astype(jnp.bfloat16)
    al = (a - ah.astype(jnp.float32)).astype(jnp.bfloat16)
    bh = b.astype(jnp.bfloat16)
    bl = (b - bh.astype(jnp.float32)).astype(jnp.bfloat16)
    return _dot(ah, bh) + _dot(ah, bl) + _dot(al, bh)


def _silu(x):
    return x * jax.nn.sigmoid(x)


def _softplus(x):
    return jnp.maximum(x, 0.0) + jnp.log1p(jnp.exp(-jnp.abs(x)))


def _masked_softmax_rows(s, mask):
    s = jnp.where(mask, s, MASK_NEG)
    e = jnp.exp(s - jnp.max(s, axis=-1, keepdims=True))
    p = e / jnp.sum(e, axis=-1, keepdims=True)
    return jnp.where(mask, p, 0.0)


def _const_spec(shape, n_grid):
    zeros = (0,) * len(shape)
    if n_grid == 1:
        return pl.BlockSpec(shape, lambda i: zeros)
    return pl.BlockSpec(shape, lambda b, i: zeros)


def _rwkv_prep(zs, w0, a0, k_k, k_a, wdec_t, waaa_t):
    bf16 = jnp.bfloat16
    n = zs.shape[1]
    r = zs[0:R_WIDTH]
    k = zs[R_WIDTH:2 * R_WIDTH]
    v = zs[2 * R_WIDTH:3 * R_WIDTH]
    wd = zs[3 * R_WIDTH:3 * R_WIDTH + DECAY_LORA]
    ad = zs[3 * R_WIDTH + DECAY_LORA:R_SHIFT_WIDTH]
    w_log = -_softplus(-(w0 + _dot(wdec_t, jnp.tanh(wd).astype(bf16)))) - 0.5
    decay = jnp.exp(-jnp.exp(w_log))
    a = jax.nn.sigmoid(a0 + _dot(waaa_t, ad.astype(bf16)))
    kk = (k * k_k).reshape(R_HEADS, R_HEAD_DIM, n)
    kk = (kk * lax.rsqrt(jnp.maximum(jnp.sum(kk * kk, axis=1, keepdims=True), 1e-24))).reshape(R_WIDTH, n)
    k = k * (1.0 + (a - 1.0) * k_a)
    return r, decay, k, v, kk, kk * a


def _prompt_proj_body(x_ref, g_ref, wz_ref, wkv_ref, wn_ref, mu_ref, w0_ref, a0_ref, kk_ref, ka_ref, wdec_ref,
                      waaa_ref, r_o, w_o, k_o, v_o, kkn_o, b_o, kvcs_o, kvw_o, gr_o, q_o, gn_o, gl_o, zl_o, carry_ref):
    bf16 = jnp.bfloat16
    tm = x_ref.shape[1]
    x = x_ref[0]
    xn = (x * lax.rsqrt(jnp.mean(x * x, axis=-1, keepdims=True) + RMS_EPS) * g_ref[...]).astype(bf16)

    z = _dot_nt(wz_ref[...], xn)
    kv = _dot_nt(wkv_ref[...], xn)
    nrm = _dot(xn, wn_ref[...])
    kvcs_o[0] = kv[0:2 * BRANCH_W]
    kvw_o[0] = kv[2 * BRANCH_W:3 * BRANCH_W]
    gr_o[0] = nrm[:, 0:R_WIDTH]
    q_o[0] = nrm[:, R_WIDTH:R_WIDTH + N_WIDTH]
    gn_o[0] = nrm[:, R_WIDTH + N_WIDTH:R_WIDTH + 2 * N_WIDTH]
    gl_o[0] = nrm[:, R_WIDTH + 2 * N_WIDTH:]

    @pl.when(pl.program_id(1) == 0)
    def _():
        carry_ref[...] = jnp.zeros(carry_ref.shape, jnp.float32)

    first = lax.broadcasted_iota(jnp.int32, (1, tm), 1) == 0
    zp = jnp.where(first, carry_ref[:, LANES - 1:LANES], pltpu.roll(z, 1, axis=1))
    carry_ref[...] = z[:, tm - LANES:tm]
    zl_o[0] = z[:, tm - LANES:tm]
    zs = z + (zp - z) * mu_ref[...]
    outs = _rwkv_prep(zs, w0_ref[...], a0_ref[...], kk_ref[...], ka_ref[...], wdec_ref[...], waaa_ref[...])
    for o_ref, val in zip((r_o, w_o, k_o, v_o, kkn_o, b_o), outs):
        o_ref[0] = val


def prompt_in_proj(x, norm_g, w_in, mu_shift, w0, w_decay_up, a0, w_aaa_up, k_k, k_a):
    B, T, D = x.shape
    tm = PROJ_TM
    assert T % tm == 0 and tm % LANES == 0
    f32 = jnp.float32
    pw = _proj_weights(norm_g, w_in, mu_shift, w0, w_decay_up, a0, w_aaa_up, k_k, k_a)
    consts = [pw["g"], pw["wz_t"], pw["wkv"].T, pw["wn"]] + pw["cols"] + pw["lora"]
    cm = lambda c: pl.BlockSpec((1, c, tm), lambda b, i: (b, 0, i))
    tok = lambda c: pl.BlockSpec((1, tm, c), lambda b, i: (b, i, 0))
    cms = lambda c: jax.ShapeDtypeStruct((B, c, T), f32)
    toks = lambda c: jax.ShapeDtypeStruct((B, T, c), f32)
    return pl.pallas_call(
        _prompt_proj_body,
        grid=(B, T // tm),
        in_specs=[tok(D)] + [_const_spec(c.shape, 2) for c in consts],
        out_specs=[cm(R_WIDTH)] * 6 + [cm(2 * BRANCH_W), cm(BRANCH_W), tok(R_WIDTH), tok(N_WIDTH), tok(N_WIDTH),
                                       tok(LANES), pl.BlockSpec((1, R_SHIFT_WIDTH, LANES), lambda b, i: (b, 0, 0))],
        out_shape=[cms(R_WIDTH)] * 6 + [cms(2 * BRANCH_W), cms(BRANCH_W), toks(R_WIDTH), toks(N_WIDTH), toks(N_WIDTH),
                                        toks(LANES), jax.ShapeDtypeStruct((B, R_SHIFT_WIDTH, LANES), f32)],
        scratch_shapes=[pltpu.VMEM((R_SHIFT_WIDTH, LANES), f32)],
        compiler_params=pltpu.CompilerParams(dimension_semantics=("parallel", "arbitrary"),
                                             vmem_limit_bytes=VMEM_LIMIT),
        name="prompt_in_proj",
    )(x, *consts)


def _proj_weights(norm_g, w_in, mu_shift, w0, w_decay_up, a0, w_aaa_up, k_k, k_a):
    f32, bf16 = jnp.float32, jnp.bfloat16
    D = w_in.shape[0]
    o = PROJ_OFFS
    wb = w_in.astype(bf16)
    gl_pad = LANES - PROJ_SPLITS[5]
    wn = jnp.concatenate([wb[:, o[1]:o[4]], wb[:, o[5]:o[6]], jnp.zeros((D, gl_pad), bf16)], axis=1)
    col = lambda p: p.astype(f32).reshape(-1, 1)
    return dict(g=norm_g.astype(f32).reshape(1, D), wz_t=wb[:, o[0]:o[1]].T, wkv=wb[:, o[4]:o[5]], wn=wn,
                cols=[col(mu_shift), col(w0), col(a0), col(k_k), col(k_a)],
                lora=[w_decay_up.astype(bf16).T, w_aaa_up.astype(bf16).T])


def _sample_proj_body(x_ref, prev_ref, g_ref, wz_ref, wkv_ref, wn_ref, mu_ref, w0_ref, a0_ref, kk_ref, ka_ref, wdec_ref,
                      waaa_ref, r_o, w_o, k_o, v_o, kkn_o, b_o, z_o, kv_o, gr_o, q_o, gn_o, gl_o, *, steps):
    bf16 = jnp.bfloat16
    m = x_ref.shape[0]
    x = x_ref[...]
    xn = (x * lax.rsqrt(jnp.mean(x * x, axis=-1, keepdims=True) + RMS_EPS) * g_ref[...]).astype(bf16)
    z = _dot_nt(wz_ref[...], xn)
    kv_o[...] = _dot(xn, wkv_ref[...])
    nrm = _dot(xn, wn_ref[...])
    gr_o[...] = nrm[:, 0:R_WIDTH]
    q_o[...] = nrm[:, R_WIDTH:R_WIDTH + N_WIDTH]
    gn_o[...] = nrm[:, R_WIDTH + N_WIDTH:R_WIDTH + 2 * N_WIDTH]
    gl_o[...] = nrm[:, R_WIDTH + 2 * N_WIDTH:]
    z_o[...] = z
    first = lax.broadcasted_iota(jnp.int32, (1, m), 1) % steps == 0
    zp = jnp.where(first, prev_ref[...], pltpu.roll(z, 1, axis=1))
    zs = z + (zp - z) * mu_ref[...]
    outs = _rwkv_prep(zs, w0_ref[...], a0_ref[...], kk_ref[...], ka_ref[...], wdec_ref[...], waaa_ref[...])
    for o_ref, val in zip((r_o, w_o, k_o, v_o, kkn_o, b_o), outs):
        o_ref[...] = val


def sample_in_proj(x, shift0, norm_g, w_in, mu_shift, w0, w_decay_up, a0, w_aaa_up, k_k, k_a):
    B, T, D = x.shape
    M = B * T
    f32 = jnp.float32
    pw = _proj_weights(norm_g, w_in, mu_shift, w0, w_decay_up, a0, w_aaa_up, k_k, k_a)
    prev = jnp.repeat(shift0.astype(f32).T, T, axis=1)
    args = [x.reshape(M, D), prev, pw["g"], pw["wz_t"], pw["wkv"], pw["wn"]] + pw["cols"] + pw["lora"]
    cm = lambda c: jax.ShapeDtypeStruct((c, M), f32)
    tok = lambda c: jax.ShapeDtypeStruct((M, c), f32)
    out_shape = [cm(R_WIDTH)] * 6 + [cm(R_SHIFT_WIDTH), tok(KV_WIDTH), tok(R_WIDTH), tok(N_WIDTH), tok(N_WIDTH),
                                     tok(LANES)]
    return pl.pallas_call(
        functools.partial(_sample_proj_body, steps=T),
        grid=(1,),
        in_specs=[_const_spec(a.shape, 1) for a in args],
        out_specs=[_const_spec(s.shape, 1) for s in out_shape],
        out_shape=out_shape,
        compiler_params=pltpu.CompilerParams(dimension_semantics=("arbitrary",), vmem_limit_bytes=VMEM_LIMIT),
        name="sample_in_proj",
    )(*args)


def _rwkv_scan_body(r_ref, w_ref, k_ref, v_ref, kk_ref, b_ref, s0_ref, rk_ref, gw_ref, gb_ref, y_ref, s_ref):
    n = s_ref.shape[0]

    @pl.when(pl.program_id(1) == 0)
    def _():
        s_ref[...] = s0_ref[...]

    def step(t, carry):
        r, w, k, kk, b = r_ref[t], w_ref[t], k_ref[t], kk_ref[t], b_ref[t]

        for i in range(n):
            s_i = s_ref[i]
            sa = jnp.sum(s_i * kk, axis=0, keepdims=True)
            s_i = s_i * w - sa * b + v_ref[t, i:i + 1, :] * k
            s_ref[i] = s_i
            y_ref[t, i:i + 1, :] = jnp.sum(s_i * r, axis=0, keepdims=True)
        y = y_ref[t]
        yc = y - jnp.mean(y, axis=0, keepdims=True)
        y = yc * lax.rsqrt(jnp.mean(yc * yc, axis=0, keepdims=True) + GN_EPS)
        y = y * gw_ref[...] + gb_ref[...]
        y_ref[t] = y + jnp.sum(r * k * rk_ref[...], axis=0, keepdims=True) * v_ref[t]
        return carry

    lax.fori_loop(0, r_ref.shape[0], step, 0)


def rwkv_scan_lanes(seqs, s0, r_k, gn_w, gn_b, n_batch):
    T, N, BH = seqs[0].shape
    assert BH % LANES == 0
    tc = min(RWKV_TC, T)
    assert T % tc == 0
    per_lane = lambda p: jnp.tile(p.astype(jnp.float32).reshape(BH // n_batch, N).T, (1, n_batch))
    seq = pl.BlockSpec((tc, N, LANES), lambda l, t: (t, 0, l))
    state = pl.BlockSpec((N, N, LANES), lambda l, t: (0, 0, l))
    param = pl.BlockSpec((N, LANES), lambda l, t: (0, l))
    return pl.pallas_call(
        _rwkv_scan_body,
        grid=(BH // LANES, T // tc),
        in_specs=[seq] * 6 + [state] + [param] * 3,
        out_specs=[seq, state],
        out_shape=[jax.ShapeDtypeStruct((T, N, BH), jnp.float32), jax.ShapeDtypeStruct((N, N, BH), jnp.float32)],
        compiler_params=pltpu.CompilerParams(dimension_semantics=("parallel", "arbitrary"),
                                             vmem_limit_bytes=VMEM_LIMIT),
        name="rwkv_scan",
    )(*seqs, s0, per_lane(r_k), per_lane(gn_w), per_lane(gn_b))


def _lanes_from_cm(a, T):
    return a.reshape(-1, R_HEAD_DIM, T).transpose(2, 1, 0)


def _y_from_lanes(y, B):
    T, N, _ = y.shape
    return y.reshape(T, N, B, R_HEADS).transpose(2, 0, 3, 1).reshape(B, T, R_WIDTH)


def _state_to_lanes(s):
    B, H, N, _ = s.shape
    return s.astype(jnp.float32).transpose(2, 3, 0, 1).reshape(N, N, B * H)


def _state_from_lanes(s, B):
    N = s.shape[0]
    return s.reshape(N, N, B, R_HEADS).transpose(2, 3, 0, 1)


def _cmp_weights(w_pos, w_mix, n_rows):
    wp = jnp.broadcast_to(w_pos.astype(jnp.float32).reshape(2, 1, CMP_BLOCK, N_HEAD_DIM).transpose(2, 0, 1, 3),
                          (CMP_BLOCK, 2, N_KV_HEADS, N_HEAD_DIM)).reshape(CMP_BLOCK, BRANCH_W)
    wa = jnp.tile(wp[:CMP_STRIDE], (n_rows // CMP_STRIDE, 1)).T
    wb = jnp.tile(wp[CMP_STRIDE:], (n_rows // CMP_STRIDE, 1)).T
    wmix = jnp.zeros((BRANCH_W, BRANCH_W), jnp.float32)
    for e in range(2):
        for kh in range(N_KV_HEADS):
            o = (e * N_KV_HEADS + kh) * N_HEAD_DIM
            wmix = wmix.at[o:o + N_HEAD_DIM, o:o + N_HEAD_DIM].set(w_mix[e].astype(jnp.float32))
    return wa, wb, wmix.T


def _compress_prompt_body(kv_ref, wa_ref, wb_ref, pool_ref, wmix_ref, o_ref):
    x = kv_ref[0]
    xw = jnp.concatenate([x * wa_ref[...], x * wb_ref[...]], axis=1)
    o_ref[0] = _dot_x3(wmix_ref[...], _dot_exact_rhs(xw, pool_ref[...]))


def compress_prompt(kv_cs, w_pos, w_mix):
    B, _, T = kv_cs.shape
    assert CMP_BLOCK == 2 * CMP_STRIDE and T % CMP_STRIDE == 0
    ncp = T // CMP_STRIDE
    wa, wb, wmix_t = _cmp_weights(w_pos, w_mix, T)
    chunk = np.arange(T) // CMP_STRIDE
    pool = np.concatenate([chunk[:, None] == np.arange(ncp)[None, :],
                           chunk[:, None] == np.arange(ncp)[None, :] + 1], axis=0)
    pool = jnp.asarray(pool, jnp.bfloat16)
    return pl.pallas_call(
        _compress_prompt_body,
        grid=(B,),
        in_specs=[pl.BlockSpec((1, BRANCH_W, T), lambda b: (b, 0, 0))] + [_const_spec(c.shape, 1)
                                                                          for c in (wa, wb, pool, wmix_t)],
        out_specs=pl.BlockSpec((1, BRANCH_W, ncp), lambda b: (b, 0, 0)),
        out_shape=jax.ShapeDtypeStruct((B, BRANCH_W, ncp), jnp.float32),
        compiler_params=pltpu.CompilerParams(dimension_semantics=("parallel",), vmem_limit_bytes=VMEM_LIMIT),
        name="compress_prompt",
    )(kv_cs, wa, wb, pool, wmix_t)


def _nsa_prompt_body(q_ref, ksel_ref, kwin_ref, kc_ref, gl_ref, gn_ref, ovt_ref, ex_ref, o_ref):
    f32, bf16 = jnp.float32, jnp.bfloat16
    TQ, KC, G = NSA_TQ, NSA_KC, N_GROUP
    ncp = kc_ref.shape[2]
    n_sel = ovt_ref.shape[0]
    topk = min(SEL_TOPK, n_sel)
    qs = pl.program_id(1) * TQ
    qpos = qs + lax.broadcasted_iota(jnp.int32, (TQ, 1), 0)
    lo_half = lax.broadcasted_iota(jnp.int32, (TQ, LANES), 1) < N_HEAD_DIM

    def q_padded(h, half):
        tile = q_ref[0, :, (h // 2) * LANES:(h // 2 + 1) * LANES] * ATTN_SCALE
        if h % 2 != half:
            tile = pltpu.roll(tile, N_HEAD_DIM, axis=1)
        return jnp.where(lo_half if half == 0 else jnp.logical_not(lo_half), tile, 0.0)

    wstart = pl.multiple_of(jnp.maximum(qs - WINDOW, 0), TQ)
    kw = kwin_ref[0, 0:LANES, pl.ds(wstart, WINDOW + TQ)].astype(bf16)
    vw = kwin_ref[0, LANES:2 * LANES, pl.ds(wstart, WINDOW + TQ)].astype(bf16)
    kposw = wstart + lax.broadcasted_iota(jnp.int32, (1, WINDOW + TQ), 1)
    mask_w = (kposw <= qpos) & (kposw > qpos - WINDOW)

    kck = kc_ref[0, 0:LANES, :]
    kcv = kc_ref[0, LANES:2 * LANES, :].astype(bf16)
    kend = lax.broadcasted_iota(jnp.int32, (1, ncp), 1) * CMP_STRIDE + (CMP_BLOCK - 1)
    mask_c = kend <= qpos

    sid_t = lax.broadcasted_iota(jnp.int32, (n_sel, 1), 0)
    cur_t = (qs + lax.broadcasted_iota(jnp.int32, (1, TQ), 1)) // SEL_BLOCK
    valid_t = sid_t <= cur_t
    forced_t = (sid_t == 0) | (sid_t == cur_t) | (sid_t == cur_t - 1)
    n_chunks = qs // KC + 1

    H = N_HEADS
    qg = jnp.concatenate([q_padded(h, h // G) for h in range(H)], axis=0)
    qb = qg.astype(bf16)

    p = _masked_softmax_rows(_dot_x3(qg, kck).reshape(H, TQ, ncp), mask_c[None])
    o_c = _dot_nt(p.reshape(H * TQ, ncp).astype(bf16), kcv).reshape(H, TQ, LANES)

    ovt = ovt_ref[...]
    selb = []
    for kvh in range(N_KV_HEADS):
        hi, mid, lo = _split3(jnp.sum(p[kvh * G:(kvh + 1) * G], axis=0))
        score = _dot_nt(ovt, hi) + _dot_nt(ovt, mid) + _dot_nt(ovt, lo)
        score = jnp.where(valid_t, score + jnp.where(forced_t, SEL_BIAS, 0.0), -SEL_BIAS)
        rank = jnp.zeros((n_sel, TQ), jnp.int32)
        for s2 in range(n_sel):
            row = score[s2:s2 + 1, :]
            ahead = (row > score) | ((row == score) & (s2 < sid_t))
            rank = rank + ahead.astype(jnp.int32)
        selb_t = jnp.concatenate([jnp.where((rank < topk) & valid_t, 0.0, SEL_NEG),
                                  jnp.zeros((LANES - n_sel, TQ), f32)], axis=0)
        selb += [selb_t.T.astype(bf16)] * G
    qa = jnp.concatenate([qb, jnp.concatenate(selb, axis=0)], axis=1)

    def chunk(c, carry, causal):
        m, l, acc = carry
        k0 = pl.multiple_of(c * KC, KC)
        ka = jnp.concatenate([ksel_ref[0, 0:LANES, pl.ds(k0, KC)].astype(bf16), ex_ref[:, pl.ds(k0, KC)]],
                             axis=0)
        va = jnp.concatenate([ksel_ref[0, LANES:2 * LANES, pl.ds(k0, KC)].astype(bf16), ones_kc], axis=0)
        sc = _dot(qa, ka).reshape(H, TQ, KC)
        if causal:
            kpos = k0 + lax.broadcasted_iota(jnp.int32, (1, KC), 1)
            sc = jnp.where((kpos <= qpos)[None], sc, SEL_NEG)
        m_new = jnp.maximum(m, jnp.max(sc, axis=-1, keepdims=True))
        alpha = jnp.exp(m - m_new)
        e = jnp.exp(sc - m_new).astype(bf16)
        pv = _dot_nt(e.reshape(H * TQ, KC), va).reshape(H, TQ, 2 * LANES)
        return m_new, alpha * l + pv[:, :, LANES:], alpha * acc + pv[:, :, :LANES]

    ones_kc = jnp.ones((LANES, KC), bf16)
    carry = (jnp.full((H, TQ, 1), SEL_NEG, f32), jnp.zeros((H, TQ, LANES), f32), jnp.zeros((H, TQ, LANES), f32))
    carry = lax.fori_loop(0, n_chunks - 1, functools.partial(chunk, causal=False), carry)
    _, l, acc = chunk(n_chunks - 1, carry, causal=True)
    o_s = acc / l

    sw = jnp.where(mask_w[None], _dot(qb, kw).reshape(H, TQ, WINDOW + TQ), MASK_NEG)
    ew = jnp.exp(sw - jnp.max(sw, axis=-1, keepdims=True)).astype(bf16)
    pw = _dot_nt(ew.reshape(H * TQ, WINDOW + TQ),
                 jnp.concatenate([vw, jnp.ones((LANES, WINDOW + TQ), bf16)], axis=0)).reshape(H, TQ, 2 * LANES)
    o_w = pw[:, :, :LANES] / pw[:, :, LANES:]

    gates = jax.nn.sigmoid(gl_ref[0])
    heads = []
    for h in range(H):
        o = (gates[:, h:h + 1] * o_c[h] + gates[:, H + h:H + h + 1] * o_s[h]
             + gates[:, 2 * H + h:2 * H + h + 1] * o_w[h])
        heads.append(pltpu.roll(o, N_HEAD_DIM, axis=1) if h % 2 != h // G else o)

    for m in range(N_HEADS // 2):
        tile = jnp.where(lo_half, heads[2 * m], heads[2 * m + 1])
        o_ref[0, :, m * LANES:(m + 1) * LANES] = tile * _silu(gn_ref[0, :, m * LANES:(m + 1) * LANES])


def nsa_prompt(q, kv_cs, kv_w, kc, gl, gate_n):
    B, T, _ = q.shape
    TQ, KC = NSA_TQ, NSA_KC
    assert N_KV_HEADS == 2 and N_HEAD_DIM * 2 == LANES and T % KC == 0 and KC % TQ == 0 and T >= WINDOW + TQ
    assert SEL_BLOCK % CMP_STRIDE == 0 and WINDOW % TQ == 0 and TQ % LANES == 0 and T // SEL_BLOCK <= LANES
    ncp = T // CMP_STRIDE
    n_sel = T // SEL_BLOCK
    cs = np.arange(ncp) * CMP_STRIDE
    ss = np.arange(n_sel) * SEL_BLOCK
    overlap = (cs[:, None] < ss[None, :] + SEL_BLOCK) & (cs[:, None] + CMP_BLOCK > ss[None, :])
    overlap[ncp - 1] = False
    overlap_t = jnp.asarray(overlap.T, jnp.bfloat16)
    expand = jnp.asarray(np.arange(T)[None, :] // SEL_BLOCK == np.arange(LANES)[:, None], jnp.bfloat16)
    return pl.pallas_call(
        _nsa_prompt_body,
        grid=(B, T // TQ),
        in_specs=[pl.BlockSpec((1, TQ, N_WIDTH), lambda b, i: (b, i, 0)),
                  pl.BlockSpec((1, BRANCH_W, T), lambda b, i: (b, 1, 0)),
                  pl.BlockSpec((1, BRANCH_W, T), lambda b, i: (b, 0, 0)),
                  pl.BlockSpec((1, BRANCH_W, ncp), lambda b, i: (b, 0, 0)),
                  pl.BlockSpec((1, TQ, LANES), lambda b, i: (b, i, 0)),
                  pl.BlockSpec((1, TQ, N_WIDTH), lambda b, i: (b, i, 0)),
                  _const_spec((n_sel, ncp), 2), _const_spec((LANES, T), 2)],
        out_specs=pl.BlockSpec((1, TQ, N_WIDTH), lambda b, i: (b, i, 0)),
        out_shape=jax.ShapeDtypeStruct((B, T, N_WIDTH), jnp.float32),
        compiler_params=pltpu.CompilerParams(dimension_semantics=("parallel", "arbitrary"),
                                             vmem_limit_bytes=VMEM_LIMIT),
        name="nsa_prompt",
    )(q, kv_cs, kv_w, kc, gl, gate_n, overlap_t, expand)


def _page_spec(k, branch):
    return pl.BlockSpec((1, BRANCH_W, PAGE_SIZE), lambda b, g, pt: (pt[b, g * SAMPLE_PG + k], branch, 0))


def _sample_compress_body(pt_ref, *refs):
    pages = refs[:SAMPLE_PG]
    wa_ref, wb_ref, pool_ref, wmix_ref, oa_ref, ob_ref = refs[SAMPLE_PG:]
    x = jnp.concatenate([p[0] for p in pages], axis=1)

    def pooled(w_ref):
        xw = x * w_ref[...]
        hi = xw.astype(jnp.bfloat16)
        lo = (xw - hi.astype(jnp.float32)).astype(jnp.bfloat16)
        return _dot(hi, pool_ref[...]) + _dot(lo, pool_ref[...])

    oa_ref[0] = _dot_x3(wmix_ref[...], pooled(wa_ref))
    ob_ref[0] = _dot_x3(wmix_ref[...], pooled(wb_ref))


def sample_compress(cache_t, page_table, w_pos, w_mix):
    B, n_pages = page_table.shape
    PG = SAMPLE_PG
    assert n_pages % PG == 0 and CMP_BLOCK == 2 * CMP_STRIDE and PAGE_SIZE % CMP_STRIDE == 0
    cols = PG * PAGE_SIZE // CMP_STRIDE
    assert cols % LANES == 0
    ncp = n_pages * PAGE_SIZE // CMP_STRIDE
    wa, wb, wmix_t = _cmp_weights(w_pos, w_mix, PG * PAGE_SIZE)
    pool = jnp.asarray(np.arange(PG * PAGE_SIZE)[:, None] // CMP_STRIDE == np.arange(cols)[None, :], jnp.bfloat16)
    const = lambda shape: pl.BlockSpec(shape, lambda b, g, pt: (0,) * len(shape))
    out = pl.BlockSpec((1, BRANCH_W, cols), lambda b, g, pt: (b, 0, g))
    return pl.pallas_call(
        _sample_compress_body,
        grid_spec=pltpu.PrefetchScalarGridSpec(
            num_scalar_prefetch=1, grid=(B, n_pages // PG),
            in_specs=[_page_spec(k, 0) for k in range(PG)] + [const(wa.shape), const(wb.shape), const(pool.shape),
                                                              const(wmix_t.shape)],
            out_specs=[out, out]),
        out_shape=[jax.ShapeDtypeStruct((B, BRANCH_W, ncp), jnp.float32)] * 2,
        compiler_params=pltpu.CompilerParams(dimension_semantics=("parallel", "arbitrary"),
                                             vmem_limit_bytes=VMEM_LIMIT),
        name="sample_compress",
    )(page_table, *([cache_t] * PG), wa, wb, pool, wmix_t)


def _online_chunk(m, l, acc, s, keep, v, v_channels_major=False):
    s = jnp.where(keep, s, MASK_NEG)
    m_new = jnp.maximum(m, jnp.max(s, axis=-1, keepdims=True))
    alpha = jnp.exp(m - m_new)
    e = jnp.where(keep, jnp.exp(s - m_new), 0.0)
    eb = e.astype(jnp.bfloat16)
    pv = _dot_nt(eb, v) if v_channels_major else _dot(eb, v)
    return m_new, alpha * l + jnp.sum(e, axis=-1, keepdims=True), alpha * acc + pv


def _sample_attend_body(pt_ref, *refs, past):
    f32, bf16 = jnp.float32, jnp.bfloat16
    PG, G = SAMPLE_PG, N_GROUP
    pages = refs[:PG]
    (ka_ref, kb_ref, q_ref, kvn_ref, gl_ref, gn_ref, win_ref, ex_ref, ov_ref, o_ref,
     qf_ref, selb_ref, tail_ref, m_ref, l_ref, acc_ref, oc_ref) = refs[PG:]
    T = q_ref.shape[1]
    R = N_HEADS * T
    ncp = ka_ref.shape[2]
    nsp = ov_ref.shape[1]
    KC = PG * PAGE_SIZE
    n_past = past // SEL_BLOCK
    bpc = KC // SEL_BLOCK
    g = pl.program_id(1)
    lo_half = lax.broadcasted_iota(jnp.int32, (T, LANES), 1) < N_HEAD_DIM
    row = lax.broadcasted_iota(jnp.int32, (R, 1), 0)
    qpos = past + row % T

    @pl.when(g == 0)
    def _():
        pieces = []
        for h in range(N_HEADS):
            kvh = h // G
            tile = q_ref[0, :, (h // 2) * LANES:(h // 2 + 1) * LANES] * ATTN_SCALE
            if h % 2 != kvh:
                tile = pltpu.roll(tile, N_HEAD_DIM, axis=1)
            pieces.append(jnp.where(lo_half if kvh == 0 else jnp.logical_not(lo_half), tile, 0.0))
        q32 = jnp.concatenate(pieces, axis=0)
        qf_ref[...] = q32

        kc = ka_ref[0] + pltpu.roll(kb_ref[0], ncp - 1, axis=1)
        kend = lax.broadcasted_iota(jnp.int32, (1, ncp), 1) * CMP_STRIDE + (CMP_BLOCK - 1)
        p = _masked_softmax_rows(_dot_x3(q32, kc[0:LANES]), kend <= qpos)
        oc_ref[...] = _dot_nt(p.astype(bf16), kc[LANES:2 * LANES].astype(bf16))
        imp = jnp.concatenate(
            [sum(p[(kvh * G + gg) * T:(kvh * G + gg + 1) * T] for gg in range(G)) for kvh in range(N_KV_HEADS)],
            axis=0)

        sid = lax.broadcasted_iota(jnp.int32, (1, nsp), 1)
        sid_f = sid.astype(f32)
        cur = (past + lax.broadcasted_iota(jnp.int32, (N_KV_HEADS * T, 1), 0) % T) // SEL_BLOCK
        valid = sid <= cur
        forced = (sid == 0) | (sid == cur) | (sid == cur - 1)
        score = _dot_exact_rhs(imp, ov_ref[...])
        score = jnp.where(valid, score + jnp.where(forced, SEL_BIAS, 0.0), -SEL_BIAS)

        def extract(_, carry):
            sc, chosen = carry
            best = jnp.max(sc, axis=-1, keepdims=True)
            first = jnp.min(jnp.where(sc == best, sid_f, float(nsp)), axis=-1, keepdims=True)
            hit = sid_f == first
            return jnp.where(hit, -3e38, sc), jnp.where(hit, 1.0, chosen)

        _, chosen = lax.fori_loop(0, min(SEL_TOPK, n_past + 1), extract, (score, jnp.zeros_like(score)))
        sel = jnp.concatenate([jnp.where(valid, chosen, 0.0)[(h // G) * T:(h // G + 1) * T] for h in range(N_HEADS)],
                              axis=0)
        tail_ref[...] = sel[:, n_past:n_past + 1]
        selb = jnp.where(sel > 0.5, 0.0, SEL_NEG)
        for c in range(n_past // bpc):
            selb_ref[c] = jnp.concatenate([selb[:, c * bpc:(c + 1) * bpc], jnp.zeros((R, LANES - bpc), f32)],
                                          axis=1).astype(bf16)
        m_ref[...] = jnp.full(m_ref.shape, SEL_NEG, f32)
        l_ref[...] = jnp.zeros(l_ref.shape, f32)
        acc_ref[...] = jnp.zeros(acc_ref.shape, f32)

    qb = qf_ref[...].astype(bf16)
    ka = jnp.concatenate([p[0, 0:LANES, :] for p in pages], axis=1).astype(bf16)
    ka = jnp.concatenate([ka, ex_ref[...]], axis=0)
    va = jnp.concatenate([jnp.concatenate([p[0, LANES:2 * LANES, :] for p in pages], axis=1).astype(bf16),
                          jnp.ones((LANES, KC), bf16)], axis=0)
    s = _dot(jnp.concatenate([qb, selb_ref[g]], axis=1), ka)
    m = jnp.maximum(m_ref[...], jnp.max(s, axis=-1, keepdims=True))
    alpha = jnp.exp(m_ref[...] - m)
    pv = _dot_nt(jnp.exp(s - m).astype(bf16), va)
    l = alpha * l_ref[...] + pv[:, LANES:]
    acc = alpha * acc_ref[...] + pv[:, :LANES]
    m_ref[...] = m
    l_ref[...] = l
    acc_ref[...] = acc

    @pl.when(g == pl.num_programs(1) - 1)
    def _():
        def new_rows(c0):
            return jnp.concatenate([kvn_ref[0, :, c0:c0 + LANES], jnp.zeros((LANES - T, LANES), f32)],
                                   axis=0).astype(bf16)

        npos = past + lax.broadcasted_iota(jnp.int32, (1, LANES), 1)
        is_new = npos < past + T
        keep_t = is_new & (npos <= qpos) & (tail_ref[...] > 0.5)
        _, l_s, acc_s = _online_chunk(m, l, acc, _dot_nt(qb, new_rows(BRANCH_W)), keep_t, new_rows(BRANCH_W + LANES))
        o_s = acc_s / l_s

        nb = win_ref.shape[2]
        bpos = past - nb + lax.broadcasted_iota(jnp.int32, (1, nb), 1)
        keep_b = (bpos <= qpos) & (bpos > qpos - WINDOW) & (bpos >= 0)
        mw, lw, aw = _online_chunk(jnp.full((R, 1), MASK_NEG, f32), jnp.zeros((R, 1), f32), jnp.zeros((R, LANES), f32),
                                   _dot(qb, win_ref[0, 0:LANES, :].astype(bf16)), keep_b,
                                   win_ref[0, LANES:2 * LANES, :].astype(bf16), True)
        keep_n = is_new & (npos <= qpos) & (npos > qpos - WINDOW)
        _, lw, aw = _online_chunk(mw, lw, aw, _dot_nt(qb, new_rows(2 * BRANCH_W)), keep_n,
                                  new_rows(2 * BRANCH_W + LANES))
        o_w = aw / lw

        gates = jax.nn.sigmoid(gl_ref[0])
        gcol = lambda br: jnp.concatenate([gates[:, br * N_HEADS + h:br * N_HEADS + h + 1] for h in range(N_HEADS)],
                                          axis=0)
        o = gcol(0) * oc_ref[...] + gcol(1) * o_s + gcol(2) * o_w
        for mt in range(N_HEADS // 2):
            halves = []
            for h in (2 * mt, 2 * mt + 1):
                piece = o[h * T:(h + 1) * T]
                if h % 2 != h // G:
                    piece = pltpu.roll(piece, N_HEAD_DIM, axis=1)
                halves.append(piece)
            o_ref[0, :, mt * LANES:(mt + 1) * LANES] = (jnp.where(lo_half, halves[0], halves[1])
                                                        * _silu(gn_ref[0, :, mt * LANES:(mt + 1) * LANES]))


def nsa_sample(q, kv_new, kc_a, kc_b, gl, gate_n, cache_t, page_table, win_t):
    B, T, _ = q.shape
    n_pages = page_table.shape[1]
    past = n_pages * PAGE_SIZE
    PG = SAMPLE_PG
    KC = PG * PAGE_SIZE
    ng = n_pages // PG
    assert n_pages % PG == 0 and past % SEL_BLOCK == 0 and T <= SEL_BLOCK and T <= LANES
    assert (past + T) // CMP_STRIDE == past // CMP_STRIDE and KC % SEL_BLOCK == 0
    ncp = past // CMP_STRIDE
    n_sel = past // SEL_BLOCK + 1
    nsp = -(-n_sel // LANES) * LANES
    cs = np.arange(ncp) * CMP_STRIDE
    ss = np.arange(nsp) * SEL_BLOCK
    overlap = (cs[:, None] < ss[None, :] + SEL_BLOCK) & (cs[:, None] + CMP_BLOCK > ss[None, :])
    overlap[ncp - 1] = False
    overlap[:, n_sel:] = False
    overlap = jnp.asarray(overlap, jnp.bfloat16)
    assert KC // SEL_BLOCK <= LANES
    expand = jnp.asarray(np.arange(KC)[None, :] // SEL_BLOCK == np.arange(LANES)[:, None], jnp.bfloat16)
    R = N_HEADS * T
    per_b = lambda shape: pl.BlockSpec((1,) + shape, lambda b, g, pt: (b, 0, 0))
    return pl.pallas_call(
        functools.partial(_sample_attend_body, past=past),
        grid_spec=pltpu.PrefetchScalarGridSpec(
            num_scalar_prefetch=1, grid=(B, ng),
            in_specs=[_page_spec(k, 1) for k in range(PG)] + [
                per_b((BRANCH_W, ncp)), per_b((BRANCH_W, ncp)), per_b((T, N_WIDTH)), per_b((T, KV_WIDTH)),
                per_b((T, LANES)), per_b((T, N_WIDTH)), per_b((BRANCH_W, win_t.shape[2])),
                pl.BlockSpec((LANES, KC), lambda b, g, pt: (0, 0)),
                pl.BlockSpec((ncp, nsp), lambda b, g, pt: (0, 0))],
            out_specs=per_b((T, N_WIDTH)),
            scratch_shapes=[pltpu.VMEM((R, LANES), jnp.float32), pltpu.VMEM((ng, R, LANES), jnp.bfloat16),
                            pltpu.VMEM((R, 1), jnp.float32), pltpu.VMEM((R, 1), jnp.float32),
                            pltpu.VMEM((R, LANES), jnp.float32), pltpu.VMEM((R, LANES), jnp.float32),
                            pltpu.VMEM((R, LANES), jnp.float32)]),
        out_shape=jax.ShapeDtypeStruct((B, T, N_WIDTH), jnp.float32),
        compiler_params=pltpu.CompilerParams(dimension_semantics=("parallel", "arbitrary"),
                                             vmem_limit_bytes=VMEM_LIMIT),
        name="nsa_sample",
    )(page_table, *([cache_t] * PG), kc_a, kc_b, q, kv_new, gl, gate_n, win_t, expand, overlap)


def _out_proj_body(yr_ref, gr_ref, on_ref, x_ref, w_ref, g_ref, o_ref, *, final_norm):
    bf16 = jnp.bfloat16
    a = (yr_ref[...] * _silu(gr_ref[...])).astype(bf16)
    y = x_ref[...] + _dot(a, w_ref[0:R_WIDTH, :]) + _dot(on_ref[...].astype(bf16), w_ref[R_WIDTH:MIX_WIDTH, :])
    if final_norm:
        y = y * lax.rsqrt(jnp.mean(y * y, axis=-1, keepdims=True) + RMS_EPS) * g_ref[...]
    o_ref[...] = y


def out_proj(y_r, gate_r, o_n, x, w_out, final_g):
    M, D = x.shape
    tm = min(OUT_TM, M)
    assert M % tm == 0
    g = (jnp.ones((D,), jnp.float32) if final_g is None else final_g.astype(jnp.float32)).reshape(1, D)
    rows = lambda c: pl.BlockSpec((tm, c), lambda i: (i, 0))
    return pl.pallas_call(
        functools.partial(_out_proj_body, final_norm=final_g is not None),
        grid=(M // tm,),
        in_specs=[rows(R_WIDTH), rows(R_WIDTH), rows(N_WIDTH), rows(D), _const_spec((MIX_WIDTH, D), 1),
                  _const_spec((1, D), 1)],
        out_specs=rows(D),
        out_shape=jax.ShapeDtypeStruct((M, D), jnp.float32),
        compiler_params=pltpu.CompilerParams(dimension_semantics=("parallel",), vmem_limit_bytes=VMEM_LIMIT),
        name="out_proj",
    )(y_r, gate_r, o_n, x, w_out.astype(jnp.bfloat16), g)


def prompt_mixers(x, norm_g, w_in, mu_shift, w0, w_decay_up, a0, w_aaa_up, k_k, k_a, r_k, gn_w, gn_b,
                  w_cmp_pos, w_cmp_mix):
    B, T, D = x.shape
    (r, w, k, v, kk, b, kv_cs, kv_w, gate_r, q, gate_n, gl, z_last) = prompt_in_proj(
        x, norm_g, w_in, mu_shift, w0, w_decay_up, a0, w_aaa_up, k_k, k_a)
    seqs = [_lanes_from_cm(a, T) for a in (r, w, k, v, kk, b)]
    o_n = nsa_prompt(q, kv_cs, kv_w, compress_prompt(kv_cs, w_cmp_pos, w_cmp_mix), gl, gate_n)
    seqs, o_n = lax.optimization_barrier((seqs, o_n))
    s0 = jnp.zeros((R_HEAD_DIM, R_HEAD_DIM, B * R_HEADS), jnp.float32)
    y, s_fin = rwkv_scan_lanes(seqs, s0, r_k, gn_w, gn_b, B)
    heads = (N_KV_HEADS, N_HEAD_DIM)
    kv_rows = kv_cs.reshape((B, 4) + heads + (T,)).transpose(0, 4, 1, 2, 3)
    n_keep = min(WINDOW, T)
    win_rows = kv_w[:, :, T - n_keep:].reshape((B, 2) + heads + (n_keep,)).transpose(0, 4, 1, 2, 3)
    outs = (kv_rows, win_rows, _state_from_lanes(s_fin, B), z_last[:, :, LANES - 1])
    return (_y_from_lanes(y, B), gate_r, o_n), outs


def sample_mixers(x, cache, win_buf, wkv0, shift0, page_table, norm_g, w_in, mu_shift, w0, w_decay_up, a0,
                  w_aaa_up, k_k, k_a, r_k, gn_w, gn_b, w_cmp_pos, w_cmp_mix):
    B, T, D = x.shape
    (r, w, k, v, kk, b, z, kv, gate_r, q, gate_n, gl) = sample_in_proj(
        x, shift0, norm_g, w_in, mu_shift, w0, w_decay_up, a0, w_aaa_up, k_k, k_a)
    seqs = [a.reshape(R_HEADS, R_HEAD_DIM, B, T).transpose(3, 1, 2, 0).reshape(T, R_HEAD_DIM, B * R_HEADS)
            for a in (r, w, k, v, kk, b)]
    y, s_fin = rwkv_scan_lanes(seqs, _state_to_lanes(wkv0), r_k, gn_w, gn_b, B)

    cache_t = cache.reshape(cache.shape[0], PAGE_SIZE, 2 * BRANCH_W).transpose(0, 2, 1)
    nb = win_buf.shape[1]
    win_t = win_buf.reshape(B, nb, BRANCH_W).transpose(0, 2, 1)
    kc_a, kc_b = sample_compress(cache_t, page_table, w_cmp_pos, w_cmp_mix)
    kv = kv.reshape(B, T, KV_WIDTH)
    o_n = nsa_sample(q.reshape(B, T, N_WIDTH), kv, kc_a, kc_b, gl.reshape(B, T, LANES),
                     gate_n.reshape(B, T, N_WIDTH), cache_t, page_table, win_t)
    kv = kv.reshape(B, T, KV_SLOTS, N_KV_HEADS, N_HEAD_DIM)
    keys_w = jnp.concatenate([win_buf.astype(kv.dtype), kv[:, :, 4:6]], axis=1)
    n_keep = min(WINDOW, nb + T)
    shift = z.reshape(R_SHIFT_WIDTH, B, T)[:, :, T - 1].T
    outs = (kv[:, :, 0:4], keys_w[:, nb + T - n_keep:], _state_from_lanes(s_fin, B), shift)
    return (_y_from_lanes(y, B), gate_r.reshape(B, T, R_WIDTH), o_n), outs


def _finish(mixed, x, w_out, final_g):
    B, T, D = x.shape
    y_r, gate_r, o_n = (a.reshape(B * T, -1) for a in mixed)
    return out_proj(y_r, gate_r, o_n, x.reshape(B * T, D), w_out, final_g).reshape(B, T, D)


def kernel(x_prompt, x_sample, cache_kv, cache_kv_win, state_wkv, state_shift, page_table,
           norm_g, w_in, mu_shift, w0, w_decay_up, a0, w_aaa_up, k_k, k_a, r_k, gn_w, gn_b,
           w_cmp_pos, w_cmp_mix, w_out, final_g):
    y_p, y_s = x_prompt, x_sample
    kvp, kvs, wnp, wns, skp, sks, shp, shs = [], [], [], [], [], [], [], []
    for layer in range(DEPTH):
        lw = (norm_g[layer], w_in[layer], mu_shift[layer], w0[layer], w_decay_up[layer], a0[layer],
              w_aaa_up[layer], k_k[layer], k_a[layer], r_k[layer], gn_w[layer], gn_b[layer],
              w_cmp_pos[layer], w_cmp_mix[layer])
        fg = final_g if layer == DEPTH - 1 else None
        mixed_p, (p_kv, p_win, p_wkv, p_shift) = prompt_mixers(y_p, *lw)
        mixed_s, (s_kv, s_win, s_wkv, s_shift) = sample_mixers(y_s, cache_kv[layer], cache_kv_win[layer],
                                                               state_wkv[layer], state_shift[layer], page_table, *lw)
        mixed_p, mixed_s = lax.optimization_barrier((mixed_p, mixed_s))
        y_p = _finish(mixed_p, y_p, w_out[layer], fg)
        y_s = _finish(mixed_s, y_s, w_out[layer], fg)
        kvp.append(p_kv); kvs.append(s_kv); wnp.append(p_win); wns.append(s_win)
        skp.append(p_wkv); sks.append(s_wkv); shp.append(p_shift); shs.append(s_shift)
    return (y_p, y_s, jnp.stack(kvp), jnp.stack(kvs), jnp.stack(wnp), jnp.stack(wns),
            jnp.stack(skp), jnp.stack(sks), jnp.stack(shp), jnp.stack(shs))
```

```python
import functools

import jax, jax.numpy as jnp
from jax import lax
import numpy as np
from jax.experimental import pallas as pl
from jax.experimental.pallas import tpu as pltpu

D_MODEL = 1024
BATCH = 16
SEQ = 2048
DEPTH = 1
DEC_BATCH = 32
DEC_SEQ = 4
PAST_LEN = 16384
PAGE_SIZE = 128

R_HEAD_DIM = 64
R_WIDTH = D_MODEL // 2
R_HEADS = R_WIDTH // R_HEAD_DIM
DECAY_LORA = 64
AAA_LORA = 64
R_SHIFT_WIDTH = 3 * R_WIDTH + DECAY_LORA + AAA_LORA
N_HEAD_DIM = 64
N_WIDTH = D_MODEL - R_WIDTH
N_HEADS = N_WIDTH // N_HEAD_DIM
N_KV_HEADS = 2
N_GROUP = N_HEADS // N_KV_HEADS
N_BRANCH = 3
KV_SLOTS = 2 * N_BRANCH
CMP_BLOCK = 32
CMP_STRIDE = 16
SEL_BLOCK = 64
SEL_TOPK = 16
WINDOW = 512
MIX_WIDTH = R_WIDTH + N_WIDTH
KV_WIDTH = KV_SLOTS * N_KV_HEADS * N_HEAD_DIM
PROJ_SPLITS = (R_SHIFT_WIDTH, R_WIDTH, N_WIDTH, N_WIDTH, KV_WIDTH, N_BRANCH * N_HEADS)
PROJ_OFFS = tuple(int(o) for o in np.cumsum((0,) + PROJ_SPLITS))
PROJ_WIDTH = sum(PROJ_SPLITS)
RMS_EPS = 1e-6
GN_EPS = 64e-5
MASK_NEG = -1e30
SEL_NEG = -(2.0 ** 100)
SEL_BIAS = 1e4
ATTN_SCALE = N_HEAD_DIM ** -0.5

LANES = 128
BRANCH_W = 2 * N_KV_HEADS * N_HEAD_DIM
NSA_TQ = 256
NSA_KC = 512
PROJ_TM = 512
PROJ_SUB = 256
OUT_TM = 512
SAMPLE_PG = 16
RWKV_TC = 32
VMEM_LIMIT = 56 * 1024 * 1024


def _split3(x):
    hi = x.astype(jnp.bfloat16)
    r1 = x - hi.astype(jnp.float32)
    mid = r1.astype(jnp.bfloat16)
    lo = (r1 - mid.astype(jnp.float32)).astype(jnp.bfloat16)
    return hi, mid, lo


def _dot(a, b):
    return jnp.dot(a, b, preferred_element_type=jnp.float32)


def _dot_nt(a, b):
    return lax.dot_general(a, b, (((1,), (1,)), ((), ())), preferred_element_type=jnp.float32)


def _dot_exact_rhs(a, b01):
    hi, mid, lo = _split3(a)
    return _dot(hi, b01) + _dot(mid, b01) + _dot(lo, b01)


def _dot_x3(a, b):
    ah = a.astype(jnp.bfloat16)
    al = (a - ah.astype(jnp.float32)).astype(jnp.bfloat16)
    bh = b.astype(jnp.bfloat16)
    bl = (b - bh.astype(jnp.float32)).astype(jnp.bfloat16)
    return _dot(ah, bh) + _dot(ah, bl) + _dot(al, bh)


def _silu(x):
    return x * jax.nn.sigmoid(x)


def _softplus(x):
    return jnp.maximum(x, 0.0) + jnp.log1p(jnp.exp(-jnp.abs(x)))


def _masked_softmax_rows(s, mask):
    s = jnp.where(mask, s, MASK_NEG)
    e = jnp.exp(s - jnp.max(s, axis=-1, keepdims=True))
    p = e / jnp.sum(e, axis=-1, keepdims=True)
    return jnp.where(mask, p, 0.0)


def _const_spec(shape, n_grid):
    zeros = (0,) * len(shape)
    if n_grid == 1:
        return pl.BlockSpec(shape, lambda i: zeros)
    return pl.BlockSpec(shape, lambda b, i: zeros)


def _rwkv_prep(zs, w0, a0, k_k, k_a, wdec_t, waaa_t):
    bf16 = jnp.bfloat16
    n = zs.shape[1]
    r = zs[0:R_WIDTH]
    k = zs[R_WIDTH:2 * R_WIDTH]
    v = zs[2 * R_WIDTH:3 * R_WIDTH]
    wd = zs[3 * R_WIDTH:3 * R_WIDTH + DECAY_LORA]
    ad = zs[3 * R_WIDTH + DECAY_LORA:R_SHIFT_WIDTH]
    w_log = -_softplus(-(w0 + _dot(wdec_t, jnp.tanh(wd).astype(bf16)))) - 0.5
    decay = jnp.exp(-jnp.exp(w_log))
    a = jax.nn.sigmoid(a0 + _dot(waaa_t, ad.astype(bf16)))
    kk = (k * k_k).reshape(R_HEADS, R_HEAD_DIM, n)
    kk = (kk * lax.rsqrt(jnp.maximum(jnp.sum(kk * kk, axis=1, keepdims=True), 1e-24))).reshape(R_WIDTH, n)
    k = k * (1.0 + (a - 1.0) * k_a)
    return r, decay, k, v, kk, kk * a


def _prompt_proj_body(x_ref, g_ref, wz_ref, wkv_ref, wn_ref, mu_ref, w0_ref, a0_ref, kk_ref, ka_ref, wdec_ref,
                      waaa_ref, r_o, w_o, k_o, v_o, kkn_o, b_o, kvcs_o, kvw_o, gr_o, q_o, gn_o, gl_o, zl_o, carry_ref):
    bf16 = jnp.bfloat16
    tm = x_ref.shape[1]
    sub = PROJ_SUB

    @pl.when(pl.program_id(1) == 0)
    def _():
        carry_ref[...] = jnp.zeros(carry_ref.shape, jnp.float32)

    first = lax.broadcasted_iota(jnp.int32, (1, sub), 1) == 0
    prev = carry_ref[:, LANES - 1:LANES]
    zs_all = []
    for h in range(tm // sub):
        rows = slice(h * sub, (h + 1) * sub)
        x = x_ref[0, rows, :]
        xn = (x * lax.rsqrt(jnp.mean(x * x, axis=-1, keepdims=True) + RMS_EPS) * g_ref[...]).astype(bf16)
        z = _dot_nt(wz_ref[...], xn)
        kv = _dot_nt(wkv_ref[...], xn)
        nrm = _dot(xn, wn_ref[...])
        kvcs_o[0, :, rows] = kv[0:2 * BRANCH_W]
        kvw_o[0, :, rows] = kv[2 * BRANCH_W:3 * BRANCH_W]
        gr_o[0, rows, :] = nrm[:, 0:R_WIDTH]
        q_o[0, rows, :] = nrm[:, R_WIDTH:R_WIDTH + N_WIDTH]
        gn_o[0, rows, :] = nrm[:, R_WIDTH + N_WIDTH:R_WIDTH + 2 * N_WIDTH]
        gl_o[0, rows, :] = nrm[:, R_WIDTH + 2 * N_WIDTH:]
        zp = jnp.where(first, prev, pltpu.roll(z, 1, axis=1))
        prev = z[:, sub - 1:sub]
        zs_all.append((rows, z, z + (zp - z) * mu_ref[...]))
    for rows, z, zs in zs_all:
        outs = _rwkv_prep(zs, w0_ref[...], a0_ref[...], kk_ref[...], ka_ref[...], wdec_ref[...], waaa_ref[...])
        for o_ref, val in zip((r_o, w_o, k_o, v_o, kkn_o, b_o), outs):
            o_ref[0, :, rows] = val
    z_last = zs_all[-1][1][:, sub - LANES:sub]
    carry_ref[...] = z_last
    zl_o[0] = z_last


def prompt_in_proj(x, norm_g, w_in, mu_shift, w0, w_decay_up, a0, w_aaa_up, k_k, k_a):
    B, T, D = x.shape
    tm = PROJ_TM
    assert T % tm == 0 and tm % PROJ_SUB == 0 and PROJ_SUB % LANES == 0
    f32 = jnp.float32
    pw = _proj_weights(norm_g, w_in, mu_shift, w0, w_decay_up, a0, w_aaa_up, k_k, k_a)
    consts = [pw["g"], pw["wz_t"], pw["wkv"].T, pw["wn"]] + pw["cols"] + pw["lora"]
    cm = lambda c: pl.BlockSpec((1, c, tm), lambda b, i: (b, 0, i))
    tok = lambda c: pl.BlockSpec((1, tm, c), lambda b, i: (b, i, 0))
    cms = lambda c: jax.ShapeDtypeStruct((B, c, T), f32)
    toks = lambda c: jax.ShapeDtypeStruct((B, T, c), f32)
    return pl.pallas_call(
        _prompt_proj_body,
        grid=(B, T // tm),
        in_specs=[tok(D)] + [_const_spec(c.shape, 2) for c in consts],
        out_specs=[cm(R_WIDTH)] * 6 + [cm(2 * BRANCH_W), cm(BRANCH_W), tok(R_WIDTH), tok(N_WIDTH), tok(N_WIDTH),
                                       tok(LANES), pl.BlockSpec((1, R_SHIFT_WIDTH, LANES), lambda b, i: (b, 0, 0))],
        out_shape=[cms(R_WIDTH)] * 6 + [cms(2 * BRANCH_W), cms(BRANCH_W), toks(R_WIDTH), toks(N_WIDTH), toks(N_WIDTH),
                                        toks(LANES), jax.ShapeDtypeStruct((B, R_SHIFT_WIDTH, LANES), f32)],
        scratch_shapes=[pltpu.VMEM((R_SHIFT_WIDTH, LANES), f32)],
        compiler_params=pltpu.CompilerParams(dimension_semantics=("parallel", "arbitrary"),
                                             vmem_limit_bytes=VMEM_LIMIT),
        name="prompt_in_proj",
    )(x, *consts)


def _proj_weights(norm_g, w_in, mu_shift, w0, w_decay_up, a0, w_aaa_up, k_k, k_a):
    f32, bf16 = jnp.float32, jnp.bfloat16
    D = w_in.shape[0]
    o = PROJ_OFFS
    wb = w_in.astype(bf16)
    gl_pad = LANES - PROJ_SPLITS[5]
    wn = jnp.concatenate([wb[:, o[1]:o[4]], wb[:, o[5]:o[6]], jnp.zeros((D, gl_pad), bf16)], axis=1)
    col = lambda p: p.astype(f32).reshape(-1, 1)
    return dict(g=norm_g.astype(f32).reshape(1, D), wz_t=wb[:, o[0]:o[1]].T, wkv=wb[:, o[4]:o[5]], wn=wn,
                cols=[col(mu_shift), col(w0), col(a0), col(k_k), col(k_a)],
                lora=[w_decay_up.astype(bf16).T, w_aaa_up.astype(bf16).T])


def _sample_proj_body(x_ref, prev_ref, g_ref, wz_ref, wkv_ref, wn_ref, mu_ref, w0_ref, a0_ref, kk_ref, ka_ref, wdec_ref,
                      waaa_ref, r_o, w_o, k_o, v_o, kkn_o, b_o, z_o, kv_o, gr_o, q_o, gn_o, gl_o, *, steps):
    bf16 = jnp.bfloat16
    m = x_ref.shape[0]
    x = x_ref[...]
    xn = (x * lax.rsqrt(jnp.mean(x * x, axis=-1, keepdims=True) + RMS_EPS) * g_ref[...]).astype(bf16)
    z = _dot_nt(wz_ref[...], xn)
    kv_o[...] = _dot(xn, wkv_ref[...])
    nrm = _dot(xn, wn_ref[...])
    gr_o[...] = nrm[:, 0:R_WIDTH]
    q_o[...] = nrm[:, R_WIDTH:R_WIDTH + N_WIDTH]
    gn_o[...] = nrm[:, R_WIDTH + N_WIDTH:R_WIDTH + 2 * N_WIDTH]
    gl_o[...] = nrm[:, R_WIDTH + 2 * N_WIDTH:]
    z_o[...] = z
    first = lax.broadcasted_iota(jnp.int32, (1, m), 1) % steps == 0
    zp = jnp.where(first, prev_ref[...], pltpu.roll(z, 1, axis=1))
    zs = z + (zp - z) * mu_ref[...]
    outs = _rwkv_prep(zs, w0_ref[...], a0_ref[...], kk_ref[...], ka_ref[...], wdec_ref[...], waaa_ref[...])
    for o_ref, val in zip((r_o, w_o, k_o, v_o, kkn_o, b_o), outs):
        o_ref[...] = val


def sample_in_proj(x, shift0, norm_g, w_in, mu_shift, w0, w_decay_up, a0, w_aaa_up, k_k, k_a):
    B, T, D = x.shape
    M = B * T
    f32 = jnp.float32
    pw = _proj_weights(norm_g, w_in, mu_shift, w0, w_decay_up, a0, w_aaa_up, k_k, k_a)
    prev = jnp.repeat(shift0.astype(f32).T, T, axis=1)
    args = [x.reshape(M, D), prev, pw["g"], pw["wz_t"], pw["wkv"], pw["wn"]] + pw["cols"] + pw["lora"]
    cm = lambda c: jax.ShapeDtypeStruct((c, M), f32)
    tok = lambda c: jax.ShapeDtypeStruct((M, c), f32)
    out_shape = [cm(R_WIDTH)] * 6 + [cm(R_SHIFT_WIDTH), tok(KV_WIDTH), tok(R_WIDTH), tok(N_WIDTH), tok(N_WIDTH),
                                     tok(LANES)]
    return pl.pallas_call(
        functools.partial(_sample_proj_body, steps=T),
        grid=(1,),
        in_specs=[_const_spec(a.shape, 1) for a in args],
        out_specs=[_const_spec(s.shape, 1) for s in out_shape],
        out_shape=out_shape,
        compiler_params=pltpu.CompilerParams(dimension_semantics=("arbitrary",), vmem_limit_bytes=VMEM_LIMIT),
        name="sample_in_proj",
    )(*args)


def _rwkv_scan_body(r_ref, w_ref, k_ref, v_ref, kk_ref, b_ref, s0_ref, rk_ref, gw_ref, gb_ref, y_ref, s_ref):
    n = s_ref.shape[0]

    @pl.when(pl.program_id(1) == 0)
    def _():
        s_ref[...] = s0_ref[...]

    def step(t, carry):
        r, w, k, kk, b = r_ref[t], w_ref[t], k_ref[t], kk_ref[t], b_ref[t]

        for i in range(n):
            s_i = s_ref[i]
            sa = jnp.sum(s_i * kk, axis=0, keepdims=True)
            s_i = s_i * w - sa * b + v_ref[t, i:i + 1, :] * k
            s_ref[i] = s_i
            y_ref[t, i:i + 1, :] = jnp.sum(s_i * r, axis=0, keepdims=True)
        y = y_ref[t]
        yc = y - jnp.mean(y, axis=0, keepdims=True)
        y = yc * lax.rsqrt(jnp.mean(yc * yc, axis=0, keepdims=True) + GN_EPS)
        y = y * gw_ref[...] + gb_ref[...]
        y_ref[t] = y + jnp.sum(r * k * rk_ref[...], axis=0, keepdims=True) * v_ref[t]
        return carry

    lax.fori_loop(0, r_ref.shape[0], step, 0)


def rwkv_scan_lanes(seqs, s0, r_k, gn_w, gn_b, n_batch):
    T, N, BH = seqs[0].shape
    assert BH % LANES == 0
    tc = min(RWKV_TC, T)
    assert T % tc == 0
    per_lane = lambda p: jnp.tile(p.astype(jnp.float32).reshape(BH // n_batch, N).T, (1, n_batch))
    seq = pl.BlockSpec((tc, N, LANES), lambda l, t: (t, 0, l))
    state = pl.BlockSpec((N, N, LANES), lambda l, t: (0, 0, l))
    param = pl.BlockSpec((N, LANES), lambda l, t: (0, l))
    return pl.pallas_call(
        _rwkv_scan_body,
        grid=(BH // LANES, T // tc),
        in_specs=[seq] * 6 + [state] + [param] * 3,
        out_specs=[seq, state],
        out_shape=[jax.ShapeDtypeStruct((T, N, BH), jnp.float32), jax.ShapeDtypeStruct((N, N, BH), jnp.float32)],
        compiler_params=pltpu.CompilerParams(dimension_semantics=("parallel", "arbitrary"),
                                             vmem_limit_bytes=VMEM_LIMIT),
        name="rwkv_scan",
    )(*seqs, s0, per_lane(r_k), per_lane(gn_w), per_lane(gn_b))


def _lanes_from_cm(a, T):
    return a.reshape(-1, R_HEAD_DIM, T).transpose(2, 1, 0)


def _y_from_lanes(y, B):
    T, N, _ = y.shape
    return y.reshape(T, N, B, R_HEADS).transpose(2, 0, 3, 1).reshape(B, T, R_WIDTH)


def _state_to_lanes(s):
    B, H, N, _ = s.shape
    return s.astype(jnp.float32).transpose(2, 3, 0, 1).reshape(N, N, B * H)


def _state_from_lanes(s, B):
    N = s.shape[0]
    return s.reshape(N, N, B, R_HEADS).transpose(2, 3, 0, 1)


def _cmp_weights(w_pos, w_mix, n_rows):
    wp = jnp.broadcast_to(w_pos.astype(jnp.float32).reshape(2, 1, CMP_BLOCK, N_HEAD_DIM).transpose(2, 0, 1, 3),
                          (CMP_BLOCK, 2, N_KV_HEADS, N_HEAD_DIM)).reshape(CMP_BLOCK, BRANCH_W)
    wa = jnp.tile(wp[:CMP_STRIDE], (n_rows // CMP_STRIDE, 1)).T
    wb = jnp.tile(wp[CMP_STRIDE:], (n_rows // CMP_STRIDE, 1)).T
    wmix = jnp.zeros((BRANCH_W, BRANCH_W), jnp.float32)
    for e in range(2):
        for kh in range(N_KV_HEADS):
            o = (e * N_KV_HEADS + kh) * N_HEAD_DIM
            wmix = wmix.at[o:o + N_HEAD_DIM, o:o + N_HEAD_DIM].set(w_mix[e].astype(jnp.float32))
    return wa, wb, wmix.T


def _compress_prompt_body(kv_ref, wa_ref, wb_ref, pool_ref, wmix_ref, o_ref):
    x = kv_ref[0]
    xw = jnp.concatenate([x * wa_ref[...], x * wb_ref[...]], axis=1)
    o_ref[0] = _dot_x3(wmix_ref[...], _dot_exact_rhs(xw, pool_ref[...]))


def compress_prompt(kv_cs, w_pos, w_mix):
    B, _, T = kv_cs.shape
    assert CMP_BLOCK == 2 * CMP_STRIDE and T % CMP_STRIDE == 0
    ncp = T // CMP_STRIDE
    wa, wb, wmix_t = _cmp_weights(w_pos, w_mix, T)
    chunk = np.arange(T) // CMP_STRIDE
    pool = np.concatenate([chunk[:, None] == np.arange(ncp)[None, :],
                           chunk[:, None] == np.arange(ncp)[None, :] + 1], axis=0)
    pool = jnp.asarray(pool, jnp.bfloat16)
    return pl.pallas_call(
        _compress_prompt_body,
        grid=(B,),
        in_specs=[pl.BlockSpec((1, BRANCH_W, T), lambda b: (b, 0, 0))] + [_const_spec(c.shape, 1)
                                                                          for c in (wa, wb, pool, wmix_t)],
        out_specs=pl.BlockSpec((1, BRANCH_W, ncp), lambda b: (b, 0, 0)),
        out_shape=jax.ShapeDtypeStruct((B, BRANCH_W, ncp), jnp.float32),
        compiler_params=pltpu.CompilerParams(dimension_semantics=("parallel",), vmem_limit_bytes=VMEM_LIMIT),
        name="compress_prompt",
    )(kv_cs, wa, wb, pool, wmix_t)


def _nsa_prompt_body(q_ref, ksel_ref, kwin_ref, kc_ref, gl_ref, gn_ref, ovt_ref, ex_ref, o_ref):
    f32, bf16 = jnp.float32, jnp.bfloat16
    TQ, KC, G = NSA_TQ, NSA_KC, N_GROUP
    ncp = kc_ref.shape[2]
    n_sel = ovt_ref.shape[0]
    topk = min(SEL_TOPK, n_sel)
    qs = pl.program_id(1) * TQ
    qpos = qs + lax.broadcasted_iota(jnp.int32, (TQ, 1), 0)
    lo_half = lax.broadcasted_iota(jnp.int32, (TQ, LANES), 1) < N_HEAD_DIM

    def q_padded(h, half):
        tile = q_ref[0, :, (h // 2) * LANES:(h // 2 + 1) * LANES] * ATTN_SCALE
        if h % 2 != half:
            tile = pltpu.roll(tile, N_HEAD_DIM, axis=1)
        return jnp.where(lo_half if half == 0 else jnp.logical_not(lo_half), tile, 0.0)

    wstart = pl.multiple_of(jnp.maximum(qs - WINDOW, 0), TQ)
    kw = kwin_ref[0, 0:LANES, pl.ds(wstart, WINDOW + TQ)].astype(bf16)
    vw = kwin_ref[0, LANES:2 * LANES, pl.ds(wstart, WINDOW + TQ)].astype(bf16)
    kposw = wstart + lax.broadcasted_iota(jnp.int32, (1, WINDOW + TQ), 1)
    mask_w = (kposw <= qpos) & (kposw > qpos - WINDOW)

    kck = kc_ref[0, 0:LANES, :]
    kcv = kc_ref[0, LANES:2 * LANES, :].astype(bf16)
    kend = lax.broadcasted_iota(jnp.int32, (1, ncp), 1) * CMP_STRIDE + (CMP_BLOCK - 1)
    mask_c = kend <= qpos

    sid_t = lax.broadcasted_iota(jnp.int32, (n_sel, 1), 0)
    cur_t = (qs + lax.broadcasted_iota(jnp.int32, (1, TQ), 1)) // SEL_BLOCK
    valid_t = sid_t <= cur_t
    forced_t = (sid_t == 0) | (sid_t == cur_t) | (sid_t == cur_t - 1)
    n_chunks = qs // KC + 1

    H = N_HEADS
    qg = jnp.concatenate([q_padded(h, h // G) for h in range(H)], axis=0)
    qb = qg.astype(bf16)

    p = _masked_softmax_rows(_dot_x3(qg, kck).reshape(H, TQ, ncp), mask_c[None])
    o_c = _dot_nt(p.reshape(H * TQ, ncp).astype(bf16), kcv).reshape(H, TQ, LANES)

    ovt = ovt_ref[...]
    selb = []
    for kvh in range(N_KV_HEADS):
        hi, mid, lo = _split3(jnp.sum(p[kvh * G:(kvh + 1) * G], axis=0))
        score = _dot_nt(ovt, hi) + _dot_nt(ovt, mid) + _dot_nt(ovt, lo)
        score = jnp.where(valid_t, score + jnp.where(forced_t, SEL_BIAS, 0.0), -SEL_BIAS)
        rank = jnp.zeros((n_sel, TQ), jnp.int32)
        for s2 in range(n_sel):
            row = score[s2:s2 + 1, :]
            ahead = (row > score) | ((row == score) & (s2 < sid_t))
            rank = rank + ahead.astype(jnp.int32)
        selb_t = jnp.concatenate([jnp.where((rank < topk) & valid_t, 0.0, SEL_NEG),
                                  jnp.zeros((LANES - n_sel, TQ), f32)], axis=0)
        selb += [selb_t.T.astype(bf16)] * G
    qa = jnp.concatenate([qb, jnp.concatenate(selb, axis=0)], axis=1)

    def chunk(c, carry, causal):
        m, l, acc = carry
        k0 = pl.multiple_of(c * KC, KC)
        ka = jnp.concatenate([ksel_ref[0, 0:LANES, pl.ds(k0, KC)].astype(bf16), ex_ref[:, pl.ds(k0, KC)]],
                             axis=0)
        va = jnp.concatenate([ksel_ref[0, LANES:2 * LANES, pl.ds(k0, KC)].astype(bf16), ones_kc], axis=0)
        sc = _dot(qa, ka).reshape(H, TQ, KC)
        if causal:
            kpos = k0 + lax.broadcasted_iota(jnp.int32, (1, KC), 1)
            sc = jnp.where((kpos <= qpos)[None], sc, SEL_NEG)
        m_new = jnp.maximum(m, jnp.max(sc, axis=-1, keepdims=True))
        alpha = jnp.exp(m - m_new)
        e = jnp.exp(sc - m_new).astype(bf16)
        pv = _dot_nt(e.reshape(H * TQ, KC), va).reshape(H, TQ, 2 * LANES)
        return m_new, alpha * l + pv[:, :, LANES:], alpha * acc + pv[:, :, :LANES]

    ones_kc = jnp.ones((LANES, KC), bf16)
    carry = (jnp.full((H, TQ, 1), SEL_NEG, f32), jnp.zeros((H, TQ, LANES), f32), jnp.zeros((H, TQ, LANES), f32))
    carry = lax.fori_loop(0, n_chunks - 1, functools.partial(chunk, causal=False), carry)
    _, l, acc = chunk(n_chunks - 1, carry, causal=True)
    o_s = acc / l

    sw = jnp.where(mask_w[None], _dot(qb, kw).reshape(H, TQ, WINDOW + TQ), MASK_NEG)
    ew = jnp.exp(sw - jnp.max(sw, axis=-1, keepdims=True)).astype(bf16)
    pw = _dot_nt(ew.reshape(H * TQ, WINDOW + TQ),
                 jnp.concatenate([vw, jnp.ones((LANES, WINDOW + TQ), bf16)], axis=0)).reshape(H, TQ, 2 * LANES)
    o_w = pw[:, :, :LANES] / pw[:, :, LANES:]

    gates = jax.nn.sigmoid(gl_ref[0])
    heads = []
    for h in range(H):
        o = (gates[:, h:h + 1] * o_c[h] + gates[:, H + h:H + h + 1] * o_s[h]
             + gates[:, 2 * H + h:2 * H + h + 1] * o_w[h])
        heads.append(pltpu.roll(o, N_HEAD_DIM, axis=1) if h % 2 != h // G else o)

    for m in range(N_HEADS // 2):
        tile = jnp.where(lo_half, heads[2 * m], heads[2 * m + 1])
        o_ref[0, :, m * LANES:(m + 1) * LANES] = (
            tile * _silu(gn_ref[0, :, m * LANES:(m + 1) * LANES])).astype(o_ref.dtype)


def nsa_prompt(q, kv_cs, kv_w, kc, gl, gate_n):
    B, T, _ = q.shape
    TQ, KC = NSA_TQ, NSA_KC
    assert N_KV_HEADS == 2 and N_HEAD_DIM * 2 == LANES and T % KC == 0 and KC % TQ == 0 and T >= WINDOW + TQ
    assert SEL_BLOCK % CMP_STRIDE == 0 and WINDOW % TQ == 0 and TQ % LANES == 0 and T // SEL_BLOCK <= LANES
    ncp = T // CMP_STRIDE
    n_sel = T // SEL_BLOCK
    cs = np.arange(ncp) * CMP_STRIDE
    ss = np.arange(n_sel) * SEL_BLOCK
    overlap = (cs[:, None] < ss[None, :] + SEL_BLOCK) & (cs[:, None] + CMP_BLOCK > ss[None, :])
    overlap[ncp - 1] = False
    overlap_t = jnp.asarray(overlap.T, jnp.bfloat16)
    expand = jnp.asarray(np.arange(T)[None, :] // SEL_BLOCK == np.arange(LANES)[:, None], jnp.bfloat16)
    return pl.pallas_call(
        _nsa_prompt_body,
        grid=(B, T // TQ),
        in_specs=[pl.BlockSpec((1, TQ, N_WIDTH), lambda b, i: (b, i, 0)),
                  pl.BlockSpec((1, BRANCH_W, T), lambda b, i: (b, 1, 0)),
                  pl.BlockSpec((1, BRANCH_W, T), lambda b, i: (b, 0, 0)),
                  pl.BlockSpec((1, BRANCH_W, ncp), lambda b, i: (b, 0, 0)),
                  pl.BlockSpec((1, TQ, LANES), lambda b, i: (b, i, 0)),
                  pl.BlockSpec((1, TQ, N_WIDTH), lambda b, i: (b, i, 0)),
                  _const_spec((n_sel, ncp), 2), _const_spec((LANES, T), 2)],
        out_specs=pl.BlockSpec((1, TQ, N_WIDTH), lambda b, i: (b, i, 0)),
        out_shape=jax.ShapeDtypeStruct((B, T, N_WIDTH), jnp.bfloat16),
        compiler_params=pltpu.CompilerParams(dimension_semantics=("parallel", "arbitrary"),
                                             vmem_limit_bytes=VMEM_LIMIT),
        name="nsa_prompt",
    )(q, kv_cs, kv_w, kc, gl, gate_n, overlap_t, expand)


def _page_spec(k, branch):
    return pl.BlockSpec((1, BRANCH_W, PAGE_SIZE), lambda b, g, pt: (pt[b, g * SAMPLE_PG + k], branch, 0))


def _sample_compress_body(pt_ref, *refs):
    pages = refs[:SAMPLE_PG]
    wa_ref, wb_ref, pool_ref, wmix_ref, oa_ref, ob_ref = refs[SAMPLE_PG:]
    x = jnp.concatenate([p[0] for p in pages], axis=1)

    def pooled(w_ref):
        xw = x * w_ref[...]
        hi = xw.astype(jnp.bfloat16)
        lo = (xw - hi.astype(jnp.float32)).astype(jnp.bfloat16)
        return _dot(hi, pool_ref[...]) + _dot(lo, pool_ref[...])

    oa_ref[0] = _dot_x3(wmix_ref[...], pooled(wa_ref))
    ob_ref[0] = _dot_x3(wmix_ref[...], pooled(wb_ref))


def sample_compress(cache_t, page_table, w_pos, w_mix):
    B, n_pages = page_table.shape
    PG = SAMPLE_PG
    assert n_pages % PG == 0 and CMP_BLOCK == 2 * CMP_STRIDE and PAGE_SIZE % CMP_STRIDE == 0
    cols = PG * PAGE_SIZE // CMP_STRIDE
    assert cols % LANES == 0
    ncp = n_pages * PAGE_SIZE // CMP_STRIDE
    wa, wb, wmix_t = _cmp_weights(w_pos, w_mix, PG * PAGE_SIZE)
    pool = jnp.asarray(np.arange(PG * PAGE_SIZE)[:, None] // CMP_STRIDE == np.arange(cols)[None, :], jnp.bfloat16)
    const = lambda shape: pl.BlockSpec(shape, lambda b, g, pt: (0,) * len(shape))
    out = pl.BlockSpec((1, BRANCH_W, cols), lambda b, g, pt: (b, 0, g))
    return pl.pallas_call(
        _sample_compress_body,
        grid_spec=pltpu.PrefetchScalarGridSpec(
            num_scalar_prefetch=1, grid=(B, n_pages // PG),
            in_specs=[_page_spec(k, 0) for k in range(PG)] + [const(wa.shape), const(wb.shape), const(pool.shape),
                                                              const(wmix_t.shape)],
            out_specs=[out, out]),
        out_shape=[jax.ShapeDtypeStruct((B, BRANCH_W, ncp), jnp.float32)] * 2,
        compiler_params=pltpu.CompilerParams(dimension_semantics=("parallel", "arbitrary"),
                                             vmem_limit_bytes=VMEM_LIMIT),
        name="sample_compress",
    )(page_table, *([cache_t] * PG), wa, wb, pool, wmix_t)


def _online_chunk(m, l, acc, s, keep, v, v_channels_major=False):
    s = jnp.where(keep, s, MASK_NEG)
    m_new = jnp.maximum(m, jnp.max(s, axis=-1, keepdims=True))
    alpha = jnp.exp(m - m_new)
    e = jnp.where(keep, jnp.exp(s - m_new), 0.0)
    eb = e.astype(jnp.bfloat16)
    pv = _dot_nt(eb, v) if v_channels_major else _dot(eb, v)
    return m_new, alpha * l + jnp.sum(e, axis=-1, keepdims=True), alpha * acc + pv


def _sample_attend_body(pt_ref, *refs, past):
    f32, bf16 = jnp.float32, jnp.bfloat16
    PG, G = SAMPLE_PG, N_GROUP
    pages = refs[:PG]
    (ka_ref, kb_ref, q_ref, kvn_ref, gl_ref, gn_ref, win_ref, ex_ref, ov_ref, o_ref,
     qf_ref, selb_ref, tail_ref, m_ref, l_ref, acc_ref, oc_ref) = refs[PG:]
    T = q_ref.shape[1]
    R = N_HEADS * T
    ncp = ka_ref.shape[2]
    nsp = ov_ref.shape[1]
    KC = PG * PAGE_SIZE
    n_past = past // SEL_BLOCK
    bpc = KC // SEL_BLOCK
    g = pl.program_id(1)
    lo_half = lax.broadcasted_iota(jnp.int32, (T, LANES), 1) < N_HEAD_DIM
    row = lax.broadcasted_iota(jnp.int32, (R, 1), 0)
    qpos = past + row % T

    @pl.when(g == 0)
    def _():
        pieces = []
        for h in range(N_HEADS):
            kvh = h // G
            tile = q_ref[0, :, (h // 2) * LANES:(h // 2 + 1) * LANES] * ATTN_SCALE
            if h % 2 != kvh:
                tile = pltpu.roll(tile, N_HEAD_DIM, axis=1)
            pieces.append(jnp.where(lo_half if kvh == 0 else jnp.logical_not(lo_half), tile, 0.0))
        q32 = jnp.concatenate(pieces, axis=0)
        qf_ref[...] = q32

        kc = ka_ref[0] + pltpu.roll(kb_ref[0], ncp - 1, axis=1)
        kend = lax.broadcasted_iota(jnp.int32, (1, ncp), 1) * CMP_STRIDE + (CMP_BLOCK - 1)
        p = _masked_softmax_rows(_dot_x3(q32, kc[0:LANES]), kend <= qpos)
        oc_ref[...] = _dot_nt(p.astype(bf16), kc[LANES:2 * LANES].astype(bf16))
        imp = jnp.concatenate(
            [sum(p[(kvh * G + gg) * T:(kvh * G + gg + 1) * T] for gg in range(G)) for kvh in range(N_KV_HEADS)],
            axis=0)

        sid = lax.broadcasted_iota(jnp.int32, (1, nsp), 1)
        sid_f = sid.astype(f32)
        cur = (past + lax.broadcasted_iota(jnp.int32, (N_KV_HEADS * T, 1), 0) % T) // SEL_BLOCK
        valid = sid <= cur
        forced = (sid == 0) | (sid == cur) | (sid == cur - 1)
        score = _dot_exact_rhs(imp, ov_ref[...])
        score = jnp.where(valid, score + jnp.where(forced, SEL_BIAS, 0.0), -SEL_BIAS)

        def extract(_, carry):
            sc, chosen = carry
            best = jnp.max(sc, axis=-1, keepdims=True)
            first = jnp.min(jnp.where(sc == best, sid_f, float(nsp)), axis=-1, keepdims=True)
            hit = sid_f == first
            return jnp.where(hit, -3e38, sc), jnp.where(hit, 1.0, chosen)

        _, chosen = lax.fori_loop(0, min(SEL_TOPK, n_past + 1), extract, (score, jnp.zeros_like(score)))
        sel = jnp.concatenate([jnp.where(valid, chosen, 0.0)[(h // G) * T:(h // G + 1) * T] for h in range(N_HEADS)],
                              axis=0)
        tail_ref[...] = sel[:, n_past:n_past + 1]
        selb = jnp.where(sel > 0.5, 0.0, SEL_NEG)
        for c in range(n_past // bpc):
            selb_ref[c] = jnp.concatenate([selb[:, c * bpc:(c + 1) * bpc], jnp.zeros((R, LANES - bpc), f32)],
                                          axis=1).astype(bf16)
        m_ref[...] = jnp.full(m_ref.shape, SEL_NEG, f32)
        l_ref[...] = jnp.zeros(l_ref.shape, f32)
        acc_ref[...] = jnp.zeros(acc_ref.shape, f32)

    qb = qf_ref[...].astype(bf16)
    ka = jnp.concatenate([p[0, 0:LANES, :] for p in pages], axis=1).astype(bf16)
    ka = jnp.concatenate([ka, ex_ref[...]], axis=0)
    va = jnp.concatenate([jnp.concatenate([p[0, LANES:2 * LANES, :] for p in pages], axis=1).astype(bf16),
                          jnp.ones((LANES, KC), bf16)], axis=0)
    s = _dot(jnp.concatenate([qb, selb_ref[g]], axis=1), ka)
    m = jnp.maximum(m_ref[...], jnp.max(s, axis=-1, keepdims=True))
    alpha = jnp.exp(m_ref[...] - m)
    pv = _dot_nt(jnp.exp(s - m).astype(bf16), va)
    l = alpha * l_ref[...] + pv[:, LANES:]
    acc = alpha * acc_ref[...] + pv[:, :LANES]
    m_ref[...] = m
    l_ref[...] = l
    acc_ref[...] = acc

    @pl.when(g == pl.num_programs(1) - 1)
    def _():
        def new_rows(c0):
            return jnp.concatenate([kvn_ref[0, :, c0:c0 + LANES], jnp.zeros((LANES - T, LANES), f32)],
                                   axis=0).astype(bf16)

        npos = past + lax.broadcasted_iota(jnp.int32, (1, LANES), 1)
        is_new = npos < past + T
        keep_t = is_new & (npos <= qpos) & (tail_ref[...] > 0.5)
        _, l_s, acc_s = _online_chunk(m, l, acc, _dot_nt(qb, new_rows(BRANCH_W)), keep_t, new_rows(BRANCH_W + LANES))
        o_s = acc_s / l_s

        nb = win_ref.shape[2]
        bpos = past - nb + lax.broadcasted_iota(jnp.int32, (1, nb), 1)
        keep_b = (bpos <= qpos) & (bpos > qpos - WINDOW) & (bpos >= 0)
        mw, lw, aw = _online_chunk(jnp.full((R, 1), MASK_NEG, f32), jnp.zeros((R, 1), f32), jnp.zeros((R, LANES), f32),
                                   _dot(qb, win_ref[0, 0:LANES, :].astype(bf16)), keep_b,
                                   win_ref[0, LANES:2 * LANES, :].astype(bf16), True)
        keep_n = is_new & (npos <= qpos) & (npos > qpos - WINDOW)
        _, lw, aw = _online_chunk(mw, lw, aw, _dot_nt(qb, new_rows(2 * BRANCH_W)), keep_n,
                                  new_rows(2 * BRANCH_W + LANES))
        o_w = aw / lw

        gates = jax.nn.sigmoid(gl_ref[0])
        gcol = lambda br: jnp.concatenate([gates[:, br * N_HEADS + h:br * N_HEADS + h + 1] for h in range(N_HEADS)],
                                          axis=0)
        o = gcol(0) * oc_ref[...] + gcol(1) * o_s + gcol(2) * o_w
        for mt in range(N_HEADS // 2):
            halves = []
            for h in (2 * mt, 2 * mt + 1):
                piece = o[h * T:(h + 1) * T]
                if h % 2 != h // G:
                    piece = pltpu.roll(piece, N_HEAD_DIM, axis=1)
                halves.append(piece)
            o_ref[0, :, mt * LANES:(mt + 1) * LANES] = (jnp.where(lo_half, halves[0], halves[1])
                                                        * _silu(gn_ref[0, :, mt * LANES:(mt + 1) * LANES]))


def nsa_sample(q, kv_new, kc_a, kc_b, gl, gate_n, cache_t, page_table, win_t):
    B, T, _ = q.shape
    n_pages = page_table.shape[1]
    past = n_pages * PAGE_SIZE
    PG = SAMPLE_PG
    KC = PG * PAGE_SIZE
    ng = n_pages // PG
    assert n_pages % PG == 0 and past % SEL_BLOCK == 0 and T <= SEL_BLOCK and T <= LANES
    assert (past + T) // CMP_STRIDE == past // CMP_STRIDE and KC % SEL_BLOCK == 0
    ncp = past // CMP_STRIDE
    n_sel = past // SEL_BLOCK + 1
    nsp = -(-n_sel // LANES) * LANES
    cs = np.arange(ncp) * CMP_STRIDE
    ss = np.arange(nsp) * SEL_BLOCK
    overlap = (cs[:, None] < ss[None, :] + SEL_BLOCK) & (cs[:, None] + CMP_BLOCK > ss[None, :])
    overlap[ncp - 1] = False
    overlap[:, n_sel:] = False
    overlap = jnp.asarray(overlap, jnp.bfloat16)
    assert KC // SEL_BLOCK <= LANES
    expand = jnp.asarray(np.arange(KC)[None, :] // SEL_BLOCK == np.arange(LANES)[:, None], jnp.bfloat16)
    R = N_HEADS * T
    per_b = lambda shape: pl.BlockSpec((1,) + shape, lambda b, g, pt: (b, 0, 0))
    return pl.pallas_call(
        functools.partial(_sample_attend_body, past=past),
        grid_spec=pltpu.PrefetchScalarGridSpec(
            num_scalar_prefetch=1, grid=(B, ng),
            in_specs=[_page_spec(k, 1) for k in range(PG)] + [
                per_b((BRANCH_W, ncp)), per_b((BRANCH_W, ncp)), per_b((T, N_WIDTH)), per_b((T, KV_WIDTH)),
                per_b((T, LANES)), per_b((T, N_WIDTH)), per_b((BRANCH_W, win_t.shape[2])),
                pl.BlockSpec((LANES, KC), lambda b, g, pt: (0, 0)),
                pl.BlockSpec((ncp, nsp), lambda b, g, pt: (0, 0))],
            out_specs=per_b((T, N_WIDTH)),
            scratch_shapes=[pltpu.VMEM((R, LANES), jnp.float32), pltpu.VMEM((ng, R, LANES), jnp.bfloat16),
                            pltpu.VMEM((R, 1), jnp.float32), pltpu.VMEM((R, 1), jnp.float32),
                            pltpu.VMEM((R, LANES), jnp.float32), pltpu.VMEM((R, LANES), jnp.float32),
                            pltpu.VMEM((R, LANES), jnp.float32)]),
        out_shape=jax.ShapeDtypeStruct((B, T, N_WIDTH), jnp.float32),
        compiler_params=pltpu.CompilerParams(dimension_semantics=("parallel", "arbitrary"),
                                             vmem_limit_bytes=VMEM_LIMIT),
        name="nsa_sample",
    )(page_table, *([cache_t] * PG), kc_a, kc_b, q, kv_new, gl, gate_n, win_t, expand, overlap)


def _out_proj_body(yr_ref, gr_ref, on_ref, x_ref, w_ref, g_ref, o_ref, *, final_norm):
    bf16 = jnp.bfloat16
    a = (yr_ref[...] * _silu(gr_ref[...])).astype(bf16)
    y = x_ref[...] + _dot(a, w_ref[0:R_WIDTH, :]) + _dot(on_ref[...].astype(bf16), w_ref[R_WIDTH:MIX_WIDTH, :])
    if final_norm:
        y = y * lax.rsqrt(jnp.mean(y * y, axis=-1, keepdims=True) + RMS_EPS) * g_ref[...]
    o_ref[...] = y


def out_proj(y_r, gate_r, o_n, x, w_out, final_g):
    M, D = x.shape
    tm = min(OUT_TM, M)
    assert M % tm == 0
    g = (jnp.ones((D,), jnp.float32) if final_g is None else final_g.astype(jnp.float32)).reshape(1, D)
    rows = lambda c: pl.BlockSpec((tm, c), lambda i: (i, 0))
    return pl.pallas_call(
        functools.partial(_out_proj_body, final_norm=final_g is not None),
        grid=(M // tm,),
        in_specs=[rows(R_WIDTH), rows(R_WIDTH), rows(N_WIDTH), rows(D), _const_spec((MIX_WIDTH, D), 1),
                  _const_spec((1, D), 1)],
        out_specs=rows(D),
        out_shape=jax.ShapeDtypeStruct((M, D), jnp.float32),
        compiler_params=pltpu.CompilerParams(dimension_semantics=("parallel",), vmem_limit_bytes=VMEM_LIMIT),
        name="out_proj",
    )(y_r, gate_r, o_n, x, w_out.astype(jnp.bfloat16), g)


def prompt_mixers(x, norm_g, w_in, mu_shift, w0, w_decay_up, a0, w_aaa_up, k_k, k_a, r_k, gn_w, gn_b,
                  w_cmp_pos, w_cmp_mix):
    B, T, D = x.shape
    (r, w, k, v, kk, b, kv_cs, kv_w, gate_r, q, gate_n, gl, z_last) = prompt_in_proj(
        x, norm_g, w_in, mu_shift, w0, w_decay_up, a0, w_aaa_up, k_k, k_a)
    seqs = [_lanes_from_cm(a, T) for a in (r, w, k, v, kk, b)]
    o_n = nsa_prompt(q, kv_cs, kv_w, compress_prompt(kv_cs, w_cmp_pos, w_cmp_mix), gl, gate_n)
    seqs, o_n = lax.optimization_barrier((seqs, o_n))
    s0 = jnp.zeros((R_HEAD_DIM, R_HEAD_DIM, B * R_HEADS), jnp.float32)
    y, s_fin = rwkv_scan_lanes(seqs, s0, r_k, gn_w, gn_b, B)
    heads = (N_KV_HEADS, N_HEAD_DIM)
    kv_rows = kv_cs.reshape((B, 4) + heads + (T,)).transpose(0, 4, 1, 2, 3)
    n_keep = min(WINDOW, T)
    win_rows = kv_w[:, :, T - n_keep:].reshape((B, 2) + heads + (n_keep,)).transpose(0, 4, 1, 2, 3)
    outs = (kv_rows, win_rows, _state_from_lanes(s_fin, B), z_last[:, :, LANES - 1])
    return (_y_from_lanes(y, B), gate_r, o_n), outs


def sample_mixers(x, cache, win_buf, wkv0, shift0, page_table, norm_g, w_in, mu_shift, w0, w_decay_up, a0,
                  w_aaa_up, k_k, k_a, r_k, gn_w, gn_b, w_cmp_pos, w_cmp_mix):
    B, T, D = x.shape
    (r, w, k, v, kk, b, z, kv, gate_r, q, gate_n, gl) = sample_in_proj(
        x, shift0, norm_g, w_in, mu_shift, w0, w_decay_up, a0, w_aaa_up, k_k, k_a)
    seqs = [a.reshape(R_HEADS, R_HEAD_DIM, B, T).transpose(3, 1, 2, 0).reshape(T, R_HEAD_DIM, B * R_HEADS)
            for a in (r, w, k, v, kk, b)]
    y, s_fin = rwkv_scan_lanes(seqs, _state_to_lanes(wkv0), r_k, gn_w, gn_b, B)

    cache_t = cache.reshape(cache.shape[0], PAGE_SIZE, 2 * BRANCH_W).transpose(0, 2, 1)
    nb = win_buf.shape[1]
    win_t = win_buf.reshape(B, nb, BRANCH_W).transpose(0, 2, 1)
    kc_a, kc_b = sample_compress(cache_t, page_table, w_cmp_pos, w_cmp_mix)
    kv = kv.reshape(B, T, KV_WIDTH)
    o_n = nsa_sample(q.reshape(B, T, N_WIDTH), kv, kc_a, kc_b, gl.reshape(B, T, LANES),
                     gate_n.reshape(B, T, N_WIDTH), cache_t, page_table, win_t)
    kv = kv.reshape(B, T, KV_SLOTS, N_KV_HEADS, N_HEAD_DIM)
    keys_w = jnp.concatenate([win_buf.astype(kv.dtype), kv[:, :, 4:6]], axis=1)
    n_keep = min(WINDOW, nb + T)
    shift = z.reshape(R_SHIFT_WIDTH, B, T)[:, :, T - 1].T
    outs = (kv[:, :, 0:4], keys_w[:, nb + T - n_keep:], _state_from_lanes(s_fin, B), shift)
    return (_y_from_lanes(y, B), gate_r.reshape(B, T, R_WIDTH), o_n), outs


def _finish(mixed, x, w_out, final_g):
    B, T, D = x.shape
    y_r, gate_r, o_n = (a.reshape(B * T, -1) for a in mixed)
    return out_proj(y_r, gate_r, o_n, x.reshape(B * T, D), w_out, final_g).reshape(B, T, D)


def kernel(x_prompt, x_sample, cache_kv, cache_kv_win, state_wkv, state_shift, page_table,
           norm_g, w_in, mu_shift, w0, w_decay_up, a0, w_aaa_up, k_k, k_a, r_k, gn_w, gn_b,
           w_cmp_pos, w_cmp_mix, w_out, final_g):
    y_p, y_s = x_prompt, x_sample
    kvp, kvs, wnp, wns, skp, sks, shp, shs = [], [], [], [], [], [], [], []
    for layer in range(DEPTH):
        lw = (norm_g[layer], w_in[layer], mu_shift[layer], w0[layer], w_decay_up[layer], a0[layer],
              w_aaa_up[layer], k_k[layer], k_a[layer], r_k[layer], gn_w[layer], gn_b[layer],
              w_cmp_pos[layer], w_cmp_mix[layer])
        fg = final_g if layer == DEPTH - 1 else None
        mixed_p, (p_kv, p_win, p_wkv, p_shift) = prompt_mixers(y_p, *lw)
        mixed_s, (s_kv, s_win, s_wkv, s_shift) = sample_mixers(y_s, cache_kv[layer], cache_kv_win[layer],
                                                               state_wkv[layer], state_shift[layer], page_table, *lw)
        mixed_p, mixed_s = lax.optimization_barrier((mixed_p, mixed_s))
        y_p = _finish(mixed_p, y_p, w_out[layer], fg)
        y_s = _finish(mixed_s, y_s, w_out[layer], fg)
        kvp.append(p_kv); kvs.append(s_kv); wnp.append(p_win); wns.append(s_win)
        skp.append(p_wkv); sks.append(s_wkv); shp.append(p_shift); shs.append(s_shift)
    return (y_p, y_s, jnp.stack(kvp), jnp.stack(kvs), jnp.stack(wnp), jnp.stack(wns),
            jnp.stack(skp), jnp.stack(sks), jnp.stack(shp), jnp.stack(shs))
```

```python
import functools

import jax, jax.numpy as jnp
from jax import lax
import numpy as np
from jax.experimental import pallas as pl
from jax.experimental.pallas import tpu as pltpu

D_MODEL = 1024
BATCH = 16
SEQ = 2048
DEPTH = 1
DEC_BATCH = 32
DEC_SEQ = 4
PAST_LEN = 16384
PAGE_SIZE = 128

R_HEAD_DIM = 64
R_WIDTH = D_MODEL // 2
R_HEADS = R_WIDTH // R_HEAD_DIM
DECAY_LORA = 64
AAA_LORA = 64
R_SHIFT_WIDTH = 3 * R_WIDTH + DECAY_LORA + AAA_LORA
N_HEAD_DIM = 64
N_WIDTH = D_MODEL - R_WIDTH
N_HEADS = N_WIDTH // N_HEAD_DIM
N_KV_HEADS = 2
N_GROUP = N_HEADS // N_KV_HEADS
N_BRANCH = 3
KV_SLOTS = 2 * N_BRANCH
CMP_BLOCK = 32
CMP_STRIDE = 16
SEL_BLOCK = 64
SEL_TOPK = 16
WINDOW = 512
MIX_WIDTH = R_WIDTH + N_WIDTH
KV_WIDTH = KV_SLOTS * N_KV_HEADS * N_HEAD_DIM
PROJ_SPLITS = (R_SHIFT_WIDTH, R_WIDTH, N_WIDTH, N_WIDTH, KV_WIDTH, N_BRANCH * N_HEADS)
PROJ_OFFS = tuple(int(o) for o in np.cumsum((0,) + PROJ_SPLITS))
PROJ_WIDTH = sum(PROJ_SPLITS)
RMS_EPS = 1e-6
GN_EPS = 64e-5
MASK_NEG = -1e30
SEL_NEG = -(2.0 ** 100)
SEL_BIAS = 1e4
ATTN_SCALE = N_HEAD_DIM ** -0.5

LANES = 128
BRANCH_W = 2 * N_KV_HEADS * N_HEAD_DIM
NSA_TQ = 256
NSA_KC = 512
PROJ_TM = 512
PROJ_SUB = 256
OUT_TM = 512
SAMPLE_PG = 16
RWKV_TC = 32
VMEM_LIMIT = 56 * 1024 * 1024


def _split3(x):
    hi = x.astype(jnp.bfloat16)
    r1 = x - hi.astype(jnp.float32)
    mid = r1.astype(jnp.bfloat16)
    lo = (r1 - mid.astype(jnp.float32)).astype(jnp.bfloat16)
    return hi, mid, lo


def _dot(a, b):
    return jnp.dot(a, b, preferred_element_type=jnp.float32)


def _dot_nt(a, b):
    return lax.dot_general(a, b, (((1,), (1,)), ((), ())), preferred_element_type=jnp.float32)


def _dot_exact_rhs(a, b01):
    hi, mid, lo = _split3(a)
    return _dot(hi, b01) + _dot(mid, b01) + _dot(lo, b01)


def _dot_nt_x3(a, b):
    ah = a.astype(jnp.bfloat16)
    al = (a - ah.astype(jnp.float32)).astype(jnp.bfloat16)
    bh = b.astype(jnp.bfloat16)
    bl = (b - bh.astype(jnp.float32)).astype(jnp.bfloat16)
    return _dot_nt(ah, bh) + _dot_nt(ah, bl) + _dot_nt(al, bh)


def _dot_x3(a, b):
    ah = a.astype(jnp.bfloat16)
    al = (a - ah.astype(jnp.float32)).astype(jnp.bfloat16)
    bh = b.astype(jnp.bfloat16)
    bl = (b - bh.astype(jnp.float32)).astype(jnp.bfloat16)
    return _dot(ah, bh) + _dot(ah, bl) + _dot(al, bh)


def _silu(x):
    return x * jax.nn.sigmoid(x)


def _softplus(x):
    return jnp.maximum(x, 0.0) + jnp.log1p(jnp.exp(-jnp.abs(x)))


def _masked_softmax_rows(s, mask):
    s = jnp.where(mask, s, MASK_NEG)
    e = jnp.exp(s - jnp.max(s, axis=-1, keepdims=True))
    p = e / jnp.sum(e, axis=-1, keepdims=True)
    return jnp.where(mask, p, 0.0)


def _const_spec(shape, n_grid):
    zeros = (0,) * len(shape)
    if n_grid == 1:
        return pl.BlockSpec(shape, lambda i: zeros)
    return pl.BlockSpec(shape, lambda b, i: zeros)


def _rwkv_prep(zs, w0, a0, k_k, k_a, wdec_t, waaa_t):
    bf16 = jnp.bfloat16
    n = zs.shape[1]
    r = zs[0:R_WIDTH]
    k = zs[R_WIDTH:2 * R_WIDTH]
    v = zs[2 * R_WIDTH:3 * R_WIDTH]
    wd = zs[3 * R_WIDTH:3 * R_WIDTH + DECAY_LORA]
    ad = zs[3 * R_WIDTH + DECAY_LORA:R_SHIFT_WIDTH]
    w_log = -_softplus(-(w0 + _dot(wdec_t, jnp.tanh(wd).astype(bf16)))) - 0.5
    decay = jnp.exp(-jnp.exp(w_log))
    a = jax.nn.sigmoid(a0 + _dot(waaa_t, ad.astype(bf16)))
    kk = (k * k_k).reshape(R_HEADS, R_HEAD_DIM, n)
    kk = (kk * lax.rsqrt(jnp.maximum(jnp.sum(kk * kk, axis=1, keepdims=True), 1e-24))).reshape(R_WIDTH, n)
    k = k * (1.0 + (a - 1.0) * k_a)
    return r, decay, k, v, kk, kk * a


def _prompt_proj_body(x_ref, g_ref, wz_ref, wkv_ref, wn_ref, mu_ref, w0_ref, a0_ref, kk_ref, ka_ref, wdec_ref,
                      waaa_ref, r_o, w_o, k_o, v_o, kkn_o, b_o, kvcs_o, kvw_o, gr_o, q_o, gn_o, gl_o, zl_o, carry_ref):
    bf16 = jnp.bfloat16
    tm = x_ref.shape[1]
    sub = PROJ_SUB

    @pl.when(pl.program_id(1) == 0)
    def _():
        carry_ref[...] = jnp.zeros(carry_ref.shape, jnp.float32)

    first = lax.broadcasted_iota(jnp.int32, (1, sub), 1) == 0
    prev = carry_ref[:, LANES - 1:LANES]
    zs_all = []
    for h in range(tm // sub):
        rows = slice(h * sub, (h + 1) * sub)
        x = x_ref[0, rows, :]
        xn = (x * lax.rsqrt(jnp.mean(x * x, axis=-1, keepdims=True) + RMS_EPS) * g_ref[...]).astype(bf16)
        z = _dot_nt(wz_ref[...], xn)
        kv = _dot_nt(wkv_ref[...], xn)
        nrm = _dot(xn, wn_ref[...])
        kvcs_o[0, :, rows] = kv[0:2 * BRANCH_W]
        kvw_o[0, :, rows] = kv[2 * BRANCH_W:3 * BRANCH_W]
        gr_o[0, rows, :] = nrm[:, 0:R_WIDTH]
        q_o[0, rows, :] = nrm[:, R_WIDTH:R_WIDTH + N_WIDTH]
        gn_o[0, rows, :] = nrm[:, R_WIDTH + N_WIDTH:R_WIDTH + 2 * N_WIDTH]
        gl_o[0, rows, :] = nrm[:, R_WIDTH + 2 * N_WIDTH:]
        zp = jnp.where(first, prev, pltpu.roll(z, 1, axis=1))
        prev = z[:, sub - 1:sub]
        zs_all.append((rows, z, z + (zp - z) * mu_ref[...]))
    for rows, z, zs in zs_all:
        outs = _rwkv_prep(zs, w0_ref[...], a0_ref[...], kk_ref[...], ka_ref[...], wdec_ref[...], waaa_ref[...])
        for o_ref, val in zip((r_o, w_o, k_o, v_o, kkn_o, b_o), outs):
            o_ref[0, :, rows] = val
    z_last = zs_all[-1][1][:, sub - LANES:sub]
    carry_ref[...] = z_last
    zl_o[0] = z_last


def prompt_in_proj(x, norm_g, w_in, mu_shift, w0, w_decay_up, a0, w_aaa_up, k_k, k_a):
    B, T, D = x.shape
    tm = PROJ_TM
    assert T % tm == 0 and tm % PROJ_SUB == 0 and PROJ_SUB % LANES == 0
    f32 = jnp.float32
    pw = _proj_weights(norm_g, w_in, mu_shift, w0, w_decay_up, a0, w_aaa_up, k_k, k_a)
    consts = [pw["g"], pw["wz_t"], pw["wkv"].T, pw["wn"]] + pw["cols"] + pw["lora"]
    cm = lambda c: pl.BlockSpec((1, c, tm), lambda b, i: (b, 0, i))
    tok = lambda c: pl.BlockSpec((1, tm, c), lambda b, i: (b, i, 0))
    cms = lambda c: jax.ShapeDtypeStruct((B, c, T), f32)
    toks = lambda c: jax.ShapeDtypeStruct((B, T, c), f32)
    return pl.pallas_call(
        _prompt_proj_body,
        grid=(B, T // tm),
        in_specs=[tok(D)] + [_const_spec(c.shape, 2) for c in consts],
        out_specs=[cm(R_WIDTH)] * 6 + [cm(2 * BRANCH_W), cm(BRANCH_W), tok(R_WIDTH), tok(N_WIDTH), tok(N_WIDTH),
                                       tok(LANES), pl.BlockSpec((1, R_SHIFT_WIDTH, LANES), lambda b, i: (b, 0, 0))],
        out_shape=[cms(R_WIDTH)] * 6 + [cms(2 * BRANCH_W), cms(BRANCH_W), toks(R_WIDTH), toks(N_WIDTH), toks(N_WIDTH),
                                        toks(LANES), jax.ShapeDtypeStruct((B, R_SHIFT_WIDTH, LANES), f32)],
        scratch_shapes=[pltpu.VMEM((R_SHIFT_WIDTH, LANES), f32)],
        compiler_params=pltpu.CompilerParams(dimension_semantics=("parallel", "arbitrary"),
                                             vmem_limit_bytes=VMEM_LIMIT),
        name="prompt_in_proj",
    )(x, *consts)


def _proj_weights(norm_g, w_in, mu_shift, w0, w_decay_up, a0, w_aaa_up, k_k, k_a):
    f32, bf16 = jnp.float32, jnp.bfloat16
    D = w_in.shape[0]
    o = PROJ_OFFS
    wb = w_in.astype(bf16)
    gl_pad = LANES - PROJ_SPLITS[5]
    wn = jnp.concatenate([wb[:, o[1]:o[4]], wb[:, o[5]:o[6]], jnp.zeros((D, gl_pad), bf16)], axis=1)
    col = lambda p: p.astype(f32).reshape(-1, 1)
    return dict(g=norm_g.astype(f32).reshape(1, D), wz_t=wb[:, o[0]:o[1]].T, wkv=wb[:, o[4]:o[5]], wn=wn,
                cols=[col(mu_shift), col(w0), col(a0), col(k_k), col(k_a)],
                lora=[w_decay_up.astype(bf16).T, w_aaa_up.astype(bf16).T])


def _sample_proj_body(x_ref, prev_ref, g_ref, wz_ref, wkv_ref, wn_ref, mu_ref, w0_ref, a0_ref, kk_ref, ka_ref, wdec_ref,
                      waaa_ref, r_o, w_o, k_o, v_o, kkn_o, b_o, z_o, kv_o, gr_o, q_o, gn_o, gl_o, *, steps):
    bf16 = jnp.bfloat16
    m = x_ref.shape[0]
    x = x_ref[...]
    xn = (x * lax.rsqrt(jnp.mean(x * x, axis=-1, keepdims=True) + RMS_EPS) * g_ref[...]).astype(bf16)
    z = _dot_nt(wz_ref[...], xn)
    kv_o[...] = _dot(xn, wkv_ref[...])
    nrm = _dot(xn, wn_ref[...])
    gr_o[...] = nrm[:, 0:R_WIDTH]
    q_o[...] = nrm[:, R_WIDTH:R_WIDTH + N_WIDTH]
    gn_o[...] = nrm[:, R_WIDTH + N_WIDTH:R_WIDTH + 2 * N_WIDTH]
    gl_o[...] = nrm[:, R_WIDTH + 2 * N_WIDTH:]
    z_o[...] = z
    first = lax.broadcasted_iota(jnp.int32, (1, m), 1) % steps == 0
    zp = jnp.where(first, prev_ref[...], pltpu.roll(z, 1, axis=1))
    zs = z + (zp - z) * mu_ref[...]
    outs = _rwkv_prep(zs, w0_ref[...], a0_ref[...], kk_ref[...], ka_ref[...], wdec_ref[...], waaa_ref[...])
    for o_ref, val in zip((r_o, w_o, k_o, v_o, kkn_o, b_o), outs):
        o_ref[...] = val


def sample_in_proj(x, shift0, norm_g, w_in, mu_shift, w0, w_decay_up, a0, w_aaa_up, k_k, k_a):
    B, T, D = x.shape
    M = B * T
    f32 = jnp.float32
    pw = _proj_weights(norm_g, w_in, mu_shift, w0, w_decay_up, a0, w_aaa_up, k_k, k_a)
    prev = jnp.repeat(shift0.astype(f32).T, T, axis=1)
    args = [x.reshape(M, D), prev, pw["g"], pw["wz_t"], pw["wkv"], pw["wn"]] + pw["cols"] + pw["lora"]
    cm = lambda c: jax.ShapeDtypeStruct((c, M), f32)
    tok = lambda c: jax.ShapeDtypeStruct((M, c), f32)
    out_shape = [cm(R_WIDTH)] * 6 + [cm(R_SHIFT_WIDTH), tok(KV_WIDTH), tok(R_WIDTH), tok(N_WIDTH), tok(N_WIDTH),
                                     tok(LANES)]
    return pl.pallas_call(
        functools.partial(_sample_proj_body, steps=T),
        grid=(1,),
        in_specs=[_const_spec(a.shape, 1) for a in args],
        out_specs=[_const_spec(s.shape, 1) for s in out_shape],
        out_shape=out_shape,
        compiler_params=pltpu.CompilerParams(dimension_semantics=("arbitrary",), vmem_limit_bytes=VMEM_LIMIT),
        name="sample_in_proj",
    )(*args)


def _rwkv_scan_body(r_ref, w_ref, k_ref, v_ref, kk_ref, b_ref, s0_ref, rk_ref, gw_ref, gb_ref, y_ref, s_ref):
    n = s_ref.shape[0]

    @pl.when(pl.program_id(1) == 0)
    def _():
        s_ref[...] = s0_ref[...]

    def step(t, carry):
        r, w, k, kk, b = r_ref[t], w_ref[t], k_ref[t], kk_ref[t], b_ref[t]

        for i in range(n):
            s_i = s_ref[i]
            sa = jnp.sum(s_i * kk, axis=0, keepdims=True)
            s_i = s_i * w - sa * b + v_ref[t, i:i + 1, :] * k
            s_ref[i] = s_i
            y_ref[t, i:i + 1, :] = jnp.sum(s_i * r, axis=0, keepdims=True)
        y = y_ref[t]
        yc = y - jnp.mean(y, axis=0, keepdims=True)
        y = yc * lax.rsqrt(jnp.mean(yc * yc, axis=0, keepdims=True) + GN_EPS)
        y = y * gw_ref[...] + gb_ref[...]
        y_ref[t] = y + jnp.sum(r * k * rk_ref[...], axis=0, keepdims=True) * v_ref[t]
        return carry

    lax.fori_loop(0, r_ref.shape[0], step, 0)


def rwkv_scan_lanes(seqs, s0, r_k, gn_w, gn_b, n_batch):
    T, N, BH = seqs[0].shape
    assert BH % LANES == 0
    tc = min(RWKV_TC, T)
    assert T % tc == 0
    per_lane = lambda p: jnp.tile(p.astype(jnp.float32).reshape(BH // n_batch, N).T, (1, n_batch))
    seq = pl.BlockSpec((tc, N, LANES), lambda l, t: (t, 0, l))
    state = pl.BlockSpec((N, N, LANES), lambda l, t: (0, 0, l))
    param = pl.BlockSpec((N, LANES), lambda l, t: (0, l))
    return pl.pallas_call(
        _rwkv_scan_body,
        grid=(BH // LANES, T // tc),
        in_specs=[seq] * 6 + [state] + [param] * 3,
        out_specs=[seq, state],
        out_shape=[jax.ShapeDtypeStruct((T, N, BH), jnp.float32), jax.ShapeDtypeStruct((N, N, BH), jnp.float32)],
        compiler_params=pltpu.CompilerParams(dimension_semantics=("parallel", "arbitrary"),
                                             vmem_limit_bytes=VMEM_LIMIT),
        name="rwkv_scan",
    )(*seqs, s0, per_lane(r_k), per_lane(gn_w), per_lane(gn_b))


def _lanes_from_cm(a, T):
    return a.reshape(-1, R_HEAD_DIM, T).transpose(2, 1, 0)


def _y_from_lanes(y, B):
    T, N, _ = y.shape
    return y.reshape(T, N, B, R_HEADS).transpose(2, 0, 3, 1).reshape(B, T, R_WIDTH)


def _state_to_lanes(s):
    B, H, N, _ = s.shape
    return s.astype(jnp.float32).transpose(2, 3, 0, 1).reshape(N, N, B * H)


def _state_from_lanes(s, B):
    N = s.shape[0]
    return s.reshape(N, N, B, R_HEADS).transpose(2, 3, 0, 1)


def _cmp_weights(w_pos, w_mix, n_rows):
    wp = jnp.broadcast_to(w_pos.astype(jnp.float32).reshape(2, 1, CMP_BLOCK, N_HEAD_DIM).transpose(2, 0, 1, 3),
                          (CMP_BLOCK, 2, N_KV_HEADS, N_HEAD_DIM)).reshape(CMP_BLOCK, BRANCH_W)
    wa = jnp.tile(wp[:CMP_STRIDE], (n_rows // CMP_STRIDE, 1)).T
    wb = jnp.tile(wp[CMP_STRIDE:], (n_rows // CMP_STRIDE, 1)).T
    wmix = jnp.zeros((BRANCH_W, BRANCH_W), jnp.float32)
    for e in range(2):
        for kh in range(N_KV_HEADS):
            o = (e * N_KV_HEADS + kh) * N_HEAD_DIM
            wmix = wmix.at[o:o + N_HEAD_DIM, o:o + N_HEAD_DIM].set(w_mix[e].astype(jnp.float32))
    return wa, wb, wmix.T


def _compress_prompt_body(kv_ref, wa_ref, wb_ref, pool_ref, wmix_ref, o_ref):
    x = kv_ref[0]
    xw = jnp.concatenate([x * wa_ref[...], x * wb_ref[...]], axis=1)
    o_ref[0] = _dot_x3(wmix_ref[...], _dot_exact_rhs(xw, pool_ref[...]))


def compress_prompt(kv_cs, w_pos, w_mix):
    B, _, T = kv_cs.shape
    assert CMP_BLOCK == 2 * CMP_STRIDE and T % CMP_STRIDE == 0
    ncp = T // CMP_STRIDE
    wa, wb, wmix_t = _cmp_weights(w_pos, w_mix, T)
    chunk = np.arange(T) // CMP_STRIDE
    pool = np.concatenate([chunk[:, None] == np.arange(ncp)[None, :],
                           chunk[:, None] == np.arange(ncp)[None, :] + 1], axis=0)
    pool = jnp.asarray(pool, jnp.bfloat16)
    return pl.pallas_call(
        _compress_prompt_body,
        grid=(B,),
        in_specs=[pl.BlockSpec((1, BRANCH_W, T), lambda b: (b, 0, 0))] + [_const_spec(c.shape, 1)
                                                                          for c in (wa, wb, pool, wmix_t)],
        out_specs=pl.BlockSpec((1, BRANCH_W, ncp), lambda b: (b, 0, 0)),
        out_shape=jax.ShapeDtypeStruct((B, BRANCH_W, ncp), jnp.float32),
        compiler_params=pltpu.CompilerParams(dimension_semantics=("parallel",), vmem_limit_bytes=VMEM_LIMIT),
        name="compress_prompt",
    )(kv_cs, wa, wb, pool, wmix_t)


def _nsa_prompt_body(q_ref, ksel_ref, kwin_ref, kc_ref, gl_ref, gn_ref, ovt_ref, ex_ref, o_ref):
    f32, bf16 = jnp.float32, jnp.bfloat16
    TQ, KC, G = NSA_TQ, NSA_KC, N_GROUP
    ncp = kc_ref.shape[2]
    n_sel = ovt_ref.shape[0]
    topk = min(SEL_TOPK, n_sel)
    qs = pl.program_id(1) * TQ
    qpos = qs + lax.broadcasted_iota(jnp.int32, (TQ, 1), 0)
    lo_half = lax.broadcasted_iota(jnp.int32, (TQ, LANES), 1) < N_HEAD_DIM

    def q_padded(h, half):
        tile = q_ref[0, :, (h // 2) * LANES:(h // 2 + 1) * LANES] * ATTN_SCALE
        if h % 2 != half:
            tile = pltpu.roll(tile, N_HEAD_DIM, axis=1)
        return jnp.where(lo_half if half == 0 else jnp.logical_not(lo_half), tile, 0.0)

    wstart = pl.multiple_of(jnp.maximum(qs - WINDOW, 0), TQ)
    kw = kwin_ref[0, 0:LANES, pl.ds(wstart, WINDOW + TQ)].astype(bf16)
    vw = kwin_ref[0, LANES:2 * LANES, pl.ds(wstart, WINDOW + TQ)].astype(bf16)
    kposw = wstart + lax.broadcasted_iota(jnp.int32, (1, WINDOW + TQ), 1)
    mask_w = (kposw <= qpos) & (kposw > qpos - WINDOW)

    kck = kc_ref[0, 0:LANES, :]
    kcv = kc_ref[0, LANES:2 * LANES, :].astype(bf16)
    kend = lax.broadcasted_iota(jnp.int32, (1, ncp), 1) * CMP_STRIDE + (CMP_BLOCK - 1)
    mask_c = kend <= qpos

    sid_t = lax.broadcasted_iota(jnp.int32, (n_sel, 1), 0)
    cur_t = (qs + lax.broadcasted_iota(jnp.int32, (1, TQ), 1)) // SEL_BLOCK
    valid_t = sid_t <= cur_t
    forced_t = (sid_t == 0) | (sid_t == cur_t) | (sid_t == cur_t - 1)
    n_chunks = qs // KC + 1

    H = N_HEADS
    qg = jnp.concatenate([q_padded(h, h // G) for h in range(H)], axis=0)
    qb = qg.astype(bf16)

    p = _masked_softmax_rows(_dot_x3(qg, kck).reshape(H, TQ, ncp), mask_c[None])
    o_c = _dot_nt(p.reshape(H * TQ, ncp).astype(bf16), kcv).reshape(H, TQ, LANES)

    ovt = ovt_ref[...]
    selb = []
    for kvh in range(N_KV_HEADS):
        hi, mid, lo = _split3(jnp.sum(p[kvh * G:(kvh + 1) * G], axis=0))
        score = _dot_nt(ovt, hi) + _dot_nt(ovt, mid) + _dot_nt(ovt, lo)
        score = jnp.where(valid_t, score + jnp.where(forced_t, SEL_BIAS, 0.0), -SEL_BIAS)
        rank = jnp.zeros((n_sel, TQ), jnp.int32)
        for s2 in range(n_sel):
            row = score[s2:s2 + 1, :]
            ahead = (row > score) | ((row == score) & (s2 < sid_t))
            rank = rank + ahead.astype(jnp.int32)
        selb_t = jnp.concatenate([jnp.where((rank < topk) & valid_t, 0.0, SEL_NEG),
                                  jnp.zeros((LANES - n_sel, TQ), f32)], axis=0)
        selb += [selb_t.T.astype(bf16)] * G
    qa = jnp.concatenate([qb, jnp.concatenate(selb, axis=0)], axis=1)

    def chunk(c, carry, causal):
        m, l, acc = carry
        k0 = pl.multiple_of(c * KC, KC)
        ka = jnp.concatenate([ksel_ref[0, 0:LANES, pl.ds(k0, KC)].astype(bf16), ex_ref[:, pl.ds(k0, KC)]],
                             axis=0)
        va = jnp.concatenate([ksel_ref[0, LANES:2 * LANES, pl.ds(k0, KC)].astype(bf16), ones_kc], axis=0)
        sc = _dot(qa, ka).reshape(H, TQ, KC)
        if causal:
            kpos = k0 + lax.broadcasted_iota(jnp.int32, (1, KC), 1)
            sc = jnp.where((kpos <= qpos)[None], sc, SEL_NEG)
        m_new = jnp.maximum(m, jnp.max(sc, axis=-1, keepdims=True))
        alpha = jnp.exp(m - m_new)
        e = jnp.exp(sc - m_new).astype(bf16)
        pv = _dot_nt(e.reshape(H * TQ, KC), va).reshape(H, TQ, 2 * LANES)
        return m_new, alpha * l + pv[:, :, LANES:], alpha * acc + pv[:, :, :LANES]

    ones_kc = jnp.ones((LANES, KC), bf16)
    carry = (jnp.full((H, TQ, 1), SEL_NEG, f32), jnp.zeros((H, TQ, LANES), f32), jnp.zeros((H, TQ, LANES), f32))
    carry = lax.fori_loop(0, n_chunks - 1, functools.partial(chunk, causal=False), carry)
    _, l, acc = chunk(n_chunks - 1, carry, causal=True)
    o_s = acc / l

    sw = jnp.where(mask_w[None], _dot(qb, kw).reshape(H, TQ, WINDOW + TQ), MASK_NEG)
    ew = jnp.exp(sw - jnp.max(sw, axis=-1, keepdims=True)).astype(bf16)
    pw = _dot_nt(ew.reshape(H * TQ, WINDOW + TQ),
                 jnp.concatenate([vw, jnp.ones((LANES, WINDOW + TQ), bf16)], axis=0)).reshape(H, TQ, 2 * LANES)
    o_w = pw[:, :, :LANES] / pw[:, :, LANES:]

    gates = jax.nn.sigmoid(gl_ref[0])
    heads = []
    for h in range(H):
        o = (gates[:, h:h + 1] * o_c[h] + gates[:, H + h:H + h + 1] * o_s[h]
             + gates[:, 2 * H + h:2 * H + h + 1] * o_w[h])
        heads.append(pltpu.roll(o, N_HEAD_DIM, axis=1) if h % 2 != h // G else o)

    for m in range(N_HEADS // 2):
        tile = jnp.where(lo_half, heads[2 * m], heads[2 * m + 1])
        o_ref[0, :, m * LANES:(m + 1) * LANES] = (
            tile * _silu(gn_ref[0, :, m * LANES:(m + 1) * LANES])).astype(o_ref.dtype)


def nsa_prompt(q, kv_cs, kv_w, kc, gl, gate_n):
    B, T, _ = q.shape
    TQ, KC = NSA_TQ, NSA_KC
    assert N_KV_HEADS == 2 and N_HEAD_DIM * 2 == LANES and T % KC == 0 and KC % TQ == 0 and T >= WINDOW + TQ
    assert SEL_BLOCK % CMP_STRIDE == 0 and WINDOW % TQ == 0 and TQ % LANES == 0 and T // SEL_BLOCK <= LANES
    ncp = T // CMP_STRIDE
    n_sel = T // SEL_BLOCK
    cs = np.arange(ncp) * CMP_STRIDE
    ss = np.arange(n_sel) * SEL_BLOCK
    overlap = (cs[:, None] < ss[None, :] + SEL_BLOCK) & (cs[:, None] + CMP_BLOCK > ss[None, :])
    overlap[ncp - 1] = False
    overlap_t = jnp.asarray(overlap.T, jnp.bfloat16)
    expand = jnp.asarray(np.arange(T)[None, :] // SEL_BLOCK == np.arange(LANES)[:, None], jnp.bfloat16)
    return pl.pallas_call(
        _nsa_prompt_body,
        grid=(B, T // TQ),
        in_specs=[pl.BlockSpec((1, TQ, N_WIDTH), lambda b, i: (b, i, 0)),
                  pl.BlockSpec((1, BRANCH_W, T), lambda b, i: (b, 1, 0)),
                  pl.BlockSpec((1, BRANCH_W, T), lambda b, i: (b, 0, 0)),
                  pl.BlockSpec((1, BRANCH_W, ncp), lambda b, i: (b, 0, 0)),
                  pl.BlockSpec((1, TQ, LANES), lambda b, i: (b, i, 0)),
                  pl.BlockSpec((1, TQ, N_WIDTH), lambda b, i: (b, i, 0)),
                  _const_spec((n_sel, ncp), 2), _const_spec((LANES, T), 2)],
        out_specs=pl.BlockSpec((1, TQ, N_WIDTH), lambda b, i: (b, i, 0)),
        out_shape=jax.ShapeDtypeStruct((B, T, N_WIDTH), jnp.bfloat16),
        compiler_params=pltpu.CompilerParams(dimension_semantics=("parallel", "arbitrary"),
                                             vmem_limit_bytes=VMEM_LIMIT),
        name="nsa_prompt",
    )(q, kv_cs, kv_w, kc, gl, gate_n, overlap_t, expand)


def _page_spec(k, branch):
    return pl.BlockSpec((1, BRANCH_W, PAGE_SIZE), lambda b, g, pt: (pt[b, g * SAMPLE_PG + k], branch, 0))


def _sample_compress_body(pt_ref, *refs):
    pages = refs[:SAMPLE_PG]
    wa_ref, wb_ref, wmix_ref, oa_ref, ob_ref = refs[SAMPLE_PG:]
    cpp = PAGE_SIZE // CMP_STRIDE
    xt = [p[0].T.reshape(cpp, CMP_STRIDE, BRANCH_W) for p in pages]
    for w_ref, o_ref in ((wa_ref, oa_ref), (wb_ref, ob_ref)):
        pooled = jnp.concatenate([jnp.sum(x * w_ref[...], axis=1) for x in xt], axis=0)
        o_ref[0] = _dot_x3(pooled, wmix_ref[...])


def sample_compress(cache_t, page_table, w_pos, w_mix):
    B, n_pages = page_table.shape
    PG = SAMPLE_PG
    assert n_pages % PG == 0 and CMP_BLOCK == 2 * CMP_STRIDE and PAGE_SIZE % CMP_STRIDE == 0
    rows = PG * PAGE_SIZE // CMP_STRIDE
    ncp = n_pages * PAGE_SIZE // CMP_STRIDE
    wa, wb, wmix_t = _cmp_weights(w_pos, w_mix, CMP_STRIDE)
    wa, wb, wmix = wa.T[None], wb.T[None], wmix_t.T
    const = lambda shape: pl.BlockSpec(shape, lambda b, g, pt: (0,) * len(shape))
    out = pl.BlockSpec((1, rows, BRANCH_W), lambda b, g, pt: (b, g, 0))
    return pl.pallas_call(
        _sample_compress_body,
        grid_spec=pltpu.PrefetchScalarGridSpec(
            num_scalar_prefetch=1, grid=(B, n_pages // PG),
            in_specs=[_page_spec(k, 0) for k in range(PG)] + [const(wa.shape), const(wb.shape), const(wmix.shape)],
            out_specs=[out, out]),
        out_shape=[jax.ShapeDtypeStruct((B, ncp, BRANCH_W), jnp.float32)] * 2,
        compiler_params=pltpu.CompilerParams(dimension_semantics=("parallel", "arbitrary"),
                                             vmem_limit_bytes=VMEM_LIMIT),
        name="sample_compress",
    )(page_table, *([cache_t] * PG), wa, wb, wmix)


def _online_chunk(m, l, acc, s, keep, v, v_channels_major=False):
    s = jnp.where(keep, s, MASK_NEG)
    m_new = jnp.maximum(m, jnp.max(s, axis=-1, keepdims=True))
    alpha = jnp.exp(m - m_new)
    e = jnp.where(keep, jnp.exp(s - m_new), 0.0)
    eb = e.astype(jnp.bfloat16)
    pv = _dot_nt(eb, v) if v_channels_major else _dot(eb, v)
    return m_new, alpha * l + jnp.sum(e, axis=-1, keepdims=True), alpha * acc + pv


def _sample_attend_body(pt_ref, *refs, past):
    f32, bf16 = jnp.float32, jnp.bfloat16
    PG, G = SAMPLE_PG, N_GROUP
    pages = refs[:PG]
    (ka_ref, kb_ref, q_ref, kvn_ref, gl_ref, gn_ref, win_ref, ex_ref, ov_ref, o_ref,
     qf_ref, selb_ref, tail_ref, m_ref, l_ref, acc_ref, oc_ref) = refs[PG:]
    T = q_ref.shape[1]
    R = N_HEADS * T
    ncp = ka_ref.shape[1]
    nsp = ov_ref.shape[1]
    KC = PG * PAGE_SIZE
    n_past = past // SEL_BLOCK
    bpc = KC // SEL_BLOCK
    g = pl.program_id(1)
    lo_half = lax.broadcasted_iota(jnp.int32, (T, LANES), 1) < N_HEAD_DIM
    row = lax.broadcasted_iota(jnp.int32, (R, 1), 0)
    qpos = past + row % T

    @pl.when(g == 0)
    def _():
        pieces = []
        for h in range(N_HEADS):
            kvh = h // G
            tile = q_ref[0, :, (h // 2) * LANES:(h // 2 + 1) * LANES] * ATTN_SCALE
            if h % 2 != kvh:
                tile = pltpu.roll(tile, N_HEAD_DIM, axis=1)
            pieces.append(jnp.where(lo_half if kvh == 0 else jnp.logical_not(lo_half), tile, 0.0))
        q32 = jnp.concatenate(pieces, axis=0)
        qf_ref[...] = q32

        kc = ka_ref[0] + pltpu.roll(kb_ref[0], ncp - 1, axis=0)
        kend = lax.broadcasted_iota(jnp.int32, (1, ncp), 1) * CMP_STRIDE + (CMP_BLOCK - 1)
        p = _masked_softmax_rows(_dot_nt_x3(q32, kc[:, 0:LANES]), kend <= qpos)
        oc_ref[...] = _dot(p.astype(bf16), kc[:, LANES:2 * LANES].astype(bf16))
        imp = jnp.concatenate(
            [sum(p[(kvh * G + gg) * T:(kvh * G + gg + 1) * T] for gg in range(G)) for kvh in range(N_KV_HEADS)],
            axis=0)

        sid = lax.broadcasted_iota(jnp.int32, (1, nsp), 1)
        sid_f = sid.astype(f32)
        cur = (past + lax.broadcasted_iota(jnp.int32, (N_KV_HEADS * T, 1), 0) % T) // SEL_BLOCK
        valid = sid <= cur
        forced = (sid == 0) | (sid == cur) | (sid == cur - 1)
        score = _dot_exact_rhs(imp, ov_ref[...])
        score = jnp.where(valid, score + jnp.where(forced, SEL_BIAS, 0.0), -SEL_BIAS)

        def extract(_, carry):
            sc, chosen = carry
            best = jnp.max(sc, axis=-1, keepdims=True)
            first = jnp.min(jnp.where(sc == best, sid_f, float(nsp)), axis=-1, keepdims=True)
            hit = sid_f == first
            return jnp.where(hit, -3e38, sc), jnp.where(hit, 1.0, chosen)

        _, chosen = lax.fori_loop(0, min(SEL_TOPK, n_past + 1), extract, (score, jnp.zeros_like(score)))
        sel = jnp.concatenate([jnp.where(valid, chosen, 0.0)[(h // G) * T:(h // G + 1) * T] for h in range(N_HEADS)],
                              axis=0)
        tail_ref[...] = sel[:, n_past:n_past + 1]
        selb = jnp.where(sel > 0.5, 0.0, SEL_NEG)
        for c in range(n_past // bpc):
            selb_ref[c] = jnp.concatenate([selb[:, c * bpc:(c + 1) * bpc], jnp.zeros((R, LANES - bpc), f32)],
                                          axis=1).astype(bf16)
        m_ref[...] = jnp.full(m_ref.shape, SEL_NEG, f32)
        l_ref[...] = jnp.zeros(l_ref.shape, f32)
        acc_ref[...] = jnp.zeros(acc_ref.shape, f32)

    qb = qf_ref[...].astype(bf16)
    ka = jnp.concatenate([p[0, 0:LANES, :] for p in pages], axis=1).astype(bf16)
    ka = jnp.concatenate([ka, ex_ref[...]], axis=0)
    va = jnp.concatenate([jnp.concatenate([p[0, LANES:2 * LANES, :] for p in pages], axis=1).astype(bf16),
                          jnp.ones((LANES, KC), bf16)], axis=0)
    s = _dot(jnp.concatenate([qb, selb_ref[g]], axis=1), ka)
    m = jnp.maximum(m_ref[...], jnp.max(s, axis=-1, keepdims=True))
    alpha = jnp.exp(m_ref[...] - m)
    pv = _dot_nt(jnp.exp(s - m).astype(bf16), va)
    l = alpha * l_ref[...] + pv[:, LANES:]
    acc = alpha * acc_ref[...] + pv[:, :LANES]
    m_ref[...] = m
    l_ref[...] = l
    acc_ref[...] = acc

    @pl.when(g == pl.num_programs(1) - 1)
    def _():
        def new_rows(c0):
            return jnp.concatenate([kvn_ref[0, :, c0:c0 + LANES], jnp.zeros((LANES - T, LANES), f32)],
                                   axis=0).astype(bf16)

        npos = past + lax.broadcasted_iota(jnp.int32, (1, LANES), 1)
        is_new = npos < past + T
        keep_t = is_new & (npos <= qpos) & (tail_ref[...] > 0.5)
        _, l_s, acc_s = _online_chunk(m, l, acc, _dot_nt(qb, new_rows(BRANCH_W)), keep_t, new_rows(BRANCH_W + LANES))
        o_s = acc_s / l_s

        nb = win_ref.shape[2]
        bpos = past - nb + lax.broadcasted_iota(jnp.int32, (1, nb), 1)
        keep_b = (bpos <= qpos) & (bpos > qpos - WINDOW) & (bpos >= 0)
        mw, lw, aw = _online_chunk(jnp.full((R, 1), MASK_NEG, f32), jnp.zeros((R, 1), f32), jnp.zeros((R, LANES), f32),
                                   _dot(qb, win_ref[0, 0:LANES, :].astype(bf16)), keep_b,
                                   win_ref[0, LANES:2 * LANES, :].astype(bf16), True)
        keep_n = is_new & (npos <= qpos) & (npos > qpos - WINDOW)
        _, lw, aw = _online_chunk(mw, lw, aw, _dot_nt(qb, new_rows(2 * BRANCH_W)), keep_n,
                                  new_rows(2 * BRANCH_W + LANES))
        o_w = aw / lw

        gates = jax.nn.sigmoid(gl_ref[0])
        gcol = lambda br: jnp.concatenate([gates[:, br * N_HEADS + h:br * N_HEADS + h + 1] for h in range(N_HEADS)],
                                          axis=0)
        o = gcol(0) * oc_ref[...] + gcol(1) * o_s + gcol(2) * o_w
        for mt in range(N_HEADS // 2):
            halves = []
            for h in (2 * mt, 2 * mt + 1):
                piece = o[h * T:(h + 1) * T]
                if h % 2 != h // G:
                    piece = pltpu.roll(piece, N_HEAD_DIM, axis=1)
                halves.append(piece)
            o_ref[0, :, mt * LANES:(mt + 1) * LANES] = (jnp.where(lo_half, halves[0], halves[1])
                                                        * _silu(gn_ref[0, :, mt * LANES:(mt + 1) * LANES]))


def nsa_sample(q, kv_new, kc_a, kc_b, gl, gate_n, cache_t, page_table, win_t):
    B, T, _ = q.shape
    n_pages = page_table.shape[1]
    past = n_pages * PAGE_SIZE
    PG = SAMPLE_PG
    KC = PG * PAGE_SIZE
    ng = n_pages // PG
    assert n_pages % PG == 0 and past % SEL_BLOCK == 0 and T <= SEL_BLOCK and T <= LANES
    assert (past + T) // CMP_STRIDE == past // CMP_STRIDE and KC % SEL_BLOCK == 0
    ncp = past // CMP_STRIDE
    n_sel = past // SEL_BLOCK + 1
    nsp = -(-n_sel // LANES) * LANES
    cs = np.arange(ncp) * CMP_STRIDE
    ss = np.arange(nsp) * SEL_BLOCK
    overlap = (cs[:, None] < ss[None, :] + SEL_BLOCK) & (cs[:, None] + CMP_BLOCK > ss[None, :])
    overlap[ncp - 1] = False
    overlap[:, n_sel:] = False
    overlap = jnp.asarray(overlap, jnp.bfloat16)
    assert KC // SEL_BLOCK <= LANES
    expand = jnp.asarray(np.arange(KC)[None, :] // SEL_BLOCK == np.arange(LANES)[:, None], jnp.bfloat16)
    R = N_HEADS * T
    per_b = lambda shape: pl.BlockSpec((1,) + shape, lambda b, g, pt: (b, 0, 0))
    return pl.pallas_call(
        functools.partial(_sample_attend_body, past=past),
        grid_spec=pltpu.PrefetchScalarGridSpec(
            num_scalar_prefetch=1, grid=(B, ng),
            in_specs=[_page_spec(k, 1) for k in range(PG)] + [
                per_b((ncp, BRANCH_W)), per_b((ncp, BRANCH_W)), per_b((T, N_WIDTH)), per_b((T, KV_WIDTH)),
                per_b((T, LANES)), per_b((T, N_WIDTH)), per_b((BRANCH_W, win_t.shape[2])),
                pl.BlockSpec((LANES, KC), lambda b, g, pt: (0, 0)),
                pl.BlockSpec((ncp, nsp), lambda b, g, pt: (0, 0))],
            out_specs=per_b((T, N_WIDTH)),
            scratch_shapes=[pltpu.VMEM((R, LANES), jnp.float32), pltpu.VMEM((ng, R, LANES), jnp.bfloat16),
                            pltpu.VMEM((R, 1), jnp.float32), pltpu.VMEM((R, 1), jnp.float32),
                            pltpu.VMEM((R, LANES), jnp.float32), pltpu.VMEM((R, LANES), jnp.float32),
                            pltpu.VMEM((R, LANES), jnp.float32)]),
        out_shape=jax.ShapeDtypeStruct((B, T, N_WIDTH), jnp.float32),
        compiler_params=pltpu.CompilerParams(dimension_semantics=("parallel", "arbitrary"),
                                             vmem_limit_bytes=VMEM_LIMIT),
        name="nsa_sample",
    )(page_table, *([cache_t] * PG), kc_a, kc_b, q, kv_new, gl, gate_n, win_t, expand, overlap)


def _out_proj_body(yr_ref, gr_ref, on_ref, x_ref, w_ref, g_ref, o_ref, *, final_norm):
    bf16 = jnp.bfloat16
    a = (yr_ref[...] * _silu(gr_ref[...])).astype(bf16)
    y = x_ref[...] + _dot(a, w_ref[0:R_WIDTH, :]) + _dot(on_ref[...].astype(bf16), w_ref[R_WIDTH:MIX_WIDTH, :])
    if final_norm:
        y = y * lax.rsqrt(jnp.mean(y * y, axis=-1, keepdims=True) + RMS_EPS) * g_ref[...]
    o_ref[...] = y


def out_proj(y_r, gate_r, o_n, x, w_out, final_g):
    M, D = x.shape
    tm = min(OUT_TM, M)
    assert M % tm == 0
    g = (jnp.ones((D,), jnp.float32) if final_g is None else final_g.astype(jnp.float32)).reshape(1, D)
    rows = lambda c: pl.BlockSpec((tm, c), lambda i: (i, 0))
    return pl.pallas_call(
        functools.partial(_out_proj_body, final_norm=final_g is not None),
        grid=(M // tm,),
        in_specs=[rows(R_WIDTH), rows(R_WIDTH), rows(N_WIDTH), rows(D), _const_spec((MIX_WIDTH, D), 1),
                  _const_spec((1, D), 1)],
        out_specs=rows(D),
        out_shape=jax.ShapeDtypeStruct((M, D), jnp.float32),
        compiler_params=pltpu.CompilerParams(dimension_semantics=("parallel",), vmem_limit_bytes=VMEM_LIMIT),
        name="out_proj",
    )(y_r, gate_r, o_n, x, w_out.astype(jnp.bfloat16), g)


def prompt_mixers(x, norm_g, w_in, mu_shift, w0, w_decay_up, a0, w_aaa_up, k_k, k_a, r_k, gn_w, gn_b,
                  w_cmp_pos, w_cmp_mix):
    B, T, D = x.shape
    (r, w, k, v, kk, b, kv_cs, kv_w, gate_r, q, gate_n, gl, z_last) = prompt_in_proj(
        x, norm_g, w_in, mu_shift, w0, w_decay_up, a0, w_aaa_up, k_k, k_a)
    seqs = [_lanes_from_cm(a, T) for a in (r, w, k, v, kk, b)]
    o_n = nsa_prompt(q, kv_cs, kv_w, compress_prompt(kv_cs, w_cmp_pos, w_cmp_mix), gl, gate_n)
    seqs, o_n = lax.optimization_barrier((seqs, o_n))
    s0 = jnp.zeros((R_HEAD_DIM, R_HEAD_DIM, B * R_HEADS), jnp.float32)
    y, s_fin = rwkv_scan_lanes(seqs, s0, r_k, gn_w, gn_b, B)
    heads = (N_KV_HEADS, N_HEAD_DIM)
    kv_rows = kv_cs.reshape((B, 4) + heads + (T,)).transpose(0, 4, 1, 2, 3)
    n_keep = min(WINDOW, T)
    win_rows = kv_w[:, :, T - n_keep:].reshape((B, 2) + heads + (n_keep,)).transpose(0, 4, 1, 2, 3)
    outs = (kv_rows, win_rows, _state_from_lanes(s_fin, B), z_last[:, :, LANES - 1])
    return (_y_from_lanes(y, B), gate_r, o_n), outs


def sample_mixers(x, cache, win_buf, wkv0, shift0, page_table, norm_g, w_in, mu_shift, w0, w_decay_up, a0,
                  w_aaa_up, k_k, k_a, r_k, gn_w, gn_b, w_cmp_pos, w_cmp_mix):
    B, T, D = x.shape
    (r, w, k, v, kk, b, z, kv, gate_r, q, gate_n, gl) = sample_in_proj(
        x, shift0, norm_g, w_in, mu_shift, w0, w_decay_up, a0, w_aaa_up, k_k, k_a)
    seqs = [a.reshape(R_HEADS, R_HEAD_DIM, B, T).transpose(3, 1, 2, 0).reshape(T, R_HEAD_DIM, B * R_HEADS)
            for a in (r, w, k, v, kk, b)]
    y, s_fin = rwkv_scan_lanes(seqs, _state_to_lanes(wkv0), r_k, gn_w, gn_b, B)

    cache_t = cache.reshape(cache.shape[0], PAGE_SIZE, 2 * BRANCH_W).transpose(0, 2, 1)
    nb = win_buf.shape[1]
    win_t = win_buf.reshape(B, nb, BRANCH_W).transpose(0, 2, 1)
    kc_a, kc_b = sample_compress(cache_t, page_table, w_cmp_pos, w_cmp_mix)
    kv = kv.reshape(B, T, KV_WIDTH)
    o_n = nsa_sample(q.reshape(B, T, N_WIDTH), kv, kc_a, kc_b, gl.reshape(B, T, LANES),
                     gate_n.reshape(B, T, N_WIDTH), cache_t, page_table, win_t)
    kv = kv.reshape(B, T, KV_SLOTS, N_KV_HEADS, N_HEAD_DIM)
    keys_w = jnp.concatenate([win_buf.astype(kv.dtype), kv[:, :, 4:6]], axis=1)
    n_keep = min(WINDOW, nb + T)
    shift = z.reshape(R_SHIFT_WIDTH, B, T)[:, :, T - 1].T
    outs = (kv[:, :, 0:4], keys_w[:, nb + T - n_keep:], _state_from_lanes(s_fin, B), shift)
    return (_y_from_lanes(y, B), gate_r.reshape(B, T, R_WIDTH), o_n), outs


def _finish(mixed, x, w_out, final_g):
    B, T, D = x.shape
    y_r, gate_r, o_n = (a.reshape(B * T, -1) for a in mixed)
    return out_proj(y_r, gate_r, o_n, x.reshape(B * T, D), w_out, final_g).reshape(B, T, D)


def kernel(x_prompt, x_sample, cache_kv, cache_kv_win, state_wkv, state_shift, page_table,
           norm_g, w_in, mu_shift, w0, w_decay_up, a0, w_aaa_up, k_k, k_a, r_k, gn_w, gn_b,
           w_cmp_pos, w_cmp_mix, w_out, final_g):
    y_p, y_s = x_prompt, x_sample
    kvp, kvs, wnp, wns, skp, sks, shp, shs = [], [], [], [], [], [], [], []
    for layer in range(DEPTH):
        lw = (norm_g[layer], w_in[layer], mu_shift[layer], w0[layer], w_decay_up[layer], a0[layer],
              w_aaa_up[layer], k_k[layer], k_a[layer], r_k[layer], gn_w[layer], gn_b[layer],
              w_cmp_pos[layer], w_cmp_mix[layer])
        fg = final_g if layer == DEPTH - 1 else None
        mixed_p, (p_kv, p_win, p_wkv, p_shift) = prompt_mixers(y_p, *lw)
        mixed_s, (s_kv, s_win, s_wkv, s_shift) = sample_mixers(y_s, cache_kv[layer], cache_kv_win[layer],
                                                               state_wkv[layer], state_shift[layer], page_table, *lw)
        mixed_p, mixed_s = lax.optimization_barrier((mixed_p, mixed_s))
        y_p = _finish(mixed_p, y_p, w_out[layer], fg)
        y_s = _finish(mixed_s, y_s, w_out[layer], fg)
        kvp.append(p_kv); kvs.append(s_kv); wnp.append(p_win); wns.append(s_win)
        skp.append(p_wkv); sks.append(s_wkv); shp.append(p_shift); shs.append(s_shift)
    return (y_p, y_s, jnp.stack(kvp), jnp.stack(kvs), jnp.stack(wnp), jnp.stack(wns),
            jnp.stack(skp), jnp.stack(sks), jnp.stack(shp), jnp.stack(shs))
```

```python
import functools

import jax, jax.numpy as jnp
from jax import lax
import numpy as np
from jax.experimental import pallas as pl
from jax.experimental.pallas import tpu as pltpu

D_MODEL = 1024
BATCH = 16
SEQ = 2048
DEPTH = 1
DEC_BATCH = 32
DEC_SEQ = 4
PAST_LEN = 16384
PAGE_SIZE = 128

R_HEAD_DIM = 64
R_WIDTH = D_MODEL // 2
R_HEADS = R_WIDTH // R_HEAD_DIM
DECAY_LORA = 64
AAA_LORA = 64
R_SHIFT_WIDTH = 3 * R_WIDTH + DECAY_LORA + AAA_LORA
N_HEAD_DIM = 64
N_WIDTH = D_MODEL - R_WIDTH
N_HEADS = N_WIDTH // N_HEAD_DIM
N_KV_HEADS = 2
N_GROUP = N_HEADS // N_KV_HEADS
N_BRANCH = 3
KV_SLOTS = 2 * N_BRANCH
CMP_BLOCK = 32
CMP_STRIDE = 16
SEL_BLOCK = 64
SEL_TOPK = 16
WINDOW = 512
MIX_WIDTH = R_WIDTH + N_WIDTH
KV_WIDTH = KV_SLOTS * N_KV_HEADS * N_HEAD_DIM
PROJ_SPLITS = (R_SHIFT_WIDTH, R_WIDTH, N_WIDTH, N_WIDTH, KV_WIDTH, N_BRANCH * N_HEADS)
PROJ_OFFS = tuple(int(o) for o in np.cumsum((0,) + PROJ_SPLITS))
PROJ_WIDTH = sum(PROJ_SPLITS)
RMS_EPS = 1e-6
GN_EPS = 64e-5
MASK_NEG = -1e30
SEL_NEG = -(2.0 ** 100)
SEL_BIAS = 1e4
ATTN_SCALE = N_HEAD_DIM ** -0.5

LANES = 128
BRANCH_W = 2 * N_KV_HEADS * N_HEAD_DIM
NSA_TQ = 256
NSA_KC = 512
PROJ_TM = 512
PROJ_SUB = 256
OUT_TM = 512
SAMPLE_PG = 16
RWKV_TC = 32
RWKV_COLS = 16
VMEM_LIMIT = 56 * 1024 * 1024


def _split3(x):
    hi = x.astype(jnp.bfloat16)
    r1 = x - hi.astype(jnp.float32)
    mid = r1.astype(jnp.bfloat16)
    lo = (r1 - mid.astype(jnp.float32)).astype(jnp.bfloat16)
    return hi, mid, lo


def _dot(a, b):
    return jnp.dot(a, b, preferred_element_type=jnp.float32)


def _dot_nt(a, b):
    return lax.dot_general(a, b, (((1,), (1,)), ((), ())), preferred_element_type=jnp.float32)


def _dot_exact_rhs(a, b01):
    hi, mid, lo = _split3(a)
    return _dot(hi, b01) + _dot(mid, b01) + _dot(lo, b01)


def _dot_nt_x3(a, b):
    ah = a.astype(jnp.bfloat16)
    al = (a - ah.astype(jnp.float32)).astype(jnp.bfloat16)
    bh = b.astype(jnp.bfloat16)
    bl = (b - bh.astype(jnp.float32)).astype(jnp.bfloat16)
    return _dot_nt(ah, bh) + _dot_nt(ah, bl) + _dot_nt(al, bh)


def _dot_x3(a, b):
    ah = a.astype(jnp.bfloat16)
    al = (a - ah.astype(jnp.float32)).astype(jnp.bfloat16)
    bh = b.astype(jnp.bfloat16)
    bl = (b - bh.astype(jnp.float32)).astype(jnp.bfloat16)
    return _dot(ah, bh) + _dot(ah, bl) + _dot(al, bh)


def _silu(x):
    return x * jax.nn.sigmoid(x)


def _softplus(x):
    return jnp.maximum(x, 0.0) + jnp.log1p(jnp.exp(-jnp.abs(x)))


def _masked_softmax_rows(s, mask):
    s = jnp.where(mask, s, MASK_NEG)
    e = jnp.exp(s - jnp.max(s, axis=-1, keepdims=True))
    p = e / jnp.sum(e, axis=-1, keepdims=True)
    return jnp.where(mask, p, 0.0)


def _const_spec(shape, n_grid):
    zeros = (0,) * len(shape)
    if n_grid == 1:
        return pl.BlockSpec(shape, lambda i: zeros)
    return pl.BlockSpec(shape, lambda b, i: zeros)


def _rwkv_prep(zs, w0, a0, k_k, k_a, wdec_t, waaa_t):
    bf16 = jnp.bfloat16
    n = zs.shape[1]
    r = zs[0:R_WIDTH]
    k = zs[R_WIDTH:2 * R_WIDTH]
    v = zs[2 * R_WIDTH:3 * R_WIDTH]
    wd = zs[3 * R_WIDTH:3 * R_WIDTH + DECAY_LORA]
    ad = zs[3 * R_WIDTH + DECAY_LORA:R_SHIFT_WIDTH]
    w_log = -_softplus(-(w0 + _dot(wdec_t, jnp.tanh(wd).astype(bf16)))) - 0.5
    decay = jnp.exp(-jnp.exp(w_log))
    a = jax.nn.sigmoid(a0 + _dot(waaa_t, ad.astype(bf16)))
    kk = (k * k_k).reshape(R_HEADS, R_HEAD_DIM, n)
    kk = (kk * lax.rsqrt(jnp.maximum(jnp.sum(kk * kk, axis=1, keepdims=True), 1e-24))).reshape(R_WIDTH, n)
    k = k * (1.0 + (a - 1.0) * k_a)
    return r, decay, k, v, kk, kk * a


def _prompt_proj_body(x_ref, g_ref, wz_ref, wkv_ref, wn_ref, mu_ref, w0_ref, a0_ref, kk_ref, ka_ref, wdec_ref,
                      waaa_ref, r_o, w_o, k_o, v_o, kkn_o, b_o, kvcs_o, kvw_o, gr_o, q_o, gn_o, gl_o, zl_o, carry_ref):
    bf16 = jnp.bfloat16
    tm = x_ref.shape[1]
    sub = PROJ_SUB

    @pl.when(pl.program_id(1) == 0)
    def _():
        carry_ref[...] = jnp.zeros(carry_ref.shape, jnp.float32)

    first = lax.broadcasted_iota(jnp.int32, (1, sub), 1) == 0
    prev = carry_ref[:, LANES - 1:LANES]
    zs_all = []
    for h in range(tm // sub):
        rows = slice(h * sub, (h + 1) * sub)
        x = x_ref[0, rows, :]
        xn = (x * lax.rsqrt(jnp.mean(x * x, axis=-1, keepdims=True) + RMS_EPS) * g_ref[...]).astype(bf16)
        z = _dot_nt(wz_ref[...], xn)
        kv = _dot_nt(wkv_ref[...], xn)
        nrm = _dot(xn, wn_ref[...])
        kvcs_o[0, :, rows] = kv[0:2 * BRANCH_W]
        kvw_o[0, :, rows] = kv[2 * BRANCH_W:3 * BRANCH_W]
        gr_o[0, rows, :] = nrm[:, 0:R_WIDTH]
        q_o[0, rows, :] = nrm[:, R_WIDTH:R_WIDTH + N_WIDTH]
        gn_o[0, rows, :] = nrm[:, R_WIDTH + N_WIDTH:R_WIDTH + 2 * N_WIDTH]
        gl_o[0, rows, :] = nrm[:, R_WIDTH + 2 * N_WIDTH:]
        zp = jnp.where(first, prev, pltpu.roll(z, 1, axis=1))
        prev = z[:, sub - 1:sub]
        zs_all.append((rows, z, z + (zp - z) * mu_ref[...]))
    for rows, z, zs in zs_all:
        outs = _rwkv_prep(zs, w0_ref[...], a0_ref[...], kk_ref[...], ka_ref[...], wdec_ref[...], waaa_ref[...])
        for o_ref, val in zip((r_o, w_o, k_o, v_o, kkn_o, b_o), outs):
            o_ref[0, :, rows] = val
    z_last = zs_all[-1][1][:, sub - LANES:sub]
    carry_ref[...] = z_last
    zl_o[0] = z_last


def prompt_in_proj(x, norm_g, w_in, mu_shift, w0, w_decay_up, a0, w_aaa_up, k_k, k_a):
    B, T, D = x.shape
    tm = PROJ_TM
    assert T % tm == 0 and tm % PROJ_SUB == 0 and PROJ_SUB % LANES == 0
    f32 = jnp.float32
    pw = _proj_weights(norm_g, w_in, mu_shift, w0, w_decay_up, a0, w_aaa_up, k_k, k_a)
    consts = [pw["g"], pw["wz_t"], pw["wkv"].T, pw["wn"]] + pw["cols"] + pw["lora"]
    cm = lambda c: pl.BlockSpec((1, c, tm), lambda b, i: (b, 0, i))
    tok = lambda c: pl.BlockSpec((1, tm, c), lambda b, i: (b, i, 0))
    cms = lambda c: jax.ShapeDtypeStruct((B, c, T), f32)
    toks = lambda c: jax.ShapeDtypeStruct((B, T, c), f32)
    return pl.pallas_call(
        _prompt_proj_body,
        grid=(B, T // tm),
        in_specs=[tok(D)] + [_const_spec(c.shape, 2) for c in consts],
        out_specs=[cm(R_WIDTH)] * 6 + [cm(2 * BRANCH_W), cm(BRANCH_W), tok(R_WIDTH), tok(N_WIDTH), tok(N_WIDTH),
                                       tok(LANES), pl.BlockSpec((1, R_SHIFT_WIDTH, LANES), lambda b, i: (b, 0, 0))],
        out_shape=[cms(R_WIDTH)] * 6 + [cms(2 * BRANCH_W), cms(BRANCH_W), toks(R_WIDTH), toks(N_WIDTH), toks(N_WIDTH),
                                        toks(LANES), jax.ShapeDtypeStruct((B, R_SHIFT_WIDTH, LANES), f32)],
        scratch_shapes=[pltpu.VMEM((R_SHIFT_WIDTH, LANES), f32)],
        compiler_params=pltpu.CompilerParams(dimension_semantics=("parallel", "arbitrary"),
                                             vmem_limit_bytes=VMEM_LIMIT),
        name="prompt_in_proj",
    )(x, *consts)


def _proj_weights(norm_g, w_in, mu_shift, w0, w_decay_up, a0, w_aaa_up, k_k, k_a):
    f32, bf16 = jnp.float32, jnp.bfloat16
    D = w_in.shape[0]
    o = PROJ_OFFS
    wb = w_in.astype(bf16)
    gl_pad = LANES - PROJ_SPLITS[5]
    wn = jnp.concatenate([wb[:, o[1]:o[4]], wb[:, o[5]:o[6]], jnp.zeros((D, gl_pad), bf16)], axis=1)
    col = lambda p: p.astype(f32).reshape(-1, 1)
    return dict(g=norm_g.astype(f32).reshape(1, D), wz_t=wb[:, o[0]:o[1]].T, wkv=wb[:, o[4]:o[5]], wn=wn,
                cols=[col(mu_shift), col(w0), col(a0), col(k_k), col(k_a)],
                lora=[w_decay_up.astype(bf16).T, w_aaa_up.astype(bf16).T])


def _sample_proj_body(x_ref, prev_ref, g_ref, wz_ref, wkv_ref, wn_ref, mu_ref, w0_ref, a0_ref, kk_ref, ka_ref, wdec_ref,
                      waaa_ref, r_o, w_o, k_o, v_o, kkn_o, b_o, z_o, kv_o, gr_o, q_o, gn_o, gl_o, *, steps):
    bf16 = jnp.bfloat16
    m = x_ref.shape[0]
    x = x_ref[...]
    xn = (x * lax.rsqrt(jnp.mean(x * x, axis=-1, keepdims=True) + RMS_EPS) * g_ref[...]).astype(bf16)
    z = _dot_nt(wz_ref[...], xn)
    kv_o[...] = _dot(xn, wkv_ref[...])
    nrm = _dot(xn, wn_ref[...])
    gr_o[...] = nrm[:, 0:R_WIDTH]
    q_o[...] = nrm[:, R_WIDTH:R_WIDTH + N_WIDTH]
    gn_o[...] = nrm[:, R_WIDTH + N_WIDTH:R_WIDTH + 2 * N_WIDTH]
    gl_o[...] = nrm[:, R_WIDTH + 2 * N_WIDTH:]
    z_o[...] = z
    first = lax.broadcasted_iota(jnp.int32, (1, m), 1) % steps == 0
    zp = jnp.where(first, prev_ref[...], pltpu.roll(z, 1, axis=1))
    zs = z + (zp - z) * mu_ref[...]
    outs = _rwkv_prep(zs, w0_ref[...], a0_ref[...], kk_ref[...], ka_ref[...], wdec_ref[...], waaa_ref[...])
    for o_ref, val in zip((r_o, w_o, k_o, v_o, kkn_o, b_o), outs):
        o_ref[...] = val


def sample_in_proj(x, shift0, norm_g, w_in, mu_shift, w0, w_decay_up, a0, w_aaa_up, k_k, k_a):
    B, T, D = x.shape
    M = B * T
    f32 = jnp.float32
    pw = _proj_weights(norm_g, w_in, mu_shift, w0, w_decay_up, a0, w_aaa_up, k_k, k_a)
    prev = jnp.repeat(shift0.astype(f32).T, T, axis=1)
    args = [x.reshape(M, D), prev, pw["g"], pw["wz_t"], pw["wkv"], pw["wn"]] + pw["cols"] + pw["lora"]
    cm = lambda c: jax.ShapeDtypeStruct((c, M), f32)
    tok = lambda c: jax.ShapeDtypeStruct((M, c), f32)
    out_shape = [cm(R_WIDTH)] * 6 + [cm(R_SHIFT_WIDTH), tok(KV_WIDTH), tok(R_WIDTH), tok(N_WIDTH), tok(N_WIDTH),
                                     tok(LANES)]
    return pl.pallas_call(
        functools.partial(_sample_proj_body, steps=T),
        grid=(1,),
        in_specs=[_const_spec(a.shape, 1) for a in args],
        out_specs=[_const_spec(s.shape, 1) for s in out_shape],
        out_shape=out_shape,
        compiler_params=pltpu.CompilerParams(dimension_semantics=("arbitrary",), vmem_limit_bytes=VMEM_LIMIT),
        name="sample_in_proj",
    )(*args)


def _tree_sum(terms):
    while len(terms) > 1:
        terms = [terms[i] + terms[i + 1] for i in range(0, len(terms) - 1, 2)] + terms[len(terms) & ~1:]
    return terms[0]


def _rwkv_scan_body(r_ref, w_ref, k_ref, v_ref, kk_ref, b_ref, s0_ref, rk_ref, gw_ref, gb_ref, y_ref, s_ref):
    n = s_ref.shape[0]

    @pl.when(pl.program_id(1) == 0)
    def _():
        s_ref[...] = s0_ref[...]

    def step(t, carry):
        row = lambda ref, j: ref[t, pl.ds(j, 1), :]
        v = v_ref[t]
        grp = RWKV_COLS

        def p1(g, sa):
            return sa + _tree_sum([s_ref[g * grp + u] * row(kk_ref, g * grp + u) for u in range(grp)])

        sa = lax.fori_loop(0, n // grp, p1, jnp.zeros(v.shape, jnp.float32))

        def p2(g, y):
            terms = []
            for u in range(grp):
                j = g * grp + u
                s_j = s_ref[j] * row(w_ref, j) - sa * row(b_ref, j) + v * row(k_ref, j)
                s_ref[j] = s_j
                terms.append(s_j * row(r_ref, j))
            return y + _tree_sum(terms)

        y = lax.fori_loop(0, n // grp, p2, jnp.zeros(v.shape, jnp.float32))
        yc = y - jnp.mean(y, axis=0, keepdims=True)
        y = yc * lax.rsqrt(jnp.mean(yc * yc, axis=0, keepdims=True) + GN_EPS)
        y = y * gw_ref[...] + gb_ref[...]
        y_ref[t] = y + jnp.sum(r_ref[t] * k_ref[t] * rk_ref[...], axis=0, keepdims=True) * v
        return carry

    lax.fori_loop(0, r_ref.shape[0], step, 0)


def rwkv_scan_lanes(seqs, s0, r_k, gn_w, gn_b, n_batch):
    T, N, BH = seqs[0].shape
    assert BH % LANES == 0
    tc = min(RWKV_TC, T)
    assert T % tc == 0
    per_lane = lambda p: jnp.tile(p.astype(jnp.float32).reshape(BH // n_batch, N).T, (1, n_batch))
    seq = pl.BlockSpec((tc, N, LANES), lambda l, t: (t, 0, l))
    state = pl.BlockSpec((N, N, LANES), lambda l, t: (0, 0, l))
    param = pl.BlockSpec((N, LANES), lambda l, t: (0, l))
    return pl.pallas_call(
        _rwkv_scan_body,
        grid=(BH // LANES, T // tc),
        in_specs=[seq] * 6 + [state] + [param] * 3,
        out_specs=[seq, state],
        out_shape=[jax.ShapeDtypeStruct((T, N, BH), jnp.float32), jax.ShapeDtypeStruct((N, N, BH), jnp.float32)],
        compiler_params=pltpu.CompilerParams(dimension_semantics=("parallel", "arbitrary"),
                                             vmem_limit_bytes=VMEM_LIMIT),
        name="rwkv_scan",
    )(*seqs, s0, per_lane(r_k), per_lane(gn_w), per_lane(gn_b))


def _lanes_from_cm(a, T):
    return a.reshape(-1, R_HEAD_DIM, T).transpose(2, 1, 0)


def _y_from_lanes(y, B):
    T, N, _ = y.shape
    return y.reshape(T, N, B, R_HEADS).transpose(2, 0, 3, 1).reshape(B, T, R_WIDTH)


def _state_to_lanes(s):
    B, H, N, _ = s.shape
    return s.astype(jnp.float32).transpose(3, 2, 0, 1).reshape(N, N, B * H)


def _state_from_lanes(s, B):
    N = s.shape[0]
    return s.reshape(N, N, B, R_HEADS).transpose(2, 3, 1, 0)


def _cmp_weights(w_pos, w_mix, n_rows):
    wp = jnp.broadcast_to(w_pos.astype(jnp.float32).reshape(2, 1, CMP_BLOCK, N_HEAD_DIM).transpose(2, 0, 1, 3),
                          (CMP_BLOCK, 2, N_KV_HEADS, N_HEAD_DIM)).reshape(CMP_BLOCK, BRANCH_W)
    wa = jnp.tile(wp[:CMP_STRIDE], (n_rows // CMP_STRIDE, 1)).T
    wb = jnp.tile(wp[CMP_STRIDE:], (n_rows // CMP_STRIDE, 1)).T
    wmix = jnp.zeros((BRANCH_W, BRANCH_W), jnp.float32)
    for e in range(2):
        for kh in range(N_KV_HEADS):
            o = (e * N_KV_HEADS + kh) * N_HEAD_DIM
            wmix = wmix.at[o:o + N_HEAD_DIM, o:o + N_HEAD_DIM].set(w_mix[e].astype(jnp.float32))
    return wa, wb, wmix.T


def _compress_prompt_body(kv_ref, wa_ref, wb_ref, pool_ref, wmix_ref, o_ref):
    x = kv_ref[0]
    xw = jnp.concatenate([x * wa_ref[...], x * wb_ref[...]], axis=1)
    o_ref[0] = _dot_x3(wmix_ref[...], _dot_exact_rhs(xw, pool_ref[...]))


def compress_prompt(kv_cs, w_pos, w_mix):
    B, _, T = kv_cs.shape
    assert CMP_BLOCK == 2 * CMP_STRIDE and T % CMP_STRIDE == 0
    ncp = T // CMP_STRIDE
    wa, wb, wmix_t = _cmp_weights(w_pos, w_mix, T)
    chunk = np.arange(T) // CMP_STRIDE
    pool = np.concatenate([chunk[:, None] == np.arange(ncp)[None, :],
                           chunk[:, None] == np.arange(ncp)[None, :] + 1], axis=0)
    pool = jnp.asarray(pool, jnp.bfloat16)
    return pl.pallas_call(
        _compress_prompt_body,
        grid=(B,),
        in_specs=[pl.BlockSpec((1, BRANCH_W, T), lambda b: (b, 0, 0))] + [_const_spec(c.shape, 1)
                                                                          for c in (wa, wb, pool, wmix_t)],
        out_specs=pl.BlockSpec((1, BRANCH_W, ncp), lambda b: (b, 0, 0)),
        out_shape=jax.ShapeDtypeStruct((B, BRANCH_W, ncp), jnp.float32),
        compiler_params=pltpu.CompilerParams(dimension_semantics=("parallel",), vmem_limit_bytes=VMEM_LIMIT),
        name="compress_prompt",
    )(kv_cs, wa, wb, pool, wmix_t)


def _nsa_prompt_body(q_ref, ksel_ref, kwin_ref, kc_ref, gl_ref, gn_ref, ovt_ref, ex_ref, o_ref):
    f32, bf16 = jnp.float32, jnp.bfloat16
    TQ, KC, G = NSA_TQ, NSA_KC, N_GROUP
    ncp = kc_ref.shape[2]
    n_sel = ovt_ref.shape[0]
    topk = min(SEL_TOPK, n_sel)
    qs = pl.program_id(1) * TQ
    qpos = qs + lax.broadcasted_iota(jnp.int32, (TQ, 1), 0)
    lo_half = lax.broadcasted_iota(jnp.int32, (TQ, LANES), 1) < N_HEAD_DIM

    def q_padded(h, half):
        tile = q_ref[0, :, (h // 2) * LANES:(h // 2 + 1) * LANES] * ATTN_SCALE
        if h % 2 != half:
            tile = pltpu.roll(tile, N_HEAD_DIM, axis=1)
        return jnp.where(lo_half if half == 0 else jnp.logical_not(lo_half), tile, 0.0)

    wstart = pl.multiple_of(jnp.maximum(qs - WINDOW, 0), TQ)
    kw = kwin_ref[0, 0:LANES, pl.ds(wstart, WINDOW + TQ)].astype(bf16)
    vw = kwin_ref[0, LANES:2 * LANES, pl.ds(wstart, WINDOW + TQ)].astype(bf16)
    kposw = wstart + lax.broadcasted_iota(jnp.int32, (1, WINDOW + TQ), 1)
    mask_w = (kposw <= qpos) & (kposw > qpos - WINDOW)

    kck = kc_ref[0, 0:LANES, :]
    kcv = kc_ref[0, LANES:2 * LANES, :].astype(bf16)
    kend = lax.broadcasted_iota(jnp.int32, (1, ncp), 1) * CMP_STRIDE + (CMP_BLOCK - 1)
    mask_c = kend <= qpos

    sid_t = lax.broadcasted_iota(jnp.int32, (n_sel, 1), 0)
    cur_t = (qs + lax.broadcasted_iota(jnp.int32, (1, TQ), 1)) // SEL_BLOCK
    valid_t = sid_t <= cur_t
    forced_t = (sid_t == 0) | (sid_t == cur_t) | (sid_t == cur_t - 1)
    n_chunks = qs // KC + 1

    H = N_HEADS
    qg = jnp.concatenate([q_padded(h, h // G) for h in range(H)], axis=0)
    qb = qg.astype(bf16)

    p = _masked_softmax_rows(_dot_x3(qg, kck).reshape(H, TQ, ncp), mask_c[None])
    o_c = _dot_nt(p.reshape(H * TQ, ncp).astype(bf16), kcv).reshape(H, TQ, LANES)

    ovt = ovt_ref[...]
    selb = []
    for kvh in range(N_KV_HEADS):
        hi, mid, lo = _split3(jnp.sum(p[kvh * G:(kvh + 1) * G], axis=0))
        score = _dot_nt(ovt, hi) + _dot_nt(ovt, mid) + _dot_nt(ovt, lo)
        score = jnp.where(valid_t, score + jnp.where(forced_t, SEL_BIAS, 0.0), -SEL_BIAS)
        rank = jnp.zeros((n_sel, TQ), jnp.int32)
        for s2 in range(n_sel):
            row = score[s2:s2 + 1, :]
            ahead = (row > score) | ((row == score) & (s2 < sid_t))
            rank = rank + ahead.astype(jnp.int32)
        selb_t = jnp.concatenate([jnp.where((rank < topk) & valid_t, 0.0, SEL_NEG),
                                  jnp.zeros((LANES - n_sel, TQ), f32)], axis=0)
        selb += [selb_t.T.astype(bf16)] * G
    qa = jnp.concatenate([qb, jnp.concatenate(selb, axis=0)], axis=1)

    def chunk(c, carry, causal):
        m, l, acc = carry
        k0 = pl.multiple_of(c * KC, KC)
        ka = jnp.concatenate([ksel_ref[0, 0:LANES, pl.ds(k0, KC)].astype(bf16), ex_ref[:, pl.ds(k0, KC)]],
                             axis=0)
        va = jnp.concatenate([ksel_ref[0, LANES:2 * LANES, pl.ds(k0, KC)].astype(bf16), ones_kc], axis=0)
        sc = _dot(qa, ka).reshape(H, TQ, KC)
        if causal:
            kpos = k0 + lax.broadcasted_iota(jnp.int32, (1, KC), 1)
            sc = jnp.where((kpos <= qpos)[None], sc, SEL_NEG)
        m_new = jnp.maximum(m, jnp.max(sc, axis=-1, keepdims=True))
        alpha = jnp.exp(m - m_new)
        e = jnp.exp(sc - m_new).astype(bf16)
        pv = _dot_nt(e.reshape(H * TQ, KC), va).reshape(H, TQ, 2 * LANES)
        return m_new, alpha * l + pv[:, :, LANES:], alpha * acc + pv[:, :, :LANES]

    ones_kc = jnp.ones((LANES, KC), bf16)
    carry = (jnp.full((H, TQ, 1), SEL_NEG, f32), jnp.zeros((H, TQ, LANES), f32), jnp.zeros((H, TQ, LANES), f32))
    carry = lax.fori_loop(0, n_chunks - 1, functools.partial(chunk, causal=False), carry)
    _, l, acc = chunk(n_chunks - 1, carry, causal=True)
    o_s = acc / l

    sw = jnp.where(mask_w[None], _dot(qb, kw).reshape(H, TQ, WINDOW + TQ), MASK_NEG)
    ew = jnp.exp(sw - jnp.max(sw, axis=-1, keepdims=True)).astype(bf16)
    pw = _dot_nt(ew.reshape(H * TQ, WINDOW + TQ),
                 jnp.concatenate([vw, jnp.ones((LANES, WINDOW + TQ), bf16)], axis=0)).reshape(H, TQ, 2 * LANES)
    o_w = pw[:, :, :LANES] / pw[:, :, LANES:]

    gates = jax.nn.sigmoid(gl_ref[0])
    heads = []
    for h in range(H):
        o = (gates[:, h:h + 1] * o_c[h] + gates[:, H + h:H + h + 1] * o_s[h]
             + gates[:, 2 * H + h:2 * H + h + 1] * o_w[h])
        heads.append(pltpu.roll(o, N_HEAD_DIM, axis=1) if h % 2 != h // G else o)

    for m in range(N_HEADS // 2):
        tile = jnp.where(lo_half, heads[2 * m], heads[2 * m + 1])
        o_ref[0, :, m * LANES:(m + 1) * LANES] = (
            tile * _silu(gn_ref[0, :, m * LANES:(m + 1) * LANES])).astype(o_ref.dtype)


def nsa_prompt(q, kv_cs, kv_w, kc, gl, gate_n):
    B, T, _ = q.shape
    TQ, KC = NSA_TQ, NSA_KC
    assert N_KV_HEADS == 2 and N_HEAD_DIM * 2 == LANES and T % KC == 0 and KC % TQ == 0 and T >= WINDOW + TQ
    assert SEL_BLOCK % CMP_STRIDE == 0 and WINDOW % TQ == 0 and TQ % LANES == 0 and T // SEL_BLOCK <= LANES
    ncp = T // CMP_STRIDE
    n_sel = T // SEL_BLOCK
    cs = np.arange(ncp) * CMP_STRIDE
    ss = np.arange(n_sel) * SEL_BLOCK
    overlap = (cs[:, None] < ss[None, :] + SEL_BLOCK) & (cs[:, None] + CMP_BLOCK > ss[None, :])
    overlap[ncp - 1] = False
    overlap_t = jnp.asarray(overlap.T, jnp.bfloat16)
    expand = jnp.asarray(np.arange(T)[None, :] // SEL_BLOCK == np.arange(LANES)[:, None], jnp.bfloat16)
    return pl.pallas_call(
        _nsa_prompt_body,
        grid=(B, T // TQ),
        in_specs=[pl.BlockSpec((1, TQ, N_WIDTH), lambda b, i: (b, i, 0)),
                  pl.BlockSpec((1, BRANCH_W, T), lambda b, i: (b, 1, 0)),
                  pl.BlockSpec((1, BRANCH_W, T), lambda b, i: (b, 0, 0)),
                  pl.BlockSpec((1, BRANCH_W, ncp), lambda b, i: (b, 0, 0)),
                  pl.BlockSpec((1, TQ, LANES), lambda b, i: (b, i, 0)),
                  pl.BlockSpec((1, TQ, N_WIDTH), lambda b, i: (b, i, 0)),
                  _const_spec((n_sel, ncp), 2), _const_spec((LANES, T), 2)],
        out_specs=pl.BlockSpec((1, TQ, N_WIDTH), lambda b, i: (b, i, 0)),
        out_shape=jax.ShapeDtypeStruct((B, T, N_WIDTH), jnp.bfloat16),
        compiler_params=pltpu.CompilerParams(dimension_semantics=("parallel", "arbitrary"),
                                             vmem_limit_bytes=VMEM_LIMIT),
        name="nsa_prompt",
    )(q, kv_cs, kv_w, kc, gl, gate_n, overlap_t, expand)


def _page_spec(k, branch):
    return pl.BlockSpec((1, BRANCH_W, PAGE_SIZE), lambda b, g, pt: (pt[b, g * SAMPLE_PG + k], branch, 0))


def _sample_compress_body(pt_ref, *refs):
    pages = refs[:SAMPLE_PG]
    wa_ref, wb_ref, wmix_ref, oa_ref, ob_ref = refs[SAMPLE_PG:]
    cpp = PAGE_SIZE // CMP_STRIDE
    xt = [p[0].T.reshape(cpp, CMP_STRIDE, BRANCH_W) for p in pages]
    for w_ref, o_ref in ((wa_ref, oa_ref), (wb_ref, ob_ref)):
        pooled = jnp.concatenate([jnp.sum(x * w_ref[...], axis=1) for x in xt], axis=0)
        o_ref[0] = _dot_x3(pooled, wmix_ref[...])


def sample_compress(cache_t, page_table, w_pos, w_mix):
    B, n_pages = page_table.shape
    PG = SAMPLE_PG
    assert n_pages % PG == 0 and CMP_BLOCK == 2 * CMP_STRIDE and PAGE_SIZE % CMP_STRIDE == 0
    rows = PG * PAGE_SIZE // CMP_STRIDE
    ncp = n_pages * PAGE_SIZE // CMP_STRIDE
    wa, wb, wmix_t = _cmp_weights(w_pos, w_mix, CMP_STRIDE)
    wa, wb, wmix = wa.T[None], wb.T[None], wmix_t.T
    const = lambda shape: pl.BlockSpec(shape, lambda b, g, pt: (0,) * len(shape))
    out = pl.BlockSpec((1, rows, BRANCH_W), lambda b, g, pt: (b, g, 0))
    return pl.pallas_call(
        _sample_compress_body,
        grid_spec=pltpu.PrefetchScalarGridSpec(
            num_scalar_prefetch=1, grid=(B, n_pages // PG),
            in_specs=[_page_spec(k, 0) for k in range(PG)] + [const(wa.shape), const(wb.shape), const(wmix.shape)],
            out_specs=[out, out]),
        out_shape=[jax.ShapeDtypeStruct((B, ncp, BRANCH_W), jnp.float32)] * 2,
        compiler_params=pltpu.CompilerParams(dimension_semantics=("parallel", "arbitrary"),
                                             vmem_limit_bytes=VMEM_LIMIT),
        name="sample_compress",
    )(page_table, *([cache_t] * PG), wa, wb, wmix)


def _online_chunk(m, l, acc, s, keep, v, v_channels_major=False):
    s = jnp.where(keep, s, MASK_NEG)
    m_new = jnp.maximum(m, jnp.max(s, axis=-1, keepdims=True))
    alpha = jnp.exp(m - m_new)
    e = jnp.where(keep, jnp.exp(s - m_new), 0.0)
    eb = e.astype(jnp.bfloat16)
    pv = _dot_nt(eb, v) if v_channels_major else _dot(eb, v)
    return m_new, alpha * l + jnp.sum(e, axis=-1, keepdims=True), alpha * acc + pv


def _sample_attend_body(pt_ref, *refs, past):
    f32, bf16 = jnp.float32, jnp.bfloat16
    PG, G = SAMPLE_PG, N_GROUP
    pages = refs[:PG]
    (ka_ref, kb_ref, q_ref, kvn_ref, gl_ref, gn_ref, win_ref, ex_ref, ov_ref, o_ref,
     qf_ref, selb_ref, tail_ref, m_ref, l_ref, acc_ref, oc_ref) = refs[PG:]
    T = q_ref.shape[1]
    R = N_HEADS * T
    ncp = ka_ref.shape[1]
    nsp = ov_ref.shape[1]
    KC = PG * PAGE_SIZE
    n_past = past // SEL_BLOCK
    bpc = KC // SEL_BLOCK
    g = pl.program_id(1)
    lo_half = lax.broadcasted_iota(jnp.int32, (T, LANES), 1) < N_HEAD_DIM
    row = lax.broadcasted_iota(jnp.int32, (R, 1), 0)
    qpos = past + row % T

    @pl.when(g == 0)
    def _():
        pieces = []
        for h in range(N_HEADS):
            kvh = h // G
            tile = q_ref[0, :, (h // 2) * LANES:(h // 2 + 1) * LANES] * ATTN_SCALE
            if h % 2 != kvh:
                tile = pltpu.roll(tile, N_HEAD_DIM, axis=1)
            pieces.append(jnp.where(lo_half if kvh == 0 else jnp.logical_not(lo_half), tile, 0.0))
        q32 = jnp.concatenate(pieces, axis=0)
        qf_ref[...] = q32

        kc = ka_ref[0] + pltpu.roll(kb_ref[0], ncp - 1, axis=0)
        kend = lax.broadcasted_iota(jnp.int32, (1, ncp), 1) * CMP_STRIDE + (CMP_BLOCK - 1)
        p = _masked_softmax_rows(_dot_nt_x3(q32, kc[:, 0:LANES]), kend <= qpos)
        oc_ref[...] = _dot(p.astype(bf16), kc[:, LANES:2 * LANES].astype(bf16))
        imp = jnp.concatenate(
            [sum(p[(kvh * G + gg) * T:(kvh * G + gg + 1) * T] for gg in range(G)) for kvh in range(N_KV_HEADS)],
            axis=0)

        sid = lax.broadcasted_iota(jnp.int32, (1, nsp), 1)
        sid_f = sid.astype(f32)
        cur = (past + lax.broadcasted_iota(jnp.int32, (N_KV_HEADS * T, 1), 0) % T) // SEL_BLOCK
        valid = sid <= cur
        forced = (sid == 0) | (sid == cur) | (sid == cur - 1)
        score = _dot_exact_rhs(imp, ov_ref[...])
        score = jnp.where(valid, score + jnp.where(forced, SEL_BIAS, 0.0), -SEL_BIAS)

        def extract(_, carry):
            sc, chosen = carry
            best = jnp.max(sc, axis=-1, keepdims=True)
            first = jnp.min(jnp.where(sc == best, sid_f, float(nsp)), axis=-1, keepdims=True)
            hit = sid_f == first
            return jnp.where(hit, -3e38, sc), jnp.where(hit, 1.0, chosen)

        _, chosen = lax.fori_loop(0, min(SEL_TOPK, n_past + 1), extract, (score, jnp.zeros_like(score)))
        sel = jnp.concatenate([jnp.where(valid, chosen, 0.0)[(h // G) * T:(h // G + 1) * T] for h in range(N_HEADS)],
                              axis=0)
        tail_ref[...] = sel[:, n_past:n_past + 1]
        selb = jnp.where(sel > 0.5, 0.0, SEL_NEG)
        for c in range(n_past // bpc):
            selb_ref[c] = jnp.concatenate([selb[:, c * bpc:(c + 1) * bpc], jnp.zeros((R, LANES - bpc), f32)],
                                          axis=1).astype(bf16)
        m_ref[...] = jnp.full(m_ref.shape, SEL_NEG, f32)
        l_ref[...] = jnp.zeros(l_ref.shape, f32)
        acc_ref[...] = jnp.zeros(acc_ref.shape, f32)

    qb = qf_ref[...].astype(bf16)
    ka = jnp.concatenate([p[0, 0:LANES, :] for p in pages], axis=1).astype(bf16)
    ka = jnp.concatenate([ka, ex_ref[...]], axis=0)
    va = jnp.concatenate([jnp.concatenate([p[0, LANES:2 * LANES, :] for p in pages], axis=1).astype(bf16),
                          jnp.ones((LANES, KC), bf16)], axis=0)
    s = _dot(jnp.concatenate([qb, selb_ref[g]], axis=1), ka)
    m = jnp.maximum(m_ref[...], jnp.max(s, axis=-1, keepdims=True))
    alpha = jnp.exp(m_ref[...] - m)
    pv = _dot_nt(jnp.exp(s - m).astype(bf16), va)
    l = alpha * l_ref[...] + pv[:, LANES:]
    acc = alpha * acc_ref[...] + pv[:, :LANES]
    m_ref[...] = m
    l_ref[...] = l
    acc_ref[...] = acc

    @pl.when(g == pl.num_programs(1) - 1)
    def _():
        def new_rows(c0):
            return jnp.concatenate([kvn_ref[0, :, c0:c0 + LANES], jnp.zeros((LANES - T, LANES), f32)],
                                   axis=0).astype(bf16)

        npos = past + lax.broadcasted_iota(jnp.int32, (1, LANES), 1)
        is_new = npos < past + T
        keep_t = is_new & (npos <= qpos) & (tail_ref[...] > 0.5)
        _, l_s, acc_s = _online_chunk(m, l, acc, _dot_nt(qb, new_rows(BRANCH_W)), keep_t, new_rows(BRANCH_W + LANES))
        o_s = acc_s / l_s

        nb = win_ref.shape[2]
        bpos = past - nb + lax.broadcasted_iota(jnp.int32, (1, nb), 1)
        keep_b = (bpos <= qpos) & (bpos > qpos - WINDOW) & (bpos >= 0)
        mw, lw, aw = _online_chunk(jnp.full((R, 1), MASK_NEG, f32), jnp.zeros((R, 1), f32), jnp.zeros((R, LANES), f32),
                                   _dot(qb, win_ref[0, 0:LANES, :].astype(bf16)), keep_b,
                                   win_ref[0, LANES:2 * LANES, :].astype(bf16), True)
        keep_n = is_new & (npos <= qpos) & (npos > qpos - WINDOW)
        _, lw, aw = _online_chunk(mw, lw, aw, _dot_nt(qb, new_rows(2 * BRANCH_W)), keep_n,
                                  new_rows(2 * BRANCH_W + LANES))
        o_w = aw / lw

        gates = jax.nn.sigmoid(gl_ref[0])
        gcol = lambda br: jnp.concatenate([gates[:, br * N_HEADS + h:br * N_HEADS + h + 1] for h in range(N_HEADS)],
                                          axis=0)
        o = gcol(0) * oc_ref[...] + gcol(1) * o_s + gcol(2) * o_w
        for mt in range(N_HEADS // 2):
            halves = []
            for h in (2 * mt, 2 * mt + 1):
                piece = o[h * T:(h + 1) * T]
                if h % 2 != h // G:
                    piece = pltpu.roll(piece, N_HEAD_DIM, axis=1)
                halves.append(piece)
            o_ref[0, :, mt * LANES:(mt + 1) * LANES] = (jnp.where(lo_half, halves[0], halves[1])
                                                        * _silu(gn_ref[0, :, mt * LANES:(mt + 1) * LANES]))


def nsa_sample(q, kv_new, kc_a, kc_b, gl, gate_n, cache_t, page_table, win_t):
    B, T, _ = q.shape
    n_pages = page_table.shape[1]
    past = n_pages * PAGE_SIZE
    PG = SAMPLE_PG
    KC = PG * PAGE_SIZE
    ng = n_pages // PG
    assert n_pages % PG == 0 and past % SEL_BLOCK == 0 and T <= SEL_BLOCK and T <= LANES
    assert (past + T) // CMP_STRIDE == past // CMP_STRIDE and KC % SEL_BLOCK == 0
    ncp = past // CMP_STRIDE
    n_sel = past // SEL_BLOCK + 1
    nsp = -(-n_sel // LANES) * LANES
    cs = np.arange(ncp) * CMP_STRIDE
    ss = np.arange(nsp) * SEL_BLOCK
    overlap = (cs[:, None] < ss[None, :] + SEL_BLOCK) & (cs[:, None] + CMP_BLOCK > ss[None, :])
    overlap[ncp - 1] = False
    overlap[:, n_sel:] = False
    overlap = jnp.asarray(overlap, jnp.bfloat16)
    assert KC // SEL_BLOCK <= LANES
    expand = jnp.asarray(np.arange(KC)[None, :] // SEL_BLOCK == np.arange(LANES)[:, None], jnp.bfloat16)
    R = N_HEADS * T
    per_b = lambda shape: pl.BlockSpec((1,) + shape, lambda b, g, pt: (b, 0, 0))
    return pl.pallas_call(
        functools.partial(_sample_attend_body, past=past),
        grid_spec=pltpu.PrefetchScalarGridSpec(
            num_scalar_prefetch=1, grid=(B, ng),
            in_specs=[_page_spec(k, 1) for k in range(PG)] + [
                per_b((ncp, BRANCH_W)), per_b((ncp, BRANCH_W)), per_b((T, N_WIDTH)), per_b((T, KV_WIDTH)),
                per_b((T, LANES)), per_b((T, N_WIDTH)), per_b((BRANCH_W, win_t.shape[2])),
                pl.BlockSpec((LANES, KC), lambda b, g, pt: (0, 0)),
                pl.BlockSpec((ncp, nsp), lambda b, g, pt: (0, 0))],
            out_specs=per_b((T, N_WIDTH)),
            scratch_shapes=[pltpu.VMEM((R, LANES), jnp.float32), pltpu.VMEM((ng, R, LANES), jnp.bfloat16),
                            pltpu.VMEM((R, 1), jnp.float32), pltpu.VMEM((R, 1), jnp.float32),
                            pltpu.VMEM((R, LANES), jnp.float32), pltpu.VMEM((R, LANES), jnp.float32),
                            pltpu.VMEM((R, LANES), jnp.float32)]),
        out_shape=jax.ShapeDtypeStruct((B, T, N_WIDTH), jnp.float32),
        compiler_params=pltpu.CompilerParams(dimension_semantics=("parallel", "arbitrary"),
                                             vmem_limit_bytes=VMEM_LIMIT),
        name="nsa_sample",
    )(page_table, *([cache_t] * PG), kc_a, kc_b, q, kv_new, gl, gate_n, win_t, expand, overlap)


def _out_proj_body(yr_ref, gr_ref, on_ref, x_ref, w_ref, g_ref, o_ref, *, final_norm):
    bf16 = jnp.bfloat16
    a = (yr_ref[...] * _silu(gr_ref[...])).astype(bf16)
    y = x_ref[...] + _dot(a, w_ref[0:R_WIDTH, :]) + _dot(on_ref[...].astype(bf16), w_ref[R_WIDTH:MIX_WIDTH, :])
    if final_norm:
        y = y * lax.rsqrt(jnp.mean(y * y, axis=-1, keepdims=True) + RMS_EPS) * g_ref[...]
    o_ref[...] = y


def out_proj(y_r, gate_r, o_n, x, w_out, final_g):
    M, D = x.shape
    tm = min(OUT_TM, M)
    assert M % tm == 0
    g = (jnp.ones((D,), jnp.float32) if final_g is None else final_g.astype(jnp.float32)).reshape(1, D)
    rows = lambda c: pl.BlockSpec((tm, c), lambda i: (i, 0))
    return pl.pallas_call(
        functools.partial(_out_proj_body, final_norm=final_g is not None),
        grid=(M // tm,),
        in_specs=[rows(R_WIDTH), rows(R_WIDTH), rows(N_WIDTH), rows(D), _const_spec((MIX_WIDTH, D), 1),
                  _const_spec((1, D), 1)],
        out_specs=rows(D),
        out_shape=jax.ShapeDtypeStruct((M, D), jnp.float32),
        compiler_params=pltpu.CompilerParams(dimension_semantics=("parallel",), vmem_limit_bytes=VMEM_LIMIT),
        name="out_proj",
    )(y_r, gate_r, o_n, x, w_out.astype(jnp.bfloat16), g)


def prompt_mixers(x, norm_g, w_in, mu_shift, w0, w_decay_up, a0, w_aaa_up, k_k, k_a, r_k, gn_w, gn_b,
                  w_cmp_pos, w_cmp_mix):
    B, T, D = x.shape
    (r, w, k, v, kk, b, kv_cs, kv_w, gate_r, q, gate_n, gl, z_last) = prompt_in_proj(
        x, norm_g, w_in, mu_shift, w0, w_decay_up, a0, w_aaa_up, k_k, k_a)
    seqs = [_lanes_from_cm(a, T) for a in (r, w, k, v, kk, b)]
    o_n = nsa_prompt(q, kv_cs, kv_w, compress_prompt(kv_cs, w_cmp_pos, w_cmp_mix), gl, gate_n)
    seqs, o_n = lax.optimization_barrier((seqs, o_n))
    s0 = jnp.zeros((R_HEAD_DIM, R_HEAD_DIM, B * R_HEADS), jnp.float32)
    y, s_fin = rwkv_scan_lanes(seqs, s0, r_k, gn_w, gn_b, B)
    heads = (N_KV_HEADS, N_HEAD_DIM)
    kv_rows = kv_cs.reshape((B, 4) + heads + (T,)).transpose(0, 4, 1, 2, 3)
    n_keep = min(WINDOW, T)
    win_rows = kv_w[:, :, T - n_keep:].reshape((B, 2) + heads + (n_keep,)).transpose(0, 4, 1, 2, 3)
    outs = (kv_rows, win_rows, _state_from_lanes(s_fin, B), z_last[:, :, LANES - 1])
    return (_y_from_lanes(y, B), gate_r, o_n), outs


def sample_mixers(x, cache, win_buf, wkv0, shift0, page_table, norm_g, w_in, mu_shift, w0, w_decay_up, a0,
                  w_aaa_up, k_k, k_a, r_k, gn_w, gn_b, w_cmp_pos, w_cmp_mix):
    B, T, D = x.shape
    (r, w, k, v, kk, b, z, kv, gate_r, q, gate_n, gl) = sample_in_proj(
        x, shift0, norm_g, w_in, mu_shift, w0, w_decay_up, a0, w_aaa_up, k_k, k_a)
    seqs = [a.reshape(R_HEADS, R_HEAD_DIM, B, T).transpose(3, 1, 2, 0).reshape(T, R_HEAD_DIM, B * R_HEADS)
            for a in (r, w, k, v, kk, b)]
    y, s_fin = rwkv_scan_lanes(seqs, _state_to_lanes(wkv0), r_k, gn_w, gn_b, B)

    cache_t = cache.reshape(cache.shape[0], PAGE_SIZE, 2 * BRANCH_W).transpose(0, 2, 1)
    nb = win_buf.shape[1]
    win_t = win_buf.reshape(B, nb, BRANCH_W).transpose(0, 2, 1)
    kc_a, kc_b = sample_compress(cache_t, page_table, w_cmp_pos, w_cmp_mix)
    kv = kv.reshape(B, T, KV_WIDTH)
    o_n = nsa_sample(q.reshape(B, T, N_WIDTH), kv, kc_a, kc_b, gl.reshape(B, T, LANES),
                     gate_n.reshape(B, T, N_WIDTH), cache_t, page_table, win_t)
    kv = kv.reshape(B, T, KV_SLOTS, N_KV_HEADS, N_HEAD_DIM)
    keys_w = jnp.concatenate([win_buf.astype(kv.dtype), kv[:, :, 4:6]], axis=1)
    n_keep = min(WINDOW, nb + T)
    shift = z.reshape(R_SHIFT_WIDTH, B, T)[:, :, T - 1].T
    outs = (kv[:, :, 0:4], keys_w[:, nb + T - n_keep:], _state_from_lanes(s_fin, B), shift)
    return (_y_from_lanes(y, B), gate_r.reshape(B, T, R_WIDTH), o_n), outs


def _finish(mixed, x, w_out, final_g):
    B, T, D = x.shape
    y_r, gate_r, o_n = (a.reshape(B * T, -1) for a in mixed)
    return out_proj(y_r, gate_r, o_n, x.reshape(B * T, D), w_out, final_g).reshape(B, T, D)


def kernel(x_prompt, x_sample, cache_kv, cache_kv_win, state_wkv, state_shift, page_table,
           norm_g, w_in, mu_shift, w0, w_decay_up, a0, w_aaa_up, k_k, k_a, r_k, gn_w, gn_b,
           w_cmp_pos, w_cmp_mix, w_out, final_g):
    y_p, y_s = x_prompt, x_sample
    kvp, kvs, wnp, wns, skp, sks, shp, shs = [], [], [], [], [], [], [], []
    for layer in range(DEPTH):
        lw = (norm_g[layer], w_in[layer], mu_shift[layer], w0[layer], w_decay_up[layer], a0[layer],
              w_aaa_up[layer], k_k[layer], k_a[layer], r_k[layer], gn_w[layer], gn_b[layer],
              w_cmp_pos[layer], w_cmp_mix[layer])
        fg = final_g if layer == DEPTH - 1 else None
        mixed_p, (p_kv, p_win, p_wkv, p_shift) = prompt_mixers(y_p, *lw)
        mixed_s, (s_kv, s_win, s_wkv, s_shift) = sample_mixers(y_s, cache_kv[layer], cache_kv_win[layer],
                                                               state_wkv[layer], state_shift[layer], page_table, *lw)
        mixed_p, mixed_s = lax.optimization_barrier((mixed_p, mixed_s))
        y_p = _finish(mixed_p, y_p, w_out[layer], fg)
        y_s = _finish(mixed_s, y_s, w_out[layer], fg)
        kvp.append(p_kv); kvs.append(s_kv); wnp.append(p_win); wns.append(s_win)
        skp.append(p_wkv); sks.append(s_wkv); shp.append(p_shift); shs.append(s_shift)
    return (y_p, y_s, jnp.stack(kvp), jnp.stack(kvs), jnp.stack(wnp), jnp.stack(wns),
            jnp.stack(skp), jnp.stack(sks), jnp.stack(shp), jnp.stack(shs))
```

```python
import functools

import jax, jax.numpy as jnp
from jax import lax
import numpy as np
from jax.experimental import pallas as pl
from jax.experimental.pallas import tpu as pltpu

D_MODEL = 1024
BATCH = 16
SEQ = 2048
DEPTH = 1
DEC_BATCH = 32
DEC_SEQ = 4
PAST_LEN = 16384
PAGE_SIZE = 128

R_HEAD_DIM = 64
R_WIDTH = D_MODEL // 2
R_HEADS = R_WIDTH // R_HEAD_DIM
DECAY_LORA = 64
AAA_LORA = 64
R_SHIFT_WIDTH = 3 * R_WIDTH + DECAY_LORA + AAA_LORA
N_HEAD_DIM = 64
N_WIDTH = D_MODEL - R_WIDTH
N_HEADS = N_WIDTH // N_HEAD_DIM
N_KV_HEADS = 2
N_GROUP = N_HEADS // N_KV_HEADS
N_BRANCH = 3
KV_SLOTS = 2 * N_BRANCH
CMP_BLOCK = 32
CMP_STRIDE = 16
SEL_BLOCK = 64
SEL_TOPK = 16
WINDOW = 512
MIX_WIDTH = R_WIDTH + N_WIDTH
KV_WIDTH = KV_SLOTS * N_KV_HEADS * N_HEAD_DIM
PROJ_SPLITS = (R_SHIFT_WIDTH, R_WIDTH, N_WIDTH, N_WIDTH, KV_WIDTH, N_BRANCH * N_HEADS)
PROJ_OFFS = tuple(int(o) for o in np.cumsum((0,) + PROJ_SPLITS))
PROJ_WIDTH = sum(PROJ_SPLITS)
RMS_EPS = 1e-6
GN_EPS = 64e-5
MASK_NEG = -1e30
SEL_NEG = -(2.0 ** 100)
SEL_BIAS = 1e4
ATTN_SCALE = N_HEAD_DIM ** -0.5
QSCALE = ATTN_SCALE * float(np.log2(np.e))

LANES = 128
BRANCH_W = 2 * N_KV_HEADS * N_HEAD_DIM
NSA_TQ = 256
NSA_KC = 512
PROJ_TM = 512
PROJ_SUB = 256
OUT_TM = 512
SAMPLE_PG = 16
RWKV_TC = 32
RWKV_COLS = 16
VMEM_LIMIT = 56 * 1024 * 1024


def _split3(x):
    hi = x.astype(jnp.bfloat16)
    r1 = x - hi.astype(jnp.float32)
    mid = r1.astype(jnp.bfloat16)
    lo = (r1 - mid.astype(jnp.float32)).astype(jnp.bfloat16)
    return hi, mid, lo


def _dot(a, b):
    return jnp.dot(a, b, preferred_element_type=jnp.float32)


def _dot_nt(a, b):
    return lax.dot_general(a, b, (((1,), (1,)), ((), ())), preferred_element_type=jnp.float32)


def _dot_exact_rhs(a, b01):
    hi, mid, lo = _split3(a)
    return _dot(hi, b01) + _dot(mid, b01) + _dot(lo, b01)


def _dot_nt_x3(a, b):
    ah = a.astype(jnp.bfloat16)
    al = (a - ah.astype(jnp.float32)).astype(jnp.bfloat16)
    bh = b.astype(jnp.bfloat16)
    bl = (b - bh.astype(jnp.float32)).astype(jnp.bfloat16)
    return _dot_nt(ah, bh) + _dot_nt(ah, bl) + _dot_nt(al, bh)


def _dot_x3(a, b):
    ah = a.astype(jnp.bfloat16)
    al = (a - ah.astype(jnp.float32)).astype(jnp.bfloat16)
    bh = b.astype(jnp.bfloat16)
    bl = (b - bh.astype(jnp.float32)).astype(jnp.bfloat16)
    return _dot(ah, bh) + _dot(ah, bl) + _dot(al, bh)


def _silu(x):
    return x * jax.nn.sigmoid(x)


def _softplus(x):
    return jnp.maximum(x, 0.0) + jnp.log1p(jnp.exp(-jnp.abs(x)))


def _masked_softmax_rows(s, mask, base2=False):
    s = jnp.where(mask, s, MASK_NEG)
    e = (jnp.exp2 if base2 else jnp.exp)(s - jnp.max(s, axis=-1, keepdims=True))
    p = e / jnp.sum(e, axis=-1, keepdims=True)
    return jnp.where(mask, p, 0.0)


def _const_spec(shape, n_grid):
    zeros = (0,) * len(shape)
    if n_grid == 1:
        return pl.BlockSpec(shape, lambda i: zeros)
    return pl.BlockSpec(shape, lambda b, i: zeros)


def _rwkv_prep(zs, w0, a0, k_k, k_a, wdec_t, waaa_t):
    bf16 = jnp.bfloat16
    n = zs.shape[1]
    r = zs[0:R_WIDTH]
    k = zs[R_WIDTH:2 * R_WIDTH]
    v = zs[2 * R_WIDTH:3 * R_WIDTH]
    wd = zs[3 * R_WIDTH:3 * R_WIDTH + DECAY_LORA]
    ad = zs[3 * R_WIDTH + DECAY_LORA:R_SHIFT_WIDTH]
    w_log = -_softplus(-(w0 + _dot(wdec_t, jnp.tanh(wd).astype(bf16)))) - 0.5
    decay = jnp.exp(-jnp.exp(w_log))
    a = jax.nn.sigmoid(a0 + _dot(waaa_t, ad.astype(bf16)))
    kk = (k * k_k).reshape(R_HEADS, R_HEAD_DIM, n)
    kk = (kk * lax.rsqrt(jnp.maximum(jnp.sum(kk * kk, axis=1, keepdims=True), 1e-24))).reshape(R_WIDTH, n)
    k = k * (1.0 + (a - 1.0) * k_a)
    return r, decay, k, v, kk, kk * a


def _prompt_proj_body(x_ref, g_ref, wz_ref, wkv_ref, wn_ref, mu_ref, w0_ref, a0_ref, kk_ref, ka_ref, wdec_ref,
                      waaa_ref, r_o, w_o, k_o, v_o, kkn_o, b_o, kvcs_o, kvw_o, gr_o, q_o, gn_o, gl_o, zl_o, carry_ref):
    bf16 = jnp.bfloat16
    tm = x_ref.shape[1]
    sub = PROJ_SUB

    @pl.when(pl.program_id(1) == 0)
    def _():
        carry_ref[...] = jnp.zeros(carry_ref.shape, jnp.float32)

    first = lax.broadcasted_iota(jnp.int32, (1, sub), 1) == 0
    prev = carry_ref[:, LANES - 1:LANES]
    zs_all = []
    for h in range(tm // sub):
        rows = slice(h * sub, (h + 1) * sub)
        x = x_ref[0, rows, :]
        xn = (x * lax.rsqrt(jnp.mean(x * x, axis=-1, keepdims=True) + RMS_EPS) * g_ref[...]).astype(bf16)
        z = _dot_nt(wz_ref[...], xn)
        kv = _dot_nt(wkv_ref[...], xn)
        nrm = _dot(xn, wn_ref[...])
        kvcs_o[0, :, rows] = kv[0:2 * BRANCH_W]
        kvw_o[0, :, rows] = kv[2 * BRANCH_W:3 * BRANCH_W]
        gr_o[0, rows, :] = nrm[:, 0:R_WIDTH]
        q_o[0, rows, :] = nrm[:, R_WIDTH:R_WIDTH + N_WIDTH]
        gn_o[0, rows, :] = nrm[:, R_WIDTH + N_WIDTH:R_WIDTH + 2 * N_WIDTH]
        gl_o[0, rows, :] = nrm[:, R_WIDTH + 2 * N_WIDTH:]
        zp = jnp.where(first, prev, pltpu.roll(z, 1, axis=1))
        prev = z[:, sub - 1:sub]
        zs_all.append((rows, z, z + (zp - z) * mu_ref[...]))
    for rows, z, zs in zs_all:
        outs = _rwkv_prep(zs, w0_ref[...], a0_ref[...], kk_ref[...], ka_ref[...], wdec_ref[...], waaa_ref[...])
        for o_ref, val in zip((r_o, w_o, k_o, v_o, kkn_o, b_o), outs):
            o_ref[0, :, rows] = val
    z_last = zs_all[-1][1][:, sub - LANES:sub]
    carry_ref[...] = z_last
    zl_o[0] = z_last


def prompt_in_proj(x, norm_g, w_in, mu_shift, w0, w_decay_up, a0, w_aaa_up, k_k, k_a):
    B, T, D = x.shape
    tm = PROJ_TM
    assert T % tm == 0 and tm % PROJ_SUB == 0 and PROJ_SUB % LANES == 0
    f32 = jnp.float32
    pw = _proj_weights(norm_g, w_in, mu_shift, w0, w_decay_up, a0, w_aaa_up, k_k, k_a)
    consts = [pw["g"], pw["wz_t"], pw["wkv"].T, pw["wn"]] + pw["cols"] + pw["lora"]
    cm = lambda c: pl.BlockSpec((1, c, tm), lambda b, i: (b, 0, i))
    tok = lambda c: pl.BlockSpec((1, tm, c), lambda b, i: (b, i, 0))
    cms = lambda c: jax.ShapeDtypeStruct((B, c, T), f32)
    toks = lambda c: jax.ShapeDtypeStruct((B, T, c), f32)
    return pl.pallas_call(
        _prompt_proj_body,
        grid=(B, T // tm),
        in_specs=[tok(D)] + [_const_spec(c.shape, 2) for c in consts],
        out_specs=[cm(R_WIDTH)] * 6 + [cm(2 * BRANCH_W), cm(BRANCH_W), tok(R_WIDTH), tok(N_WIDTH), tok(N_WIDTH),
                                       tok(LANES), pl.BlockSpec((1, R_SHIFT_WIDTH, LANES), lambda b, i: (b, 0, 0))],
        out_shape=[cms(R_WIDTH)] * 6 + [cms(2 * BRANCH_W), cms(BRANCH_W), toks(R_WIDTH), toks(N_WIDTH), toks(N_WIDTH),
                                        toks(LANES), jax.ShapeDtypeStruct((B, R_SHIFT_WIDTH, LANES), f32)],
        scratch_shapes=[pltpu.VMEM((R_SHIFT_WIDTH, LANES), f32)],
        compiler_params=pltpu.CompilerParams(dimension_semantics=("parallel", "arbitrary"),
                                             vmem_limit_bytes=VMEM_LIMIT),
        name="prompt_in_proj",
    )(x, *consts)


def _proj_weights(norm_g, w_in, mu_shift, w0, w_decay_up, a0, w_aaa_up, k_k, k_a):
    f32, bf16 = jnp.float32, jnp.bfloat16
    D = w_in.shape[0]
    o = PROJ_OFFS
    wb = w_in.astype(bf16)
    gl_pad = LANES - PROJ_SPLITS[5]
    wn = jnp.concatenate([wb[:, o[1]:o[4]], wb[:, o[5]:o[6]], jnp.zeros((D, gl_pad), bf16)], axis=1)
    col = lambda p: p.astype(f32).reshape(-1, 1)
    return dict(g=norm_g.astype(f32).reshape(1, D), wz_t=wb[:, o[0]:o[1]].T, wkv=wb[:, o[4]:o[5]], wn=wn,
                cols=[col(mu_shift), col(w0), col(a0), col(k_k), col(k_a)],
                lora=[w_decay_up.astype(bf16).T, w_aaa_up.astype(bf16).T])


def _sample_proj_body(x_ref, prev_ref, g_ref, wz_ref, wkv_ref, wn_ref, mu_ref, w0_ref, a0_ref, kk_ref, ka_ref, wdec_ref,
                      waaa_ref, r_o, w_o, k_o, v_o, kkn_o, b_o, z_o, kv_o, gr_o, q_o, gn_o, gl_o, *, steps):
    bf16 = jnp.bfloat16
    m = x_ref.shape[0]
    x = x_ref[...]
    xn = (x * lax.rsqrt(jnp.mean(x * x, axis=-1, keepdims=True) + RMS_EPS) * g_ref[...]).astype(bf16)
    z = _dot_nt(wz_ref[...], xn)
    kv_o[...] = _dot(xn, wkv_ref[...])
    nrm = _dot(xn, wn_ref[...])
    gr_o[...] = nrm[:, 0:R_WIDTH]
    q_o[...] = nrm[:, R_WIDTH:R_WIDTH + N_WIDTH]
    gn_o[...] = nrm[:, R_WIDTH + N_WIDTH:R_WIDTH + 2 * N_WIDTH]
    gl_o[...] = nrm[:, R_WIDTH + 2 * N_WIDTH:]
    z_o[...] = z
    first = lax.broadcasted_iota(jnp.int32, (1, m), 1) % steps == 0
    zp = jnp.where(first, prev_ref[...], pltpu.roll(z, 1, axis=1))
    zs = z + (zp - z) * mu_ref[...]
    outs = _rwkv_prep(zs, w0_ref[...], a0_ref[...], kk_ref[...], ka_ref[...], wdec_ref[...], waaa_ref[...])
    for o_ref, val in zip((r_o, w_o, k_o, v_o, kkn_o, b_o), outs):
        o_ref[...] = val


def sample_in_proj(x, shift0, norm_g, w_in, mu_shift, w0, w_decay_up, a0, w_aaa_up, k_k, k_a):
    B, T, D = x.shape
    M = B * T
    f32 = jnp.float32
    pw = _proj_weights(norm_g, w_in, mu_shift, w0, w_decay_up, a0, w_aaa_up, k_k, k_a)
    prev = jnp.repeat(shift0.astype(f32).T, T, axis=1)
    args = [x.reshape(M, D), prev, pw["g"], pw["wz_t"], pw["wkv"], pw["wn"]] + pw["cols"] + pw["lora"]
    cm = lambda c: jax.ShapeDtypeStruct((c, M), f32)
    tok = lambda c: jax.ShapeDtypeStruct((M, c), f32)
    out_shape = [cm(R_WIDTH)] * 6 + [cm(R_SHIFT_WIDTH), tok(KV_WIDTH), tok(R_WIDTH), tok(N_WIDTH), tok(N_WIDTH),
                                     tok(LANES)]
    return pl.pallas_call(
        functools.partial(_sample_proj_body, steps=T),
        grid=(1,),
        in_specs=[_const_spec(a.shape, 1) for a in args],
        out_specs=[_const_spec(s.shape, 1) for s in out_shape],
        out_shape=out_shape,
        compiler_params=pltpu.CompilerParams(dimension_semantics=("arbitrary",), vmem_limit_bytes=VMEM_LIMIT),
        name="sample_in_proj",
    )(*args)


def _tree_sum(terms):
    while len(terms) > 1:
        terms = [terms[i] + terms[i + 1] for i in range(0, len(terms) - 1, 2)] + terms[len(terms) & ~1:]
    return terms[0]


def _rwkv_scan_body(r_ref, w_ref, k_ref, v_ref, kk_ref, b_ref, s0_ref, rk_ref, gw_ref, gb_ref, y_ref, s_ref):
    n = s_ref.shape[0]

    @pl.when(pl.program_id(1) == 0)
    def _():
        s_ref[...] = s0_ref[...]

    def step(t, carry):
        row = lambda ref, j: ref[t, pl.ds(j, 1), :]
        v = v_ref[t]
        grp = RWKV_COLS

        def p1(g, sa):
            return sa + _tree_sum([s_ref[g * grp + u] * row(kk_ref, g * grp + u) for u in range(grp)])

        sa = lax.fori_loop(0, n // grp, p1, jnp.zeros(v.shape, jnp.float32))

        def p2(g, y):
            terms = []
            for u in range(grp):
                j = g * grp + u
                s_j = s_ref[j] * row(w_ref, j) - sa * row(b_ref, j) + v * row(k_ref, j)
                s_ref[j] = s_j
                terms.append(s_j * row(r_ref, j))
            return y + _tree_sum(terms)

        y = lax.fori_loop(0, n // grp, p2, jnp.zeros(v.shape, jnp.float32))
        yc = y - jnp.mean(y, axis=0, keepdims=True)
        y = yc * lax.rsqrt(jnp.mean(yc * yc, axis=0, keepdims=True) + GN_EPS)
        y = y * gw_ref[...] + gb_ref[...]
        y_ref[t] = y + jnp.sum(r_ref[t] * k_ref[t] * rk_ref[...], axis=0, keepdims=True) * v
        return carry

    lax.fori_loop(0, r_ref.shape[0], step, 0)


def rwkv_scan_lanes(seqs, s0, r_k, gn_w, gn_b, n_batch):
    T, N, BH = seqs[0].shape
    assert BH % LANES == 0
    tc = min(RWKV_TC, T)
    assert T % tc == 0
    per_lane = lambda p: jnp.tile(p.astype(jnp.float32).reshape(BH // n_batch, N).T, (1, n_batch))
    seq = pl.BlockSpec((tc, N, LANES), lambda l, t: (t, 0, l))
    state = pl.BlockSpec((N, N, LANES), lambda l, t: (0, 0, l))
    param = pl.BlockSpec((N, LANES), lambda l, t: (0, l))
    return pl.pallas_call(
        _rwkv_scan_body,
        grid=(BH // LANES, T // tc),
        in_specs=[seq] * 6 + [state] + [param] * 3,
        out_specs=[seq, state],
        out_shape=[jax.ShapeDtypeStruct((T, N, BH), jnp.float32), jax.ShapeDtypeStruct((N, N, BH), jnp.float32)],
        compiler_params=pltpu.CompilerParams(dimension_semantics=("parallel", "arbitrary"),
                                             vmem_limit_bytes=VMEM_LIMIT),
        name="rwkv_scan",
    )(*seqs, s0, per_lane(r_k), per_lane(gn_w), per_lane(gn_b))


def _lanes_from_cm(a, T):
    return a.reshape(-1, R_HEAD_DIM, T).transpose(2, 1, 0)


def _y_from_lanes(y, B):
    T, N, _ = y.shape
    return y.reshape(T, N, B, R_HEADS).transpose(2, 0, 3, 1).reshape(B, T, R_WIDTH)


def _state_to_lanes(s):
    B, H, N, _ = s.shape
    return s.astype(jnp.float32).transpose(3, 2, 0, 1).reshape(N, N, B * H)


def _state_from_lanes(s, B):
    N = s.shape[0]
    return s.reshape(N, N, B, R_HEADS).transpose(2, 3, 1, 0)


def _cmp_weights(w_pos, w_mix, n_rows):
    wp = jnp.broadcast_to(w_pos.astype(jnp.float32).reshape(2, 1, CMP_BLOCK, N_HEAD_DIM).transpose(2, 0, 1, 3),
                          (CMP_BLOCK, 2, N_KV_HEADS, N_HEAD_DIM)).reshape(CMP_BLOCK, BRANCH_W)
    wa = jnp.tile(wp[:CMP_STRIDE], (n_rows // CMP_STRIDE, 1)).T
    wb = jnp.tile(wp[CMP_STRIDE:], (n_rows // CMP_STRIDE, 1)).T
    wmix = jnp.zeros((BRANCH_W, BRANCH_W), jnp.float32)
    for e in range(2):
        for kh in range(N_KV_HEADS):
            o = (e * N_KV_HEADS + kh) * N_HEAD_DIM
            wmix = wmix.at[o:o + N_HEAD_DIM, o:o + N_HEAD_DIM].set(w_mix[e].astype(jnp.float32))
    return wa, wb, wmix.T


def _compress_prompt_body(kv_ref, wa_ref, wb_ref, pool_ref, wmix_ref, o_ref):
    x = kv_ref[0]
    xw = jnp.concatenate([x * wa_ref[...], x * wb_ref[...]], axis=1)
    o_ref[0] = _dot_x3(wmix_ref[...], _dot_exact_rhs(xw, pool_ref[...]))


def compress_prompt(kv_cs, w_pos, w_mix):
    B, _, T = kv_cs.shape
    assert CMP_BLOCK == 2 * CMP_STRIDE and T % CMP_STRIDE == 0
    ncp = T // CMP_STRIDE
    wa, wb, wmix_t = _cmp_weights(w_pos, w_mix, T)
    chunk = np.arange(T) // CMP_STRIDE
    pool = np.concatenate([chunk[:, None] == np.arange(ncp)[None, :],
                           chunk[:, None] == np.arange(ncp)[None, :] + 1], axis=0)
    pool = jnp.asarray(pool, jnp.bfloat16)
    return pl.pallas_call(
        _compress_prompt_body,
        grid=(B,),
        in_specs=[pl.BlockSpec((1, BRANCH_W, T), lambda b: (b, 0, 0))] + [_const_spec(c.shape, 1)
                                                                          for c in (wa, wb, pool, wmix_t)],
        out_specs=pl.BlockSpec((1, BRANCH_W, ncp), lambda b: (b, 0, 0)),
        out_shape=jax.ShapeDtypeStruct((B, BRANCH_W, ncp), jnp.float32),
        compiler_params=pltpu.CompilerParams(dimension_semantics=("parallel",), vmem_limit_bytes=VMEM_LIMIT),
        name="compress_prompt",
    )(kv_cs, wa, wb, pool, wmix_t)


def _nsa_prompt_body(q_ref, ksel_ref, kwin_ref, kc_ref, gl_ref, gn_ref, ovt_ref, ex_ref, o_ref):
    f32, bf16 = jnp.float32, jnp.bfloat16
    TQ, KC, G = NSA_TQ, NSA_KC, N_GROUP
    ncp = kc_ref.shape[2]
    n_sel = ovt_ref.shape[0]
    topk = min(SEL_TOPK, n_sel)
    qs = pl.program_id(1) * TQ
    qpos = qs + lax.broadcasted_iota(jnp.int32, (TQ, 1), 0)
    lo_half = lax.broadcasted_iota(jnp.int32, (TQ, LANES), 1) < N_HEAD_DIM

    def q_padded(h, half):
        tile = q_ref[0, :, (h // 2) * LANES:(h // 2 + 1) * LANES] * QSCALE
        if h % 2 != half:
            tile = pltpu.roll(tile, N_HEAD_DIM, axis=1)
        return jnp.where(lo_half if half == 0 else jnp.logical_not(lo_half), tile, 0.0)

    wstart = pl.multiple_of(jnp.maximum(qs - WINDOW, 0), TQ)
    kw = kwin_ref[0, 0:LANES, pl.ds(wstart, WINDOW + TQ)].astype(bf16)
    vw = kwin_ref[0, LANES:2 * LANES, pl.ds(wstart, WINDOW + TQ)].astype(bf16)
    kposw = wstart + lax.broadcasted_iota(jnp.int32, (1, WINDOW + TQ), 1)
    mask_w = (kposw <= qpos) & (kposw > qpos - WINDOW)

    kck = kc_ref[0, 0:LANES, :]
    kcv = kc_ref[0, LANES:2 * LANES, :].astype(bf16)
    kend = lax.broadcasted_iota(jnp.int32, (1, ncp), 1) * CMP_STRIDE + (CMP_BLOCK - 1)
    mask_c = kend <= qpos

    sid_t = lax.broadcasted_iota(jnp.int32, (n_sel, 1), 0)
    cur_t = (qs + lax.broadcasted_iota(jnp.int32, (1, TQ), 1)) // SEL_BLOCK
    valid_t = sid_t <= cur_t
    forced_t = (sid_t == 0) | (sid_t == cur_t) | (sid_t == cur_t - 1)
    n_chunks = qs // KC + 1

    H = N_HEADS
    qg = jnp.concatenate([q_padded(h, h // G) for h in range(H)], axis=0)
    qb = qg.astype(bf16)

    p = _masked_softmax_rows(_dot_x3(qg, kck).reshape(H, TQ, ncp), mask_c[None], base2=True)
    o_c = _dot_nt(p.reshape(H * TQ, ncp).astype(bf16), kcv).reshape(H, TQ, LANES)

    ovt = ovt_ref[...]
    selb = []
    for kvh in range(N_KV_HEADS):
        hi, mid, lo = _split3(jnp.sum(p[kvh * G:(kvh + 1) * G], axis=0))
        score = _dot_nt(ovt, hi) + _dot_nt(ovt, mid) + _dot_nt(ovt, lo)
        score = jnp.where(valid_t, score + jnp.where(forced_t, SEL_BIAS, 0.0), -SEL_BIAS)
        rank = jnp.zeros((n_sel, TQ), jnp.int32)
        for s2 in range(n_sel):
            row = score[s2:s2 + 1, :]
            ahead = (row > score) | ((row == score) & (s2 < sid_t))
            rank = rank + ahead.astype(jnp.int32)
        selb_t = jnp.concatenate([jnp.where((rank < topk) & valid_t, 0.0, SEL_NEG),
                                  jnp.zeros((LANES - n_sel, TQ), f32)], axis=0)
        selb += [selb_t.T.astype(bf16)] * G
    qa = jnp.concatenate([qb, jnp.concatenate(selb, axis=0)], axis=1)

    def chunk(c, carry, causal):
        m, l, acc = carry
        k0 = pl.multiple_of(c * KC, KC)
        ka = jnp.concatenate([ksel_ref[0, 0:LANES, pl.ds(k0, KC)].astype(bf16), ex_ref[:, pl.ds(k0, KC)]],
                             axis=0)
        va = jnp.concatenate([ksel_ref[0, LANES:2 * LANES, pl.ds(k0, KC)].astype(bf16), ones_kc], axis=0)
        sc = _dot(qa, ka).reshape(H, TQ, KC)
        if causal:
            kpos = k0 + lax.broadcasted_iota(jnp.int32, (1, KC), 1)
            sc = jnp.where((kpos <= qpos)[None], sc, SEL_NEG)
        m_new = jnp.maximum(m, jnp.max(sc, axis=-1, keepdims=True))
        alpha = jnp.exp2(m - m_new)
        e = jnp.exp2(sc - m_new).astype(bf16)
        pv = _dot_nt(e.reshape(H * TQ, KC), va).reshape(H, TQ, 2 * LANES)
        return m_new, alpha * l + pv[:, :, LANES:], alpha * acc + pv[:, :, :LANES]

    ones_kc = jnp.ones((LANES, KC), bf16)
    carry = (jnp.full((H, TQ, 1), SEL_NEG, f32), jnp.zeros((H, TQ, LANES), f32), jnp.zeros((H, TQ, LANES), f32))
    carry = lax.fori_loop(0, n_chunks - 1, functools.partial(chunk, causal=False), carry)
    _, l, acc = chunk(n_chunks - 1, carry, causal=True)
    o_s = acc / l

    sw = jnp.where(mask_w[None], _dot(qb, kw).reshape(H, TQ, WINDOW + TQ), MASK_NEG)
    ew = jnp.exp2(sw - jnp.max(sw, axis=-1, keepdims=True)).astype(bf16)
    pw = _dot_nt(ew.reshape(H * TQ, WINDOW + TQ),
                 jnp.concatenate([vw, jnp.ones((LANES, WINDOW + TQ), bf16)], axis=0)).reshape(H, TQ, 2 * LANES)
    o_w = pw[:, :, :LANES] / pw[:, :, LANES:]

    gates = jax.nn.sigmoid(gl_ref[0])
    heads = []
    for h in range(H):
        o = (gates[:, h:h + 1] * o_c[h] + gates[:, H + h:H + h + 1] * o_s[h]
             + gates[:, 2 * H + h:2 * H + h + 1] * o_w[h])
        heads.append(pltpu.roll(o, N_HEAD_DIM, axis=1) if h % 2 != h // G else o)

    for m in range(N_HEADS // 2):
        tile = jnp.where(lo_half, heads[2 * m], heads[2 * m + 1])
        o_ref[0, :, m * LANES:(m + 1) * LANES] = (
            tile * _silu(gn_ref[0, :, m * LANES:(m + 1) * LANES])).astype(o_ref.dtype)


def nsa_prompt(q, kv_cs, kv_w, kc, gl, gate_n):
    B, T, _ = q.shape
    TQ, KC = NSA_TQ, NSA_KC
    assert N_KV_HEADS == 2 and N_HEAD_DIM * 2 == LANES and T % KC == 0 and KC % TQ == 0 and T >= WINDOW + TQ
    assert SEL_BLOCK % CMP_STRIDE == 0 and WINDOW % TQ == 0 and TQ % LANES == 0 and T // SEL_BLOCK <= LANES
    ncp = T // CMP_STRIDE
    n_sel = T // SEL_BLOCK
    cs = np.arange(ncp) * CMP_STRIDE
    ss = np.arange(n_sel) * SEL_BLOCK
    overlap = (cs[:, None] < ss[None, :] + SEL_BLOCK) & (cs[:, None] + CMP_BLOCK > ss[None, :])
    overlap[ncp - 1] = False
    overlap_t = jnp.asarray(overlap.T, jnp.bfloat16)
    expand = jnp.asarray(np.arange(T)[None, :] // SEL_BLOCK == np.arange(LANES)[:, None], jnp.bfloat16)
    return pl.pallas_call(
        _nsa_prompt_body,
        grid=(B, T // TQ),
        in_specs=[pl.BlockSpec((1, TQ, N_WIDTH), lambda b, i: (b, i, 0)),
                  pl.BlockSpec((1, BRANCH_W, T), lambda b, i: (b, 1, 0)),
                  pl.BlockSpec((1, BRANCH_W, T), lambda b, i: (b, 0, 0)),
                  pl.BlockSpec((1, BRANCH_W, ncp), lambda b, i: (b, 0, 0)),
                  pl.BlockSpec((1, TQ, LANES), lambda b, i: (b, i, 0)),
                  pl.BlockSpec((1, TQ, N_WIDTH), lambda b, i: (b, i, 0)),
                  _const_spec((n_sel, ncp), 2), _const_spec((LANES, T), 2)],
        out_specs=pl.BlockSpec((1, TQ, N_WIDTH), lambda b, i: (b, i, 0)),
        out_shape=jax.ShapeDtypeStruct((B, T, N_WIDTH), jnp.bfloat16),
        compiler_params=pltpu.CompilerParams(dimension_semantics=("parallel", "arbitrary"),
                                             vmem_limit_bytes=VMEM_LIMIT),
        name="nsa_prompt",
    )(q, kv_cs, kv_w, kc, gl, gate_n, overlap_t, expand)


def _page_spec(k, branch):
    return pl.BlockSpec((1, BRANCH_W, PAGE_SIZE), lambda b, g, pt: (pt[b, g * SAMPLE_PG + k], branch, 0))


def _sample_compress_body(pt_ref, *refs):
    pages = refs[:SAMPLE_PG]
    wa_ref, wb_ref, wmix_ref, oa_ref, ob_ref = refs[SAMPLE_PG:]
    cpp = PAGE_SIZE // CMP_STRIDE
    xt = [p[0].T.reshape(cpp, CMP_STRIDE, BRANCH_W) for p in pages]
    for w_ref, o_ref in ((wa_ref, oa_ref), (wb_ref, ob_ref)):
        pooled = jnp.concatenate([jnp.sum(x * w_ref[...], axis=1) for x in xt], axis=0)
        o_ref[0] = _dot_x3(pooled, wmix_ref[...])


def sample_compress(cache_t, page_table, w_pos, w_mix):
    B, n_pages = page_table.shape
    PG = SAMPLE_PG
    assert n_pages % PG == 0 and CMP_BLOCK == 2 * CMP_STRIDE and PAGE_SIZE % CMP_STRIDE == 0
    rows = PG * PAGE_SIZE // CMP_STRIDE
    ncp = n_pages * PAGE_SIZE // CMP_STRIDE
    wa, wb, wmix_t = _cmp_weights(w_pos, w_mix, CMP_STRIDE)
    wa, wb, wmix = wa.T[None], wb.T[None], wmix_t.T
    const = lambda shape: pl.BlockSpec(shape, lambda b, g, pt: (0,) * len(shape))
    out = pl.BlockSpec((1, rows, BRANCH_W), lambda b, g, pt: (b, g, 0))
    return pl.pallas_call(
        _sample_compress_body,
        grid_spec=pltpu.PrefetchScalarGridSpec(
            num_scalar_prefetch=1, grid=(B, n_pages // PG),
            in_specs=[_page_spec(k, 0) for k in range(PG)] + [const(wa.shape), const(wb.shape), const(wmix.shape)],
            out_specs=[out, out]),
        out_shape=[jax.ShapeDtypeStruct((B, ncp, BRANCH_W), jnp.float32)] * 2,
        compiler_params=pltpu.CompilerParams(dimension_semantics=("parallel", "arbitrary"),
                                             vmem_limit_bytes=VMEM_LIMIT),
        name="sample_compress",
    )(page_table, *([cache_t] * PG), wa, wb, wmix)


def _online_chunk(m, l, acc, s, keep, v, v_channels_major=False):
    s = jnp.where(keep, s, MASK_NEG)
    m_new = jnp.maximum(m, jnp.max(s, axis=-1, keepdims=True))
    alpha = jnp.exp(m - m_new)
    e = jnp.where(keep, jnp.exp(s - m_new), 0.0)
    eb = e.astype(jnp.bfloat16)
    pv = _dot_nt(eb, v) if v_channels_major else _dot(eb, v)
    return m_new, alpha * l + jnp.sum(e, axis=-1, keepdims=True), alpha * acc + pv


def _sample_attend_body(pt_ref, *refs, past):
    f32, bf16 = jnp.float32, jnp.bfloat16
    PG, G = SAMPLE_PG, N_GROUP
    pages = refs[:PG]
    (ka_ref, kb_ref, q_ref, kvn_ref, gl_ref, gn_ref, win_ref, ex_ref, ov_ref, o_ref,
     qf_ref, selb_ref, tail_ref, m_ref, l_ref, acc_ref, oc_ref) = refs[PG:]
    T = q_ref.shape[1]
    R = N_HEADS * T
    ncp = ka_ref.shape[1]
    nsp = ov_ref.shape[1]
    KC = PG * PAGE_SIZE
    n_past = past // SEL_BLOCK
    bpc = KC // SEL_BLOCK
    g = pl.program_id(1)
    lo_half = lax.broadcasted_iota(jnp.int32, (T, LANES), 1) < N_HEAD_DIM
    row = lax.broadcasted_iota(jnp.int32, (R, 1), 0)
    qpos = past + row % T

    @pl.when(g == 0)
    def _():
        pieces = []
        for h in range(N_HEADS):
            kvh = h // G
            tile = q_ref[0, :, (h // 2) * LANES:(h // 2 + 1) * LANES] * ATTN_SCALE
            if h % 2 != kvh:
                tile = pltpu.roll(tile, N_HEAD_DIM, axis=1)
            pieces.append(jnp.where(lo_half if kvh == 0 else jnp.logical_not(lo_half), tile, 0.0))
        q32 = jnp.concatenate(pieces, axis=0)
        qf_ref[...] = q32

        kc = ka_ref[0] + pltpu.roll(kb_ref[0], ncp - 1, axis=0)
        kend = lax.broadcasted_iota(jnp.int32, (1, ncp), 1) * CMP_STRIDE + (CMP_BLOCK - 1)
        p = _masked_softmax_rows(_dot_nt_x3(q32, kc[:, 0:LANES]), kend <= qpos)
        oc_ref[...] = _dot(p.astype(bf16), kc[:, LANES:2 * LANES].astype(bf16))
        imp = jnp.concatenate(
            [sum(p[(kvh * G + gg) * T:(kvh * G + gg + 1) * T] for gg in range(G)) for kvh in range(N_KV_HEADS)],
            axis=0)

        sid = lax.broadcasted_iota(jnp.int32, (1, nsp), 1)
        sid_f = sid.astype(f32)
        cur = (past + lax.broadcasted_iota(jnp.int32, (N_KV_HEADS * T, 1), 0) % T) // SEL_BLOCK
        valid = sid <= cur
        forced = (sid == 0) | (sid == cur) | (sid == cur - 1)
        score = _dot_exact_rhs(imp, ov_ref[...])
        score = jnp.where(valid, score + jnp.where(forced, SEL_BIAS, 0.0), -SEL_BIAS)

        def extract(_, carry):
            sc, chosen = carry
            best = jnp.max(sc, axis=-1, keepdims=True)
            first = jnp.min(jnp.where(sc == best, sid_f, float(nsp)), axis=-1, keepdims=True)
            hit = sid_f == first
            return jnp.where(hit, -3e38, sc), jnp.where(hit, 1.0, chosen)

        _, chosen = lax.fori_loop(0, min(SEL_TOPK, n_past + 1), extract, (score, jnp.zeros_like(score)))
        sel = jnp.concatenate([jnp.where(valid, chosen, 0.0)[(h // G) * T:(h // G + 1) * T] for h in range(N_HEADS)],
                              axis=0)
        tail_ref[...] = sel[:, n_past:n_past + 1]
        selb = jnp.where(sel > 0.5, 0.0, SEL_NEG)
        for c in range(n_past // bpc):
            selb_ref[c] = jnp.concatenate([selb[:, c * bpc:(c + 1) * bpc], jnp.zeros((R, LANES - bpc), f32)],
                                          axis=1).astype(bf16)
        m_ref[...] = jnp.full(m_ref.shape, SEL_NEG, f32)
        l_ref[...] = jnp.zeros(l_ref.shape, f32)
        acc_ref[...] = jnp.zeros(acc_ref.shape, f32)

    qb = qf_ref[...].astype(bf16)
    ka = jnp.concatenate([p[0, 0:LANES, :] for p in pages], axis=1).astype(bf16)
    ka = jnp.concatenate([ka, ex_ref[...]], axis=0)
    va = jnp.concatenate([jnp.concatenate([p[0, LANES:2 * LANES, :] for p in pages], axis=1).astype(bf16),
                          jnp.ones((LANES, KC), bf16)], axis=0)
    s = _dot(jnp.concatenate([qb, selb_ref[g]], axis=1), ka)
    m = jnp.maximum(m_ref[...], jnp.max(s, axis=-1, keepdims=True))
    alpha = jnp.exp(m_ref[...] - m)
    pv = _dot_nt(jnp.exp(s - m).astype(bf16), va)
    l = alpha * l_ref[...] + pv[:, LANES:]
    acc = alpha * acc_ref[...] + pv[:, :LANES]
    m_ref[...] = m
    l_ref[...] = l
    acc_ref[...] = acc

    @pl.when(g == pl.num_programs(1) - 1)
    def _():
        def new_rows(c0):
            return jnp.concatenate([kvn_ref[0, :, c0:c0 + LANES], jnp.zeros((LANES - T, LANES), f32)],
                                   axis=0).astype(bf16)

        npos = past + lax.broadcasted_iota(jnp.int32, (1, LANES), 1)
        is_new = npos < past + T
        keep_t = is_new & (npos <= qpos) & (tail_ref[...] > 0.5)
        _, l_s, acc_s = _online_chunk(m, l, acc, _dot_nt(qb, new_rows(BRANCH_W)), keep_t, new_rows(BRANCH_W + LANES))
        o_s = acc_s / l_s

        nb = win_ref.shape[2]
        bpos = past - nb + lax.broadcasted_iota(jnp.int32, (1, nb), 1)
        keep_b = (bpos <= qpos) & (bpos > qpos - WINDOW) & (bpos >= 0)
        mw, lw, aw = _online_chunk(jnp.full((R, 1), MASK_NEG, f32), jnp.zeros((R, 1), f32), jnp.zeros((R, LANES), f32),
                                   _dot(qb, win_ref[0, 0:LANES, :].astype(bf16)), keep_b,
                                   win_ref[0, LANES:2 * LANES, :].astype(bf16), True)
        keep_n = is_new & (npos <= qpos) & (npos > qpos - WINDOW)
        _, lw, aw = _online_chunk(mw, lw, aw, _dot_nt(qb, new_rows(2 * BRANCH_W)), keep_n,
                                  new_rows(2 * BRANCH_W + LANES))
        o_w = aw / lw

        gates = jax.nn.sigmoid(gl_ref[0])
        gcol = lambda br: jnp.concatenate([gates[:, br * N_HEADS + h:br * N_HEADS + h + 1] for h in range(N_HEADS)],
                                          axis=0)
        o = gcol(0) * oc_ref[...] + gcol(1) * o_s + gcol(2) * o_w
        for mt in range(N_HEADS // 2):
            halves = []
            for h in (2 * mt, 2 * mt + 1):
                piece = o[h * T:(h + 1) * T]
                if h % 2 != h // G:
                    piece = pltpu.roll(piece, N_HEAD_DIM, axis=1)
                halves.append(piece)
            o_ref[0, :, mt * LANES:(mt + 1) * LANES] = (jnp.where(lo_half, halves[0], halves[1])
                                                        * _silu(gn_ref[0, :, mt * LANES:(mt + 1) * LANES]))


def nsa_sample(q, kv_new, kc_a, kc_b, gl, gate_n, cache_t, page_table, win_t):
    B, T, _ = q.shape
    n_pages = page_table.shape[1]
    past = n_pages * PAGE_SIZE
    PG = SAMPLE_PG
    KC = PG * PAGE_SIZE
    ng = n_pages // PG
    assert n_pages % PG == 0 and past % SEL_BLOCK == 0 and T <= SEL_BLOCK and T <= LANES
    assert (past + T) // CMP_STRIDE == past // CMP_STRIDE and KC % SEL_BLOCK == 0
    ncp = past // CMP_STRIDE
    n_sel = past // SEL_BLOCK + 1
    nsp = -(-n_sel // LANES) * LANES
    cs = np.arange(ncp) * CMP_STRIDE
    ss = np.arange(nsp) * SEL_BLOCK
    overlap = (cs[:, None] < ss[None, :] + SEL_BLOCK) & (cs[:, None] + CMP_BLOCK > ss[None, :])
    overlap[ncp - 1] = False
    overlap[:, n_sel:] = False
    overlap = jnp.asarray(overlap, jnp.bfloat16)
    assert KC // SEL_BLOCK <= LANES
    expand = jnp.asarray(np.arange(KC)[None, :] // SEL_BLOCK == np.arange(LANES)[:, None], jnp.bfloat16)
    R = N_HEADS * T
    per_b = lambda shape: pl.BlockSpec((1,) + shape, lambda b, g, pt: (b, 0, 0))
    return pl.pallas_call(
        functools.partial(_sample_attend_body, past=past),
        grid_spec=pltpu.PrefetchScalarGridSpec(
            num_scalar_prefetch=1, grid=(B, ng),
            in_specs=[_page_spec(k, 1) for k in range(PG)] + [
                per_b((ncp, BRANCH_W)), per_b((ncp, BRANCH_W)), per_b((T, N_WIDTH)), per_b((T, KV_WIDTH)),
                per_b((T, LANES)), per_b((T, N_WIDTH)), per_b((BRANCH_W, win_t.shape[2])),
                pl.BlockSpec((LANES, KC), lambda b, g, pt: (0, 0)),
                pl.BlockSpec((ncp, nsp), lambda b, g, pt: (0, 0))],
            out_specs=per_b((T, N_WIDTH)),
            scratch_shapes=[pltpu.VMEM((R, LANES), jnp.float32), pltpu.VMEM((ng, R, LANES), jnp.bfloat16),
                            pltpu.VMEM((R, 1), jnp.float32), pltpu.VMEM((R, 1), jnp.float32),
                            pltpu.VMEM((R, LANES), jnp.float32), pltpu.VMEM((R, LANES), jnp.float32),
                            pltpu.VMEM((R, LANES), jnp.float32)]),
        out_shape=jax.ShapeDtypeStruct((B, T, N_WIDTH), jnp.float32),
        compiler_params=pltpu.CompilerParams(dimension_semantics=("parallel", "arbitrary"),
                                             vmem_limit_bytes=VMEM_LIMIT),
        name="nsa_sample",
    )(page_table, *([cache_t] * PG), kc_a, kc_b, q, kv_new, gl, gate_n, win_t, expand, overlap)


def _out_proj_body(yr_ref, gr_ref, on_ref, x_ref, w_ref, g_ref, o_ref, *, final_norm):
    bf16 = jnp.bfloat16
    a = (yr_ref[...] * _silu(gr_ref[...])).astype(bf16)
    y = x_ref[...] + _dot(a, w_ref[0:R_WIDTH, :]) + _dot(on_ref[...].astype(bf16), w_ref[R_WIDTH:MIX_WIDTH, :])
    if final_norm:
        y = y * lax.rsqrt(jnp.mean(y * y, axis=-1, keepdims=True) + RMS_EPS) * g_ref[...]
    o_ref[...] = y


def out_proj(y_r, gate_r, o_n, x, w_out, final_g):
    M, D = x.shape
    tm = min(OUT_TM, M)
    assert M % tm == 0
    g = (jnp.ones((D,), jnp.float32) if final_g is None else final_g.astype(jnp.float32)).reshape(1, D)
    rows = lambda c: pl.BlockSpec((tm, c), lambda i: (i, 0))
    return pl.pallas_call(
        functools.partial(_out_proj_body, final_norm=final_g is not None),
        grid=(M // tm,),
        in_specs=[rows(R_WIDTH), rows(R_WIDTH), rows(N_WIDTH), rows(D), _const_spec((MIX_WIDTH, D), 1),
                  _const_spec((1, D), 1)],
        out_specs=rows(D),
        out_shape=jax.ShapeDtypeStruct((M, D), jnp.float32),
        compiler_params=pltpu.CompilerParams(dimension_semantics=("parallel",), vmem_limit_bytes=VMEM_LIMIT),
        name="out_proj",
    )(y_r, gate_r, o_n, x, w_out.astype(jnp.bfloat16), g)


def prompt_mixers(x, norm_g, w_in, mu_shift, w0, w_decay_up, a0, w_aaa_up, k_k, k_a, r_k, gn_w, gn_b,
                  w_cmp_pos, w_cmp_mix):
    B, T, D = x.shape
    (r, w, k, v, kk, b, kv_cs, kv_w, gate_r, q, gate_n, gl, z_last) = prompt_in_proj(
        x, norm_g, w_in, mu_shift, w0, w_decay_up, a0, w_aaa_up, k_k, k_a)
    seqs = [_lanes_from_cm(a, T) for a in (r, w, k, v, kk, b)]
    o_n = nsa_prompt(q, kv_cs, kv_w, compress_prompt(kv_cs, w_cmp_pos, w_cmp_mix), gl, gate_n)
    seqs, o_n = lax.optimization_barrier((seqs, o_n))
    s0 = jnp.zeros((R_HEAD_DIM, R_HEAD_DIM, B * R_HEADS), jnp.float32)
    y, s_fin = rwkv_scan_lanes(seqs, s0, r_k, gn_w, gn_b, B)
    heads = (N_KV_HEADS, N_HEAD_DIM)
    kv_rows = kv_cs.reshape((B, 4) + heads + (T,)).transpose(0, 4, 1, 2, 3)
    n_keep = min(WINDOW, T)
    win_rows = kv_w[:, :, T - n_keep:].reshape((B, 2) + heads + (n_keep,)).transpose(0, 4, 1, 2, 3)
    outs = (kv_rows, win_rows, _state_from_lanes(s_fin, B), z_last[:, :, LANES - 1])
    return (_y_from_lanes(y, B), gate_r, o_n), outs


def sample_mixers(x, cache, win_buf, wkv0, shift0, page_table, norm_g, w_in, mu_shift, w0, w_decay_up, a0,
                  w_aaa_up, k_k, k_a, r_k, gn_w, gn_b, w_cmp_pos, w_cmp_mix):
    B, T, D = x.shape
    (r, w, k, v, kk, b, z, kv, gate_r, q, gate_n, gl) = sample_in_proj(
        x, shift0, norm_g, w_in, mu_shift, w0, w_decay_up, a0, w_aaa_up, k_k, k_a)
    seqs = [a.reshape(R_HEADS, R_HEAD_DIM, B, T).transpose(3, 1, 2, 0).reshape(T, R_HEAD_DIM, B * R_HEADS)
            for a in (r, w, k, v, kk, b)]
    y, s_fin = rwkv_scan_lanes(seqs, _state_to_lanes(wkv0), r_k, gn_w, gn_b, B)

    cache_t = cache.reshape(cache.shape[0], PAGE_SIZE, 2 * BRANCH_W).transpose(0, 2, 1)
    nb = win_buf.shape[1]
    win_t = win_buf.reshape(B, nb, BRANCH_W).transpose(0, 2, 1)
    kc_a, kc_b = sample_compress(cache_t, page_table, w_cmp_pos, w_cmp_mix)
    kv = kv.reshape(B, T, KV_WIDTH)
    o_n = nsa_sample(q.reshape(B, T, N_WIDTH), kv, kc_a, kc_b, gl.reshape(B, T, LANES),
                     gate_n.reshape(B, T, N_WIDTH), cache_t, page_table, win_t)
    kv = kv.reshape(B, T, KV_SLOTS, N_KV_HEADS, N_HEAD_DIM)
    keys_w = jnp.concatenate([win_buf.astype(kv.dtype), kv[:, :, 4:6]], axis=1)
    n_keep = min(WINDOW, nb + T)
    shift = z.reshape(R_SHIFT_WIDTH, B, T)[:, :, T - 1].T
    outs = (kv[:, :, 0:4], keys_w[:, nb + T - n_keep:], _state_from_lanes(s_fin, B), shift)
    return (_y_from_lanes(y, B), gate_r.reshape(B, T, R_WIDTH), o_n), outs


def _finish(mixed, x, w_out, final_g):
    B, T, D = x.shape
    y_r, gate_r, o_n = (a.reshape(B * T, -1) for a in mixed)
    return out_proj(y_r, gate_r, o_n, x.reshape(B * T, D), w_out, final_g).reshape(B, T, D)


def kernel(x_prompt, x_sample, cache_kv, cache_kv_win, state_wkv, state_shift, page_table,
           norm_g, w_in, mu_shift, w0, w_decay_up, a0, w_aaa_up, k_k, k_a, r_k, gn_w, gn_b,
           w_cmp_pos, w_cmp_mix, w_out, final_g):
    y_p, y_s = x_prompt, x_sample
    kvp, kvs, wnp, wns, skp, sks, shp, shs = [], [], [], [], [], [], [], []
    for layer in range(DEPTH):
        lw = (norm_g[layer], w_in[layer], mu_shift[layer], w0[layer], w_decay_up[layer], a0[layer],
              w_aaa_up[layer], k_k[layer], k_a[layer], r_k[layer], gn_w[layer], gn_b[layer],
              w_cmp_pos[layer], w_cmp_mix[layer])
        fg = final_g if layer == DEPTH - 1 else None
        mixed_p, (p_kv, p_win, p_wkv, p_shift) = prompt_mixers(y_p, *lw)
        mixed_s, (s_kv, s_win, s_wkv, s_shift) = sample_mixers(y_s, cache_kv[layer], cache_kv_win[layer],
                                                               state_wkv[layer], state_shift[layer], page_table, *lw)
        mixed_p, mixed_s = lax.optimization_barrier((mixed_p, mixed_s))
        y_p = _finish(mixed_p, y_p, w_out[layer], fg)
        y_s = _finish(mixed_s, y_s, w_out[layer], fg)
        kvp.append(p_kv); kvs.append(s_kv); wnp.append(p_win); wns.append(s_win)
        skp.append(p_wkv); sks.append(s_wkv); shp.append(p_shift); shs.append(s_shift)
    return (y_p, y_s, jnp.stack(kvp), jnp.stack(kvs), jnp.stack(wnp), jnp.stack(wns),
            jnp.stack(skp), jnp.stack(sks), jnp.stack(shp), jnp.stack(shs))
```

```python
import functools

import jax, jax.numpy as jnp
from jax import lax
import numpy as np
from jax.experimental import pallas as pl
from jax.experimental.pallas import tpu as pltpu

D_MODEL = 1024
BATCH = 16
SEQ = 2048
DEPTH = 1
DEC_BATCH = 32
DEC_SEQ = 4
PAST_LEN = 16384
PAGE_SIZE = 128

R_HEAD_DIM = 64
R_WIDTH = D_MODEL // 2
R_HEADS = R_WIDTH // R_HEAD_DIM
DECAY_LORA = 64
AAA_LORA = 64
R_SHIFT_WIDTH = 3 * R_WIDTH + DECAY_LORA + AAA_LORA
N_HEAD_DIM = 64
N_WIDTH = D_MODEL - R_WIDTH
N_HEADS = N_WIDTH // N_HEAD_DIM
N_KV_HEADS = 2
N_GROUP = N_HEADS // N_KV_HEADS
N_BRANCH = 3
KV_SLOTS = 2 * N_BRANCH
CMP_BLOCK = 32
CMP_STRIDE = 16
SEL_BLOCK = 64
SEL_TOPK = 16
WINDOW = 512
MIX_WIDTH = R_WIDTH + N_WIDTH
KV_WIDTH = KV_SLOTS * N_KV_HEADS * N_HEAD_DIM
PROJ_SPLITS = (R_SHIFT_WIDTH, R_WIDTH, N_WIDTH, N_WIDTH, KV_WIDTH, N_BRANCH * N_HEADS)
PROJ_OFFS = tuple(int(o) for o in np.cumsum((0,) + PROJ_SPLITS))
PROJ_WIDTH = sum(PROJ_SPLITS)
RMS_EPS = 1e-6
GN_EPS = 64e-5
MASK_NEG = -1e30
SEL_NEG = -(2.0 ** 100)
TAKEN = -3e38
SEL_BIAS = 1e4
ATTN_SCALE = N_HEAD_DIM ** -0.5
QSCALE = ATTN_SCALE * float(np.log2(np.e))

LANES = 128
BRANCH_W = 2 * N_KV_HEADS * N_HEAD_DIM
NSA_TQ = 256
NSA_KC = 512
PROJ_TM = 512
PROJ_SUB = 256
OUT_TM = 512
SAMPLE_PG_CMP = 16
SAMPLE_PG_ATT = 32
RWKV_TC = 32
RWKV_COLS = 16
VMEM_LIMIT = 56 * 1024 * 1024


def _split3(x):
    hi = x.astype(jnp.bfloat16)
    r1 = x - hi.astype(jnp.float32)
    mid = r1.astype(jnp.bfloat16)
    lo = (r1 - mid.astype(jnp.float32)).astype(jnp.bfloat16)
    return hi, mid, lo


def _dot(a, b):
    return jnp.dot(a, b, preferred_element_type=jnp.float32)


def _dot_nt(a, b):
    return lax.dot_general(a, b, (((1,), (1,)), ((), ())), preferred_element_type=jnp.float32)


def _dot_exact_rhs(a, b01):
    hi, mid, lo = _split3(a)
    return _dot(hi, b01) + _dot(mid, b01) + _dot(lo, b01)


def _dot_nt_x3(a, b):
    ah = a.astype(jnp.bfloat16)
    al = (a - ah.astype(jnp.float32)).astype(jnp.bfloat16)
    bh = b.astype(jnp.bfloat16)
    bl = (b - bh.astype(jnp.float32)).astype(jnp.bfloat16)
    return _dot_nt(ah, bh) + _dot_nt(ah, bl) + _dot_nt(al, bh)


def _dot_x3(a, b):
    ah = a.astype(jnp.bfloat16)
    al = (a - ah.astype(jnp.float32)).astype(jnp.bfloat16)
    bh = b.astype(jnp.bfloat16)
    bl = (b - bh.astype(jnp.float32)).astype(jnp.bfloat16)
    return _dot(ah, bh) + _dot(ah, bl) + _dot(al, bh)


def _silu(x):
    return x * jax.nn.sigmoid(x)


def _softplus(x):
    return jnp.maximum(x, 0.0) + jnp.log1p(jnp.exp(-jnp.abs(x)))


def _masked_softmax_rows(s, mask, base2=False):
    s = jnp.where(mask, s, MASK_NEG)
    e = (jnp.exp2 if base2 else jnp.exp)(s - jnp.max(s, axis=-1, keepdims=True))
    p = e / jnp.sum(e, axis=-1, keepdims=True)
    return jnp.where(mask, p, 0.0)


def _const_spec(shape, n_grid):
    zeros = (0,) * len(shape)
    if n_grid == 1:
        return pl.BlockSpec(shape, lambda i: zeros)
    return pl.BlockSpec(shape, lambda b, i: zeros)


def _rwkv_prep(zs, w0, a0, k_k, k_a, wdec_t, waaa_t):
    bf16 = jnp.bfloat16
    n = zs.shape[1]
    r = zs[0:R_WIDTH]
    k = zs[R_WIDTH:2 * R_WIDTH]
    v = zs[2 * R_WIDTH:3 * R_WIDTH]
    wd = zs[3 * R_WIDTH:3 * R_WIDTH + DECAY_LORA]
    ad = zs[3 * R_WIDTH + DECAY_LORA:R_SHIFT_WIDTH]
    w_log = -_softplus(-(w0 + _dot(wdec_t, jnp.tanh(wd).astype(bf16)))) - 0.5
    decay = jnp.exp(-jnp.exp(w_log))
    a = jax.nn.sigmoid(a0 + _dot(waaa_t, ad.astype(bf16)))
    kk = (k * k_k).reshape(R_HEADS, R_HEAD_DIM, n)
    kk = (kk * lax.rsqrt(jnp.maximum(jnp.sum(kk * kk, axis=1, keepdims=True), 1e-24))).reshape(R_WIDTH, n)
    k = k * (1.0 + (a - 1.0) * k_a)
    return r, decay, k, v, kk, kk * a


def _prompt_proj_body(x_ref, g_ref, wz_ref, wkv_ref, wn_ref, mu_ref, w0_ref, a0_ref, kk_ref, ka_ref, wdec_ref,
                      waaa_ref, r_o, w_o, k_o, v_o, kkn_o, b_o, kvcs_o, kvw_o, gr_o, q_o, gn_o, gl_o, zl_o, carry_ref):
    bf16 = jnp.bfloat16
    tm = x_ref.shape[1]
    sub = PROJ_SUB

    @pl.when(pl.program_id(1) == 0)
    def _():
        carry_ref[...] = jnp.zeros(carry_ref.shape, jnp.float32)

    first = lax.broadcasted_iota(jnp.int32, (1, sub), 1) == 0
    prev = carry_ref[:, LANES - 1:LANES]
    zs_all = []
    for h in range(tm // sub):
        rows = slice(h * sub, (h + 1) * sub)
        x = x_ref[0, rows, :]
        xn = (x * lax.rsqrt(jnp.mean(x * x, axis=-1, keepdims=True) + RMS_EPS) * g_ref[...]).astype(bf16)
        z = _dot_nt(wz_ref[...], xn)
        kv = _dot_nt(wkv_ref[...], xn)
        nrm = _dot(xn, wn_ref[...])
        kvcs_o[0, :, rows] = kv[0:2 * BRANCH_W]
        kvw_o[0, :, rows] = kv[2 * BRANCH_W:3 * BRANCH_W]
        gr_o[0, rows, :] = nrm[:, 0:R_WIDTH]
        q_o[0, rows, :] = nrm[:, R_WIDTH:R_WIDTH + N_WIDTH]
        gn_o[0, rows, :] = nrm[:, R_WIDTH + N_WIDTH:R_WIDTH + 2 * N_WIDTH]
        gl_o[0, rows, :] = nrm[:, R_WIDTH + 2 * N_WIDTH:]
        zp = jnp.where(first, prev, pltpu.roll(z, 1, axis=1))
        prev = z[:, sub - 1:sub]
        zs_all.append((rows, z, z + (zp - z) * mu_ref[...]))
    for rows, z, zs in zs_all:
        outs = _rwkv_prep(zs, w0_ref[...], a0_ref[...], kk_ref[...], ka_ref[...], wdec_ref[...], waaa_ref[...])
        for o_ref, val in zip((r_o, w_o, k_o, v_o, kkn_o, b_o), outs):
            o_ref[0, :, rows] = val
    z_last = zs_all[-1][1][:, sub - LANES:sub]
    carry_ref[...] = z_last
    zl_o[0] = z_last


def prompt_in_proj(x, norm_g, w_in, mu_shift, w0, w_decay_up, a0, w_aaa_up, k_k, k_a):
    B, T, D = x.shape
    tm = PROJ_TM
    assert T % tm == 0 and tm % PROJ_SUB == 0 and PROJ_SUB % LANES == 0
    f32 = jnp.float32
    pw = _proj_weights(norm_g, w_in, mu_shift, w0, w_decay_up, a0, w_aaa_up, k_k, k_a)
    consts = [pw["g"], pw["wz_t"], pw["wkv"].T, pw["wn"]] + pw["cols"] + pw["lora"]
    cm = lambda c: pl.BlockSpec((1, c, tm), lambda b, i: (b, 0, i))
    tok = lambda c: pl.BlockSpec((1, tm, c), lambda b, i: (b, i, 0))
    cms = lambda c: jax.ShapeDtypeStruct((B, c, T), f32)
    toks = lambda c: jax.ShapeDtypeStruct((B, T, c), f32)
    return pl.pallas_call(
        _prompt_proj_body,
        grid=(B, T // tm),
        in_specs=[tok(D)] + [_const_spec(c.shape, 2) for c in consts],
        out_specs=[cm(R_WIDTH)] * 6 + [cm(2 * BRANCH_W), cm(BRANCH_W), tok(R_WIDTH), tok(N_WIDTH), tok(N_WIDTH),
                                       tok(LANES), pl.BlockSpec((1, R_SHIFT_WIDTH, LANES), lambda b, i: (b, 0, 0))],
        out_shape=[cms(R_WIDTH)] * 6 + [cms(2 * BRANCH_W), cms(BRANCH_W), toks(R_WIDTH), toks(N_WIDTH), toks(N_WIDTH),
                                        toks(LANES), jax.ShapeDtypeStruct((B, R_SHIFT_WIDTH, LANES), f32)],
        scratch_shapes=[pltpu.VMEM((R_SHIFT_WIDTH, LANES), f32)],
        compiler_params=pltpu.CompilerParams(dimension_semantics=("parallel", "arbitrary"),
                                             vmem_limit_bytes=VMEM_LIMIT),
        name="prompt_in_proj",
    )(x, *consts)


def _proj_weights(norm_g, w_in, mu_shift, w0, w_decay_up, a0, w_aaa_up, k_k, k_a):
    f32, bf16 = jnp.float32, jnp.bfloat16
    D = w_in.shape[0]
    o = PROJ_OFFS
    wb = w_in.astype(bf16)
    gl_pad = LANES - PROJ_SPLITS[5]
    wn = jnp.concatenate([wb[:, o[1]:o[4]], wb[:, o[5]:o[6]], jnp.zeros((D, gl_pad), bf16)], axis=1)
    col = lambda p: p.astype(f32).reshape(-1, 1)
    return dict(g=norm_g.astype(f32).reshape(1, D), wz_t=wb[:, o[0]:o[1]].T, wkv=wb[:, o[4]:o[5]], wn=wn,
                cols=[col(mu_shift), col(w0), col(a0), col(k_k), col(k_a)],
                lora=[w_decay_up.astype(bf16).T, w_aaa_up.astype(bf16).T])


def _sample_proj_body(x_ref, prev_ref, g_ref, wz_ref, wkv_ref, wn_ref, mu_ref, w0_ref, a0_ref, kk_ref, ka_ref, wdec_ref,
                      waaa_ref, r_o, w_o, k_o, v_o, kkn_o, b_o, z_o, kv_o, gr_o, q_o, gn_o, gl_o, *, steps):
    bf16 = jnp.bfloat16
    m = x_ref.shape[0]
    x = x_ref[...]
    xn = (x * lax.rsqrt(jnp.mean(x * x, axis=-1, keepdims=True) + RMS_EPS) * g_ref[...]).astype(bf16)
    z = _dot_nt(wz_ref[...], xn)
    kv_o[...] = _dot(xn, wkv_ref[...])
    nrm = _dot(xn, wn_ref[...])
    gr_o[...] = nrm[:, 0:R_WIDTH]
    q_o[...] = nrm[:, R_WIDTH:R_WIDTH + N_WIDTH]
    gn_o[...] = nrm[:, R_WIDTH + N_WIDTH:R_WIDTH + 2 * N_WIDTH]
    gl_o[...] = nrm[:, R_WIDTH + 2 * N_WIDTH:]
    z_o[...] = z
    first = lax.broadcasted_iota(jnp.int32, (1, m), 1) % steps == 0
    zp = jnp.where(first, prev_ref[...], pltpu.roll(z, 1, axis=1))
    zs = z + (zp - z) * mu_ref[...]
    outs = _rwkv_prep(zs, w0_ref[...], a0_ref[...], kk_ref[...], ka_ref[...], wdec_ref[...], waaa_ref[...])
    for o_ref, val in zip((r_o, w_o, k_o, v_o, kkn_o, b_o), outs):
        o_ref[...] = val


def sample_in_proj(x, shift0, norm_g, w_in, mu_shift, w0, w_decay_up, a0, w_aaa_up, k_k, k_a):
    B, T, D = x.shape
    M = B * T
    f32 = jnp.float32
    pw = _proj_weights(norm_g, w_in, mu_shift, w0, w_decay_up, a0, w_aaa_up, k_k, k_a)
    prev = jnp.repeat(shift0.astype(f32).T, T, axis=1)
    args = [x.reshape(M, D), prev, pw["g"], pw["wz_t"], pw["wkv"], pw["wn"]] + pw["cols"] + pw["lora"]
    cm = lambda c: jax.ShapeDtypeStruct((c, M), f32)
    tok = lambda c: jax.ShapeDtypeStruct((M, c), f32)
    out_shape = [cm(R_WIDTH)] * 6 + [cm(R_SHIFT_WIDTH), tok(KV_WIDTH), tok(R_WIDTH), tok(N_WIDTH), tok(N_WIDTH),
                                     tok(LANES)]
    return pl.pallas_call(
        functools.partial(_sample_proj_body, steps=T),
        grid=(1,),
        in_specs=[_const_spec(a.shape, 1) for a in args],
        out_specs=[_const_spec(s.shape, 1) for s in out_shape],
        out_shape=out_shape,
        compiler_params=pltpu.CompilerParams(dimension_semantics=("arbitrary",), vmem_limit_bytes=VMEM_LIMIT),
        name="sample_in_proj",
    )(*args)


def _tree_sum(terms):
    while len(terms) > 1:
        terms = [terms[i] + terms[i + 1] for i in range(0, len(terms) - 1, 2)] + terms[len(terms) & ~1:]
    return terms[0]


def _rwkv_scan_body(r_ref, w_ref, k_ref, v_ref, kk_ref, b_ref, s0_ref, rk_ref, gw_ref, gb_ref, y_ref, s_ref):
    n = s_ref.shape[0]

    @pl.when(pl.program_id(1) == 0)
    def _():
        s_ref[...] = s0_ref[...]

    def step(t, carry):
        row = lambda ref, j: ref[t, pl.ds(j, 1), :]
        v = v_ref[t]
        grp = RWKV_COLS

        def p1(g, sa):
            return sa + _tree_sum([s_ref[g * grp + u] * row(kk_ref, g * grp + u) for u in range(grp)])

        sa = lax.fori_loop(0, n // grp, p1, jnp.zeros(v.shape, jnp.float32))

        def p2(g, y):
            terms = []
            for u in range(grp):
                j = g * grp + u
                s_j = s_ref[j] * row(w_ref, j) - sa * row(b_ref, j) + v * row(k_ref, j)
                s_ref[j] = s_j
                terms.append(s_j * row(r_ref, j))
            return y + _tree_sum(terms)

        y = lax.fori_loop(0, n // grp, p2, jnp.zeros(v.shape, jnp.float32))
        yc = y - jnp.mean(y, axis=0, keepdims=True)
        y = yc * lax.rsqrt(jnp.mean(yc * yc, axis=0, keepdims=True) + GN_EPS)
        y = y * gw_ref[...] + gb_ref[...]
        y_ref[t] = y + jnp.sum(r_ref[t] * k_ref[t] * rk_ref[...], axis=0, keepdims=True) * v
        return carry

    lax.fori_loop(0, r_ref.shape[0], step, 0)


def rwkv_scan_lanes(seqs, s0, r_k, gn_w, gn_b, n_batch):
    T, N, BH = seqs[0].shape
    assert BH % LANES == 0
    tc = min(RWKV_TC, T)
    assert T % tc == 0
    per_lane = lambda p: jnp.tile(p.astype(jnp.float32).reshape(BH // n_batch, N).T, (1, n_batch))
    seq = pl.BlockSpec((tc, N, LANES), lambda l, t: (t, 0, l))
    state = pl.BlockSpec((N, N, LANES), lambda l, t: (0, 0, l))
    param = pl.BlockSpec((N, LANES), lambda l, t: (0, l))
    return pl.pallas_call(
        _rwkv_scan_body,
        grid=(BH // LANES, T // tc),
        in_specs=[seq] * 6 + [state] + [param] * 3,
        out_specs=[seq, state],
        out_shape=[jax.ShapeDtypeStruct((T, N, BH), jnp.float32), jax.ShapeDtypeStruct((N, N, BH), jnp.float32)],
        compiler_params=pltpu.CompilerParams(dimension_semantics=("parallel", "arbitrary"),
                                             vmem_limit_bytes=VMEM_LIMIT),
        name="rwkv_scan",
    )(*seqs, s0, per_lane(r_k), per_lane(gn_w), per_lane(gn_b))


def _lanes_from_cm(a, T):
    return a.reshape(-1, R_HEAD_DIM, T).transpose(2, 1, 0)


def _y_from_lanes(y, B):
    T, N, _ = y.shape
    return y.reshape(T, N, B, R_HEADS).transpose(2, 0, 3, 1).reshape(B, T, R_WIDTH)


def _state_to_lanes(s):
    B, H, N, _ = s.shape
    return s.astype(jnp.float32).transpose(3, 2, 0, 1).reshape(N, N, B * H)


def _state_from_lanes(s, B):
    N = s.shape[0]
    return s.reshape(N, N, B, R_HEADS).transpose(2, 3, 1, 0)


def _cmp_weights(w_pos, w_mix, n_rows):
    wp = jnp.broadcast_to(w_pos.astype(jnp.float32).reshape(2, 1, CMP_BLOCK, N_HEAD_DIM).transpose(2, 0, 1, 3),
                          (CMP_BLOCK, 2, N_KV_HEADS, N_HEAD_DIM)).reshape(CMP_BLOCK, BRANCH_W)
    wa = jnp.tile(wp[:CMP_STRIDE], (n_rows // CMP_STRIDE, 1)).T
    wb = jnp.tile(wp[CMP_STRIDE:], (n_rows // CMP_STRIDE, 1)).T
    wmix = jnp.zeros((BRANCH_W, BRANCH_W), jnp.float32)
    for e in range(2):
        for kh in range(N_KV_HEADS):
            o = (e * N_KV_HEADS + kh) * N_HEAD_DIM
            wmix = wmix.at[o:o + N_HEAD_DIM, o:o + N_HEAD_DIM].set(w_mix[e].astype(jnp.float32))
    return wa, wb, wmix.T


def _compress_prompt_body(kv_ref, wa_ref, wb_ref, pool_ref, wmix_ref, o_ref):
    x = kv_ref[0]
    xw = jnp.concatenate([x * wa_ref[...], x * wb_ref[...]], axis=1)
    o_ref[0] = _dot_x3(wmix_ref[...], _dot_exact_rhs(xw, pool_ref[...]))


def compress_prompt(kv_cs, w_pos, w_mix):
    B, _, T = kv_cs.shape
    assert CMP_BLOCK == 2 * CMP_STRIDE and T % CMP_STRIDE == 0
    ncp = T // CMP_STRIDE
    wa, wb, wmix_t = _cmp_weights(w_pos, w_mix, T)
    chunk = np.arange(T) // CMP_STRIDE
    pool = np.concatenate([chunk[:, None] == np.arange(ncp)[None, :],
                           chunk[:, None] == np.arange(ncp)[None, :] + 1], axis=0)
    pool = jnp.asarray(pool, jnp.bfloat16)
    return pl.pallas_call(
        _compress_prompt_body,
        grid=(B,),
        in_specs=[pl.BlockSpec((1, BRANCH_W, T), lambda b: (b, 0, 0))] + [_const_spec(c.shape, 1)
                                                                          for c in (wa, wb, pool, wmix_t)],
        out_specs=pl.BlockSpec((1, BRANCH_W, ncp), lambda b: (b, 0, 0)),
        out_shape=jax.ShapeDtypeStruct((B, BRANCH_W, ncp), jnp.float32),
        compiler_params=pltpu.CompilerParams(dimension_semantics=("parallel",), vmem_limit_bytes=VMEM_LIMIT),
        name="compress_prompt",
    )(kv_cs, wa, wb, pool, wmix_t)


def _nsa_prompt_body(q_ref, ksel_ref, kwin_ref, kc_ref, gl_ref, gn_ref, ovt_ref, ex_ref, o_ref):
    f32, bf16 = jnp.float32, jnp.bfloat16
    TQ, KC, G = NSA_TQ, NSA_KC, N_GROUP
    ncp = kc_ref.shape[2]
    n_sel = ovt_ref.shape[0]
    topk = min(SEL_TOPK, n_sel)
    qs = pl.program_id(1) * TQ
    qpos = qs + lax.broadcasted_iota(jnp.int32, (TQ, 1), 0)
    lo_half = lax.broadcasted_iota(jnp.int32, (TQ, LANES), 1) < N_HEAD_DIM

    def q_padded(h, half):
        tile = q_ref[0, :, (h // 2) * LANES:(h // 2 + 1) * LANES] * QSCALE
        if h % 2 != half:
            tile = pltpu.roll(tile, N_HEAD_DIM, axis=1)
        return jnp.where(lo_half if half == 0 else jnp.logical_not(lo_half), tile, 0.0)

    wstart = pl.multiple_of(jnp.maximum(qs - WINDOW, 0), TQ)
    kw = kwin_ref[0, 0:LANES, pl.ds(wstart, WINDOW + TQ)].astype(bf16)
    vw = kwin_ref[0, LANES:2 * LANES, pl.ds(wstart, WINDOW + TQ)].astype(bf16)
    kposw = wstart + lax.broadcasted_iota(jnp.int32, (1, WINDOW + TQ), 1)
    mask_w = (kposw <= qpos) & (kposw > qpos - WINDOW)

    kck = kc_ref[0, 0:LANES, :]
    kcv = kc_ref[0, LANES:2 * LANES, :].astype(bf16)
    kend = lax.broadcasted_iota(jnp.int32, (1, ncp), 1) * CMP_STRIDE + (CMP_BLOCK - 1)
    mask_c = kend <= qpos

    sid_t = lax.broadcasted_iota(jnp.int32, (n_sel, 1), 0)
    cur_t = (qs + lax.broadcasted_iota(jnp.int32, (1, TQ), 1)) // SEL_BLOCK
    valid_t = sid_t <= cur_t
    forced_t = (sid_t == 0) | (sid_t == cur_t) | (sid_t == cur_t - 1)
    n_chunks = qs // KC + 1

    H = N_HEADS
    qg = jnp.concatenate([q_padded(h, h // G) for h in range(H)], axis=0)
    qb = qg.astype(bf16)

    p = _masked_softmax_rows(_dot_x3(qg, kck).reshape(H, TQ, ncp), mask_c[None], base2=True)
    o_c = _dot_nt(p.reshape(H * TQ, ncp).astype(bf16), kcv).reshape(H, TQ, LANES)

    ovt = ovt_ref[...]
    selb = []
    for kvh in range(N_KV_HEADS):
        hi, mid, lo = _split3(jnp.sum(p[kvh * G:(kvh + 1) * G], axis=0))
        score = _dot_nt(ovt, hi) + _dot_nt(ovt, mid) + _dot_nt(ovt, lo)
        score = jnp.where(valid_t, score + jnp.where(forced_t, SEL_BIAS, 0.0), -SEL_BIAS)
        rank = jnp.zeros((n_sel, TQ), jnp.int32)
        for s2 in range(n_sel):
            row = score[s2:s2 + 1, :]
            ahead = (row > score) | ((row == score) & (s2 < sid_t))
            rank = rank + ahead.astype(jnp.int32)
        selb_t = jnp.concatenate([jnp.where((rank < topk) & valid_t, 0.0, SEL_NEG),
                                  jnp.zeros((LANES - n_sel, TQ), f32)], axis=0)
        selb += [selb_t.T.astype(bf16)] * G
    qa = jnp.concatenate([qb, jnp.concatenate(selb, axis=0)], axis=1)

    def chunk(c, carry, causal):
        m, l, acc = carry
        k0 = pl.multiple_of(c * KC, KC)
        ka = jnp.concatenate([ksel_ref[0, 0:LANES, pl.ds(k0, KC)].astype(bf16), ex_ref[:, pl.ds(k0, KC)]],
                             axis=0)
        va = jnp.concatenate([ksel_ref[0, LANES:2 * LANES, pl.ds(k0, KC)].astype(bf16), ones_kc], axis=0)
        sc = _dot(qa, ka).reshape(H, TQ, KC)
        if causal:
            kpos = k0 + lax.broadcasted_iota(jnp.int32, (1, KC), 1)
            sc = jnp.where((kpos <= qpos)[None], sc, SEL_NEG)
        m_new = jnp.maximum(m, jnp.max(sc, axis=-1, keepdims=True))
        alpha = jnp.exp2(m - m_new)
        e = jnp.exp2(sc - m_new).astype(bf16)
        pv = _dot_nt(e.reshape(H * TQ, KC), va).reshape(H, TQ, 2 * LANES)
        return m_new, alpha * l + pv[:, :, LANES:], alpha * acc + pv[:, :, :LANES]

    ones_kc = jnp.ones((LANES, KC), bf16)
    carry = (jnp.full((H, TQ, 1), SEL_NEG, f32), jnp.zeros((H, TQ, LANES), f32), jnp.zeros((H, TQ, LANES), f32))
    carry = lax.fori_loop(0, n_chunks - 1, functools.partial(chunk, causal=False), carry)
    _, l, acc = chunk(n_chunks - 1, carry, causal=True)
    o_s = acc / l

    sw = jnp.where(mask_w[None], _dot(qb, kw).reshape(H, TQ, WINDOW + TQ), MASK_NEG)
    ew = jnp.exp2(sw - jnp.max(sw, axis=-1, keepdims=True)).astype(bf16)
    pw = _dot_nt(ew.reshape(H * TQ, WINDOW + TQ),
                 jnp.concatenate([vw, jnp.ones((LANES, WINDOW + TQ), bf16)], axis=0)).reshape(H, TQ, 2 * LANES)
    o_w = pw[:, :, :LANES] / pw[:, :, LANES:]

    gates = jax.nn.sigmoid(gl_ref[0])
    heads = []
    for h in range(H):
        o = (gates[:, h:h + 1] * o_c[h] + gates[:, H + h:H + h + 1] * o_s[h]
             + gates[:, 2 * H + h:2 * H + h + 1] * o_w[h])
        heads.append(pltpu.roll(o, N_HEAD_DIM, axis=1) if h % 2 != h // G else o)

    for m in range(N_HEADS // 2):
        tile = jnp.where(lo_half, heads[2 * m], heads[2 * m + 1])
        o_ref[0, :, m * LANES:(m + 1) * LANES] = (
            tile * _silu(gn_ref[0, :, m * LANES:(m + 1) * LANES])).astype(o_ref.dtype)


def nsa_prompt(q, kv_cs, kv_w, kc, gl, gate_n):
    B, T, _ = q.shape
    TQ, KC = NSA_TQ, NSA_KC
    assert N_KV_HEADS == 2 and N_HEAD_DIM * 2 == LANES and T % KC == 0 and KC % TQ == 0 and T >= WINDOW + TQ
    assert SEL_BLOCK % CMP_STRIDE == 0 and WINDOW % TQ == 0 and TQ % LANES == 0 and T // SEL_BLOCK <= LANES
    ncp = T // CMP_STRIDE
    n_sel = T // SEL_BLOCK
    cs = np.arange(ncp) * CMP_STRIDE
    ss = np.arange(n_sel) * SEL_BLOCK
    overlap = (cs[:, None] < ss[None, :] + SEL_BLOCK) & (cs[:, None] + CMP_BLOCK > ss[None, :])
    overlap[ncp - 1] = False
    overlap_t = jnp.asarray(overlap.T, jnp.bfloat16)
    expand = jnp.asarray(np.arange(T)[None, :] // SEL_BLOCK == np.arange(LANES)[:, None], jnp.bfloat16)
    return pl.pallas_call(
        _nsa_prompt_body,
        grid=(B, T // TQ),
        in_specs=[pl.BlockSpec((1, TQ, N_WIDTH), lambda b, i: (b, i, 0)),
                  pl.BlockSpec((1, BRANCH_W, T), lambda b, i: (b, 1, 0)),
                  pl.BlockSpec((1, BRANCH_W, T), lambda b, i: (b, 0, 0)),
                  pl.BlockSpec((1, BRANCH_W, ncp), lambda b, i: (b, 0, 0)),
                  pl.BlockSpec((1, TQ, LANES), lambda b, i: (b, i, 0)),
                  pl.BlockSpec((1, TQ, N_WIDTH), lambda b, i: (b, i, 0)),
                  _const_spec((n_sel, ncp), 2), _const_spec((LANES, T), 2)],
        out_specs=pl.BlockSpec((1, TQ, N_WIDTH), lambda b, i: (b, i, 0)),
        out_shape=jax.ShapeDtypeStruct((B, T, N_WIDTH), jnp.bfloat16),
        compiler_params=pltpu.CompilerParams(dimension_semantics=("parallel", "arbitrary"),
                                             vmem_limit_bytes=VMEM_LIMIT),
        name="nsa_prompt",
    )(q, kv_cs, kv_w, kc, gl, gate_n, overlap_t, expand)


def _page_spec(k, branch, pages_per_step):
    return pl.BlockSpec((1, BRANCH_W, PAGE_SIZE), lambda b, g, pt: (pt[b, g * pages_per_step + k], branch, 0))


def _sample_compress_body(pt_ref, *refs):
    pages = refs[:SAMPLE_PG_CMP]
    wa_ref, wb_ref, wmix_ref, oa_ref, ob_ref = refs[SAMPLE_PG_CMP:]
    cpp = PAGE_SIZE // CMP_STRIDE
    xt = [p[0].T.reshape(cpp, CMP_STRIDE, BRANCH_W) for p in pages]
    for w_ref, o_ref in ((wa_ref, oa_ref), (wb_ref, ob_ref)):
        pooled = jnp.concatenate([jnp.sum(x * w_ref[...], axis=1) for x in xt], axis=0)
        o_ref[0] = _dot_x3(pooled, wmix_ref[...])


def sample_compress(cache_t, page_table, w_pos, w_mix):
    B, n_pages = page_table.shape
    PG = SAMPLE_PG_CMP
    assert n_pages % PG == 0 and CMP_BLOCK == 2 * CMP_STRIDE and PAGE_SIZE % CMP_STRIDE == 0
    rows = PG * PAGE_SIZE // CMP_STRIDE
    ncp = n_pages * PAGE_SIZE // CMP_STRIDE
    wa, wb, wmix_t = _cmp_weights(w_pos, w_mix, CMP_STRIDE)
    wa, wb, wmix = wa.T[None], wb.T[None], wmix_t.T
    const = lambda shape: pl.BlockSpec(shape, lambda b, g, pt: (0,) * len(shape))
    out = pl.BlockSpec((1, rows, BRANCH_W), lambda b, g, pt: (b, g, 0))
    return pl.pallas_call(
        _sample_compress_body,
        grid_spec=pltpu.PrefetchScalarGridSpec(
            num_scalar_prefetch=1, grid=(B, n_pages // PG),
            in_specs=[_page_spec(k, 0, PG) for k in range(PG)] + [const(wa.shape), const(wb.shape), const(wmix.shape)],
            out_specs=[out, out]),
        out_shape=[jax.ShapeDtypeStruct((B, ncp, BRANCH_W), jnp.float32)] * 2,
        compiler_params=pltpu.CompilerParams(dimension_semantics=("parallel", "arbitrary"),
                                             vmem_limit_bytes=VMEM_LIMIT),
        name="sample_compress",
    )(page_table, *([cache_t] * PG), wa, wb, wmix)


def _online_chunk(m, l, acc, s, keep, v, v_channels_major=False):
    s = jnp.where(keep, s, MASK_NEG)
    m_new = jnp.maximum(m, jnp.max(s, axis=-1, keepdims=True))
    alpha = jnp.exp(m - m_new)
    e = jnp.where(keep, jnp.exp(s - m_new), 0.0)
    eb = e.astype(jnp.bfloat16)
    pv = _dot_nt(eb, v) if v_channels_major else _dot(eb, v)
    return m_new, alpha * l + jnp.sum(e, axis=-1, keepdims=True), alpha * acc + pv


def _sample_attend_body(pt_ref, *refs, past):
    f32, bf16 = jnp.float32, jnp.bfloat16
    PG, G = SAMPLE_PG_ATT, N_GROUP
    pages = refs[:PG]
    (ka_ref, kb_ref, q_ref, kvn_ref, gl_ref, gn_ref, win_ref, ex_ref, ov_ref, o_ref,
     qf_ref, selb_ref, tail_ref, m_ref, l_ref, acc_ref, oc_ref) = refs[PG:]
    T = q_ref.shape[1]
    R = N_HEADS * T
    ncp = ka_ref.shape[1]
    nsp = ov_ref.shape[1]
    KC = PG * PAGE_SIZE
    n_past = past // SEL_BLOCK
    bpc = KC // SEL_BLOCK
    g = pl.program_id(1)
    lo_half = lax.broadcasted_iota(jnp.int32, (T, LANES), 1) < N_HEAD_DIM
    row = lax.broadcasted_iota(jnp.int32, (R, 1), 0)
    qpos = past + row % T

    @pl.when(g == 0)
    def _():
        pieces = []
        for h in range(N_HEADS):
            kvh = h // G
            tile = q_ref[0, :, (h // 2) * LANES:(h // 2 + 1) * LANES] * ATTN_SCALE
            if h % 2 != kvh:
                tile = pltpu.roll(tile, N_HEAD_DIM, axis=1)
            pieces.append(jnp.where(lo_half if kvh == 0 else jnp.logical_not(lo_half), tile, 0.0))
        q32 = jnp.concatenate(pieces, axis=0)
        qf_ref[...] = q32

        kc = ka_ref[0] + pltpu.roll(kb_ref[0], ncp - 1, axis=0)
        kend = lax.broadcasted_iota(jnp.int32, (1, ncp), 1) * CMP_STRIDE + (CMP_BLOCK - 1)
        p = _masked_softmax_rows(_dot_nt_x3(q32, kc[:, 0:LANES]), kend <= qpos)
        oc_ref[...] = _dot(p.astype(bf16), kc[:, LANES:2 * LANES].astype(bf16))
        imp = jnp.concatenate(
            [sum(p[(kvh * G + gg) * T:(kvh * G + gg + 1) * T] for gg in range(G)) for kvh in range(N_KV_HEADS)],
            axis=0)

        sid = lax.broadcasted_iota(jnp.int32, (1, nsp), 1)
        sid_f = sid.astype(f32)
        cur = (past + lax.broadcasted_iota(jnp.int32, (N_KV_HEADS * T, 1), 0) % T) // SEL_BLOCK
        valid = sid <= cur
        forced = (sid == 0) | (sid == cur) | (sid == cur - 1)
        score = _dot_exact_rhs(imp, ov_ref[...])
        score = jnp.where(valid, score + jnp.where(forced, SEL_BIAS, 0.0), -SEL_BIAS)

        def extract(_, carry):
            sc, chosen = carry
            best = jnp.max(sc, axis=-1, keepdims=True)
            first = jnp.min(jnp.where(sc == best, sid_f, float(nsp)), axis=-1, keepdims=True)
            hit = sid_f == first
            return jnp.where(hit, TAKEN, sc), jnp.where(hit, 1.0, chosen)

        _, chosen = lax.fori_loop(0, min(SEL_TOPK, n_past + 1), extract, (score, jnp.zeros_like(score)))
        sel = jnp.concatenate([jnp.where(valid, chosen, 0.0)[(h // G) * T:(h // G + 1) * T] for h in range(N_HEADS)],
                              axis=0)
        tail_ref[...] = sel[:, n_past:n_past + 1]
        selb = jnp.where(sel > 0.5, 0.0, SEL_NEG)
        for c in range(n_past // bpc):
            selb_ref[c] = jnp.concatenate([selb[:, c * bpc:(c + 1) * bpc], jnp.zeros((R, LANES - bpc), f32)],
                                          axis=1).astype(bf16)
        m_ref[...] = jnp.full(m_ref.shape, SEL_NEG, f32)
        l_ref[...] = jnp.zeros(l_ref.shape, f32)
        acc_ref[...] = jnp.zeros(acc_ref.shape, f32)

    qb = qf_ref[...].astype(bf16)
    ka = jnp.concatenate([p[0, 0:LANES, :] for p in pages], axis=1).astype(bf16)
    ka = jnp.concatenate([ka, ex_ref[...]], axis=0)
    va = jnp.concatenate([jnp.concatenate([p[0, LANES:2 * LANES, :] for p in pages], axis=1).astype(bf16),
                          jnp.ones((LANES, KC), bf16)], axis=0)
    s = _dot(jnp.concatenate([qb, selb_ref[g]], axis=1), ka)
    m = jnp.maximum(m_ref[...], jnp.max(s, axis=-1, keepdims=True))
    alpha = jnp.exp(m_ref[...] - m)
    pv = _dot_nt(jnp.exp(s - m).astype(bf16), va)
    l = alpha * l_ref[...] + pv[:, LANES:]
    acc = alpha * acc_ref[...] + pv[:, :LANES]
    m_ref[...] = m
    l_ref[...] = l
    acc_ref[...] = acc

    @pl.when(g == pl.num_programs(1) - 1)
    def _():
        def new_rows(c0):
            return jnp.concatenate([kvn_ref[0, :, c0:c0 + LANES], jnp.zeros((LANES - T, LANES), f32)],
                                   axis=0).astype(bf16)

        npos = past + lax.broadcasted_iota(jnp.int32, (1, LANES), 1)
        is_new = npos < past + T
        keep_t = is_new & (npos <= qpos) & (tail_ref[...] > 0.5)
        _, l_s, acc_s = _online_chunk(m, l, acc, _dot_nt(qb, new_rows(BRANCH_W)), keep_t, new_rows(BRANCH_W + LANES))
        o_s = acc_s / l_s

        nb = win_ref.shape[2]
        bpos = past - nb + lax.broadcasted_iota(jnp.int32, (1, nb), 1)
        keep_b = (bpos <= qpos) & (bpos > qpos - WINDOW) & (bpos >= 0)
        mw, lw, aw = _online_chunk(jnp.full((R, 1), MASK_NEG, f32), jnp.zeros((R, 1), f32), jnp.zeros((R, LANES), f32),
                                   _dot(qb, win_ref[0, 0:LANES, :].astype(bf16)), keep_b,
                                   win_ref[0, LANES:2 * LANES, :].astype(bf16), True)
        keep_n = is_new & (npos <= qpos) & (npos > qpos - WINDOW)
        _, lw, aw = _online_chunk(mw, lw, aw, _dot_nt(qb, new_rows(2 * BRANCH_W)), keep_n,
                                  new_rows(2 * BRANCH_W + LANES))
        o_w = aw / lw

        gates = jax.nn.sigmoid(gl_ref[0])
        gcol = lambda br: jnp.concatenate([gates[:, br * N_HEADS + h:br * N_HEADS + h + 1] for h in range(N_HEADS)],
                                          axis=0)
        o = gcol(0) * oc_ref[...] + gcol(1) * o_s + gcol(2) * o_w
        for mt in range(N_HEADS // 2):
            halves = []
            for h in (2 * mt, 2 * mt + 1):
                piece = o[h * T:(h + 1) * T]
                if h % 2 != h // G:
                    piece = pltpu.roll(piece, N_HEAD_DIM, axis=1)
                halves.append(piece)
            o_ref[0, :, mt * LANES:(mt + 1) * LANES] = (jnp.where(lo_half, halves[0], halves[1])
                                                        * _silu(gn_ref[0, :, mt * LANES:(mt + 1) * LANES]))


def nsa_sample(q, kv_new, kc_a, kc_b, gl, gate_n, cache_t, page_table, win_t):
    B, T, _ = q.shape
    n_pages = page_table.shape[1]
    past = n_pages * PAGE_SIZE
    PG = SAMPLE_PG_ATT
    KC = PG * PAGE_SIZE
    ng = n_pages // PG
    assert n_pages % PG == 0 and past % SEL_BLOCK == 0 and T <= SEL_BLOCK and T <= LANES
    assert (past + T) // CMP_STRIDE == past // CMP_STRIDE and KC % SEL_BLOCK == 0
    ncp = past // CMP_STRIDE
    n_sel = past // SEL_BLOCK + 1
    nsp = -(-n_sel // LANES) * LANES
    cs = np.arange(ncp) * CMP_STRIDE
    ss = np.arange(nsp) * SEL_BLOCK
    overlap = (cs[:, None] < ss[None, :] + SEL_BLOCK) & (cs[:, None] + CMP_BLOCK > ss[None, :])
    overlap[ncp - 1] = False
    overlap[:, n_sel:] = False
    overlap = jnp.asarray(overlap, jnp.bfloat16)
    assert KC // SEL_BLOCK <= LANES
    expand = jnp.asarray(np.arange(KC)[None, :] // SEL_BLOCK == np.arange(LANES)[:, None], jnp.bfloat16)
    R = N_HEADS * T
    per_b = lambda shape: pl.BlockSpec((1,) + shape, lambda b, g, pt: (b, 0, 0))
    return pl.pallas_call(
        functools.partial(_sample_attend_body, past=past),
        grid_spec=pltpu.PrefetchScalarGridSpec(
            num_scalar_prefetch=1, grid=(B, ng),
            in_specs=[_page_spec(k, 1, PG) for k in range(PG)] + [
                per_b((ncp, BRANCH_W)), per_b((ncp, BRANCH_W)), per_b((T, N_WIDTH)), per_b((T, KV_WIDTH)),
                per_b((T, LANES)), per_b((T, N_WIDTH)), per_b((BRANCH_W, win_t.shape[2])),
                pl.BlockSpec((LANES, KC), lambda b, g, pt: (0, 0)),
                pl.BlockSpec((ncp, nsp), lambda b, g, pt: (0, 0))],
            out_specs=per_b((T, N_WIDTH)),
            scratch_shapes=[pltpu.VMEM((R, LANES), jnp.float32), pltpu.VMEM((ng, R, LANES), jnp.bfloat16),
                            pltpu.VMEM((R, 1), jnp.float32), pltpu.VMEM((R, 1), jnp.float32),
                            pltpu.VMEM((R, LANES), jnp.float32), pltpu.VMEM((R, LANES), jnp.float32),
                            pltpu.VMEM((R, LANES), jnp.float32)]),
        out_shape=jax.ShapeDtypeStruct((B, T, N_WIDTH), jnp.float32),
        compiler_params=pltpu.CompilerParams(dimension_semantics=("parallel", "arbitrary"),
                                             vmem_limit_bytes=VMEM_LIMIT),
        name="nsa_sample",
    )(page_table, *([cache_t] * PG), kc_a, kc_b, q, kv_new, gl, gate_n, win_t, expand, overlap)


def _out_proj_body(yr_ref, gr_ref, on_ref, x_ref, w_ref, g_ref, o_ref, *, final_norm):
    bf16 = jnp.bfloat16
    a = (yr_ref[...] * _silu(gr_ref[...])).astype(bf16)
    y = x_ref[...] + _dot(a, w_ref[0:R_WIDTH, :]) + _dot(on_ref[...].astype(bf16), w_ref[R_WIDTH:MIX_WIDTH, :])
    if final_norm:
        y = y * lax.rsqrt(jnp.mean(y * y, axis=-1, keepdims=True) + RMS_EPS) * g_ref[...]
    o_ref[...] = y


def out_proj(y_r, gate_r, o_n, x, w_out, final_g):
    M, D = x.shape
    tm = min(OUT_TM, M)
    assert M % tm == 0
    g = (jnp.ones((D,), jnp.float32) if final_g is None else final_g.astype(jnp.float32)).reshape(1, D)
    rows = lambda c: pl.BlockSpec((tm, c), lambda i: (i, 0))
    return pl.pallas_call(
        functools.partial(_out_proj_body, final_norm=final_g is not None),
        grid=(M // tm,),
        in_specs=[rows(R_WIDTH), rows(R_WIDTH), rows(N_WIDTH), rows(D), _const_spec((MIX_WIDTH, D), 1),
                  _const_spec((1, D), 1)],
        out_specs=rows(D),
        out_shape=jax.ShapeDtypeStruct((M, D), jnp.float32),
        compiler_params=pltpu.CompilerParams(dimension_semantics=("parallel",), vmem_limit_bytes=VMEM_LIMIT),
        name="out_proj",
    )(y_r, gate_r, o_n, x, w_out.astype(jnp.bfloat16), g)


def prompt_mixers(x, norm_g, w_in, mu_shift, w0, w_decay_up, a0, w_aaa_up, k_k, k_a, r_k, gn_w, gn_b,
                  w_cmp_pos, w_cmp_mix):
    B, T, D = x.shape
    (r, w, k, v, kk, b, kv_cs, kv_w, gate_r, q, gate_n, gl, z_last) = prompt_in_proj(
        x, norm_g, w_in, mu_shift, w0, w_decay_up, a0, w_aaa_up, k_k, k_a)
    seqs = [_lanes_from_cm(a, T) for a in (r, w, k, v, kk, b)]
    o_n = nsa_prompt(q, kv_cs, kv_w, compress_prompt(kv_cs, w_cmp_pos, w_cmp_mix), gl, gate_n)
    seqs, o_n = lax.optimization_barrier((seqs, o_n))
    s0 = jnp.zeros((R_HEAD_DIM, R_HEAD_DIM, B * R_HEADS), jnp.float32)
    y, s_fin = rwkv_scan_lanes(seqs, s0, r_k, gn_w, gn_b, B)
    heads = (N_KV_HEADS, N_HEAD_DIM)
    kv_rows = kv_cs.reshape((B, 4) + heads + (T,)).transpose(0, 4, 1, 2, 3)
    n_keep = min(WINDOW, T)
    win_rows = kv_w[:, :, T - n_keep:].reshape((B, 2) + heads + (n_keep,)).transpose(0, 4, 1, 2, 3)
    outs = (kv_rows, win_rows, _state_from_lanes(s_fin, B), z_last[:, :, LANES - 1])
    return (_y_from_lanes(y, B), gate_r, o_n), outs


def sample_mixers(x, cache, win_buf, wkv0, shift0, page_table, norm_g, w_in, mu_shift, w0, w_decay_up, a0,
                  w_aaa_up, k_k, k_a, r_k, gn_w, gn_b, w_cmp_pos, w_cmp_mix):
    B, T, D = x.shape
    (r, w, k, v, kk, b, z, kv, gate_r, q, gate_n, gl) = sample_in_proj(
        x, shift0, norm_g, w_in, mu_shift, w0, w_decay_up, a0, w_aaa_up, k_k, k_a)
    seqs = [a.reshape(R_HEADS, R_HEAD_DIM, B, T).transpose(3, 1, 2, 0).reshape(T, R_HEAD_DIM, B * R_HEADS)
            for a in (r, w, k, v, kk, b)]
    y, s_fin = rwkv_scan_lanes(seqs, _state_to_lanes(wkv0), r_k, gn_w, gn_b, B)

    cache_t = cache.reshape(cache.shape[0], PAGE_SIZE, 2 * BRANCH_W).transpose(0, 2, 1)
    nb = win_buf.shape[1]
    win_t = win_buf.reshape(B, nb, BRANCH_W).transpose(0, 2, 1)
    kc_a, kc_b = sample_compress(cache_t, page_table, w_cmp_pos, w_cmp_mix)
    kv = kv.reshape(B, T, KV_WIDTH)
    o_n = nsa_sample(q.reshape(B, T, N_WIDTH), kv, kc_a, kc_b, gl.reshape(B, T, LANES),
                     gate_n.reshape(B, T, N_WIDTH), cache_t, page_table, win_t)
    kv = kv.reshape(B, T, KV_SLOTS, N_KV_HEADS, N_HEAD_DIM)
    keys_w = jnp.concatenate([win_buf.astype(kv.dtype), kv[:, :, 4:6]], axis=1)
    n_keep = min(WINDOW, nb + T)
    shift = z.reshape(R_SHIFT_WIDTH, B, T)[:, :, T - 1].T
    outs = (kv[:, :, 0:4], keys_w[:, nb + T - n_keep:], _state_from_lanes(s_fin, B), shift)
    return (_y_from_lanes(y, B), gate_r.reshape(B, T, R_WIDTH), o_n), outs


def _finish(mixed, x, w_out, final_g):
    B, T, D = x.shape
    y_r, gate_r, o_n = (a.reshape(B * T, -1) for a in mixed)
    return out_proj(y_r, gate_r, o_n, x.reshape(B * T, D), w_out, final_g).reshape(B, T, D)


def kernel(x_prompt, x_sample, cache_kv, cache_kv_win, state_wkv, state_shift, page_table,
           norm_g, w_in, mu_shift, w0, w_decay_up, a0, w_aaa_up, k_k, k_a, r_k, gn_w, gn_b,
           w_cmp_pos, w_cmp_mix, w_out, final_g):
    y_p, y_s = x_prompt, x_sample
    kvp, kvs, wnp, wns, skp, sks, shp, shs = [], [], [], [], [], [], [], []
    for layer in range(DEPTH):
        lw = (norm_g[layer], w_in[layer], mu_shift[layer], w0[layer], w_decay_up[layer], a0[layer],
              w_aaa_up[layer], k_k[layer], k_a[layer], r_k[layer], gn_w[layer], gn_b[layer],
              w_cmp_pos[layer], w_cmp_mix[layer])
        fg = final_g if layer == DEPTH - 1 else None
        mixed_p, (p_kv, p_win, p_wkv, p_shift) = prompt_mixers(y_p, *lw)
        mixed_s, (s_kv, s_win, s_wkv, s_shift) = sample_mixers(y_s, cache_kv[layer], cache_kv_win[layer],
                                                               state_wkv[layer], state_shift[layer], page_table, *lw)
        mixed_p, mixed_s = lax.optimization_barrier((mixed_p, mixed_s))
        y_p = _finish(mixed_p, y_p, w_out[layer], fg)
        y_s = _finish(mixed_s, y_s, w_out[layer], fg)
        kvp.append(p_kv); kvs.append(s_kv); wnp.append(p_win); wns.append(s_win)
        skp.append(p_wkv); sks.append(s_wkv); shp.append(p_shift); shs.append(s_shift)
    return (y_p, y_s, jnp.stack(kvp), jnp.stack(kvs), jnp.stack(wnp), jnp.stack(wns),
            jnp.stack(skp), jnp.stack(sks), jnp.stack(shp), jnp.stack(shs))
```

```python
import functools

import jax, jax.numpy as jnp
from jax import lax
import numpy as np
from jax.experimental import pallas as pl
from jax.experimental.pallas import tpu as pltpu

D_MODEL = 1024
BATCH = 16
SEQ = 2048
DEPTH = 1
DEC_BATCH = 32
DEC_SEQ = 4
PAST_LEN = 16384
PAGE_SIZE = 128

R_HEAD_DIM = 64
R_WIDTH = D_MODEL // 2
R_HEADS = R_WIDTH // R_HEAD_DIM
DECAY_LORA = 64
AAA_LORA = 64
R_SHIFT_WIDTH = 3 * R_WIDTH + DECAY_LORA + AAA_LORA
N_HEAD_DIM = 64
N_WIDTH = D_MODEL - R_WIDTH
N_HEADS = N_WIDTH // N_HEAD_DIM
N_KV_HEADS = 2
N_GROUP = N_HEADS // N_KV_HEADS
N_BRANCH = 3
KV_SLOTS = 2 * N_BRANCH
CMP_BLOCK = 32
CMP_STRIDE = 16
SEL_BLOCK = 64
SEL_TOPK = 16
WINDOW = 512
MIX_WIDTH = R_WIDTH + N_WIDTH
KV_WIDTH = KV_SLOTS * N_KV_HEADS * N_HEAD_DIM
PROJ_SPLITS = (R_SHIFT_WIDTH, R_WIDTH, N_WIDTH, N_WIDTH, KV_WIDTH, N_BRANCH * N_HEADS)
PROJ_OFFS = tuple(int(o) for o in np.cumsum((0,) + PROJ_SPLITS))
PROJ_WIDTH = sum(PROJ_SPLITS)
RMS_EPS = 1e-6
GN_EPS = 64e-5
MASK_NEG = -1e30
SEL_NEG = -(2.0 ** 100)
TAKEN = -3e38
SEL_BIAS = 1e4
ATTN_SCALE = N_HEAD_DIM ** -0.5
QSCALE = ATTN_SCALE * float(np.log2(np.e))

LANES = 128
BRANCH_W = 2 * N_KV_HEADS * N_HEAD_DIM
NSA_TQ = 256
NSA_KC = 512
PROJ_TM = 512
PROJ_SUB = 256
OUT_TM = 512
SAMPLE_PG_CMP = 16
SAMPLE_PG_ATT = 32
RWKV_TC = 32
RWKV_COLS = 16
VMEM_LIMIT = 56 * 1024 * 1024


def _split3(x):
    hi = x.astype(jnp.bfloat16)
    r1 = x - hi.astype(jnp.float32)
    mid = r1.astype(jnp.bfloat16)
    lo = (r1 - mid.astype(jnp.float32)).astype(jnp.bfloat16)
    return hi, mid, lo


def _dot(a, b):
    return jnp.dot(a, b, preferred_element_type=jnp.float32)


def _dot_nt(a, b):
    return lax.dot_general(a, b, (((1,), (1,)), ((), ())), preferred_element_type=jnp.float32)


def _dot_exact_rhs(a, b01):
    hi, mid, lo = _split3(a)
    return _dot(hi, b01) + _dot(mid, b01) + _dot(lo, b01)


def _dot_nt_x3(a, b):
    ah = a.astype(jnp.bfloat16)
    al = (a - ah.astype(jnp.float32)).astype(jnp.bfloat16)
    bh = b.astype(jnp.bfloat16)
    bl = (b - bh.astype(jnp.float32)).astype(jnp.bfloat16)
    return _dot_nt(ah, bh) + _dot_nt(ah, bl) + _dot_nt(al, bh)


def _dot_x3(a, b):
    ah = a.astype(jnp.bfloat16)
    al = (a - ah.astype(jnp.float32)).astype(jnp.bfloat16)
    bh = b.astype(jnp.bfloat16)
    bl = (b - bh.astype(jnp.float32)).astype(jnp.bfloat16)
    return _dot(ah, bh) + _dot(ah, bl) + _dot(al, bh)


def _silu(x):
    return x * jax.nn.sigmoid(x)


def _softplus(x):
    return jnp.maximum(x, 0.0) + jnp.log1p(jnp.exp(-jnp.abs(x)))


def _masked_softmax_rows(s, mask, base2=False):
    s = jnp.where(mask, s, MASK_NEG)
    e = (jnp.exp2 if base2 else jnp.exp)(s - jnp.max(s, axis=-1, keepdims=True))
    p = e / jnp.sum(e, axis=-1, keepdims=True)
    return jnp.where(mask, p, 0.0)


def _const_spec(shape, n_grid):
    zeros = (0,) * len(shape)
    if n_grid == 1:
        return pl.BlockSpec(shape, lambda i: zeros)
    return pl.BlockSpec(shape, lambda b, i: zeros)


def _rwkv_prep(zs, w0, a0, k_k, k_a, wdec_t, waaa_t):
    bf16 = jnp.bfloat16
    n = zs.shape[1]
    r = zs[0:R_WIDTH]
    k = zs[R_WIDTH:2 * R_WIDTH]
    v = zs[2 * R_WIDTH:3 * R_WIDTH]
    wd = zs[3 * R_WIDTH:3 * R_WIDTH + DECAY_LORA]
    ad = zs[3 * R_WIDTH + DECAY_LORA:R_SHIFT_WIDTH]
    w_log = -_softplus(-(w0 + _dot(wdec_t, jnp.tanh(wd).astype(bf16)))) - 0.5
    decay = jnp.exp(-jnp.exp(w_log))
    a = jax.nn.sigmoid(a0 + _dot(waaa_t, ad.astype(bf16)))
    kk = (k * k_k).reshape(R_HEADS, R_HEAD_DIM, n)
    kk = (kk * lax.rsqrt(jnp.maximum(jnp.sum(kk * kk, axis=1, keepdims=True), 1e-24))).reshape(R_WIDTH, n)
    k = k * (1.0 + (a - 1.0) * k_a)
    return r, decay, k, v, kk, kk * a


def _prompt_proj_body(x_ref, g_ref, wz_ref, wkv_ref, wn_ref, mu_ref, w0_ref, a0_ref, kk_ref, ka_ref, wdec_ref,
                      waaa_ref, r_o, w_o, k_o, v_o, kkn_o, b_o, kvcs_o, kvw_o, gr_o, q_o, gn_o, gl_o, zl_o, carry_ref):
    bf16 = jnp.bfloat16
    tm = x_ref.shape[1]
    sub = PROJ_SUB

    @pl.when(pl.program_id(1) == 0)
    def _():
        carry_ref[...] = jnp.zeros(carry_ref.shape, jnp.float32)

    first = lax.broadcasted_iota(jnp.int32, (1, sub), 1) == 0
    prev = carry_ref[:, LANES - 1:LANES]
    zs_all = []
    for h in range(tm // sub):
        rows = slice(h * sub, (h + 1) * sub)
        x = x_ref[0, rows, :]
        xn = (x * lax.rsqrt(jnp.mean(x * x, axis=-1, keepdims=True) + RMS_EPS) * g_ref[...]).astype(bf16)
        z = _dot_nt(wz_ref[...], xn)
        kv = _dot_nt(wkv_ref[...], xn)
        nrm = _dot(xn, wn_ref[...])
        kvcs_o[0, :, rows] = kv[0:2 * BRANCH_W]
        kvw_o[0, :, rows] = kv[2 * BRANCH_W:3 * BRANCH_W]
        gr_o[0, rows, :] = nrm[:, 0:R_WIDTH]
        q_o[0, rows, :] = nrm[:, R_WIDTH:R_WIDTH + N_WIDTH]
        gn_o[0, rows, :] = nrm[:, R_WIDTH + N_WIDTH:R_WIDTH + 2 * N_WIDTH]
        gl_o[0, rows, :] = nrm[:, R_WIDTH + 2 * N_WIDTH:]
        zp = jnp.where(first, prev, pltpu.roll(z, 1, axis=1))
        prev = z[:, sub - 1:sub]
        zs_all.append((rows, z, z + (zp - z) * mu_ref[...]))
    for rows, z, zs in zs_all:
        outs = _rwkv_prep(zs, w0_ref[...], a0_ref[...], kk_ref[...], ka_ref[...], wdec_ref[...], waaa_ref[...])
        for o_ref, val in zip((r_o, w_o, k_o, v_o, kkn_o, b_o), outs):
            o_ref[0, :, rows] = val
    z_last = zs_all[-1][1][:, sub - LANES:sub]
    carry_ref[...] = z_last
    zl_o[0] = z_last


def prompt_in_proj(x, norm_g, w_in, mu_shift, w0, w_decay_up, a0, w_aaa_up, k_k, k_a):
    B, T, D = x.shape
    tm = PROJ_TM
    assert T % tm == 0 and tm % PROJ_SUB == 0 and PROJ_SUB % LANES == 0
    f32 = jnp.float32
    pw = _proj_weights(norm_g, w_in, mu_shift, w0, w_decay_up, a0, w_aaa_up, k_k, k_a)
    consts = [pw["g"], pw["wz_t"], pw["wkv"].T, pw["wn"]] + pw["cols"] + pw["lora"]
    cm = lambda c: pl.BlockSpec((1, c, tm), lambda b, i: (b, 0, i))
    tok = lambda c: pl.BlockSpec((1, tm, c), lambda b, i: (b, i, 0))
    cms = lambda c: jax.ShapeDtypeStruct((B, c, T), f32)
    toks = lambda c: jax.ShapeDtypeStruct((B, T, c), f32)
    return pl.pallas_call(
        _prompt_proj_body,
        grid=(B, T // tm),
        in_specs=[tok(D)] + [_const_spec(c.shape, 2) for c in consts],
        out_specs=[cm(R_WIDTH)] * 6 + [cm(2 * BRANCH_W), cm(BRANCH_W), tok(R_WIDTH), tok(N_WIDTH), tok(N_WIDTH),
                                       tok(LANES), pl.BlockSpec((1, R_SHIFT_WIDTH, LANES), lambda b, i: (b, 0, 0))],
        out_shape=[cms(R_WIDTH)] * 6 + [cms(2 * BRANCH_W), cms(BRANCH_W), toks(R_WIDTH), toks(N_WIDTH), toks(N_WIDTH),
                                        toks(LANES), jax.ShapeDtypeStruct((B, R_SHIFT_WIDTH, LANES), f32)],
        scratch_shapes=[pltpu.VMEM((R_SHIFT_WIDTH, LANES), f32)],
        compiler_params=pltpu.CompilerParams(dimension_semantics=("parallel", "arbitrary"),
                                             vmem_limit_bytes=VMEM_LIMIT),
        name="prompt_in_proj",
    )(x, *consts)


def _proj_weights(norm_g, w_in, mu_shift, w0, w_decay_up, a0, w_aaa_up, k_k, k_a):
    f32, bf16 = jnp.float32, jnp.bfloat16
    D = w_in.shape[0]
    o = PROJ_OFFS
    wb = w_in.astype(bf16)
    gl_pad = LANES - PROJ_SPLITS[5]
    wn = jnp.concatenate([wb[:, o[1]:o[4]], wb[:, o[5]:o[6]], jnp.zeros((D, gl_pad), bf16)], axis=1)
    col = lambda p: p.astype(f32).reshape(-1, 1)
    return dict(g=norm_g.astype(f32).reshape(1, D), wz_t=wb[:, o[0]:o[1]].T, wkv=wb[:, o[4]:o[5]], wn=wn,
                cols=[col(mu_shift), col(w0), col(a0), col(k_k), col(k_a)],
                lora=[w_decay_up.astype(bf16).T, w_aaa_up.astype(bf16).T])


def _sample_proj_body(x_ref, prev_ref, g_ref, wz_ref, wkv_ref, wn_ref, mu_ref, w0_ref, a0_ref, kk_ref, ka_ref, wdec_ref,
                      waaa_ref, r_o, w_o, k_o, v_o, kkn_o, b_o, z_o, kv_o, gr_o, q_o, gn_o, gl_o, *, steps):
    bf16 = jnp.bfloat16
    m = x_ref.shape[0]
    x = x_ref[...]
    xn = (x * lax.rsqrt(jnp.mean(x * x, axis=-1, keepdims=True) + RMS_EPS) * g_ref[...]).astype(bf16)
    z = _dot_nt(wz_ref[...], xn)
    kv_o[...] = _dot(xn, wkv_ref[...])
    nrm = _dot(xn, wn_ref[...])
    gr_o[...] = nrm[:, 0:R_WIDTH]
    q_o[...] = nrm[:, R_WIDTH:R_WIDTH + N_WIDTH]
    gn_o[...] = nrm[:, R_WIDTH + N_WIDTH:R_WIDTH + 2 * N_WIDTH]
    gl_o[...] = nrm[:, R_WIDTH + 2 * N_WIDTH:]
    z_o[...] = z
    first = lax.broadcasted_iota(jnp.int32, (1, m), 1) % steps == 0
    zp = jnp.where(first, prev_ref[...], pltpu.roll(z, 1, axis=1))
    zs = z + (zp - z) * mu_ref[...]
    outs = _rwkv_prep(zs, w0_ref[...], a0_ref[...], kk_ref[...], ka_ref[...], wdec_ref[...], waaa_ref[...])
    for o_ref, val in zip((r_o, w_o, k_o, v_o, kkn_o, b_o), outs):
        o_ref[...] = val


def sample_in_proj(x, shift0, norm_g, w_in, mu_shift, w0, w_decay_up, a0, w_aaa_up, k_k, k_a):
    B, T, D = x.shape
    M = B * T
    f32 = jnp.float32
    pw = _proj_weights(norm_g, w_in, mu_shift, w0, w_decay_up, a0, w_aaa_up, k_k, k_a)
    prev = jnp.repeat(shift0.astype(f32).T, T, axis=1)
    args = [x.reshape(M, D), prev, pw["g"], pw["wz_t"], pw["wkv"], pw["wn"]] + pw["cols"] + pw["lora"]
    cm = lambda c: jax.ShapeDtypeStruct((c, M), f32)
    tok = lambda c: jax.ShapeDtypeStruct((M, c), f32)
    out_shape = [cm(R_WIDTH)] * 6 + [cm(R_SHIFT_WIDTH), tok(KV_WIDTH), tok(R_WIDTH), tok(N_WIDTH), tok(N_WIDTH),
                                     tok(LANES)]
    return pl.pallas_call(
        functools.partial(_sample_proj_body, steps=T),
        grid=(1,),
        in_specs=[_const_spec(a.shape, 1) for a in args],
        out_specs=[_const_spec(s.shape, 1) for s in out_shape],
        out_shape=out_shape,
        compiler_params=pltpu.CompilerParams(dimension_semantics=("arbitrary",), vmem_limit_bytes=VMEM_LIMIT),
        name="sample_in_proj",
    )(*args)


def _tree_sum(terms):
    while len(terms) > 1:
        terms = [terms[i] + terms[i + 1] for i in range(0, len(terms) - 1, 2)] + terms[len(terms) & ~1:]
    return terms[0]


def _rwkv_scan_body(r_ref, w_ref, k_ref, v_ref, kk_ref, b_ref, s0_ref, rk_ref, gw_ref, gb_ref, y_ref, s_ref):
    n = s_ref.shape[0]

    @pl.when(pl.program_id(1) == 0)
    def _():
        s_ref[...] = s0_ref[...]

    def step(t, carry):
        row = lambda ref, j: ref[t, pl.ds(j, 1), :]
        v = v_ref[t]
        grp = RWKV_COLS

        def p1(g, sa):
            return sa + _tree_sum([s_ref[g * grp + u] * row(kk_ref, g * grp + u) for u in range(grp)])

        sa = lax.fori_loop(0, n // grp, p1, jnp.zeros(v.shape, jnp.float32))

        def p2(g, y):
            terms = []
            for u in range(grp):
                j = g * grp + u
                s_j = s_ref[j] * row(w_ref, j) - sa * row(b_ref, j) + v * row(k_ref, j)
                s_ref[j] = s_j
                terms.append(s_j * row(r_ref, j))
            return y + _tree_sum(terms)

        y_ref[t] = lax.fori_loop(0, n // grp, p2, jnp.zeros(v.shape, jnp.float32))
        return carry

    lax.fori_loop(0, r_ref.shape[0], step, 0)

    y = y_ref[...]
    yc = y - jnp.mean(y, axis=1, keepdims=True)
    y = yc * lax.rsqrt(jnp.mean(yc * yc, axis=1, keepdims=True) + GN_EPS)
    y = y * gw_ref[...] + gb_ref[...]
    y_ref[...] = y + jnp.sum(r_ref[...] * k_ref[...] * rk_ref[...], axis=1, keepdims=True) * v_ref[...]


def rwkv_scan_lanes(seqs, s0, r_k, gn_w, gn_b, n_batch):
    T, N, BH = seqs[0].shape
    assert BH % LANES == 0
    tc = min(RWKV_TC, T)
    assert T % tc == 0
    per_lane = lambda p: jnp.tile(p.astype(jnp.float32).reshape(BH // n_batch, N).T, (1, n_batch))
    seq = pl.BlockSpec((tc, N, LANES), lambda l, t: (t, 0, l))
    state = pl.BlockSpec((N, N, LANES), lambda l, t: (0, 0, l))
    param = pl.BlockSpec((N, LANES), lambda l, t: (0, l))
    return pl.pallas_call(
        _rwkv_scan_body,
        grid=(BH // LANES, T // tc),
        in_specs=[seq] * 6 + [state] + [param] * 3,
        out_specs=[seq, state],
        out_shape=[jax.ShapeDtypeStruct((T, N, BH), jnp.float32), jax.ShapeDtypeStruct((N, N, BH), jnp.float32)],
        compiler_params=pltpu.CompilerParams(dimension_semantics=("parallel", "arbitrary"),
                                             vmem_limit_bytes=VMEM_LIMIT),
        name="rwkv_scan",
    )(*seqs, s0, per_lane(r_k), per_lane(gn_w), per_lane(gn_b))


def _lanes_from_cm(a, T):
    return a.reshape(-1, R_HEAD_DIM, T).transpose(2, 1, 0)


def _y_from_lanes(y, B):
    T, N, _ = y.shape
    return y.reshape(T, N, B, R_HEADS).transpose(2, 0, 3, 1).reshape(B, T, R_WIDTH)


def _state_to_lanes(s):
    B, H, N, _ = s.shape
    return s.astype(jnp.float32).transpose(3, 2, 0, 1).reshape(N, N, B * H)


def _state_from_lanes(s, B):
    N = s.shape[0]
    return s.reshape(N, N, B, R_HEADS).transpose(2, 3, 1, 0)


def _cmp_weights(w_pos, w_mix, n_rows):
    wp = jnp.broadcast_to(w_pos.astype(jnp.float32).reshape(2, 1, CMP_BLOCK, N_HEAD_DIM).transpose(2, 0, 1, 3),
                          (CMP_BLOCK, 2, N_KV_HEADS, N_HEAD_DIM)).reshape(CMP_BLOCK, BRANCH_W)
    wa = jnp.tile(wp[:CMP_STRIDE], (n_rows // CMP_STRIDE, 1)).T
    wb = jnp.tile(wp[CMP_STRIDE:], (n_rows // CMP_STRIDE, 1)).T
    wmix = jnp.zeros((BRANCH_W, BRANCH_W), jnp.float32)
    for e in range(2):
        for kh in range(N_KV_HEADS):
            o = (e * N_KV_HEADS + kh) * N_HEAD_DIM
            wmix = wmix.at[o:o + N_HEAD_DIM, o:o + N_HEAD_DIM].set(w_mix[e].astype(jnp.float32))
    return wa, wb, wmix.T


def _compress_prompt_body(kv_ref, wa_ref, wb_ref, pool_ref, wmix_ref, o_ref):
    x = kv_ref[0]
    xw = jnp.concatenate([x * wa_ref[...], x * wb_ref[...]], axis=1)
    o_ref[0] = _dot_x3(wmix_ref[...], _dot_exact_rhs(xw, pool_ref[...]))


def compress_prompt(kv_cs, w_pos, w_mix):
    B, _, T = kv_cs.shape
    assert CMP_BLOCK == 2 * CMP_STRIDE and T % CMP_STRIDE == 0
    ncp = T // CMP_STRIDE
    wa, wb, wmix_t = _cmp_weights(w_pos, w_mix, T)
    chunk = np.arange(T) // CMP_STRIDE
    pool = np.concatenate([chunk[:, None] == np.arange(ncp)[None, :],
                           chunk[:, None] == np.arange(ncp)[None, :] + 1], axis=0)
    pool = jnp.asarray(pool, jnp.bfloat16)
    return pl.pallas_call(
        _compress_prompt_body,
        grid=(B,),
        in_specs=[pl.BlockSpec((1, BRANCH_W, T), lambda b: (b, 0, 0))] + [_const_spec(c.shape, 1)
                                                                          for c in (wa, wb, pool, wmix_t)],
        out_specs=pl.BlockSpec((1, BRANCH_W, ncp), lambda b: (b, 0, 0)),
        out_shape=jax.ShapeDtypeStruct((B, BRANCH_W, ncp), jnp.float32),
        compiler_params=pltpu.CompilerParams(dimension_semantics=("parallel",), vmem_limit_bytes=VMEM_LIMIT),
        name="compress_prompt",
    )(kv_cs, wa, wb, pool, wmix_t)


def _nsa_prompt_body(q_ref, ksel_ref, kwin_ref, kc_ref, gl_ref, gn_ref, ovt_ref, ex_ref, o_ref):
    f32, bf16 = jnp.float32, jnp.bfloat16
    TQ, KC, G = NSA_TQ, NSA_KC, N_GROUP
    ncp = kc_ref.shape[2]
    n_sel = ovt_ref.shape[0]
    topk = min(SEL_TOPK, n_sel)
    qs = pl.program_id(1) * TQ
    qpos = qs + lax.broadcasted_iota(jnp.int32, (TQ, 1), 0)
    lo_half = lax.broadcasted_iota(jnp.int32, (TQ, LANES), 1) < N_HEAD_DIM

    def q_padded(h, half):
        tile = q_ref[0, :, (h // 2) * LANES:(h // 2 + 1) * LANES] * QSCALE
        if h % 2 != half:
            tile = pltpu.roll(tile, N_HEAD_DIM, axis=1)
        return jnp.where(lo_half if half == 0 else jnp.logical_not(lo_half), tile, 0.0)

    wstart = pl.multiple_of(jnp.maximum(qs - WINDOW, 0), TQ)
    kw = kwin_ref[0, 0:LANES, pl.ds(wstart, WINDOW + TQ)].astype(bf16)
    vw = kwin_ref[0, LANES:2 * LANES, pl.ds(wstart, WINDOW + TQ)].astype(bf16)
    kposw = wstart + lax.broadcasted_iota(jnp.int32, (1, WINDOW + TQ), 1)
    mask_w = (kposw <= qpos) & (kposw > qpos - WINDOW)

    kck = kc_ref[0, 0:LANES, :]
    kcv = kc_ref[0, LANES:2 * LANES, :].astype(bf16)
    kend = lax.broadcasted_iota(jnp.int32, (1, ncp), 1) * CMP_STRIDE + (CMP_BLOCK - 1)
    mask_c = kend <= qpos

    sid_t = lax.broadcasted_iota(jnp.int32, (n_sel, 1), 0)
    cur_t = (qs + lax.broadcasted_iota(jnp.int32, (1, TQ), 1)) // SEL_BLOCK
    valid_t = sid_t <= cur_t
    forced_t = (sid_t == 0) | (sid_t == cur_t) | (sid_t == cur_t - 1)
    n_chunks = qs // KC + 1

    H = N_HEADS
    qg = jnp.concatenate([q_padded(h, h // G) for h in range(H)], axis=0)
    qb = qg.astype(bf16)

    p = _masked_softmax_rows(_dot_x3(qg, kck).reshape(H, TQ, ncp), mask_c[None], base2=True)
    o_c = _dot_nt(p.reshape(H * TQ, ncp).astype(bf16), kcv).reshape(H, TQ, LANES)

    ovt = ovt_ref[...]
    selb = []
    for kvh in range(N_KV_HEADS):
        hi, mid, lo = _split3(jnp.sum(p[kvh * G:(kvh + 1) * G], axis=0))
        score = _dot_nt(ovt, hi) + _dot_nt(ovt, mid) + _dot_nt(ovt, lo)
        score = jnp.where(valid_t, score + jnp.where(forced_t, SEL_BIAS, 0.0), -SEL_BIAS)
        rank = jnp.zeros((n_sel, TQ), jnp.int32)
        for s2 in range(n_sel):
            row = score[s2:s2 + 1, :]
            ahead = (row > score) | ((row == score) & (s2 < sid_t))
            rank = rank + ahead.astype(jnp.int32)
        selb_t = jnp.concatenate([jnp.where((rank < topk) & valid_t, 0.0, SEL_NEG),
                                  jnp.zeros((LANES - n_sel, TQ), f32)], axis=0)
        selb += [selb_t.T.astype(bf16)] * G
    qa = jnp.concatenate([qb, jnp.concatenate(selb, axis=0)], axis=1)

    def chunk(c, carry, causal):
        m, l, acc = carry
        k0 = pl.multiple_of(c * KC, KC)
        ka = jnp.concatenate([ksel_ref[0, 0:LANES, pl.ds(k0, KC)].astype(bf16), ex_ref[:, pl.ds(k0, KC)]],
                             axis=0)
        va = jnp.concatenate([ksel_ref[0, LANES:2 * LANES, pl.ds(k0, KC)].astype(bf16), ones_kc], axis=0)
        sc = _dot(qa, ka).reshape(H, TQ, KC)
        if causal:
            kpos = k0 + lax.broadcasted_iota(jnp.int32, (1, KC), 1)
            sc = jnp.where((kpos <= qpos)[None], sc, SEL_NEG)
        m_new = jnp.maximum(m, jnp.max(sc, axis=-1, keepdims=True))
        alpha = jnp.exp2(m - m_new)
        e = jnp.exp2(sc - m_new).astype(bf16)
        pv = _dot_nt(e.reshape(H * TQ, KC), va).reshape(H, TQ, 2 * LANES)
        return m_new, alpha * l + pv[:, :, LANES:], alpha * acc + pv[:, :, :LANES]

    ones_kc = jnp.ones((LANES, KC), bf16)
    carry = (jnp.full((H, TQ, 1), SEL_NEG, f32), jnp.zeros((H, TQ, LANES), f32), jnp.zeros((H, TQ, LANES), f32))
    carry = lax.fori_loop(0, n_chunks - 1, functools.partial(chunk, causal=False), carry)
    _, l, acc = chunk(n_chunks - 1, carry, causal=True)
    o_s = acc / l

    sw = jnp.where(mask_w[None], _dot(qb, kw).reshape(H, TQ, WINDOW + TQ), MASK_NEG)
    ew = jnp.exp2(sw - jnp.max(sw, axis=-1, keepdims=True)).astype(bf16)
    pw = _dot_nt(ew.reshape(H * TQ, WINDOW + TQ),
                 jnp.concatenate([vw, jnp.ones((LANES, WINDOW + TQ), bf16)], axis=0)).reshape(H, TQ, 2 * LANES)
    o_w = pw[:, :, :LANES] / pw[:, :, LANES:]

    gates = jax.nn.sigmoid(gl_ref[0])
    heads = []
    for h in range(H):
        o = (gates[:, h:h + 1] * o_c[h] + gates[:, H + h:H + h + 1] * o_s[h]
             + gates[:, 2 * H + h:2 * H + h + 1] * o_w[h])
        heads.append(pltpu.roll(o, N_HEAD_DIM, axis=1) if h % 2 != h // G else o)

    for m in range(N_HEADS // 2):
        tile = jnp.where(lo_half, heads[2 * m], heads[2 * m + 1])
        o_ref[0, :, m * LANES:(m + 1) * LANES] = (
            tile * _silu(gn_ref[0, :, m * LANES:(m + 1) * LANES])).astype(o_ref.dtype)


def nsa_prompt(q, kv_cs, kv_w, kc, gl, gate_n):
    B, T, _ = q.shape
    TQ, KC = NSA_TQ, NSA_KC
    assert N_KV_HEADS == 2 and N_HEAD_DIM * 2 == LANES and T % KC == 0 and KC % TQ == 0 and T >= WINDOW + TQ
    assert SEL_BLOCK % CMP_STRIDE == 0 and WINDOW % TQ == 0 and TQ % LANES == 0 and T // SEL_BLOCK <= LANES
    ncp = T // CMP_STRIDE
    n_sel = T // SEL_BLOCK
    cs = np.arange(ncp) * CMP_STRIDE
    ss = np.arange(n_sel) * SEL_BLOCK
    overlap = (cs[:, None] < ss[None, :] + SEL_BLOCK) & (cs[:, None] + CMP_BLOCK > ss[None, :])
    overlap[ncp - 1] = False
    overlap_t = jnp.asarray(overlap.T, jnp.bfloat16)
    expand = jnp.asarray(np.arange(T)[None, :] // SEL_BLOCK == np.arange(LANES)[:, None], jnp.bfloat16)
    return pl.pallas_call(
        _nsa_prompt_body,
        grid=(B, T // TQ),
        in_specs=[pl.BlockSpec((1, TQ, N_WIDTH), lambda b, i: (b, i, 0)),
                  pl.BlockSpec((1, BRANCH_W, T), lambda b, i: (b, 1, 0)),
                  pl.BlockSpec((1, BRANCH_W, T), lambda b, i: (b, 0, 0)),
                  pl.BlockSpec((1, BRANCH_W, ncp), lambda b, i: (b, 0, 0)),
                  pl.BlockSpec((1, TQ, LANES), lambda b, i: (b, i, 0)),
                  pl.BlockSpec((1, TQ, N_WIDTH), lambda b, i: (b, i, 0)),
                  _const_spec((n_sel, ncp), 2), _const_spec((LANES, T), 2)],
        out_specs=pl.BlockSpec((1, TQ, N_WIDTH), lambda b, i: (b, i, 0)),
        out_shape=jax.ShapeDtypeStruct((B, T, N_WIDTH), jnp.bfloat16),
        compiler_params=pltpu.CompilerParams(dimension_semantics=("parallel", "arbitrary"),
                                             vmem_limit_bytes=VMEM_LIMIT),
        name="nsa_prompt",
    )(q, kv_cs, kv_w, kc, gl, gate_n, overlap_t, expand)


def _page_spec(k, branch, pages_per_step):
    return pl.BlockSpec((1, BRANCH_W, PAGE_SIZE), lambda b, g, pt: (pt[b, g * pages_per_step + k], branch, 0))


def _sample_compress_body(pt_ref, *refs):
    pages = refs[:SAMPLE_PG_CMP]
    wa_ref, wb_ref, wmix_ref, oa_ref, ob_ref = refs[SAMPLE_PG_CMP:]
    cpp = PAGE_SIZE // CMP_STRIDE
    xt = [p[0].T.reshape(cpp, CMP_STRIDE, BRANCH_W) for p in pages]
    for w_ref, o_ref in ((wa_ref, oa_ref), (wb_ref, ob_ref)):
        pooled = jnp.concatenate([jnp.sum(x * w_ref[...], axis=1) for x in xt], axis=0)
        o_ref[0] = _dot_x3(pooled, wmix_ref[...])


def sample_compress(cache_t, page_table, w_pos, w_mix):
    B, n_pages = page_table.shape
    PG = SAMPLE_PG_CMP
    assert n_pages % PG == 0 and CMP_BLOCK == 2 * CMP_STRIDE and PAGE_SIZE % CMP_STRIDE == 0
    rows = PG * PAGE_SIZE // CMP_STRIDE
    ncp = n_pages * PAGE_SIZE // CMP_STRIDE
    wa, wb, wmix_t = _cmp_weights(w_pos, w_mix, CMP_STRIDE)
    wa, wb, wmix = wa.T[None], wb.T[None], wmix_t.T
    const = lambda shape: pl.BlockSpec(shape, lambda b, g, pt: (0,) * len(shape))
    out = pl.BlockSpec((1, rows, BRANCH_W), lambda b, g, pt: (b, g, 0))
    return pl.pallas_call(
        _sample_compress_body,
        grid_spec=pltpu.PrefetchScalarGridSpec(
            num_scalar_prefetch=1, grid=(B, n_pages // PG),
            in_specs=[_page_spec(k, 0, PG) for k in range(PG)] + [const(wa.shape), const(wb.shape), const(wmix.shape)],
            out_specs=[out, out]),
        out_shape=[jax.ShapeDtypeStruct((B, ncp, BRANCH_W), jnp.float32)] * 2,
        compiler_params=pltpu.CompilerParams(dimension_semantics=("parallel", "arbitrary"),
                                             vmem_limit_bytes=VMEM_LIMIT),
        name="sample_compress",
    )(page_table, *([cache_t] * PG), wa, wb, wmix)


def _online_chunk(m, l, acc, s, keep, v, v_channels_major=False):
    s = jnp.where(keep, s, MASK_NEG)
    m_new = jnp.maximum(m, jnp.max(s, axis=-1, keepdims=True))
    alpha = jnp.exp(m - m_new)
    e = jnp.where(keep, jnp.exp(s - m_new), 0.0)
    eb = e.astype(jnp.bfloat16)
    pv = _dot_nt(eb, v) if v_channels_major else _dot(eb, v)
    return m_new, alpha * l + jnp.sum(e, axis=-1, keepdims=True), alpha * acc + pv


def _sample_attend_body(pt_ref, *refs, past):
    f32, bf16 = jnp.float32, jnp.bfloat16
    PG, G = SAMPLE_PG_ATT, N_GROUP
    pages = refs[:PG]
    (ka_ref, kb_ref, q_ref, kvn_ref, gl_ref, gn_ref, win_ref, ex_ref, ov_ref, o_ref,
     qf_ref, selb_ref, tail_ref, m_ref, l_ref, acc_ref, oc_ref) = refs[PG:]
    T = q_ref.shape[1]
    R = N_HEADS * T
    ncp = ka_ref.shape[1]
    nsp = ov_ref.shape[1]
    KC = PG * PAGE_SIZE
    n_past = past // SEL_BLOCK
    bpc = KC // SEL_BLOCK
    g = pl.program_id(1)
    lo_half = lax.broadcasted_iota(jnp.int32, (T, LANES), 1) < N_HEAD_DIM
    row = lax.broadcasted_iota(jnp.int32, (R, 1), 0)
    qpos = past + row % T

    @pl.when(g == 0)
    def _():
        pieces = []
        for h in range(N_HEADS):
            kvh = h // G
            tile = q_ref[0, :, (h // 2) * LANES:(h // 2 + 1) * LANES] * ATTN_SCALE
            if h % 2 != kvh:
                tile = pltpu.roll(tile, N_HEAD_DIM, axis=1)
            pieces.append(jnp.where(lo_half if kvh == 0 else jnp.logical_not(lo_half), tile, 0.0))
        q32 = jnp.concatenate(pieces, axis=0)
        qf_ref[...] = q32

        kc = ka_ref[0] + pltpu.roll(kb_ref[0], ncp - 1, axis=0)
        kend = lax.broadcasted_iota(jnp.int32, (1, ncp), 1) * CMP_STRIDE + (CMP_BLOCK - 1)
        p = _masked_softmax_rows(_dot_nt_x3(q32, kc[:, 0:LANES]), kend <= qpos)
        oc_ref[...] = _dot(p.astype(bf16), kc[:, LANES:2 * LANES].astype(bf16))
        imp = jnp.concatenate(
            [sum(p[(kvh * G + gg) * T:(kvh * G + gg + 1) * T] for gg in range(G)) for kvh in range(N_KV_HEADS)],
            axis=0)

        sid = lax.broadcasted_iota(jnp.int32, (1, nsp), 1)
        sid_f = sid.astype(f32)
        cur = (past + lax.broadcasted_iota(jnp.int32, (N_KV_HEADS * T, 1), 0) % T) // SEL_BLOCK
        valid = sid <= cur
        forced = (sid == 0) | (sid == cur) | (sid == cur - 1)
        score = _dot_exact_rhs(imp, ov_ref[...])
        score = jnp.where(valid, score + jnp.where(forced, SEL_BIAS, 0.0), -SEL_BIAS)

        def extract(_, carry):
            sc, chosen = carry
            best = jnp.max(sc, axis=-1, keepdims=True)
            first = jnp.min(jnp.where(sc == best, sid_f, float(nsp)), axis=-1, keepdims=True)
            hit = sid_f == first
            return jnp.where(hit, TAKEN, sc), jnp.where(hit, 1.0, chosen)

        _, chosen = lax.fori_loop(0, min(SEL_TOPK, n_past + 1), extract, (score, jnp.zeros_like(score)))
        sel = jnp.concatenate([jnp.where(valid, chosen, 0.0)[(h // G) * T:(h // G + 1) * T] for h in range(N_HEADS)],
                              axis=0)
        tail_ref[...] = sel[:, n_past:n_past + 1]
        selb = jnp.where(sel > 0.5, 0.0, SEL_NEG)
        for c in range(n_past // bpc):
            selb_ref[c] = jnp.concatenate([selb[:, c * bpc:(c + 1) * bpc], jnp.zeros((R, LANES - bpc), f32)],
                                          axis=1).astype(bf16)
        m_ref[...] = jnp.full(m_ref.shape, SEL_NEG, f32)
        l_ref[...] = jnp.zeros(l_ref.shape, f32)
        acc_ref[...] = jnp.zeros(acc_ref.shape, f32)

    qb = qf_ref[...].astype(bf16)
    ka = jnp.concatenate([p[0, 0:LANES, :] for p in pages], axis=1).astype(bf16)
    ka = jnp.concatenate([ka, ex_ref[...]], axis=0)
    va = jnp.concatenate([jnp.concatenate([p[0, LANES:2 * LANES, :] for p in pages], axis=1).astype(bf16),
                          jnp.ones((LANES, KC), bf16)], axis=0)
    s = _dot(jnp.concatenate([qb, selb_ref[g]], axis=1), ka)
    m = jnp.maximum(m_ref[...], jnp.max(s, axis=-1, keepdims=True))
    alpha = jnp.exp(m_ref[...] - m)
    pv = _dot_nt(jnp.exp(s - m).astype(bf16), va)
    l = alpha * l_ref[...] + pv[:, LANES:]
    acc = alpha * acc_ref[...] + pv[:, :LANES]
    m_ref[...] = m
    l_ref[...] = l
    acc_ref[...] = acc

    @pl.when(g == pl.num_programs(1) - 1)
    def _():
        def new_rows(c0):
            return jnp.concatenate([kvn_ref[0, :, c0:c0 + LANES], jnp.zeros((LANES - T, LANES), f32)],
                                   axis=0).astype(bf16)

        npos = past + lax.broadcasted_iota(jnp.int32, (1, LANES), 1)
        is_new = npos < past + T
        keep_t = is_new & (npos <= qpos) & (tail_ref[...] > 0.5)
        _, l_s, acc_s = _online_chunk(m, l, acc, _dot_nt(qb, new_rows(BRANCH_W)), keep_t, new_rows(BRANCH_W + LANES))
        o_s = acc_s / l_s

        nb = win_ref.shape[2]
        bpos = past - nb + lax.broadcasted_iota(jnp.int32, (1, nb), 1)
        keep_b = (bpos <= qpos) & (bpos > qpos - WINDOW) & (bpos >= 0)
        mw, lw, aw = _online_chunk(jnp.full((R, 1), MASK_NEG, f32), jnp.zeros((R, 1), f32), jnp.zeros((R, LANES), f32),
                                   _dot(qb, win_ref[0, 0:LANES, :].astype(bf16)), keep_b,
                                   win_ref[0, LANES:2 * LANES, :].astype(bf16), True)
        keep_n = is_new & (npos <= qpos) & (npos > qpos - WINDOW)
        _, lw, aw = _online_chunk(mw, lw, aw, _dot_nt(qb, new_rows(2 * BRANCH_W)), keep_n,
                                  new_rows(2 * BRANCH_W + LANES))
        o_w = aw / lw

        gates = jax.nn.sigmoid(gl_ref[0])
        gcol = lambda br: jnp.concatenate([gates[:, br * N_HEADS + h:br * N_HEADS + h + 1] for h in range(N_HEADS)],
                                          axis=0)
        o = gcol(0) * oc_ref[...] + gcol(1) * o_s + gcol(2) * o_w
        for mt in range(N_HEADS // 2):
            halves = []
            for h in (2 * mt, 2 * mt + 1):
                piece = o[h * T:(h + 1) * T]
                if h % 2 != h // G:
                    piece = pltpu.roll(piece, N_HEAD_DIM, axis=1)
                halves.append(piece)
            o_ref[0, :, mt * LANES:(mt + 1) * LANES] = (jnp.where(lo_half, halves[0], halves[1])
                                                        * _silu(gn_ref[0, :, mt * LANES:(mt + 1) * LANES]))


def nsa_sample(q, kv_new, kc_a, kc_b, gl, gate_n, cache_t, page_table, win_t):
    B, T, _ = q.shape
    n_pages = page_table.shape[1]
    past = n_pages * PAGE_SIZE
    PG = SAMPLE_PG_ATT
    KC = PG * PAGE_SIZE
    ng = n_pages // PG
    assert n_pages % PG == 0 and past % SEL_BLOCK == 0 and T <= SEL_BLOCK and T <= LANES
    assert (past + T) // CMP_STRIDE == past // CMP_STRIDE and KC % SEL_BLOCK == 0
    ncp = past // CMP_STRIDE
    n_sel = past // SEL_BLOCK + 1
    nsp = -(-n_sel // LANES) * LANES
    cs = np.arange(ncp) * CMP_STRIDE
    ss = np.arange(nsp) * SEL_BLOCK
    overlap = (cs[:, None] < ss[None, :] + SEL_BLOCK) & (cs[:, None] + CMP_BLOCK > ss[None, :])
    overlap[ncp - 1] = False
    overlap[:, n_sel:] = False
    overlap = jnp.asarray(overlap, jnp.bfloat16)
    assert KC // SEL_BLOCK <= LANES
    expand = jnp.asarray(np.arange(KC)[None, :] // SEL_BLOCK == np.arange(LANES)[:, None], jnp.bfloat16)
    R = N_HEADS * T
    per_b = lambda shape: pl.BlockSpec((1,) + shape, lambda b, g, pt: (b, 0, 0))
    return pl.pallas_call(
        functools.partial(_sample_attend_body, past=past),
        grid_spec=pltpu.PrefetchScalarGridSpec(
            num_scalar_prefetch=1, grid=(B, ng),
            in_specs=[_page_spec(k, 1, PG) for k in range(PG)] + [
                per_b((ncp, BRANCH_W)), per_b((ncp, BRANCH_W)), per_b((T, N_WIDTH)), per_b((T, KV_WIDTH)),
                per_b((T, LANES)), per_b((T, N_WIDTH)), per_b((BRANCH_W, win_t.shape[2])),
                pl.BlockSpec((LANES, KC), lambda b, g, pt: (0, 0)),
                pl.BlockSpec((ncp, nsp), lambda b, g, pt: (0, 0))],
            out_specs=per_b((T, N_WIDTH)),
            scratch_shapes=[pltpu.VMEM((R, LANES), jnp.float32), pltpu.VMEM((ng, R, LANES), jnp.bfloat16),
                            pltpu.VMEM((R, 1), jnp.float32), pltpu.VMEM((R, 1), jnp.float32),
                            pltpu.VMEM((R, LANES), jnp.float32), pltpu.VMEM((R, LANES), jnp.float32),
                            pltpu.VMEM((R, LANES), jnp.float32)]),
        out_shape=jax.ShapeDtypeStruct((B, T, N_WIDTH), jnp.float32),
        compiler_params=pltpu.CompilerParams(dimension_semantics=("parallel", "arbitrary"),
                                             vmem_limit_bytes=VMEM_LIMIT),
        name="nsa_sample",
    )(page_table, *([cache_t] * PG), kc_a, kc_b, q, kv_new, gl, gate_n, win_t, expand, overlap)


def _out_proj_body(yr_ref, gr_ref, on_ref, x_ref, w_ref, g_ref, o_ref, *, final_norm):
    bf16 = jnp.bfloat16
    a = (yr_ref[...] * _silu(gr_ref[...])).astype(bf16)
    y = x_ref[...] + _dot(a, w_ref[0:R_WIDTH, :]) + _dot(on_ref[...].astype(bf16), w_ref[R_WIDTH:MIX_WIDTH, :])
    if final_norm:
        y = y * lax.rsqrt(jnp.mean(y * y, axis=-1, keepdims=True) + RMS_EPS) * g_ref[...]
    o_ref[...] = y


def out_proj(y_r, gate_r, o_n, x, w_out, final_g):
    M, D = x.shape
    tm = min(OUT_TM, M)
    assert M % tm == 0
    g = (jnp.ones((D,), jnp.float32) if final_g is None else final_g.astype(jnp.float32)).reshape(1, D)
    rows = lambda c: pl.BlockSpec((tm, c), lambda i: (i, 0))
    return pl.pallas_call(
        functools.partial(_out_proj_body, final_norm=final_g is not None),
        grid=(M // tm,),
        in_specs=[rows(R_WIDTH), rows(R_WIDTH), rows(N_WIDTH), rows(D), _const_spec((MIX_WIDTH, D), 1),
                  _const_spec((1, D), 1)],
        out_specs=rows(D),
        out_shape=jax.ShapeDtypeStruct((M, D), jnp.float32),
        compiler_params=pltpu.CompilerParams(dimension_semantics=("parallel",), vmem_limit_bytes=VMEM_LIMIT),
        name="out_proj",
    )(y_r, gate_r, o_n, x, w_out.astype(jnp.bfloat16), g)


def prompt_mixers(x, norm_g, w_in, mu_shift, w0, w_decay_up, a0, w_aaa_up, k_k, k_a, r_k, gn_w, gn_b,
                  w_cmp_pos, w_cmp_mix):
    B, T, D = x.shape
    (r, w, k, v, kk, b, kv_cs, kv_w, gate_r, q, gate_n, gl, z_last) = prompt_in_proj(
        x, norm_g, w_in, mu_shift, w0, w_decay_up, a0, w_aaa_up, k_k, k_a)
    seqs = [_lanes_from_cm(a, T) for a in (r, w, k, v, kk, b)]
    o_n = nsa_prompt(q, kv_cs, kv_w, compress_prompt(kv_cs, w_cmp_pos, w_cmp_mix), gl, gate_n)
    seqs, o_n = lax.optimization_barrier((seqs, o_n))
    s0 = jnp.zeros((R_HEAD_DIM, R_HEAD_DIM, B * R_HEADS), jnp.float32)
    y, s_fin = rwkv_scan_lanes(seqs, s0, r_k, gn_w, gn_b, B)
    heads = (N_KV_HEADS, N_HEAD_DIM)
    kv_rows = kv_cs.reshape((B, 4) + heads + (T,)).transpose(0, 4, 1, 2, 3)
    n_keep = min(WINDOW, T)
    win_rows = kv_w[:, :, T - n_keep:].reshape((B, 2) + heads + (n_keep,)).transpose(0, 4, 1, 2, 3)
    outs = (kv_rows, win_rows, _state_from_lanes(s_fin, B), z_last[:, :, LANES - 1])
    return (_y_from_lanes(y, B), gate_r, o_n), outs


def sample_mixers(x, cache, win_buf, wkv0, shift0, page_table, norm_g, w_in, mu_shift, w0, w_decay_up, a0,
                  w_aaa_up, k_k, k_a, r_k, gn_w, gn_b, w_cmp_pos, w_cmp_mix):
    B, T, D = x.shape
    (r, w, k, v, kk, b, z, kv, gate_r, q, gate_n, gl) = sample_in_proj(
        x, shift0, norm_g, w_in, mu_shift, w0, w_decay_up, a0, w_aaa_up, k_k, k_a)
    seqs = [a.reshape(R_HEADS, R_HEAD_DIM, B, T).transpose(3, 1, 2, 0).reshape(T, R_HEAD_DIM, B * R_HEADS)
            for a in (r, w, k, v, kk, b)]
    y, s_fin = rwkv_scan_lanes(seqs, _state_to_lanes(wkv0), r_k, gn_w, gn_b, B)

    cache_t = cache.reshape(cache.shape[0], PAGE_SIZE, 2 * BRANCH_W).transpose(0, 2, 1)
    nb = win_buf.shape[1]
    win_t = win_buf.reshape(B, nb, BRANCH_W).transpose(0, 2, 1)
    kc_a, kc_b = sample_compress(cache_t, page_table, w_cmp_pos, w_cmp_mix)
    kv = kv.reshape(B, T, KV_WIDTH)
    o_n = nsa_sample(q.reshape(B, T, N_WIDTH), kv, kc_a, kc_b, gl.reshape(B, T, LANES),
                     gate_n.reshape(B, T, N_WIDTH), cache_t, page_table, win_t)
    kv = kv.reshape(B, T, KV_SLOTS, N_KV_HEADS, N_HEAD_DIM)
    keys_w = jnp.concatenate([win_buf.astype(kv.dtype), kv[:, :, 4:6]], axis=1)
    n_keep = min(WINDOW, nb + T)
    shift = z.reshape(R_SHIFT_WIDTH, B, T)[:, :, T - 1].T
    outs = (kv[:, :, 0:4], keys_w[:, nb + T - n_keep:], _state_from_lanes(s_fin, B), shift)
    return (_y_from_lanes(y, B), gate_r.reshape(B, T, R_WIDTH), o_n), outs


def _finish(mixed, x, w_out, final_g):
    B, T, D = x.shape
    y_r, gate_r, o_n = (a.reshape(B * T, -1) for a in mixed)
    return out_proj(y_r, gate_r, o_n, x.reshape(B * T, D), w_out, final_g).reshape(B, T, D)


def kernel(x_prompt, x_sample, cache_kv, cache_kv_win, state_wkv, state_shift, page_table,
           norm_g, w_in, mu_shift, w0, w_decay_up, a0, w_aaa_up, k_k, k_a, r_k, gn_w, gn_b,
           w_cmp_pos, w_cmp_mix, w_out, final_g):
    y_p, y_s = x_prompt, x_sample
    kvp, kvs, wnp, wns, skp, sks, shp, shs = [], [], [], [], [], [], [], []
    for layer in range(DEPTH):
        lw = (norm_g[layer], w_in[layer], mu_shift[layer], w0[layer], w_decay_up[layer], a0[layer],
              w_aaa_up[layer], k_k[layer], k_a[layer], r_k[layer], gn_w[layer], gn_b[layer],
              w_cmp_pos[layer], w_cmp_mix[layer])
        fg = final_g if layer == DEPTH - 1 else None
        mixed_p, (p_kv, p_win, p_wkv, p_shift) = prompt_mixers(y_p, *lw)
        mixed_s, (s_kv, s_win, s_wkv, s_shift) = sample_mixers(y_s, cache_kv[layer], cache_kv_win[layer],
                                                               state_wkv[layer], state_shift[layer], page_table, *lw)
        mixed_p, mixed_s = lax.optimization_barrier((mixed_p, mixed_s))
        y_p = _finish(mixed_p, y_p, w_out[layer], fg)
        y_s = _finish(mixed_s, y_s, w_out[layer], fg)
        kvp.append(p_kv); kvs.append(s_kv); wnp.append(p_win); wns.append(s_win)
        skp.append(p_wkv); sks.append(s_wkv); shp.append(p_shift); shs.append(s_shift)
    return (y_p, y_s, jnp.stack(kvp), jnp.stack(kvs), jnp.stack(wnp), jnp.stack(wns),
            jnp.stack(skp), jnp.stack(sks), jnp.stack(shp), jnp.stack(shs))
```

```python
import functools

import jax, jax.numpy as jnp
from jax import lax
import numpy as np
from jax.experimental import pallas as pl
from jax.experimental.pallas import tpu as pltpu

D_MODEL = 1024
BATCH = 16
SEQ = 2048
DEPTH = 1
DEC_BATCH = 32
DEC_SEQ = 4
PAST_LEN = 16384
PAGE_SIZE = 128

R_HEAD_DIM = 64
R_WIDTH = D_MODEL // 2
R_HEADS = R_WIDTH // R_HEAD_DIM
DECAY_LORA = 64
AAA_LORA = 64
R_SHIFT_WIDTH = 3 * R_WIDTH + DECAY_LORA + AAA_LORA
N_HEAD_DIM = 64
N_WIDTH = D_MODEL - R_WIDTH
N_HEADS = N_WIDTH // N_HEAD_DIM
N_KV_HEADS = 2
N_GROUP = N_HEADS // N_KV_HEADS
N_BRANCH = 3
KV_SLOTS = 2 * N_BRANCH
CMP_BLOCK = 32
CMP_STRIDE = 16
SEL_BLOCK = 64
SEL_TOPK = 16
WINDOW = 512
MIX_WIDTH = R_WIDTH + N_WIDTH
KV_WIDTH = KV_SLOTS * N_KV_HEADS * N_HEAD_DIM
PROJ_SPLITS = (R_SHIFT_WIDTH, R_WIDTH, N_WIDTH, N_WIDTH, KV_WIDTH, N_BRANCH * N_HEADS)
PROJ_OFFS = tuple(int(o) for o in np.cumsum((0,) + PROJ_SPLITS))
PROJ_WIDTH = sum(PROJ_SPLITS)
RMS_EPS = 1e-6
GN_EPS = 64e-5
MASK_NEG = -1e30
SEL_NEG = -(2.0 ** 100)
TAKEN = -3e38
SEL_BIAS = 1e4
ATTN_SCALE = N_HEAD_DIM ** -0.5
QSCALE = ATTN_SCALE * float(np.log2(np.e))

LANES = 128
BRANCH_W = 2 * N_KV_HEADS * N_HEAD_DIM
NSA_TQ = 256
NSA_KC = 512
PROJ_TM = 512
PROJ_SUB = 256
OUT_TM = 512
SAMPLE_PG_CMP = 16
SAMPLE_PG_ATT = 32
RWKV_TC = 32
RWKV_COLS = 16
VMEM_LIMIT = 56 * 1024 * 1024


def _split3(x):
    hi = x.astype(jnp.bfloat16)
    r1 = x - hi.astype(jnp.float32)
    mid = r1.astype(jnp.bfloat16)
    lo = (r1 - mid.astype(jnp.float32)).astype(jnp.bfloat16)
    return hi, mid, lo


def _dot(a, b):
    return jnp.dot(a, b, preferred_element_type=jnp.float32)


def _dot_nt(a, b):
    return lax.dot_general(a, b, (((1,), (1,)), ((), ())), preferred_element_type=jnp.float32)


def _dot_exact_rhs(a, b01):
    hi, mid, lo = _split3(a)
    return _dot(hi, b01) + _dot(mid, b01) + _dot(lo, b01)


def _dot_nt_x3(a, b):
    ah = a.astype(jnp.bfloat16)
    al = (a - ah.astype(jnp.float32)).astype(jnp.bfloat16)
    bh = b.astype(jnp.bfloat16)
    bl = (b - bh.astype(jnp.float32)).astype(jnp.bfloat16)
    return _dot_nt(ah, bh) + _dot_nt(ah, bl) + _dot_nt(al, bh)


def _dot_x3(a, b):
    ah = a.astype(jnp.bfloat16)
    al = (a - ah.astype(jnp.float32)).astype(jnp.bfloat16)
    bh = b.astype(jnp.bfloat16)
    bl = (b - bh.astype(jnp.float32)).astype(jnp.bfloat16)
    return _dot(ah, bh) + _dot(ah, bl) + _dot(al, bh)


def _silu(x):
    return x * jax.nn.sigmoid(x)


def _softplus(x):
    return jnp.maximum(x, 0.0) + jnp.log1p(jnp.exp(-jnp.abs(x)))


def _masked_softmax_rows(s, mask, base2=False):
    s = jnp.where(mask, s, MASK_NEG)
    e = (jnp.exp2 if base2 else jnp.exp)(s - jnp.max(s, axis=-1, keepdims=True))
    p = e / jnp.sum(e, axis=-1, keepdims=True)
    return jnp.where(mask, p, 0.0)


def _const_spec(shape, n_grid):
    zeros = (0,) * len(shape)
    if n_grid == 1:
        return pl.BlockSpec(shape, lambda i: zeros)
    return pl.BlockSpec(shape, lambda b, i: zeros)


def _rwkv_prep(zs, w0, a0, k_k, k_a, wdec_t, waaa_t):
    bf16 = jnp.bfloat16
    n = zs.shape[1]
    r = zs[0:R_WIDTH]
    k = zs[R_WIDTH:2 * R_WIDTH]
    v = zs[2 * R_WIDTH:3 * R_WIDTH]
    wd = zs[3 * R_WIDTH:3 * R_WIDTH + DECAY_LORA]
    ad = zs[3 * R_WIDTH + DECAY_LORA:R_SHIFT_WIDTH]
    w_log = -_softplus(-(w0 + _dot(wdec_t, jnp.tanh(wd).astype(bf16)))) - 0.5
    decay = jnp.exp(-jnp.exp(w_log))
    a = jax.nn.sigmoid(a0 + _dot(waaa_t, ad.astype(bf16)))
    kk = (k * k_k).reshape(R_HEADS, R_HEAD_DIM, n)
    kk = (kk * lax.rsqrt(jnp.maximum(jnp.sum(kk * kk, axis=1, keepdims=True), 1e-24))).reshape(R_WIDTH, n)
    k = k * (1.0 + (a - 1.0) * k_a)
    return r, decay, k, v, kk, kk * a


def _prompt_proj_body(x_ref, g_ref, wz_ref, wkv_ref, wn_ref, mu_ref, w0_ref, a0_ref, kk_ref, ka_ref, wdec_ref,
                      waaa_ref, r_o, w_o, k_o, v_o, kkn_o, b_o, kvcs_o, kvw_o, gr_o, q_o, gn_o, gl_o, zl_o, carry_ref):
    bf16 = jnp.bfloat16
    tm = x_ref.shape[1]
    sub = PROJ_SUB

    @pl.when(pl.program_id(1) == 0)
    def _():
        carry_ref[...] = jnp.zeros(carry_ref.shape, jnp.float32)

    first = lax.broadcasted_iota(jnp.int32, (1, sub), 1) == 0
    prev = carry_ref[:, LANES - 1:LANES]
    zs_all = []
    for h in range(tm // sub):
        rows = slice(h * sub, (h + 1) * sub)
        x = x_ref[0, rows, :]
        xn = (x * lax.rsqrt(jnp.mean(x * x, axis=-1, keepdims=True) + RMS_EPS) * g_ref[...]).astype(bf16)
        z = _dot_nt(wz_ref[...], xn)
        kv = _dot_nt(wkv_ref[...], xn)
        nrm = _dot(xn, wn_ref[...])
        kvcs_o[0, :, rows] = kv[0:2 * BRANCH_W]
        kvw_o[0, :, rows] = kv[2 * BRANCH_W:3 * BRANCH_W]
        gr_o[0, rows, :] = nrm[:, 0:R_WIDTH]
        q_o[0, rows, :] = nrm[:, R_WIDTH:R_WIDTH + N_WIDTH]
        gn_o[0, rows, :] = nrm[:, R_WIDTH + N_WIDTH:R_WIDTH + 2 * N_WIDTH]
        gl_o[0, rows, :] = nrm[:, R_WIDTH + 2 * N_WIDTH:]
        zp = jnp.where(first, prev, pltpu.roll(z, 1, axis=1))
        prev = z[:, sub - 1:sub]
        zs_all.append((rows, z, z + (zp - z) * mu_ref[...]))
    for rows, z, zs in zs_all:
        outs = _rwkv_prep(zs, w0_ref[...], a0_ref[...], kk_ref[...], ka_ref[...], wdec_ref[...], waaa_ref[...])
        for o_ref, val in zip((r_o, w_o, k_o, v_o, kkn_o, b_o), outs):
            o_ref[0, :, rows] = val
    z_last = zs_all[-1][1][:, sub - LANES:sub]
    carry_ref[...] = z_last
    zl_o[0] = z_last


def prompt_in_proj(x, norm_g, w_in, mu_shift, w0, w_decay_up, a0, w_aaa_up, k_k, k_a):
    B, T, D = x.shape
    tm = PROJ_TM
    assert T % tm == 0 and tm % PROJ_SUB == 0 and PROJ_SUB % LANES == 0
    f32 = jnp.float32
    pw = _proj_weights(norm_g, w_in, mu_shift, w0, w_decay_up, a0, w_aaa_up, k_k, k_a)
    consts = [pw["g"], pw["wz_t"], pw["wkv"].T, pw["wn"]] + pw["cols"] + pw["lora"]
    cm = lambda c: pl.BlockSpec((1, c, tm), lambda b, i: (b, 0, i))
    tok = lambda c: pl.BlockSpec((1, tm, c), lambda b, i: (b, i, 0))
    cms = lambda c: jax.ShapeDtypeStruct((B, c, T), f32)
    toks = lambda c: jax.ShapeDtypeStruct((B, T, c), f32)
    return pl.pallas_call(
        _prompt_proj_body,
        grid=(B, T // tm),
        in_specs=[tok(D)] + [_const_spec(c.shape, 2) for c in consts],
        out_specs=[cm(R_WIDTH)] * 6 + [cm(2 * BRANCH_W), cm(BRANCH_W), tok(R_WIDTH), tok(N_WIDTH), tok(N_WIDTH),
                                       tok(LANES), pl.BlockSpec((1, R_SHIFT_WIDTH, LANES), lambda b, i: (b, 0, 0))],
        out_shape=[cms(R_WIDTH)] * 6 + [cms(2 * BRANCH_W), cms(BRANCH_W), toks(R_WIDTH), toks(N_WIDTH), toks(N_WIDTH),
                                        toks(LANES), jax.ShapeDtypeStruct((B, R_SHIFT_WIDTH, LANES), f32)],
        scratch_shapes=[pltpu.VMEM((R_SHIFT_WIDTH, LANES), f32)],
        compiler_params=pltpu.CompilerParams(dimension_semantics=("parallel", "arbitrary"),
                                             vmem_limit_bytes=VMEM_LIMIT),
        name="prompt_in_proj",
    )(x, *consts)


def _proj_weights(norm_g, w_in, mu_shift, w0, w_decay_up, a0, w_aaa_up, k_k, k_a):
    f32, bf16 = jnp.float32, jnp.bfloat16
    D = w_in.shape[0]
    o = PROJ_OFFS
    wb = w_in.astype(bf16)
    gl_pad = LANES - PROJ_SPLITS[5]
    wn = jnp.concatenate([wb[:, o[1]:o[4]], wb[:, o[5]:o[6]], jnp.zeros((D, gl_pad), bf16)], axis=1)
    col = lambda p: p.astype(f32).reshape(-1, 1)
    return dict(g=norm_g.astype(f32).reshape(1, D), wz_t=wb[:, o[0]:o[1]].T, wkv=wb[:, o[4]:o[5]], wn=wn,
                cols=[col(mu_shift), col(w0), col(a0), col(k_k), col(k_a)],
                lora=[w_decay_up.astype(bf16).T, w_aaa_up.astype(bf16).T])


def _sample_proj_body(x_ref, prev_ref, g_ref, wz_ref, wkv_ref, wn_ref, mu_ref, w0_ref, a0_ref, kk_ref, ka_ref, wdec_ref,
                      waaa_ref, r_o, w_o, k_o, v_o, kkn_o, b_o, z_o, kv_o, gr_o, q_o, gn_o, gl_o, *, steps):
    bf16 = jnp.bfloat16
    m = x_ref.shape[0]
    x = x_ref[...]
    xn = (x * lax.rsqrt(jnp.mean(x * x, axis=-1, keepdims=True) + RMS_EPS) * g_ref[...]).astype(bf16)
    z = _dot_nt(wz_ref[...], xn)
    kv_o[...] = _dot(xn, wkv_ref[...])
    nrm = _dot(xn, wn_ref[...])
    gr_o[...] = nrm[:, 0:R_WIDTH]
    q_o[...] = nrm[:, R_WIDTH:R_WIDTH + N_WIDTH]
    gn_o[...] = nrm[:, R_WIDTH + N_WIDTH:R_WIDTH + 2 * N_WIDTH]
    gl_o[...] = nrm[:, R_WIDTH + 2 * N_WIDTH:]
    z_o[...] = z
    first = lax.broadcasted_iota(jnp.int32, (1, m), 1) % steps == 0
    zp = jnp.where(first, prev_ref[...], pltpu.roll(z, 1, axis=1))
    zs = z + (zp - z) * mu_ref[...]
    outs = _rwkv_prep(zs, w0_ref[...], a0_ref[...], kk_ref[...], ka_ref[...], wdec_ref[...], waaa_ref[...])
    for o_ref, val in zip((r_o, w_o, k_o, v_o, kkn_o, b_o), outs):
        o_ref[...] = val


def sample_in_proj(x, shift0, norm_g, w_in, mu_shift, w0, w_decay_up, a0, w_aaa_up, k_k, k_a):
    B, T, D = x.shape
    M = B * T
    f32 = jnp.float32
    pw = _proj_weights(norm_g, w_in, mu_shift, w0, w_decay_up, a0, w_aaa_up, k_k, k_a)
    prev = jnp.repeat(shift0.astype(f32).T, T, axis=1)
    args = [x.reshape(M, D), prev, pw["g"], pw["wz_t"], pw["wkv"], pw["wn"]] + pw["cols"] + pw["lora"]
    cm = lambda c: jax.ShapeDtypeStruct((c, M), f32)
    tok = lambda c: jax.ShapeDtypeStruct((M, c), f32)
    out_shape = [cm(R_WIDTH)] * 6 + [cm(R_SHIFT_WIDTH), tok(KV_WIDTH), tok(R_WIDTH), tok(N_WIDTH), tok(N_WIDTH),
                                     tok(LANES)]
    return pl.pallas_call(
        functools.partial(_sample_proj_body, steps=T),
        grid=(1,),
        in_specs=[_const_spec(a.shape, 1) for a in args],
        out_specs=[_const_spec(s.shape, 1) for s in out_shape],
        out_shape=out_shape,
        compiler_params=pltpu.CompilerParams(dimension_semantics=("arbitrary",), vmem_limit_bytes=VMEM_LIMIT),
        name="sample_in_proj",
    )(*args)


def _tree_sum(terms):
    while len(terms) > 1:
        terms = [terms[i] + terms[i + 1] for i in range(0, len(terms) - 1, 2)] + terms[len(terms) & ~1:]
    return terms[0]


def _rwkv_scan_body(r_ref, w_ref, k_ref, v_ref, kk_ref, b_ref, s0_ref, rk_ref, gw_ref, gb_ref, y_ref, s_ref):
    n = s_ref.shape[0]

    @pl.when(pl.program_id(1) == 0)
    def _():
        s_ref[...] = s0_ref[...]

    def step(t, carry):
        row = lambda ref, j: ref[t, pl.ds(j, 1), :]
        v = v_ref[t]
        grp = RWKV_COLS

        def p1(g, sa):
            return sa + _tree_sum([s_ref[g * grp + u] * row(kk_ref, g * grp + u) for u in range(grp)])

        sa = lax.fori_loop(0, n // grp, p1, jnp.zeros(v.shape, jnp.float32))

        def p2(g, y):
            terms = []
            for u in range(grp):
                j = g * grp + u
                s_j = s_ref[j] * row(w_ref, j) - sa * row(b_ref, j) + v * row(k_ref, j)
                s_ref[j] = s_j
                terms.append(s_j * row(r_ref, j))
            return y + _tree_sum(terms)

        y_ref[t] = lax.fori_loop(0, n // grp, p2, jnp.zeros(v.shape, jnp.float32))
        return carry

    lax.fori_loop(0, r_ref.shape[0], step, 0)

    y = y_ref[...]
    yc = y - jnp.mean(y, axis=1, keepdims=True)
    y = yc * lax.rsqrt(jnp.mean(yc * yc, axis=1, keepdims=True) + GN_EPS)
    y = y * gw_ref[...] + gb_ref[...]
    y_ref[...] = y + jnp.sum(r_ref[...] * k_ref[...] * rk_ref[...], axis=1, keepdims=True) * v_ref[...]


def rwkv_scan_lanes(seqs, s0, r_k, gn_w, gn_b, n_batch):
    T, N, BH = seqs[0].shape
    assert BH % LANES == 0
    tc = min(RWKV_TC, T)
    assert T % tc == 0
    per_lane = lambda p: jnp.tile(p.astype(jnp.float32).reshape(BH // n_batch, N).T, (1, n_batch))
    seq = pl.BlockSpec((tc, N, LANES), lambda l, t: (t, 0, l))
    state = pl.BlockSpec((N, N, LANES), lambda l, t: (0, 0, l))
    param = pl.BlockSpec((N, LANES), lambda l, t: (0, l))
    return pl.pallas_call(
        _rwkv_scan_body,
        grid=(BH // LANES, T // tc),
        in_specs=[seq] * 6 + [state] + [param] * 3,
        out_specs=[seq, state],
        out_shape=[jax.ShapeDtypeStruct((T, N, BH), jnp.float32), jax.ShapeDtypeStruct((N, N, BH), jnp.float32)],
        compiler_params=pltpu.CompilerParams(dimension_semantics=("parallel", "arbitrary"),
                                             vmem_limit_bytes=VMEM_LIMIT),
        name="rwkv_scan",
    )(*seqs, s0, per_lane(r_k), per_lane(gn_w), per_lane(gn_b))


def _lanes_from_cm(a, T):
    return a.reshape(-1, R_HEAD_DIM, T).transpose(2, 1, 0)


def _y_from_lanes(y, B):
    T, N, _ = y.shape
    return y.reshape(T, N, B, R_HEADS).transpose(2, 0, 3, 1).reshape(B, T, R_WIDTH)


def _state_to_lanes(s):
    B, H, N, _ = s.shape
    return s.astype(jnp.float32).transpose(3, 2, 0, 1).reshape(N, N, B * H)


def _state_from_lanes(s, B):
    N = s.shape[0]
    return s.reshape(N, N, B, R_HEADS).transpose(2, 3, 1, 0)


def _cmp_weights(w_pos, w_mix, n_rows):
    wp = jnp.broadcast_to(w_pos.astype(jnp.float32).reshape(2, 1, CMP_BLOCK, N_HEAD_DIM).transpose(2, 0, 1, 3),
                          (CMP_BLOCK, 2, N_KV_HEADS, N_HEAD_DIM)).reshape(CMP_BLOCK, BRANCH_W)
    wa = jnp.tile(wp[:CMP_STRIDE], (n_rows // CMP_STRIDE, 1)).T
    wb = jnp.tile(wp[CMP_STRIDE:], (n_rows // CMP_STRIDE, 1)).T
    wmix = jnp.zeros((BRANCH_W, BRANCH_W), jnp.float32)
    for e in range(2):
        for kh in range(N_KV_HEADS):
            o = (e * N_KV_HEADS + kh) * N_HEAD_DIM
            wmix = wmix.at[o:o + N_HEAD_DIM, o:o + N_HEAD_DIM].set(w_mix[e].astype(jnp.float32))
    return wa, wb, wmix.T


def _compress_prompt_body(kv_ref, wa_ref, wb_ref, pool_ref, wmix_ref, o_ref):
    x = kv_ref[0]
    xw = jnp.concatenate([x * wa_ref[...], x * wb_ref[...]], axis=1)
    o_ref[0] = _dot_x3(wmix_ref[...], _dot_exact_rhs(xw, pool_ref[...]))


def compress_prompt(kv_cs, w_pos, w_mix):
    B, _, T = kv_cs.shape
    assert CMP_BLOCK == 2 * CMP_STRIDE and T % CMP_STRIDE == 0
    ncp = T // CMP_STRIDE
    wa, wb, wmix_t = _cmp_weights(w_pos, w_mix, T)
    chunk = np.arange(T) // CMP_STRIDE
    pool = np.concatenate([chunk[:, None] == np.arange(ncp)[None, :],
                           chunk[:, None] == np.arange(ncp)[None, :] + 1], axis=0)
    pool = jnp.asarray(pool, jnp.bfloat16)
    return pl.pallas_call(
        _compress_prompt_body,
        grid=(B,),
        in_specs=[pl.BlockSpec((1, BRANCH_W, T), lambda b: (b, 0, 0))] + [_const_spec(c.shape, 1)
                                                                          for c in (wa, wb, pool, wmix_t)],
        out_specs=pl.BlockSpec((1, BRANCH_W, ncp), lambda b: (b, 0, 0)),
        out_shape=jax.ShapeDtypeStruct((B, BRANCH_W, ncp), jnp.float32),
        compiler_params=pltpu.CompilerParams(dimension_semantics=("parallel",), vmem_limit_bytes=VMEM_LIMIT),
        name="compress_prompt",
    )(kv_cs, wa, wb, pool, wmix_t)


def _nsa_prompt_body(q_ref, ksel_ref, kwin_ref, kc_ref, gl_ref, gn_ref, ovt_ref, ex_ref, o_ref):
    f32, bf16 = jnp.float32, jnp.bfloat16
    TQ, KC, G = NSA_TQ, NSA_KC, N_GROUP
    ncp = kc_ref.shape[2]
    n_sel = ovt_ref.shape[0]
    topk = min(SEL_TOPK, n_sel)
    qs = pl.program_id(1) * TQ
    qpos = qs + lax.broadcasted_iota(jnp.int32, (TQ, 1), 0)
    lo_half = lax.broadcasted_iota(jnp.int32, (TQ, LANES), 1) < N_HEAD_DIM

    def q_padded(h, half):
        tile = q_ref[0, :, (h // 2) * LANES:(h // 2 + 1) * LANES] * QSCALE
        if h % 2 != half:
            tile = pltpu.roll(tile, N_HEAD_DIM, axis=1)
        return jnp.where(lo_half if half == 0 else jnp.logical_not(lo_half), tile, 0.0)

    wstart = pl.multiple_of(jnp.maximum(qs - WINDOW, 0), TQ)
    kw = kwin_ref[0, 0:LANES, pl.ds(wstart, WINDOW + TQ)].astype(bf16)
    vw = kwin_ref[0, LANES:2 * LANES, pl.ds(wstart, WINDOW + TQ)].astype(bf16)
    kposw = wstart + lax.broadcasted_iota(jnp.int32, (1, WINDOW + TQ), 1)
    mask_w = (kposw <= qpos) & (kposw > qpos - WINDOW)

    kck = kc_ref[0, 0:LANES, :]
    kcv = kc_ref[0, LANES:2 * LANES, :].astype(bf16)
    kend = lax.broadcasted_iota(jnp.int32, (1, ncp), 1) * CMP_STRIDE + (CMP_BLOCK - 1)
    mask_c = kend <= qpos

    sid_t = lax.broadcasted_iota(jnp.int32, (n_sel, 1), 0)
    cur_t = (qs + lax.broadcasted_iota(jnp.int32, (1, TQ), 1)) // SEL_BLOCK
    valid_t = sid_t <= cur_t
    forced_t = (sid_t == 0) | (sid_t == cur_t) | (sid_t == cur_t - 1)
    n_chunks = qs // KC + 1

    H = N_HEADS
    qg = jnp.concatenate([q_padded(h, h // G) for h in range(H)], axis=0)
    qb = qg.astype(bf16)

    p = _masked_softmax_rows(_dot_x3(qg, kck).reshape(H, TQ, ncp), mask_c[None], base2=True)
    o_c = _dot_nt(p.reshape(H * TQ, ncp).astype(bf16), kcv).reshape(H, TQ, LANES)

    ovt = ovt_ref[...]
    selb = []
    for kvh in range(N_KV_HEADS):
        hi, mid, lo = _split3(jnp.sum(p[kvh * G:(kvh + 1) * G], axis=0))
        score = _dot_nt(ovt, hi) + _dot_nt(ovt, mid) + _dot_nt(ovt, lo)
        score = jnp.where(valid_t, score + jnp.where(forced_t, SEL_BIAS, 0.0), -SEL_BIAS)
        rank = jnp.zeros((n_sel, TQ), jnp.int32)
        for s2 in range(n_sel):
            row = score[s2:s2 + 1, :]
            ahead = (row > score) | ((row == score) & (s2 < sid_t))
            rank = rank + ahead.astype(jnp.int32)
        selb_t = jnp.concatenate([jnp.where((rank < topk) & valid_t, 0.0, SEL_NEG),
                                  jnp.zeros((LANES - n_sel, TQ), f32)], axis=0)
        selb += [selb_t.T.astype(bf16)] * G
    qa = jnp.concatenate([qb, jnp.concatenate(selb, axis=0)], axis=1)

    def pv_sums(e, v_t):
        n = v_t.shape[1]
        ones = jnp.ones((N_HEAD_DIM, n), bf16)
        va = (jnp.concatenate([v_t[0:N_HEAD_DIM], ones], axis=0), jnp.concatenate([ones, v_t[N_HEAD_DIM:]], axis=0))
        return jnp.concatenate([_dot_nt(e[kvh * G:(kvh + 1) * G].reshape(G * TQ, n), va[kvh]).reshape(G, TQ, LANES)
                                for kvh in range(N_KV_HEADS)], axis=0)

    def normalised(accl):
        return accl / pltpu.roll(accl, N_HEAD_DIM, axis=2)

    def chunk(c, carry, causal):
        m, accl = carry
        k0 = pl.multiple_of(c * KC, KC)
        ka = jnp.concatenate([ksel_ref[0, 0:LANES, pl.ds(k0, KC)].astype(bf16), ex_ref[:, pl.ds(k0, KC)]],
                             axis=0)
        sc = _dot(qa, ka).reshape(H, TQ, KC)
        if causal:
            kpos = k0 + lax.broadcasted_iota(jnp.int32, (1, KC), 1)
            sc = jnp.where((kpos <= qpos)[None], sc, SEL_NEG)
        m_new = jnp.maximum(m, jnp.max(sc, axis=-1, keepdims=True))
        e = jnp.exp2(sc - m_new).astype(bf16)
        return m_new, jnp.exp2(m - m_new) * accl + pv_sums(e, ksel_ref[0, LANES:2 * LANES, pl.ds(k0, KC)].astype(bf16))

    carry = (jnp.full((H, TQ, 1), SEL_NEG, f32), jnp.zeros((H, TQ, LANES), f32))
    carry = lax.fori_loop(0, n_chunks - 1, functools.partial(chunk, causal=False), carry)
    o_s = normalised(chunk(n_chunks - 1, carry, causal=True)[1])

    sw = jnp.where(mask_w[None], _dot(qb, kw).reshape(H, TQ, WINDOW + TQ), MASK_NEG)
    o_w = normalised(pv_sums(jnp.exp2(sw - jnp.max(sw, axis=-1, keepdims=True)).astype(bf16), vw))

    gates = jax.nn.sigmoid(gl_ref[0])
    heads = []
    for h in range(H):
        o = (gates[:, h:h + 1] * o_c[h] + gates[:, H + h:H + h + 1] * o_s[h]
             + gates[:, 2 * H + h:2 * H + h + 1] * o_w[h])
        heads.append(pltpu.roll(o, N_HEAD_DIM, axis=1) if h % 2 != h // G else o)

    for m in range(N_HEADS // 2):
        tile = jnp.where(lo_half, heads[2 * m], heads[2 * m + 1])
        o_ref[0, :, m * LANES:(m + 1) * LANES] = (
            tile * _silu(gn_ref[0, :, m * LANES:(m + 1) * LANES])).astype(o_ref.dtype)


def nsa_prompt(q, kv_cs, kv_w, kc, gl, gate_n):
    B, T, _ = q.shape
    TQ, KC = NSA_TQ, NSA_KC
    assert N_KV_HEADS == 2 and N_HEAD_DIM * 2 == LANES and T % KC == 0 and KC % TQ == 0 and T >= WINDOW + TQ
    assert SEL_BLOCK % CMP_STRIDE == 0 and WINDOW % TQ == 0 and TQ % LANES == 0 and T // SEL_BLOCK <= LANES
    ncp = T // CMP_STRIDE
    n_sel = T // SEL_BLOCK
    cs = np.arange(ncp) * CMP_STRIDE
    ss = np.arange(n_sel) * SEL_BLOCK
    overlap = (cs[:, None] < ss[None, :] + SEL_BLOCK) & (cs[:, None] + CMP_BLOCK > ss[None, :])
    overlap[ncp - 1] = False
    overlap_t = jnp.asarray(overlap.T, jnp.bfloat16)
    expand = jnp.asarray(np.arange(T)[None, :] // SEL_BLOCK == np.arange(LANES)[:, None], jnp.bfloat16)
    return pl.pallas_call(
        _nsa_prompt_body,
        grid=(B, T // TQ),
        in_specs=[pl.BlockSpec((1, TQ, N_WIDTH), lambda b, i: (b, i, 0)),
                  pl.BlockSpec((1, BRANCH_W, T), lambda b, i: (b, 1, 0)),
                  pl.BlockSpec((1, BRANCH_W, T), lambda b, i: (b, 0, 0)),
                  pl.BlockSpec((1, BRANCH_W, ncp), lambda b, i: (b, 0, 0)),
                  pl.BlockSpec((1, TQ, LANES), lambda b, i: (b, i, 0)),
                  pl.BlockSpec((1, TQ, N_WIDTH), lambda b, i: (b, i, 0)),
                  _const_spec((n_sel, ncp), 2), _const_spec((LANES, T), 2)],
        out_specs=pl.BlockSpec((1, TQ, N_WIDTH), lambda b, i: (b, i, 0)),
        out_shape=jax.ShapeDtypeStruct((B, T, N_WIDTH), jnp.bfloat16),
        compiler_params=pltpu.CompilerParams(dimension_semantics=("parallel", "arbitrary"),
                                             vmem_limit_bytes=VMEM_LIMIT),
        name="nsa_prompt",
    )(q, kv_cs, kv_w, kc, gl, gate_n, overlap_t, expand)


def _page_spec(k, branch, pages_per_step):
    return pl.BlockSpec((1, BRANCH_W, PAGE_SIZE), lambda b, g, pt: (pt[b, g * pages_per_step + k], branch, 0))


def _sample_compress_body(pt_ref, *refs):
    pages = refs[:SAMPLE_PG_CMP]
    wa_ref, wb_ref, wmix_ref, oa_ref, ob_ref = refs[SAMPLE_PG_CMP:]
    cpp = PAGE_SIZE // CMP_STRIDE
    xt = [p[0].T.reshape(cpp, CMP_STRIDE, BRANCH_W) for p in pages]
    for w_ref, o_ref in ((wa_ref, oa_ref), (wb_ref, ob_ref)):
        pooled = jnp.concatenate([jnp.sum(x * w_ref[...], axis=1) for x in xt], axis=0)
        o_ref[0] = _dot_x3(pooled, wmix_ref[...])


def sample_compress(cache_t, page_table, w_pos, w_mix):
    B, n_pages = page_table.shape
    PG = SAMPLE_PG_CMP
    assert n_pages % PG == 0 and CMP_BLOCK == 2 * CMP_STRIDE and PAGE_SIZE % CMP_STRIDE == 0
    rows = PG * PAGE_SIZE // CMP_STRIDE
    ncp = n_pages * PAGE_SIZE // CMP_STRIDE
    wa, wb, wmix_t = _cmp_weights(w_pos, w_mix, CMP_STRIDE)
    wa, wb, wmix = wa.T[None], wb.T[None], wmix_t.T
    const = lambda shape: pl.BlockSpec(shape, lambda b, g, pt: (0,) * len(shape))
    out = pl.BlockSpec((1, rows, BRANCH_W), lambda b, g, pt: (b, g, 0))
    return pl.pallas_call(
        _sample_compress_body,
        grid_spec=pltpu.PrefetchScalarGridSpec(
            num_scalar_prefetch=1, grid=(B, n_pages // PG),
            in_specs=[_page_spec(k, 0, PG) for k in range(PG)] + [const(wa.shape), const(wb.shape), const(wmix.shape)],
            out_specs=[out, out]),
        out_shape=[jax.ShapeDtypeStruct((B, ncp, BRANCH_W), jnp.float32)] * 2,
        compiler_params=pltpu.CompilerParams(dimension_semantics=("parallel", "arbitrary"),
                                             vmem_limit_bytes=VMEM_LIMIT),
        name="sample_compress",
    )(page_table, *([cache_t] * PG), wa, wb, wmix)


def _online_chunk(m, l, acc, s, keep, v, v_channels_major=False):
    s = jnp.where(keep, s, MASK_NEG)
    m_new = jnp.maximum(m, jnp.max(s, axis=-1, keepdims=True))
    alpha = jnp.exp(m - m_new)
    e = jnp.where(keep, jnp.exp(s - m_new), 0.0)
    eb = e.astype(jnp.bfloat16)
    pv = _dot_nt(eb, v) if v_channels_major else _dot(eb, v)
    return m_new, alpha * l + jnp.sum(e, axis=-1, keepdims=True), alpha * acc + pv


def _sample_attend_body(pt_ref, *refs, past):
    f32, bf16 = jnp.float32, jnp.bfloat16
    PG, G = SAMPLE_PG_ATT, N_GROUP
    pages = refs[:PG]
    (ka_ref, kb_ref, q_ref, kvn_ref, gl_ref, gn_ref, win_ref, ex_ref, ov_ref, o_ref,
     qf_ref, selb_ref, tail_ref, m_ref, l_ref, acc_ref, oc_ref) = refs[PG:]
    T = q_ref.shape[1]
    R = N_HEADS * T
    ncp = ka_ref.shape[1]
    nsp = ov_ref.shape[1]
    KC = PG * PAGE_SIZE
    n_past = past // SEL_BLOCK
    bpc = KC // SEL_BLOCK
    g = pl.program_id(1)
    lo_half = lax.broadcasted_iota(jnp.int32, (T, LANES), 1) < N_HEAD_DIM
    row = lax.broadcasted_iota(jnp.int32, (R, 1), 0)
    qpos = past + row % T

    @pl.when(g == 0)
    def _():
        pieces = []
        for h in range(N_HEADS):
            kvh = h // G
            tile = q_ref[0, :, (h // 2) * LANES:(h // 2 + 1) * LANES] * ATTN_SCALE
            if h % 2 != kvh:
                tile = pltpu.roll(tile, N_HEAD_DIM, axis=1)
            pieces.append(jnp.where(lo_half if kvh == 0 else jnp.logical_not(lo_half), tile, 0.0))
        q32 = jnp.concatenate(pieces, axis=0)
        qf_ref[...] = q32

        kc = ka_ref[0] + pltpu.roll(kb_ref[0], ncp - 1, axis=0)
        kend = lax.broadcasted_iota(jnp.int32, (1, ncp), 1) * CMP_STRIDE + (CMP_BLOCK - 1)
        p = _masked_softmax_rows(_dot_nt_x3(q32, kc[:, 0:LANES]), kend <= qpos)
        oc_ref[...] = _dot(p.astype(bf16), kc[:, LANES:2 * LANES].astype(bf16))
        imp = jnp.concatenate(
            [sum(p[(kvh * G + gg) * T:(kvh * G + gg + 1) * T] for gg in range(G)) for kvh in range(N_KV_HEADS)],
            axis=0)

        sid = lax.broadcasted_iota(jnp.int32, (1, nsp), 1)
        sid_f = sid.astype(f32)
        cur = (past + lax.broadcasted_iota(jnp.int32, (N_KV_HEADS * T, 1), 0) % T) // SEL_BLOCK
        valid = sid <= cur
        forced = (sid == 0) | (sid == cur) | (sid == cur - 1)
        score = _dot_exact_rhs(imp, ov_ref[...])
        score = jnp.where(valid, score + jnp.where(forced, SEL_BIAS, 0.0), -SEL_BIAS)

        def extract(_, carry):
            sc, chosen = carry
            best = jnp.max(sc, axis=-1, keepdims=True)
            first = jnp.min(jnp.where(sc == best, sid_f, float(nsp)), axis=-1, keepdims=True)
            hit = sid_f == first
            return jnp.where(hit, TAKEN, sc), jnp.where(hit, 1.0, chosen)

        _, chosen = lax.fori_loop(0, min(SEL_TOPK, n_past + 1), extract, (score, jnp.zeros_like(score)))
        sel = jnp.concatenate([jnp.where(valid, chosen, 0.0)[(h // G) * T:(h // G + 1) * T] for h in range(N_HEADS)],
                              axis=0)
        tail_ref[...] = sel[:, n_past:n_past + 1]
        selb = jnp.where(sel > 0.5, 0.0, SEL_NEG)
        for c in range(n_past // bpc):
            selb_ref[c] = jnp.concatenate([selb[:, c * bpc:(c + 1) * bpc], jnp.zeros((R, LANES - bpc), f32)],
                                          axis=1).astype(bf16)
        m_ref[...] = jnp.full(m_ref.shape, SEL_NEG, f32)
        l_ref[...] = jnp.zeros(l_ref.shape, f32)
        acc_ref[...] = jnp.zeros(acc_ref.shape, f32)

    qb = qf_ref[...].astype(bf16)
    ka = jnp.concatenate([p[0, 0:LANES, :] for p in pages], axis=1).astype(bf16)
    ka = jnp.concatenate([ka, ex_ref[...]], axis=0)
    va = jnp.concatenate([jnp.concatenate([p[0, LANES:2 * LANES, :] for p in pages], axis=1).astype(bf16),
                          jnp.ones((LANES, KC), bf16)], axis=0)
    s = _dot(jnp.concatenate([qb, selb_ref[g]], axis=1), ka)
    m = jnp.maximum(m_ref[...], jnp.max(s, axis=-1, keepdims=True))
    alpha = jnp.exp(m_ref[...] - m)
    pv = _dot_nt(jnp.exp(s - m).astype(bf16), va)
    l = alpha * l_ref[...] + pv[:, LANES:]
    acc = alpha * acc_ref[...] + pv[:, :LANES]
    m_ref[...] = m
    l_ref[...] = l
    acc_ref[...] = acc

    @pl.when(g == pl.num_programs(1) - 1)
    def _():
        def new_rows(c0):
            return jnp.concatenate([kvn_ref[0, :, c0:c0 + LANES], jnp.zeros((LANES - T, LANES), f32)],
                                   axis=0).astype(bf16)

        npos = past + lax.broadcasted_iota(jnp.int32, (1, LANES), 1)
        is_new = npos < past + T
        keep_t = is_new & (npos <= qpos) & (tail_ref[...] > 0.5)
        _, l_s, acc_s = _online_chunk(m, l, acc, _dot_nt(qb, new_rows(BRANCH_W)), keep_t, new_rows(BRANCH_W + LANES))
        o_s = acc_s / l_s

        nb = win_ref.shape[2]
        bpos = past - nb + lax.broadcasted_iota(jnp.int32, (1, nb), 1)
        keep_b = (bpos <= qpos) & (bpos > qpos - WINDOW) & (bpos >= 0)
        mw, lw, aw = _online_chunk(jnp.full((R, 1), MASK_NEG, f32), jnp.zeros((R, 1), f32), jnp.zeros((R, LANES), f32),
                                   _dot(qb, win_ref[0, 0:LANES, :].astype(bf16)), keep_b,
                                   win_ref[0, LANES:2 * LANES, :].astype(bf16), True)
        keep_n = is_new & (npos <= qpos) & (npos > qpos - WINDOW)
        _, lw, aw = _online_chunk(mw, lw, aw, _dot_nt(qb, new_rows(2 * BRANCH_W)), keep_n,
                                  new_rows(2 * BRANCH_W + LANES))
        o_w = aw / lw

        gates = jax.nn.sigmoid(gl_ref[0])
        gcol = lambda br: jnp.concatenate([gates[:, br * N_HEADS + h:br * N_HEADS + h + 1] for h in range(N_HEADS)],
                                          axis=0)
        o = gcol(0) * oc_ref[...] + gcol(1) * o_s + gcol(2) * o_w
        for mt in range(N_HEADS // 2):
            halves = []
            for h in (2 * mt, 2 * mt + 1):
                piece = o[h * T:(h + 1) * T]
                if h % 2 != h // G:
                    piece = pltpu.roll(piece, N_HEAD_DIM, axis=1)
                halves.append(piece)
            o_ref[0, :, mt * LANES:(mt + 1) * LANES] = (jnp.where(lo_half, halves[0], halves[1])
                                                        * _silu(gn_ref[0, :, mt * LANES:(mt + 1) * LANES]))


def nsa_sample(q, kv_new, kc_a, kc_b, gl, gate_n, cache_t, page_table, win_t):
    B, T, _ = q.shape
    n_pages = page_table.shape[1]
    past = n_pages * PAGE_SIZE
    PG = SAMPLE_PG_ATT
    KC = PG * PAGE_SIZE
    ng = n_pages // PG
    assert n_pages % PG == 0 and past % SEL_BLOCK == 0 and T <= SEL_BLOCK and T <= LANES
    assert (past + T) // CMP_STRIDE == past // CMP_STRIDE and KC % SEL_BLOCK == 0
    ncp = past // CMP_STRIDE
    n_sel = past // SEL_BLOCK + 1
    nsp = -(-n_sel // LANES) * LANES
    cs = np.arange(ncp) * CMP_STRIDE
    ss = np.arange(nsp) * SEL_BLOCK
    overlap = (cs[:, None] < ss[None, :] + SEL_BLOCK) & (cs[:, None] + CMP_BLOCK > ss[None, :])
    overlap[ncp - 1] = False
    overlap[:, n_sel:] = False
    overlap = jnp.asarray(overlap, jnp.bfloat16)
    assert KC // SEL_BLOCK <= LANES
    expand = jnp.asarray(np.arange(KC)[None, :] // SEL_BLOCK == np.arange(LANES)[:, None], jnp.bfloat16)
    R = N_HEADS * T
    per_b = lambda shape: pl.BlockSpec((1,) + shape, lambda b, g, pt: (b, 0, 0))
    return pl.pallas_call(
        functools.partial(_sample_attend_body, past=past),
        grid_spec=pltpu.PrefetchScalarGridSpec(
            num_scalar_prefetch=1, grid=(B, ng),
            in_specs=[_page_spec(k, 1, PG) for k in range(PG)] + [
                per_b((ncp, BRANCH_W)), per_b((ncp, BRANCH_W)), per_b((T, N_WIDTH)), per_b((T, KV_WIDTH)),
                per_b((T, LANES)), per_b((T, N_WIDTH)), per_b((BRANCH_W, win_t.shape[2])),
                pl.BlockSpec((LANES, KC), lambda b, g, pt: (0, 0)),
                pl.BlockSpec((ncp, nsp), lambda b, g, pt: (0, 0))],
            out_specs=per_b((T, N_WIDTH)),
            scratch_shapes=[pltpu.VMEM((R, LANES), jnp.float32), pltpu.VMEM((ng, R, LANES), jnp.bfloat16),
                            pltpu.VMEM((R, 1), jnp.float32), pltpu.VMEM((R, 1), jnp.float32),
                            pltpu.VMEM((R, LANES), jnp.float32), pltpu.VMEM((R, LANES), jnp.float32),
                            pltpu.VMEM((R, LANES), jnp.float32)]),
        out_shape=jax.ShapeDtypeStruct((B, T, N_WIDTH), jnp.float32),
        compiler_params=pltpu.CompilerParams(dimension_semantics=("parallel", "arbitrary"),
                                             vmem_limit_bytes=VMEM_LIMIT),
        name="nsa_sample",
    )(page_table, *([cache_t] * PG), kc_a, kc_b, q, kv_new, gl, gate_n, win_t, expand, overlap)


def _out_proj_body(yr_ref, gr_ref, on_ref, x_ref, w_ref, g_ref, o_ref, *, final_norm):
    bf16 = jnp.bfloat16
    a = (yr_ref[...] * _silu(gr_ref[...])).astype(bf16)
    y = x_ref[...] + _dot(a, w_ref[0:R_WIDTH, :]) + _dot(on_ref[...].astype(bf16), w_ref[R_WIDTH:MIX_WIDTH, :])
    if final_norm:
        y = y * lax.rsqrt(jnp.mean(y * y, axis=-1, keepdims=True) + RMS_EPS) * g_ref[...]
    o_ref[...] = y


def out_proj(y_r, gate_r, o_n, x, w_out, final_g):
    M, D = x.shape
    tm = min(OUT_TM, M)
    assert M % tm == 0
    g = (jnp.ones((D,), jnp.float32) if final_g is None else final_g.astype(jnp.float32)).reshape(1, D)
    rows = lambda c: pl.BlockSpec((tm, c), lambda i: (i, 0))
    return pl.pallas_call(
        functools.partial(_out_proj_body, final_norm=final_g is not None),
        grid=(M // tm,),
        in_specs=[rows(R_WIDTH), rows(R_WIDTH), rows(N_WIDTH), rows(D), _const_spec((MIX_WIDTH, D), 1),
                  _const_spec((1, D), 1)],
        out_specs=rows(D),
        out_shape=jax.ShapeDtypeStruct((M, D), jnp.float32),
        compiler_params=pltpu.CompilerParams(dimension_semantics=("parallel",), vmem_limit_bytes=VMEM_LIMIT),
        name="out_proj",
    )(y_r, gate_r, o_n, x, w_out.astype(jnp.bfloat16), g)


def prompt_mixers(x, norm_g, w_in, mu_shift, w0, w_decay_up, a0, w_aaa_up, k_k, k_a, r_k, gn_w, gn_b,
                  w_cmp_pos, w_cmp_mix):
    B, T, D = x.shape
    (r, w, k, v, kk, b, kv_cs, kv_w, gate_r, q, gate_n, gl, z_last) = prompt_in_proj(
        x, norm_g, w_in, mu_shift, w0, w_decay_up, a0, w_aaa_up, k_k, k_a)
    seqs = [_lanes_from_cm(a, T) for a in (r, w, k, v, kk, b)]
    o_n = nsa_prompt(q, kv_cs, kv_w, compress_prompt(kv_cs, w_cmp_pos, w_cmp_mix), gl, gate_n)
    seqs, o_n = lax.optimization_barrier((seqs, o_n))
    s0 = jnp.zeros((R_HEAD_DIM, R_HEAD_DIM, B * R_HEADS), jnp.float32)
    y, s_fin = rwkv_scan_lanes(seqs, s0, r_k, gn_w, gn_b, B)
    heads = (N_KV_HEADS, N_HEAD_DIM)
    kv_rows = kv_cs.reshape((B, 4) + heads + (T,)).transpose(0, 4, 1, 2, 3)
    n_keep = min(WINDOW, T)
    win_rows = kv_w[:, :, T - n_keep:].reshape((B, 2) + heads + (n_keep,)).transpose(0, 4, 1, 2, 3)
    outs = (kv_rows, win_rows, _state_from_lanes(s_fin, B), z_last[:, :, LANES - 1])
    return (_y_from_lanes(y, B), gate_r, o_n), outs


def sample_mixers(x, cache, win_buf, wkv0, shift0, page_table, norm_g, w_in, mu_shift, w0, w_decay_up, a0,
                  w_aaa_up, k_k, k_a, r_k, gn_w, gn_b, w_cmp_pos, w_cmp_mix):
    B, T, D = x.shape
    (r, w, k, v, kk, b, z, kv, gate_r, q, gate_n, gl) = sample_in_proj(
        x, shift0, norm_g, w_in, mu_shift, w0, w_decay_up, a0, w_aaa_up, k_k, k_a)
    seqs = [a.reshape(R_HEADS, R_HEAD_DIM, B, T).transpose(3, 1, 2, 0).reshape(T, R_HEAD_DIM, B * R_HEADS)
            for a in (r, w, k, v, kk, b)]
    y, s_fin = rwkv_scan_lanes(seqs, _state_to_lanes(wkv0), r_k, gn_w, gn_b, B)

    cache_t = cache.reshape(cache.shape[0], PAGE_SIZE, 2 * BRANCH_W).transpose(0, 2, 1)
    nb = win_buf.shape[1]
    win_t = win_buf.reshape(B, nb, BRANCH_W).transpose(0, 2, 1)
    kc_a, kc_b = sample_compress(cache_t, page_table, w_cmp_pos, w_cmp_mix)
    kv = kv.reshape(B, T, KV_WIDTH)
    o_n = nsa_sample(q.reshape(B, T, N_WIDTH), kv, kc_a, kc_b, gl.reshape(B, T, LANES),
                     gate_n.reshape(B, T, N_WIDTH), cache_t, page_table, win_t)
    kv = kv.reshape(B, T, KV_SLOTS, N_KV_HEADS, N_HEAD_DIM)
    keys_w = jnp.concatenate([win_buf.astype(kv.dtype), kv[:, :, 4:6]], axis=1)
    n_keep = min(WINDOW, nb + T)
    shift = z.reshape(R_SHIFT_WIDTH, B, T)[:, :, T - 1].T
    outs = (kv[:, :, 0:4], keys_w[:, nb + T - n_keep:], _state_from_lanes(s_fin, B), shift)
    return (_y_from_lanes(y, B), gate_r.reshape(B, T, R_WIDTH), o_n), outs


def _finish(mixed, x, w_out, final_g):
    B, T, D = x.shape
    y_r, gate_r, o_n = (a.reshape(B * T, -1) for a in mixed)
    return out_proj(y_r, gate_r, o_n, x.reshape(B * T, D), w_out, final_g).reshape(B, T, D)


def kernel(x_prompt, x_sample, cache_kv, cache_kv_win, state_wkv, state_shift, page_table,
           norm_g, w_in, mu_shift, w0, w_decay_up, a0, w_aaa_up, k_k, k_a, r_k, gn_w, gn_b,
           w_cmp_pos, w_cmp_mix, w_out, final_g):
    y_p, y_s = x_prompt, x_sample
    kvp, kvs, wnp, wns, skp, sks, shp, shs = [], [], [], [], [], [], [], []
    for layer in range(DEPTH):
        lw = (norm_g[layer], w_in[layer], mu_shift[layer], w0[layer], w_decay_up[layer], a0[layer],
              w_aaa_up[layer], k_k[layer], k_a[layer], r_k[layer], gn_w[layer], gn_b[layer],
              w_cmp_pos[layer], w_cmp_mix[layer])
        fg = final_g if layer == DEPTH - 1 else None
        mixed_p, (p_kv, p_win, p_wkv, p_shift) = prompt_mixers(y_p, *lw)
        mixed_s, (s_kv, s_win, s_wkv, s_shift) = sample_mixers(y_s, cache_kv[layer], cache_kv_win[layer],
                                                               state_wkv[layer], state_shift[layer], page_table, *lw)
        mixed_p, mixed_s = lax.optimization_barrier((mixed_p, mixed_s))
        y_p = _finish(mixed_p, y_p, w_out[layer], fg)
        y_s = _finish(mixed_s, y_s, w_out[layer], fg)
        kvp.append(p_kv); kvs.append(s_kv); wnp.append(p_win); wns.append(s_win)
        skp.append(p_wkv); sks.append(s_wkv); shp.append(p_shift); shs.append(s_shift)
    return (y_p, y_s, jnp.stack(kvp), jnp.stack(kvs), jnp.stack(wnp), jnp.stack(wns),
            jnp.stack(skp), jnp.stack(sks), jnp.stack(shp), jnp.stack(shs))
```

```python
import functools

import jax, jax.numpy as jnp
from jax import lax
import numpy as np
from jax.experimental import pallas as pl
from jax.experimental.pallas import tpu as pltpu

D_MODEL = 1024
DEPTH = 1
PAGE_SIZE = 128

R_HEAD_DIM = 64
R_WIDTH = D_MODEL // 2
R_HEADS = R_WIDTH // R_HEAD_DIM
DECAY_LORA = 64
AAA_LORA = 64
R_SHIFT_WIDTH = 3 * R_WIDTH + DECAY_LORA + AAA_LORA
N_HEAD_DIM = 64
N_WIDTH = D_MODEL - R_WIDTH
N_HEADS = N_WIDTH // N_HEAD_DIM
N_KV_HEADS = 2
N_GROUP = N_HEADS // N_KV_HEADS
N_BRANCH = 3
KV_SLOTS = 2 * N_BRANCH
CMP_BLOCK = 32
CMP_STRIDE = 16
SEL_BLOCK = 64
SEL_TOPK = 16
WINDOW = 512
MIX_WIDTH = R_WIDTH + N_WIDTH
KV_WIDTH = KV_SLOTS * N_KV_HEADS * N_HEAD_DIM
PROJ_SPLITS = (R_SHIFT_WIDTH, R_WIDTH, N_WIDTH, N_WIDTH, KV_WIDTH, N_BRANCH * N_HEADS)
PROJ_OFFS = tuple(int(o) for o in np.cumsum((0,) + PROJ_SPLITS))
RMS_EPS = 1e-6
GN_EPS = 64e-5
MASK_NEG = -1e30
SEL_NEG = -(2.0 ** 100)
TAKEN = -3e38
SEL_BIAS = 1e4
ATTN_SCALE = N_HEAD_DIM ** -0.5
QSCALE = ATTN_SCALE * float(np.log2(np.e))

LANES = 128
MXU_DEPTH = 256
BRANCH_W = 2 * N_KV_HEADS * N_HEAD_DIM
NSA_TQ = 256
NSA_KC = 512
PROJ_TM = 512
PROJ_SUB = 256
OUT_TM = 512
SAMPLE_PG_CMP = 16
SAMPLE_PG_ATT = 32
RWKV_TC = 32
RWKV_COLS = 16
VMEM_LIMIT = 56 * 1024 * 1024


def _split3(x):
    hi = x.astype(jnp.bfloat16)
    r1 = x - hi.astype(jnp.float32)
    mid = r1.astype(jnp.bfloat16)
    lo = (r1 - mid.astype(jnp.float32)).astype(jnp.bfloat16)
    return hi, mid, lo


def _dot(a, b):
    return jnp.dot(a, b, preferred_element_type=jnp.float32)


def _dot_nt(a, b):
    return lax.dot_general(a, b, (((1,), (1,)), ((), ())), preferred_element_type=jnp.float32)


def _dot_exact_rhs(a, b01):
    hi, mid, lo = _split3(a)
    return _dot(hi, b01) + _dot(mid, b01) + _dot(lo, b01)


def _dot_nt_x3(a, b):
    ah = a.astype(jnp.bfloat16)
    al = (a - ah.astype(jnp.float32)).astype(jnp.bfloat16)
    bh = b.astype(jnp.bfloat16)
    bl = (b - bh.astype(jnp.float32)).astype(jnp.bfloat16)
    return _dot_nt(ah, bh) + _dot_nt(ah, bl) + _dot_nt(al, bh)


def _dot_x3(a, b):
    ah = a.astype(jnp.bfloat16)
    al = (a - ah.astype(jnp.float32)).astype(jnp.bfloat16)
    bh = b.astype(jnp.bfloat16)
    bl = (b - bh.astype(jnp.float32)).astype(jnp.bfloat16)
    if 2 * a.shape[1] <= MXU_DEPTH:
        return _dot(jnp.concatenate([ah, al], axis=1), jnp.concatenate([bh, bh], axis=0)) + _dot(ah, bl)
    return _dot(ah, bh) + _dot(al, bh) + _dot(ah, bl)


def _silu(x):
    return x * jax.nn.sigmoid(x)


def _softplus(x):
    return jnp.maximum(x, 0.0) + jnp.log1p(jnp.exp(-jnp.abs(x)))


def _masked_softmax_rows(s, mask, base2=False):
    s = jnp.where(mask, s, MASK_NEG)
    e = (jnp.exp2 if base2 else jnp.exp)(s - jnp.max(s, axis=-1, keepdims=True))
    p = e / jnp.sum(e, axis=-1, keepdims=True)
    return jnp.where(mask, p, 0.0)


def _const_spec(shape, n_grid):
    zeros = (0,) * len(shape)
    if n_grid == 1:
        return pl.BlockSpec(shape, lambda i: zeros)
    return pl.BlockSpec(shape, lambda b, i: zeros)


def _rwkv_prep(zs, w0, a0, k_k, k_a, wdec_t, waaa_t):
    bf16 = jnp.bfloat16
    n = zs.shape[1]
    r = zs[0:R_WIDTH]
    k = zs[R_WIDTH:2 * R_WIDTH]
    v = zs[2 * R_WIDTH:3 * R_WIDTH]
    wd = zs[3 * R_WIDTH:3 * R_WIDTH + DECAY_LORA]
    ad = zs[3 * R_WIDTH + DECAY_LORA:R_SHIFT_WIDTH]
    w_log = -_softplus(-(w0 + _dot(wdec_t, jnp.tanh(wd).astype(bf16)))) - 0.5
    decay = jnp.exp(-jnp.exp(w_log))
    a = jax.nn.sigmoid(a0 + _dot(waaa_t, ad.astype(bf16)))
    kk = (k * k_k).reshape(R_HEADS, R_HEAD_DIM, n)
    kk = (kk * lax.rsqrt(jnp.maximum(jnp.sum(kk * kk, axis=1, keepdims=True), 1e-24))).reshape(R_WIDTH, n)
    k = k * (1.0 + (a - 1.0) * k_a)
    return r, decay, k, v, kk, kk * a


def _prompt_proj_body(x_ref, g_ref, wz_ref, wkv_ref, wn_ref, mu_ref, w0_ref, a0_ref, kk_ref, ka_ref, wdec_ref,
                      waaa_ref, r_o, w_o, k_o, v_o, kkn_o, b_o, kvcs_o, kvw_o, gr_o, q_o, gn_o, gl_o, zl_o, carry_ref):
    bf16 = jnp.bfloat16
    tm = x_ref.shape[1]
    sub = PROJ_SUB

    @pl.when(pl.program_id(1) == 0)
    def _():
        carry_ref[...] = jnp.zeros(carry_ref.shape, jnp.float32)

    first = lax.broadcasted_iota(jnp.int32, (1, sub), 1) == 0
    prev = carry_ref[:, LANES - 1:LANES]
    zs_all = []
    for h in range(tm // sub):
        rows = slice(h * sub, (h + 1) * sub)
        x = x_ref[0, rows, :]
        xn = (x * lax.rsqrt(jnp.mean(x * x, axis=-1, keepdims=True) + RMS_EPS) * g_ref[...]).astype(bf16)
        z = _dot_nt(wz_ref[...], xn)
        kv = _dot_nt(wkv_ref[...], xn)
        nrm = _dot(xn, wn_ref[...])
        kvcs_o[0, :, rows] = kv[0:2 * BRANCH_W]
        kvw_o[0, :, rows] = kv[2 * BRANCH_W:3 * BRANCH_W]
        gr_o[0, rows, :] = nrm[:, 0:R_WIDTH]
        q_o[0, rows, :] = nrm[:, R_WIDTH:R_WIDTH + N_WIDTH]
        gn_o[0, rows, :] = nrm[:, R_WIDTH + N_WIDTH:R_WIDTH + 2 * N_WIDTH]
        gl_o[0, rows, :] = nrm[:, R_WIDTH + 2 * N_WIDTH:]
        zp = jnp.where(first, prev, pltpu.roll(z, 1, axis=1))
        prev = z[:, sub - 1:sub]
        zs_all.append((rows, z, z + (zp - z) * mu_ref[...]))
    for rows, z, zs in zs_all:
        outs = _rwkv_prep(zs, w0_ref[...], a0_ref[...], kk_ref[...], ka_ref[...], wdec_ref[...], waaa_ref[...])
        for o_ref, val in zip((r_o, w_o, k_o, v_o, kkn_o, b_o), outs):
            o_ref[0, :, rows] = val
    z_last = zs_all[-1][1][:, sub - LANES:sub]
    carry_ref[...] = z_last
    zl_o[0] = z_last


def prompt_in_proj(x, norm_g, w_in, mu_shift, w0, w_decay_up, a0, w_aaa_up, k_k, k_a):
    B, T, D = x.shape
    tm = PROJ_TM
    assert T % tm == 0 and tm % PROJ_SUB == 0 and PROJ_SUB % LANES == 0
    f32 = jnp.float32
    pw = _proj_weights(norm_g, w_in, mu_shift, w0, w_decay_up, a0, w_aaa_up, k_k, k_a)
    consts = [pw["g"], pw["wz_t"], pw["wkv"].T, pw["wn"]] + pw["cols"] + pw["lora"]
    cm = lambda c: pl.BlockSpec((1, c, tm), lambda b, i: (b, 0, i))
    tok = lambda c: pl.BlockSpec((1, tm, c), lambda b, i: (b, i, 0))
    cms = lambda c: jax.ShapeDtypeStruct((B, c, T), f32)
    toks = lambda c: jax.ShapeDtypeStruct((B, T, c), f32)
    return pl.pallas_call(
        _prompt_proj_body,
        grid=(B, T // tm),
        in_specs=[tok(D)] + [_const_spec(c.shape, 2) for c in consts],
        out_specs=[cm(R_WIDTH)] * 6 + [cm(2 * BRANCH_W), cm(BRANCH_W), tok(R_WIDTH), tok(N_WIDTH), tok(N_WIDTH),
                                       tok(LANES), pl.BlockSpec((1, R_SHIFT_WIDTH, LANES), lambda b, i: (b, 0, 0))],
        out_shape=[cms(R_WIDTH)] * 6 + [cms(2 * BRANCH_W), cms(BRANCH_W), toks(R_WIDTH), toks(N_WIDTH), toks(N_WIDTH),
                                        toks(LANES), jax.ShapeDtypeStruct((B, R_SHIFT_WIDTH, LANES), f32)],
        scratch_shapes=[pltpu.VMEM((R_SHIFT_WIDTH, LANES), f32)],
        compiler_params=pltpu.CompilerParams(dimension_semantics=("parallel", "arbitrary"),
                                             vmem_limit_bytes=VMEM_LIMIT),
        name="prompt_in_proj",
    )(x, *consts)


def _proj_weights(norm_g, w_in, mu_shift, w0, w_decay_up, a0, w_aaa_up, k_k, k_a):
    f32, bf16 = jnp.float32, jnp.bfloat16
    D = w_in.shape[0]
    o = PROJ_OFFS
    wb = w_in.astype(bf16)
    gl_pad = LANES - PROJ_SPLITS[5]
    wn = jnp.concatenate([wb[:, o[1]:o[4]], wb[:, o[5]:o[6]], jnp.zeros((D, gl_pad), bf16)], axis=1)
    col = lambda p: p.astype(f32).reshape(-1, 1)
    return dict(g=norm_g.astype(f32).reshape(1, D), wz_t=wb[:, o[0]:o[1]].T, wkv=wb[:, o[4]:o[5]], wn=wn,
                cols=[col(mu_shift), col(w0), col(a0), col(k_k), col(k_a)],
                lora=[w_decay_up.astype(bf16).T, w_aaa_up.astype(bf16).T])


def _sample_proj_body(x_ref, prev_ref, g_ref, wz_ref, wkv_ref, wn_ref, mu_ref, w0_ref, a0_ref, kk_ref, ka_ref, wdec_ref,
                      waaa_ref, r_o, w_o, k_o, v_o, kkn_o, b_o, z_o, kv_o, gr_o, q_o, gn_o, gl_o, *, steps):
    bf16 = jnp.bfloat16
    m = x_ref.shape[0]
    x = x_ref[...]
    xn = (x * lax.rsqrt(jnp.mean(x * x, axis=-1, keepdims=True) + RMS_EPS) * g_ref[...]).astype(bf16)
    z = _dot_nt(wz_ref[...], xn)
    kv_o[...] = _dot(xn, wkv_ref[...])
    nrm = _dot(xn, wn_ref[...])
    gr_o[...] = nrm[:, 0:R_WIDTH]
    q_o[...] = nrm[:, R_WIDTH:R_WIDTH + N_WIDTH]
    gn_o[...] = nrm[:, R_WIDTH + N_WIDTH:R_WIDTH + 2 * N_WIDTH]
    gl_o[...] = nrm[:, R_WIDTH + 2 * N_WIDTH:]
    z_o[...] = z
    first = lax.broadcasted_iota(jnp.int32, (1, m), 1) % steps == 0
    zp = jnp.where(first, prev_ref[...], pltpu.roll(z, 1, axis=1))
    zs = z + (zp - z) * mu_ref[...]
    outs = _rwkv_prep(zs, w0_ref[...], a0_ref[...], kk_ref[...], ka_ref[...], wdec_ref[...], waaa_ref[...])
    for o_ref, val in zip((r_o, w_o, k_o, v_o, kkn_o, b_o), outs):
        o_ref[...] = val


def sample_in_proj(x, shift0, norm_g, w_in, mu_shift, w0, w_decay_up, a0, w_aaa_up, k_k, k_a):
    B, T, D = x.shape
    M = B * T
    f32 = jnp.float32
    pw = _proj_weights(norm_g, w_in, mu_shift, w0, w_decay_up, a0, w_aaa_up, k_k, k_a)
    prev = jnp.repeat(shift0.astype(f32).T, T, axis=1)
    args = [x.reshape(M, D), prev, pw["g"], pw["wz_t"], pw["wkv"], pw["wn"]] + pw["cols"] + pw["lora"]
    cm = lambda c: jax.ShapeDtypeStruct((c, M), f32)
    tok = lambda c: jax.ShapeDtypeStruct((M, c), f32)
    out_shape = [cm(R_WIDTH)] * 6 + [cm(R_SHIFT_WIDTH), tok(KV_WIDTH), tok(R_WIDTH), tok(N_WIDTH), tok(N_WIDTH),
                                     tok(LANES)]
    return pl.pallas_call(
        functools.partial(_sample_proj_body, steps=T),
        grid=(1,),
        in_specs=[_const_spec(a.shape, 1) for a in args],
        out_specs=[_const_spec(s.shape, 1) for s in out_shape],
        out_shape=out_shape,
        compiler_params=pltpu.CompilerParams(dimension_semantics=("arbitrary",), vmem_limit_bytes=VMEM_LIMIT),
        name="sample_in_proj",
    )(*args)


def _tree_sum(terms):
    while len(terms) > 1:
        terms = [terms[i] + terms[i + 1] for i in range(0, len(terms) - 1, 2)] + terms[len(terms) & ~1:]
    return terms[0]


def _rwkv_scan_body(r_ref, w_ref, k_ref, v_ref, kk_ref, b_ref, s0_ref, rk_ref, gw_ref, gb_ref, y_ref, s_ref):
    n = s_ref.shape[0]

    @pl.when(pl.program_id(1) == 0)
    def _():
        s_ref[...] = s0_ref[...]

    def step(t, carry):
        row = lambda ref, j: ref[t, pl.ds(j, 1), :]
        v = v_ref[t]
        grp = RWKV_COLS

        def p1(g, sa):
            return sa + _tree_sum([s_ref[g * grp + u] * row(kk_ref, g * grp + u) for u in range(grp)])

        sa = lax.fori_loop(0, n // grp, p1, jnp.zeros(v.shape, jnp.float32))

        def p2(g, y):
            terms = []
            for u in range(grp):
                j = g * grp + u
                s_j = s_ref[j] * row(w_ref, j) - sa * row(b_ref, j) + v * row(k_ref, j)
                s_ref[j] = s_j
                terms.append(s_j * row(r_ref, j))
            return y + _tree_sum(terms)

        y_ref[t] = lax.fori_loop(0, n // grp, p2, jnp.zeros(v.shape, jnp.float32))
        return carry

    lax.fori_loop(0, r_ref.shape[0], step, 0)

    y = y_ref[...]
    yc = y - jnp.mean(y, axis=1, keepdims=True)
    y = yc * lax.rsqrt(jnp.mean(yc * yc, axis=1, keepdims=True) + GN_EPS)
    y = y * gw_ref[...] + gb_ref[...]
    y_ref[...] = y + jnp.sum(r_ref[...] * k_ref[...] * rk_ref[...], axis=1, keepdims=True) * v_ref[...]


def rwkv_scan_lanes(seqs, s0, r_k, gn_w, gn_b, n_batch):
    T, N, BH = seqs[0].shape
    assert BH % LANES == 0
    tc = min(RWKV_TC, T)
    assert T % tc == 0
    per_lane = lambda p: jnp.tile(p.astype(jnp.float32).reshape(BH // n_batch, N).T, (1, n_batch))
    seq = pl.BlockSpec((tc, N, LANES), lambda l, t: (t, 0, l))
    state = pl.BlockSpec((N, N, LANES), lambda l, t: (0, 0, l))
    param = pl.BlockSpec((N, LANES), lambda l, t: (0, l))
    return pl.pallas_call(
        _rwkv_scan_body,
        grid=(BH // LANES, T // tc),
        in_specs=[seq] * 6 + [state] + [param] * 3,
        out_specs=[seq, state],
        out_shape=[jax.ShapeDtypeStruct((T, N, BH), jnp.float32), jax.ShapeDtypeStruct((N, N, BH), jnp.float32)],
        compiler_params=pltpu.CompilerParams(dimension_semantics=("parallel", "arbitrary"),
                                             vmem_limit_bytes=VMEM_LIMIT),
        name="rwkv_scan",
    )(*seqs, s0, per_lane(r_k), per_lane(gn_w), per_lane(gn_b))


def _lanes_from_cm(a, T):
    return a.reshape(-1, R_HEAD_DIM, T).transpose(2, 1, 0)


def _y_from_lanes(y, B):
    T, N, _ = y.shape
    return y.reshape(T, N, B, R_HEADS).transpose(2, 0, 3, 1).reshape(B, T, R_WIDTH)


def _state_to_lanes(s):
    B, H, N, _ = s.shape
    return s.astype(jnp.float32).transpose(3, 2, 0, 1).reshape(N, N, B * H)


def _state_from_lanes(s, B):
    N = s.shape[0]
    return s.reshape(N, N, B, R_HEADS).transpose(2, 3, 1, 0)


def _cmp_weights(w_pos, w_mix, n_rows):
    wp = jnp.broadcast_to(w_pos.astype(jnp.float32).reshape(2, 1, CMP_BLOCK, N_HEAD_DIM).transpose(2, 0, 1, 3),
                          (CMP_BLOCK, 2, N_KV_HEADS, N_HEAD_DIM)).reshape(CMP_BLOCK, BRANCH_W)
    wa = jnp.tile(wp[:CMP_STRIDE], (n_rows // CMP_STRIDE, 1)).T
    wb = jnp.tile(wp[CMP_STRIDE:], (n_rows // CMP_STRIDE, 1)).T
    wmix = jnp.zeros((BRANCH_W, BRANCH_W), jnp.float32)
    for e in range(2):
        for kh in range(N_KV_HEADS):
            o = (e * N_KV_HEADS + kh) * N_HEAD_DIM
            wmix = wmix.at[o:o + N_HEAD_DIM, o:o + N_HEAD_DIM].set(w_mix[e].astype(jnp.float32))
    return wa, wb, wmix.T


def _compress_prompt_body(kv_ref, wa_ref, wb_ref, pool_ref, wmix_ref, o_ref):
    x = kv_ref[0]
    xw = jnp.concatenate([x * wa_ref[...], x * wb_ref[...]], axis=1)
    o_ref[0] = _dot_x3(wmix_ref[...], _dot_exact_rhs(xw, pool_ref[...]))


def compress_prompt(kv_cs, w_pos, w_mix):
    B, _, T = kv_cs.shape
    assert CMP_BLOCK == 2 * CMP_STRIDE and T % CMP_STRIDE == 0
    ncp = T // CMP_STRIDE
    wa, wb, wmix_t = _cmp_weights(w_pos, w_mix, T)
    chunk = np.arange(T) // CMP_STRIDE
    pool = np.concatenate([chunk[:, None] == np.arange(ncp)[None, :],
                           chunk[:, None] == np.arange(ncp)[None, :] + 1], axis=0)
    pool = jnp.asarray(pool, jnp.bfloat16)
    return pl.pallas_call(
        _compress_prompt_body,
        grid=(B,),
        in_specs=[pl.BlockSpec((1, BRANCH_W, T), lambda b: (b, 0, 0))] + [_const_spec(c.shape, 1)
                                                                          for c in (wa, wb, pool, wmix_t)],
        out_specs=pl.BlockSpec((1, BRANCH_W, ncp), lambda b: (b, 0, 0)),
        out_shape=jax.ShapeDtypeStruct((B, BRANCH_W, ncp), jnp.float32),
        compiler_params=pltpu.CompilerParams(dimension_semantics=("parallel",), vmem_limit_bytes=VMEM_LIMIT),
        name="compress_prompt",
    )(kv_cs, wa, wb, pool, wmix_t)


def _nsa_prompt_body(q_ref, ksel_ref, kwin_ref, kc_ref, gl_ref, gn_ref, ovt_ref, ex_ref, o_ref):
    f32, bf16 = jnp.float32, jnp.bfloat16
    TQ, KC, G = NSA_TQ, NSA_KC, N_GROUP
    ncp = kc_ref.shape[2]
    n_sel = ovt_ref.shape[0]
    topk = min(SEL_TOPK, n_sel)
    qs = pl.program_id(1) * TQ
    qpos = qs + lax.broadcasted_iota(jnp.int32, (TQ, 1), 0)
    lo_half = lax.broadcasted_iota(jnp.int32, (TQ, LANES), 1) < N_HEAD_DIM

    def q_padded(h, half):
        tile = q_ref[0, :, (h // 2) * LANES:(h // 2 + 1) * LANES] * QSCALE
        if h % 2 != half:
            tile = pltpu.roll(tile, N_HEAD_DIM, axis=1)
        return jnp.where(lo_half if half == 0 else jnp.logical_not(lo_half), tile, 0.0)

    wstart = pl.multiple_of(jnp.maximum(qs - WINDOW, 0), TQ)
    kw = kwin_ref[0, 0:LANES, pl.ds(wstart, WINDOW + TQ)].astype(bf16)
    vw = kwin_ref[0, LANES:2 * LANES, pl.ds(wstart, WINDOW + TQ)].astype(bf16)
    kposw = wstart + lax.broadcasted_iota(jnp.int32, (1, WINDOW + TQ), 1)
    mask_w = (kposw <= qpos) & (kposw > qpos - WINDOW)

    kck = kc_ref[0, 0:LANES, :]
    kcv = kc_ref[0, LANES:2 * LANES, :].astype(bf16)
    kend = lax.broadcasted_iota(jnp.int32, (1, ncp), 1) * CMP_STRIDE + (CMP_BLOCK - 1)
    mask_c = kend <= qpos

    sid_t = lax.broadcasted_iota(jnp.int32, (n_sel, 1), 0)
    cur_t = (qs + lax.broadcasted_iota(jnp.int32, (1, TQ), 1)) // SEL_BLOCK
    valid_t = sid_t <= cur_t
    forced_t = (sid_t == 0) | (sid_t == cur_t) | (sid_t == cur_t - 1)
    n_chunks = qs // KC + 1

    H = N_HEADS
    qg = jnp.concatenate([q_padded(h, h // G) for h in range(H)], axis=0)
    qb = qg.astype(bf16)

    p = _masked_softmax_rows(_dot_x3(qg, kck).reshape(H, TQ, ncp), mask_c[None], base2=True)
    o_c = _dot_nt(p.reshape(H * TQ, ncp).astype(bf16), kcv).reshape(H, TQ, LANES)

    ovt = ovt_ref[...]
    selb = []
    for kvh in range(N_KV_HEADS):
        hi, mid, lo = _split3(jnp.sum(p[kvh * G:(kvh + 1) * G], axis=0))
        score = _dot_nt(ovt, hi) + _dot_nt(ovt, mid) + _dot_nt(ovt, lo)
        score = jnp.where(valid_t, score + jnp.where(forced_t, SEL_BIAS, 0.0), -SEL_BIAS)
        rank = jnp.zeros((n_sel, TQ), jnp.int32)
        for s2 in range(n_sel):
            row = score[s2:s2 + 1, :]
            ahead = (row > score) | ((row == score) & (s2 < sid_t))
            rank = rank + ahead.astype(jnp.int32)
        selb_t = jnp.concatenate([jnp.where((rank < topk) & valid_t, 0.0, SEL_NEG),
                                  jnp.zeros((LANES - n_sel, TQ), f32)], axis=0)
        selb += [selb_t.T.astype(bf16)] * G
    qa = jnp.concatenate([qb, jnp.concatenate(selb, axis=0)], axis=1)

    def pv_sums(e, v_t):
        n = v_t.shape[1]
        ones = jnp.ones((N_HEAD_DIM, n), bf16)
        va = (jnp.concatenate([v_t[0:N_HEAD_DIM], ones], axis=0), jnp.concatenate([ones, v_t[N_HEAD_DIM:]], axis=0))
        return jnp.concatenate([_dot_nt(e[kvh * G:(kvh + 1) * G].reshape(G * TQ, n), va[kvh]).reshape(G, TQ, LANES)
                                for kvh in range(N_KV_HEADS)], axis=0)

    def normalised(accl):
        return accl / pltpu.roll(accl, N_HEAD_DIM, axis=2)

    def chunk(c, carry, causal):
        m, accl = carry
        k0 = pl.multiple_of(c * KC, KC)
        ka = jnp.concatenate([ksel_ref[0, 0:LANES, pl.ds(k0, KC)].astype(bf16), ex_ref[:, pl.ds(k0, KC)]],
                             axis=0)
        sc = _dot(qa, ka).reshape(H, TQ, KC)
        if causal:
            kpos = k0 + lax.broadcasted_iota(jnp.int32, (1, KC), 1)
            sc = jnp.where((kpos <= qpos)[None], sc, SEL_NEG)
        m_new = jnp.maximum(m, jnp.max(sc, axis=-1, keepdims=True))
        e = jnp.exp2(sc - m_new).astype(bf16)
        return m_new, jnp.exp2(m - m_new) * accl + pv_sums(e, ksel_ref[0, LANES:2 * LANES, pl.ds(k0, KC)].astype(bf16))

    carry = (jnp.full((H, TQ, 1), SEL_NEG, f32), jnp.zeros((H, TQ, LANES), f32))
    carry = lax.fori_loop(0, n_chunks - 1, functools.partial(chunk, causal=False), carry)
    o_s = normalised(chunk(n_chunks - 1, carry, causal=True)[1])

    sw = jnp.where(mask_w[None], _dot(qb, kw).reshape(H, TQ, WINDOW + TQ), MASK_NEG)
    o_w = normalised(pv_sums(jnp.exp2(sw - jnp.max(sw, axis=-1, keepdims=True)).astype(bf16), vw))

    gates = jax.nn.sigmoid(gl_ref[0])
    heads = []
    for h in range(H):
        o = (gates[:, h:h + 1] * o_c[h] + gates[:, H + h:H + h + 1] * o_s[h]
             + gates[:, 2 * H + h:2 * H + h + 1] * o_w[h])
        heads.append(pltpu.roll(o, N_HEAD_DIM, axis=1) if h % 2 != h // G else o)

    for m in range(N_HEADS // 2):
        tile = jnp.where(lo_half, heads[2 * m], heads[2 * m + 1])
        o_ref[0, :, m * LANES:(m + 1) * LANES] = (
            tile * _silu(gn_ref[0, :, m * LANES:(m + 1) * LANES])).astype(o_ref.dtype)


def nsa_prompt(q, kv_cs, kv_w, kc, gl, gate_n):
    B, T, _ = q.shape
    TQ, KC = NSA_TQ, NSA_KC
    assert N_KV_HEADS == 2 and N_HEAD_DIM * 2 == LANES and T % KC == 0 and KC % TQ == 0 and T >= WINDOW + TQ
    assert SEL_BLOCK % CMP_STRIDE == 0 and WINDOW % TQ == 0 and TQ % LANES == 0 and T // SEL_BLOCK <= LANES
    ncp = T // CMP_STRIDE
    n_sel = T // SEL_BLOCK
    cs = np.arange(ncp) * CMP_STRIDE
    ss = np.arange(n_sel) * SEL_BLOCK
    overlap = (cs[:, None] < ss[None, :] + SEL_BLOCK) & (cs[:, None] + CMP_BLOCK > ss[None, :])
    overlap[ncp - 1] = False
    overlap_t = jnp.asarray(overlap.T, jnp.bfloat16)
    expand = jnp.asarray(np.arange(T)[None, :] // SEL_BLOCK == np.arange(LANES)[:, None], jnp.bfloat16)
    return pl.pallas_call(
        _nsa_prompt_body,
        grid=(B, T // TQ),
        in_specs=[pl.BlockSpec((1, TQ, N_WIDTH), lambda b, i: (b, i, 0)),
                  pl.BlockSpec((1, BRANCH_W, T), lambda b, i: (b, 1, 0)),
                  pl.BlockSpec((1, BRANCH_W, T), lambda b, i: (b, 0, 0)),
                  pl.BlockSpec((1, BRANCH_W, ncp), lambda b, i: (b, 0, 0)),
                  pl.BlockSpec((1, TQ, LANES), lambda b, i: (b, i, 0)),
                  pl.BlockSpec((1, TQ, N_WIDTH), lambda b, i: (b, i, 0)),
                  _const_spec((n_sel, ncp), 2), _const_spec((LANES, T), 2)],
        out_specs=pl.BlockSpec((1, TQ, N_WIDTH), lambda b, i: (b, i, 0)),
        out_shape=jax.ShapeDtypeStruct((B, T, N_WIDTH), jnp.bfloat16),
        compiler_params=pltpu.CompilerParams(dimension_semantics=("parallel", "arbitrary"),
                                             vmem_limit_bytes=VMEM_LIMIT),
        name="nsa_prompt",
    )(q, kv_cs, kv_w, kc, gl, gate_n, overlap_t, expand)


def _page_spec(k, branch, pages_per_step):
    return pl.BlockSpec((1, BRANCH_W, PAGE_SIZE), lambda b, g, pt: (pt[b, g * pages_per_step + k], branch, 0))


def _sample_compress_body(pt_ref, *refs):
    pages = refs[:SAMPLE_PG_CMP]
    wa_ref, wb_ref, wmix_ref, oa_ref, ob_ref = refs[SAMPLE_PG_CMP:]
    cpp = PAGE_SIZE // CMP_STRIDE
    xt = [p[0].T.reshape(cpp, CMP_STRIDE, BRANCH_W) for p in pages]
    for w_ref, o_ref in ((wa_ref, oa_ref), (wb_ref, ob_ref)):
        pooled = jnp.concatenate([jnp.sum(x * w_ref[...], axis=1) for x in xt], axis=0)
        o_ref[0] = _dot_x3(pooled, wmix_ref[...])


def sample_compress(cache_t, page_table, w_pos, w_mix):
    B, n_pages = page_table.shape
    PG = SAMPLE_PG_CMP
    assert n_pages % PG == 0 and CMP_BLOCK == 2 * CMP_STRIDE and PAGE_SIZE % CMP_STRIDE == 0
    rows = PG * PAGE_SIZE // CMP_STRIDE
    ncp = n_pages * PAGE_SIZE // CMP_STRIDE
    wa, wb, wmix_t = _cmp_weights(w_pos, w_mix, CMP_STRIDE)
    wa, wb, wmix = wa.T[None], wb.T[None], wmix_t.T
    const = lambda shape: pl.BlockSpec(shape, lambda b, g, pt: (0,) * len(shape))
    out = pl.BlockSpec((1, rows, BRANCH_W), lambda b, g, pt: (b, g, 0))
    return pl.pallas_call(
        _sample_compress_body,
        grid_spec=pltpu.PrefetchScalarGridSpec(
            num_scalar_prefetch=1, grid=(B, n_pages // PG),
            in_specs=[_page_spec(k, 0, PG) for k in range(PG)] + [const(wa.shape), const(wb.shape), const(wmix.shape)],
            out_specs=[out, out]),
        out_shape=[jax.ShapeDtypeStruct((B, ncp, BRANCH_W), jnp.float32)] * 2,
        compiler_params=pltpu.CompilerParams(dimension_semantics=("parallel", "arbitrary"),
                                             vmem_limit_bytes=VMEM_LIMIT),
        name="sample_compress",
    )(page_table, *([cache_t] * PG), wa, wb, wmix)


def _online_chunk(m, l, acc, s, keep, v, v_channels_major=False):
    s = jnp.where(keep, s, MASK_NEG)
    m_new = jnp.maximum(m, jnp.max(s, axis=-1, keepdims=True))
    alpha = jnp.exp(m - m_new)
    e = jnp.where(keep, jnp.exp(s - m_new), 0.0)
    eb = e.astype(jnp.bfloat16)
    pv = _dot_nt(eb, v) if v_channels_major else _dot(eb, v)
    return m_new, alpha * l + jnp.sum(e, axis=-1, keepdims=True), alpha * acc + pv


def _sample_attend_body(pt_ref, *refs, past):
    f32, bf16 = jnp.float32, jnp.bfloat16
    PG, G = SAMPLE_PG_ATT, N_GROUP
    pages = refs[:PG]
    (ka_ref, kb_ref, q_ref, kvn_ref, gl_ref, gn_ref, win_ref, ex_ref, ov_ref, o_ref,
     qf_ref, selb_ref, tail_ref, m_ref, l_ref, acc_ref, oc_ref) = refs[PG:]
    T = q_ref.shape[1]
    R = N_HEADS * T
    ncp = ka_ref.shape[1]
    nsp = ov_ref.shape[1]
    KC = PG * PAGE_SIZE
    n_past = past // SEL_BLOCK
    bpc = KC // SEL_BLOCK
    g = pl.program_id(1)
    lo_half = lax.broadcasted_iota(jnp.int32, (T, LANES), 1) < N_HEAD_DIM
    row = lax.broadcasted_iota(jnp.int32, (R, 1), 0)
    qpos = past + row % T

    @pl.when(g == 0)
    def _():
        pieces = []
        for h in range(N_HEADS):
            kvh = h // G
            tile = q_ref[0, :, (h // 2) * LANES:(h // 2 + 1) * LANES] * ATTN_SCALE
            if h % 2 != kvh:
                tile = pltpu.roll(tile, N_HEAD_DIM, axis=1)
            pieces.append(jnp.where(lo_half if kvh == 0 else jnp.logical_not(lo_half), tile, 0.0))
        q32 = jnp.concatenate(pieces, axis=0)
        qf_ref[...] = q32

        kc = ka_ref[0] + pltpu.roll(kb_ref[0], ncp - 1, axis=0)
        kend = lax.broadcasted_iota(jnp.int32, (1, ncp), 1) * CMP_STRIDE + (CMP_BLOCK - 1)
        p = _masked_softmax_rows(_dot_nt_x3(q32, kc[:, 0:LANES]), kend <= qpos)
        oc_ref[...] = _dot(p.astype(bf16), kc[:, LANES:2 * LANES].astype(bf16))
        imp = jnp.concatenate(
            [sum(p[(kvh * G + gg) * T:(kvh * G + gg + 1) * T] for gg in range(G)) for kvh in range(N_KV_HEADS)],
            axis=0)

        sid = lax.broadcasted_iota(jnp.int32, (1, nsp), 1)
        sid_f = sid.astype(f32)
        cur = (past + lax.broadcasted_iota(jnp.int32, (N_KV_HEADS * T, 1), 0) % T) // SEL_BLOCK
        valid = sid <= cur
        forced = (sid == 0) | (sid == cur) | (sid == cur - 1)
        score = _dot_exact_rhs(imp, ov_ref[...])
        score = jnp.where(valid, score + jnp.where(forced, SEL_BIAS, 0.0), -SEL_BIAS)

        def extract(_, carry):
            sc, chosen = carry
            best = jnp.max(sc, axis=-1, keepdims=True)
            first = jnp.min(jnp.where(sc == best, sid_f, float(nsp)), axis=-1, keepdims=True)
            hit = sid_f == first
            return jnp.where(hit, TAKEN, sc), jnp.where(hit, 1.0, chosen)

        _, chosen = lax.fori_loop(0, min(SEL_TOPK, n_past + 1), extract, (score, jnp.zeros_like(score)))
        sel = jnp.concatenate([jnp.where(valid, chosen, 0.0)[(h // G) * T:(h // G + 1) * T] for h in range(N_HEADS)],
                              axis=0)
        tail_ref[...] = sel[:, n_past:n_past + 1]
        selb = jnp.where(sel > 0.5, 0.0, SEL_NEG)
        for c in range(n_past // bpc):
            selb_ref[c] = jnp.concatenate([selb[:, c * bpc:(c + 1) * bpc], jnp.zeros((R, LANES - bpc), f32)],
                                          axis=1).astype(bf16)
        m_ref[...] = jnp.full(m_ref.shape, SEL_NEG, f32)
        l_ref[...] = jnp.zeros(l_ref.shape, f32)
        acc_ref[...] = jnp.zeros(acc_ref.shape, f32)

    qb = qf_ref[...].astype(bf16)
    ka = jnp.concatenate([p[0, 0:LANES, :] for p in pages], axis=1).astype(bf16)
    ka = jnp.concatenate([ka, ex_ref[...]], axis=0)
    va = jnp.concatenate([jnp.concatenate([p[0, LANES:2 * LANES, :] for p in pages], axis=1).astype(bf16),
                          jnp.ones((LANES, KC), bf16)], axis=0)
    s = _dot(jnp.concatenate([qb, selb_ref[g]], axis=1), ka)
    m = jnp.maximum(m_ref[...], jnp.max(s, axis=-1, keepdims=True))
    alpha = jnp.exp(m_ref[...] - m)
    pv = _dot_nt(jnp.exp(s - m).astype(bf16), va)
    l = alpha * l_ref[...] + pv[:, LANES:]
    acc = alpha * acc_ref[...] + pv[:, :LANES]
    m_ref[...] = m
    l_ref[...] = l
    acc_ref[...] = acc

    @pl.when(g == pl.num_programs(1) - 1)
    def _():
        def new_rows(c0):
            return jnp.concatenate([kvn_ref[0, :, c0:c0 + LANES], jnp.zeros((LANES - T, LANES), f32)],
                                   axis=0).astype(bf16)

        npos = past + lax.broadcasted_iota(jnp.int32, (1, LANES), 1)
        is_new = npos < past + T
        keep_t = is_new & (npos <= qpos) & (tail_ref[...] > 0.5)
        _, l_s, acc_s = _online_chunk(m, l, acc, _dot_nt(qb, new_rows(BRANCH_W)), keep_t, new_rows(BRANCH_W + LANES))
        o_s = acc_s / l_s

        nb = win_ref.shape[2]
        bpos = past - nb + lax.broadcasted_iota(jnp.int32, (1, nb), 1)
        keep_b = (bpos <= qpos) & (bpos > qpos - WINDOW) & (bpos >= 0)
        mw, lw, aw = _online_chunk(jnp.full((R, 1), MASK_NEG, f32), jnp.zeros((R, 1), f32), jnp.zeros((R, LANES), f32),
                                   _dot(qb, win_ref[0, 0:LANES, :].astype(bf16)), keep_b,
                                   win_ref[0, LANES:2 * LANES, :].astype(bf16), True)
        keep_n = is_new & (npos <= qpos) & (npos > qpos - WINDOW)
        _, lw, aw = _online_chunk(mw, lw, aw, _dot_nt(qb, new_rows(2 * BRANCH_W)), keep_n,
                                  new_rows(2 * BRANCH_W + LANES))
        o_w = aw / lw

        gates = jax.nn.sigmoid(gl_ref[0])
        gcol = lambda br: jnp.concatenate([gates[:, br * N_HEADS + h:br * N_HEADS + h + 1] for h in range(N_HEADS)],
                                          axis=0)
        o = gcol(0) * oc_ref[...] + gcol(1) * o_s + gcol(2) * o_w
        for mt in range(N_HEADS // 2):
            halves = []
            for h in (2 * mt, 2 * mt + 1):
                piece = o[h * T:(h + 1) * T]
                if h % 2 != h // G:
                    piece = pltpu.roll(piece, N_HEAD_DIM, axis=1)
                halves.append(piece)
            o_ref[0, :, mt * LANES:(mt + 1) * LANES] = (jnp.where(lo_half, halves[0], halves[1])
                                                        * _silu(gn_ref[0, :, mt * LANES:(mt + 1) * LANES]))


def nsa_sample(q, kv_new, kc_a, kc_b, gl, gate_n, cache_t, page_table, win_t):
    B, T, _ = q.shape
    n_pages = page_table.shape[1]
    past = n_pages * PAGE_SIZE
    PG = SAMPLE_PG_ATT
    KC = PG * PAGE_SIZE
    ng = n_pages // PG
    assert n_pages % PG == 0 and past % SEL_BLOCK == 0 and T <= SEL_BLOCK and T <= LANES
    assert (past + T) // CMP_STRIDE == past // CMP_STRIDE and KC % SEL_BLOCK == 0
    ncp = past // CMP_STRIDE
    n_sel = past // SEL_BLOCK + 1
    nsp = -(-n_sel // LANES) * LANES
    cs = np.arange(ncp) * CMP_STRIDE
    ss = np.arange(nsp) * SEL_BLOCK
    overlap = (cs[:, None] < ss[None, :] + SEL_BLOCK) & (cs[:, None] + CMP_BLOCK > ss[None, :])
    overlap[ncp - 1] = False
    overlap[:, n_sel:] = False
    overlap = jnp.asarray(overlap, jnp.bfloat16)
    assert KC // SEL_BLOCK <= LANES
    expand = jnp.asarray(np.arange(KC)[None, :] // SEL_BLOCK == np.arange(LANES)[:, None], jnp.bfloat16)
    R = N_HEADS * T
    per_b = lambda shape: pl.BlockSpec((1,) + shape, lambda b, g, pt: (b, 0, 0))
    return pl.pallas_call(
        functools.partial(_sample_attend_body, past=past),
        grid_spec=pltpu.PrefetchScalarGridSpec(
            num_scalar_prefetch=1, grid=(B, ng),
            in_specs=[_page_spec(k, 1, PG) for k in range(PG)] + [
                per_b((ncp, BRANCH_W)), per_b((ncp, BRANCH_W)), per_b((T, N_WIDTH)), per_b((T, KV_WIDTH)),
                per_b((T, LANES)), per_b((T, N_WIDTH)), per_b((BRANCH_W, win_t.shape[2])),
                pl.BlockSpec((LANES, KC), lambda b, g, pt: (0, 0)),
                pl.BlockSpec((ncp, nsp), lambda b, g, pt: (0, 0))],
            out_specs=per_b((T, N_WIDTH)),
            scratch_shapes=[pltpu.VMEM((R, LANES), jnp.float32), pltpu.VMEM((ng, R, LANES), jnp.bfloat16),
                            pltpu.VMEM((R, 1), jnp.float32), pltpu.VMEM((R, 1), jnp.float32),
                            pltpu.VMEM((R, LANES), jnp.float32), pltpu.VMEM((R, LANES), jnp.float32),
                            pltpu.VMEM((R, LANES), jnp.float32)]),
        out_shape=jax.ShapeDtypeStruct((B, T, N_WIDTH), jnp.float32),
        compiler_params=pltpu.CompilerParams(dimension_semantics=("parallel", "arbitrary"),
                                             vmem_limit_bytes=VMEM_LIMIT),
        name="nsa_sample",
    )(page_table, *([cache_t] * PG), kc_a, kc_b, q, kv_new, gl, gate_n, win_t, expand, overlap)


def _out_proj_body(yr_ref, gr_ref, on_ref, x_ref, w_ref, g_ref, o_ref, *, final_norm):
    bf16 = jnp.bfloat16
    a = (yr_ref[...] * _silu(gr_ref[...])).astype(bf16)
    y = x_ref[...] + _dot(a, w_ref[0:R_WIDTH, :]) + _dot(on_ref[...].astype(bf16), w_ref[R_WIDTH:MIX_WIDTH, :])
    if final_norm:
        y = y * lax.rsqrt(jnp.mean(y * y, axis=-1, keepdims=True) + RMS_EPS) * g_ref[...]
    o_ref[...] = y


def out_proj(y_r, gate_r, o_n, x, w_out, final_g):
    M, D = x.shape
    tm = min(OUT_TM, M)
    assert M % tm == 0
    g = (jnp.ones((D,), jnp.float32) if final_g is None else final_g.astype(jnp.float32)).reshape(1, D)
    rows = lambda c: pl.BlockSpec((tm, c), lambda i: (i, 0))
    return pl.pallas_call(
        functools.partial(_out_proj_body, final_norm=final_g is not None),
        grid=(M // tm,),
        in_specs=[rows(R_WIDTH), rows(R_WIDTH), rows(N_WIDTH), rows(D), _const_spec((MIX_WIDTH, D), 1),
                  _const_spec((1, D), 1)],
        out_specs=rows(D),
        out_shape=jax.ShapeDtypeStruct((M, D), jnp.float32),
        compiler_params=pltpu.CompilerParams(dimension_semantics=("parallel",), vmem_limit_bytes=VMEM_LIMIT),
        name="out_proj",
    )(y_r, gate_r, o_n, x, w_out.astype(jnp.bfloat16), g)


def prompt_mixers(x, norm_g, w_in, mu_shift, w0, w_decay_up, a0, w_aaa_up, k_k, k_a, r_k, gn_w, gn_b,
                  w_cmp_pos, w_cmp_mix):
    B, T, D = x.shape
    (r, w, k, v, kk, b, kv_cs, kv_w, gate_r, q, gate_n, gl, z_last) = prompt_in_proj(
        x, norm_g, w_in, mu_shift, w0, w_decay_up, a0, w_aaa_up, k_k, k_a)
    seqs = [_lanes_from_cm(a, T) for a in (r, w, k, v, kk, b)]
    o_n = nsa_prompt(q, kv_cs, kv_w, compress_prompt(kv_cs, w_cmp_pos, w_cmp_mix), gl, gate_n)
    seqs, o_n = lax.optimization_barrier((seqs, o_n))
    s0 = jnp.zeros((R_HEAD_DIM, R_HEAD_DIM, B * R_HEADS), jnp.float32)
    y, s_fin = rwkv_scan_lanes(seqs, s0, r_k, gn_w, gn_b, B)
    heads = (N_KV_HEADS, N_HEAD_DIM)
    kv_rows = kv_cs.reshape((B, 4) + heads + (T,)).transpose(0, 4, 1, 2, 3)
    n_keep = min(WINDOW, T)
    win_rows = kv_w[:, :, T - n_keep:].reshape((B, 2) + heads + (n_keep,)).transpose(0, 4, 1, 2, 3)
    outs = (kv_rows, win_rows, _state_from_lanes(s_fin, B), z_last[:, :, LANES - 1])
    return (_y_from_lanes(y, B), gate_r, o_n), outs


def sample_mixers(x, cache, win_buf, wkv0, shift0, page_table, norm_g, w_in, mu_shift, w0, w_decay_up, a0,
                  w_aaa_up, k_k, k_a, r_k, gn_w, gn_b, w_cmp_pos, w_cmp_mix):
    B, T, D = x.shape
    (r, w, k, v, kk, b, z, kv, gate_r, q, gate_n, gl) = sample_in_proj(
        x, shift0, norm_g, w_in, mu_shift, w0, w_decay_up, a0, w_aaa_up, k_k, k_a)
    seqs = [a.reshape(R_HEADS, R_HEAD_DIM, B, T).transpose(3, 1, 2, 0).reshape(T, R_HEAD_DIM, B * R_HEADS)
            for a in (r, w, k, v, kk, b)]
    y, s_fin = rwkv_scan_lanes(seqs, _state_to_lanes(wkv0), r_k, gn_w, gn_b, B)

    cache_t = cache.reshape(cache.shape[0], PAGE_SIZE, 2 * BRANCH_W).transpose(0, 2, 1)
    nb = win_buf.shape[1]
    win_t = win_buf.reshape(B, nb, BRANCH_W).transpose(0, 2, 1)
    kc_a, kc_b = sample_compress(cache_t, page_table, w_cmp_pos, w_cmp_mix)
    kv = kv.reshape(B, T, KV_WIDTH)
    o_n = nsa_sample(q.reshape(B, T, N_WIDTH), kv, kc_a, kc_b, gl.reshape(B, T, LANES),
                     gate_n.reshape(B, T, N_WIDTH), cache_t, page_table, win_t)
    kv = kv.reshape(B, T, KV_SLOTS, N_KV_HEADS, N_HEAD_DIM)
    keys_w = jnp.concatenate([win_buf.astype(kv.dtype), kv[:, :, 4:6]], axis=1)
    n_keep = min(WINDOW, nb + T)
    shift = z.reshape(R_SHIFT_WIDTH, B, T)[:, :, T - 1].T
    outs = (kv[:, :, 0:4], keys_w[:, nb + T - n_keep:], _state_from_lanes(s_fin, B), shift)
    return (_y_from_lanes(y, B), gate_r.reshape(B, T, R_WIDTH), o_n), outs


def _finish(mixed, x, w_out, final_g):
    B, T, D = x.shape
    y_r, gate_r, o_n = (a.reshape(B * T, -1) for a in mixed)
    return out_proj(y_r, gate_r, o_n, x.reshape(B * T, D), w_out, final_g).reshape(B, T, D)


def kernel(x_prompt, x_sample, cache_kv, cache_kv_win, state_wkv, state_shift, page_table,
           norm_g, w_in, mu_shift, w0, w_decay_up, a0, w_aaa_up, k_k, k_a, r_k, gn_w, gn_b,
           w_cmp_pos, w_cmp_mix, w_out, final_g):
    y_p, y_s = x_prompt, x_sample
    kvp, kvs, wnp, wns, skp, sks, shp, shs = [], [], [], [], [], [], [], []
    for layer in range(DEPTH):
        lw = (norm_g[layer], w_in[layer], mu_shift[layer], w0[layer], w_decay_up[layer], a0[layer],
              w_aaa_up[layer], k_k[layer], k_a[layer], r_k[layer], gn_w[layer], gn_b[layer],
              w_cmp_pos[layer], w_cmp_mix[layer])
        fg = final_g if layer == DEPTH - 1 else None
        mixed_p, (p_kv, p_win, p_wkv, p_shift) = prompt_mixers(y_p, *lw)
        mixed_s, (s_kv, s_win, s_wkv, s_shift) = sample_mixers(y_s, cache_kv[layer], cache_kv_win[layer],
                                                               state_wkv[layer], state_shift[layer], page_table, *lw)
        mixed_p, mixed_s = lax.optimization_barrier((mixed_p, mixed_s))
        y_p = _finish(mixed_p, y_p, w_out[layer], fg)
        y_s = _finish(mixed_s, y_s, w_out[layer], fg)
        kvp.append(p_kv); kvs.append(s_kv); wnp.append(p_win); wns.append(s_win)
        skp.append(p_wkv); sks.append(s_wkv); shp.append(p_shift); shs.append(s_shift)
    return (y_p, y_s, jnp.stack(kvp), jnp.stack(kvs), jnp.stack(wnp), jnp.stack(wns),
            jnp.stack(skp), jnp.stack(sks), jnp.stack(shp), jnp.stack(shs))
```
